```python
import math
import jax, jax.numpy as jnp
from jax import lax
import numpy as np

D_MODEL = 2048
BATCH = 2
SEQ = 4096
DEPTH = 1

HEAD_DIM = 128
D_MIX = D_MODEL
FOX_HEADS = 8
NSA_HEADS = 8
NSA_KV_HEADS = 2
NSA_GROUP = NSA_HEADS // NSA_KV_HEADS
CMP_LEN = 32
CMP_STRIDE = 16
CMP_HIDDEN = 2 * HEAD_DIM
SLC_LEN = 64
SLC_TOPK = 16
WINDOW = 512
Q_BLOCK = 128
D_FF = ((8 * D_MODEL // 3 + 255) // 256) * 256
ROPE_THETA = 10000.0
NORM_EPS = 1e-6
MASK_VALUE = -1e30
SEL_BONUS = 1e6
D_IN = (3 * FOX_HEADS * HEAD_DIM + FOX_HEADS + NSA_HEADS * HEAD_DIM
        + 6 * NSA_KV_HEADS * HEAD_DIM + 3 * NSA_HEADS)

kernel_name = "hymba_fox_nsa_adaln_block"


def _rmsnorm(x, g):
    xf = x.astype(jnp.float32)
    y = xf * lax.rsqrt(jnp.mean(xf * xf, axis=-1, keepdims=True) + NORM_EPS)
    return y.astype(x.dtype) * g


def _modulate(h, shift, scale):
    return h * (1 + scale[:, None, :]) + shift[:, None, :]


def _rope(x, pos):
    d = x.shape[-1]
    inv_freq = ROPE_THETA ** (-jnp.arange(0, d, 2, dtype=jnp.float32) / d)
    ang = pos.astype(jnp.float32)[..., None] * inv_freq
    cos = jnp.cos(ang)[:, :, None, :].astype(x.dtype)
    sin = jnp.sin(ang)[:, :, None, :].astype(x.dtype)
    x1, x2 = jnp.split(x, 2, axis=-1)
    return jnp.concatenate([x1 * cos - x2 * sin, x2 * cos + x1 * sin], axis=-1)


def _masked_softmax(s, mask):
    s = jnp.where(mask, s.astype(jnp.float32), MASK_VALUE)
    m = jnp.max(s, axis=-1, keepdims=True)
    e = jnp.where(mask, jnp.exp(s - m), 0.0)
    return e / jnp.maximum(jnp.sum(e, axis=-1, keepdims=True), 1e-30)


def _split_points():
    sizes = ([FOX_HEADS * HEAD_DIM] * 3 + [FOX_HEADS] + [NSA_HEADS * HEAD_DIM]
             + [NSA_KV_HEADS * HEAD_DIM] * 6 + [3 * NSA_HEADS])
    points, acc = [], 0
    for s in sizes[:-1]:
        acc += s
        points.append(acc)
    return points


def _fox_attention(q, k, v, log_f):
    B, S, H, D = q.shape
    cum = jnp.cumsum(log_f, axis=1).transpose(0, 2, 1)
    kpos = jnp.arange(S)
    scale = D ** -0.5

    def block(i):
        start = i * Q_BLOCK
        qb = lax.dynamic_slice_in_dim(q, start, Q_BLOCK, axis=1)
        cb = lax.dynamic_slice_in_dim(cum, start, Q_BLOCK, axis=2)
        qpos = start + jnp.arange(Q_BLOCK)
        s = jnp.einsum('bqhd,bkhd->bhqk', qb, k).astype(jnp.float32) * scale
        s = s + (cb[..., :, None] - cum[..., None, :])
        mask = (kpos[None, :] <= qpos[:, None])[None, None]
        p = _masked_softmax(s, mask)
        return jnp.einsum('bhqk,bkhd->bqhd', p.astype(v.dtype), v)

    out = lax.map(block, jnp.arange(S // Q_BLOCK))
    return out.transpose(1, 0, 2, 3, 4).reshape(B, S, H, D)


def _compress(x, cmp_pos, w1, w2, idx):
    B, _, G, D = x.shape
    blk = x[:, idx] + cmp_pos[None, None, :, None, :]
    blk = blk.transpose(0, 1, 3, 2, 4).reshape(B, idx.shape[0], G, CMP_LEN * D)
    return jax.nn.gelu(blk @ w1) @ w2


def _nsa_attention(q, k_c, v_c, k_s, v_s, k_w, v_w, gates, positions,
                   cmp_pos, w_kc1, w_kc2, w_vc1, w_vc2):
    B, S, H, D = q.shape
    G, R = NSA_KV_HEADS, NSA_GROUP
    scale = D ** -0.5
    q = _rope(q, positions)
    k_s = _rope(k_s, positions)
    k_w = _rope(k_w, positions)

    n_cmp = (S - CMP_LEN) // CMP_STRIDE + 1
    cmp_idx = jnp.arange(n_cmp)[:, None] * CMP_STRIDE + jnp.arange(CMP_LEN)[None, :]
    cmp_end = cmp_idx[:, -1]
    kc = _rope(_compress(k_c, cmp_pos, w_kc1, w_kc2, cmp_idx), positions[:, cmp_end])
    vc = _compress(v_c, cmp_pos, w_vc1, w_vc2, cmp_idx)

    n_slc = S // SLC_LEN
    n_sel = min(SLC_TOPK, n_slc)
    cs = jnp.arange(n_cmp) * CMP_STRIDE
    ss = jnp.arange(n_slc) * SLC_LEN
    ov = (jnp.minimum(cs[:, None] + CMP_LEN, ss[None, :] + SLC_LEN)
          - jnp.maximum(cs[:, None], ss[None, :]))
    overlap = jnp.clip(ov, 0).astype(jnp.float32) / CMP_STRIDE

    k_sb = k_s.reshape(B, n_slc, SLC_LEN, G, D).transpose(0, 3, 1, 2, 4)
    v_sb = v_s.reshape(B, n_slc, SLC_LEN, G, D).transpose(0, 3, 1, 2, 4)
    pad = ((0, 0), (WINDOW, 0), (0, 0), (0, 0))
    k_wp = jnp.pad(k_w, pad)
    v_wp = jnp.pad(v_w, pad)
    q_g = q.reshape(B, S, G, R, D)
    gates_g = gates.reshape(B, S, G, R, 3)
    b_idx = jnp.arange(B)[:, None, None, None]
    g_idx = jnp.arange(G)[None, :, None, None]
    blk_ids = jnp.arange(n_slc)

    def block(i):
        start = i * Q_BLOCK
        qb = lax.dynamic_slice_in_dim(q_g, start, Q_BLOCK, axis=1)
        qpos = start + jnp.arange(Q_BLOCK)

        s_c = jnp.einsum('bqgrd,bcgd->bgrqc', qb, kc) * scale
        mask_c = (cmp_end[None, :] <= qpos[:, None])[None, None, None]
        p_c = _masked_softmax(s_c, mask_c)
        o_c = jnp.einsum('bgrqc,bcgd->bqgrd', p_c.astype(vc.dtype), vc)

        imp = jnp.einsum('bgrqc,cn->bgqn', p_c, overlap)
        cur = qpos // SLC_LEN
        forced = ((blk_ids[None, :] == 0) | (blk_ids[None, :] == cur[:, None])
                  | (blk_ids[None, :] == cur[:, None] - 1))
        causal_blk = blk_ids[None, :] <= cur[:, None]
        score = jnp.where(forced[None, None], SEL_BONUS, imp)
        score = jnp.where(causal_blk[None, None], score, -SEL_BONUS)
        _, sel = lax.top_k(score, n_sel)
        ks_sel = k_sb[b_idx, g_idx, sel].reshape(B, G, Q_BLOCK, n_sel * SLC_LEN, D)
        vs_sel = v_sb[b_idx, g_idx, sel].reshape(B, G, Q_BLOCK, n_sel * SLC_LEN, D)
        tok = (sel[..., None] * SLC_LEN + jnp.arange(SLC_LEN)).reshape(B, G, Q_BLOCK, -1)
        mask_s = (tok <= qpos[None, None, :, None])[:, :, None]
        s_s = jnp.einsum('bqgrd,bgqnd->bgrqn', qb, ks_sel) * scale
        p_s = _masked_softmax(s_s, mask_s)
        o_s = jnp.einsum('bgrqn,bgqnd->bqgrd', p_s.astype(vs_sel.dtype), vs_sel)

        kw = lax.dynamic_slice_in_dim(k_wp, start, Q_BLOCK + WINDOW, axis=1)
        vw = lax.dynamic_slice_in_dim(v_wp, start, Q_BLOCK + WINDOW, axis=1)
        kpos = start - WINDOW + jnp.arange(Q_BLOCK + WINDOW)
        mask_w = ((kpos[None, :] <= qpos[:, None]) & (qpos[:, None] - kpos[None, :] < WINDOW)
                  & (kpos[None, :] >= 0))[None, None, None]
        s_w = jnp.einsum('bqgrd,bkgd->bgrqk', qb, kw) * scale
        p_w = _masked_softmax(s_w, mask_w)
        o_w = jnp.einsum('bgrqk,bkgd->bqgrd', p_w.astype(vw.dtype), vw)

        gb = lax.dynamic_slice_in_dim(gates_g, start, Q_BLOCK, axis=1)
        o = gb[..., 0:1] * o_c + gb[..., 1:2] * o_s + gb[..., 2:3] * o_w
        return o.reshape(B, Q_BLOCK, H, D)

    out = lax.map(block, jnp.arange(S // Q_BLOCK))
    return out.transpose(1, 0, 2, 3, 4).reshape(B, S, H, D)


def _hybrid_mixer(u, positions, w_in, b_fgate, cmp_pos, w_kc1, w_kc2, w_vc1, w_vc2,
                  beta_fox, beta_nsa, w_out):
    B, S, _ = u.shape
    z = u @ w_in
    (q_f, k_f, v_f, f_logit, q_n, k_c, v_c, k_s, v_s, k_w, v_w, g_n) = jnp.split(
        z, _split_points(), axis=-1)
    fh = lambda t: t.reshape(B, S, FOX_HEADS, HEAD_DIM)
    nq = lambda t: t.reshape(B, S, NSA_HEADS, HEAD_DIM)
    nkv = lambda t: t.reshape(B, S, NSA_KV_HEADS, HEAD_DIM)
    log_f = jax.nn.log_sigmoid((f_logit + b_fgate).astype(jnp.float32))
    o_f = _fox_attention(fh(q_f), fh(k_f), fh(v_f), log_f)
    gates = jax.nn.sigmoid(g_n.reshape(B, S, NSA_HEADS, 3))
    o_n = _nsa_attention(nq(q_n), nkv(k_c), nkv(v_c), nkv(k_s), nkv(v_s), nkv(k_w), nkv(v_w),
                         gates, positions, cmp_pos, w_kc1, w_kc2, w_vc1, w_vc2)
    y_f = _rmsnorm(o_f.reshape(B, S, FOX_HEADS * HEAD_DIM), beta_fox)
    y_n = _rmsnorm(o_n.reshape(B, S, NSA_HEADS * HEAD_DIM), beta_nsa)
    return jnp.concatenate([y_f, y_n], axis=-1) @ w_out


def _swiglu(u, w_gate, w_up, w_down):
    return (jax.nn.silu(u @ w_gate) * (u @ w_up)) @ w_down


def setup_inputs(seed: int = 0) -> dict:
    key = jax.random.key(seed)
    ks = jax.random.split(key, 24)
    f32 = jnp.float32
    nrm = lambda k, shape, s: jax.random.normal(k, shape, f32) * s
    L = DEPTH
    x = jax.random.normal(ks[0], (BATCH, SEQ, D_MODEL), f32)
    c = jax.random.normal(ks[1], (BATCH, D_MODEL), f32)
    offset = jax.random.randint(ks[2], (BATCH, 1), 0, 1024, dtype=jnp.int32)
    positions = offset + jnp.arange(SEQ, dtype=jnp.int32)[None, :]
    return {
        "x": x,
        "c": c,
        "positions": positions,
        "w_ada": nrm(ks[3], (L, D_MODEL, 6 * D_MODEL), 0.5 * D_MODEL ** -0.5),
        "b_ada": nrm(ks[4], (L, 6 * D_MODEL), 0.02),
        "norm_attn": 1.0 + nrm(ks[5], (L, D_MODEL), 0.02),
        "norm_ffn": 1.0 + nrm(ks[6], (L, D_MODEL), 0.02),
        "w_in": nrm(ks[7], (L, D_MODEL, D_IN), D_MODEL ** -0.5),
        "b_fgate": 2.0 + nrm(ks[8], (L, FOX_HEADS), 0.5),
        "cmp_pos": nrm(ks[9], (L, CMP_LEN, HEAD_DIM), 0.02),
        "w_kc1": nrm(ks[10], (L, CMP_LEN * HEAD_DIM, CMP_HIDDEN), (CMP_LEN * HEAD_DIM) ** -0.5),
        "w_kc2": nrm(ks[11], (L, CMP_HIDDEN, HEAD_DIM), CMP_HIDDEN ** -0.5),
        "w_vc1": nrm(ks[12], (L, CMP_LEN * HEAD_DIM, CMP_HIDDEN), (CMP_LEN * HEAD_DIM) ** -0.5),
        "w_vc2": nrm(ks[13], (L, CMP_HIDDEN, HEAD_DIM), CMP_HIDDEN ** -0.5),
        "beta_fox": 1.0 + nrm(ks[14], (L, FOX_HEADS * HEAD_DIM), 0.02),
        "beta_nsa": 1.0 + nrm(ks[15], (L, NSA_HEADS * HEAD_DIM), 0.02),
        "w_out": nrm(ks[16], (L, D_MIX, D_MODEL), D_MIX ** -0.5),
        "w_gate": nrm(ks[17], (L, D_MODEL, D_FF), D_MODEL ** -0.5),
        "w_up": nrm(ks[18], (L, D_MODEL, D_FF), D_MODEL ** -0.5),
        "w_down": nrm(ks[19], (L, D_FF, D_MODEL), D_FF ** -0.5),
        "final_norm": 1.0 + nrm(ks[20], (D_MODEL,), 0.02),
    }


def reference(x, c, positions, w_ada, b_ada, norm_attn, norm_ffn, w_in, b_fgate, cmp_pos,
              w_kc1, w_kc2, w_vc1, w_vc2, beta_fox, beta_nsa, w_out, w_gate, w_up, w_down,
              final_norm):
    h = x
    for l in range(DEPTH):
        mod = jax.nn.silu(c) @ w_ada[l] + b_ada[l]
        sh1, sc1, g1, sh2, sc2, g2 = jnp.split(mod, 6, axis=-1)
        u = _modulate(_rmsnorm(h, norm_attn[l]), sh1, sc1)
        y = _hybrid_mixer(u, positions, w_in[l], b_fgate[l], cmp_pos[l], w_kc1[l], w_kc2[l],
                          w_vc1[l], w_vc2[l], beta_fox[l], beta_nsa[l], w_out[l])
        h = h + g1[:, None, :] * y
        u = _modulate(_rmsnorm(h, norm_ffn[l]), sh2, sc2)
        h = h + g2[:, None, :] * _swiglu(u, w_gate[l], w_up[l], w_down[l])
    return _rmsnorm(h, final_norm)
```

```python
import functools

import jax
import jax.numpy as jnp
from jax import lax
from jax.experimental import pallas as pl
from jax.experimental.pallas import tpu as pltpu

D_MODEL = 2048
HEAD_DIM = 128
FOX_HEADS = 8
NSA_HEADS = 8
NSA_KV_HEADS = 2
NSA_GROUP = NSA_HEADS // NSA_KV_HEADS
CMP_LEN = 32
CMP_STRIDE = 16
CMP_HIDDEN = 2 * HEAD_DIM
SLC_LEN = 64
SLC_TOPK = 16
WINDOW = 512
D_FF = 5632
ROPE_THETA = 10000.0
NORM_EPS = 1e-6
MASK_VALUE = -1e30
SEL_BONUS = 1e6

FOX_W = FOX_HEADS * HEAD_DIM
NSA_W = NSA_HEADS * HEAD_DIM
KV_W = NSA_KV_HEADS * HEAD_DIM
Z_MAIN = 3 * FOX_W + NSA_W + 6 * KV_W
Z_SMALL = 128
COL_QN = 3 * FOX_W
COL_KC = COL_QN + NSA_W
COL_KS = COL_KC + 2 * KV_W
COL_KW = COL_KS + 2 * KV_W

VMEM_LIMIT = 48 * 1024 * 1024
SCALE = HEAD_DIM ** -0.5


def _cparams(sem):
    return pltpu.CompilerParams(dimension_semantics=sem, vmem_limit_bytes=VMEM_LIMIT)


def _dot(a, b):
    return jnp.dot(a, b, preferred_element_type=jnp.float32)


def _dot_nt(a, b):
    return lax.dot_general(a, b, (((1,), (1,)), ((), ())), preferred_element_type=jnp.float32)


def _rms(x):
    return x * lax.rsqrt(jnp.mean(x * x, axis=-1, keepdims=True) + NORM_EPS)


ADA_TN = 1024
ADA_KC = 256


def _ada_kernel(ct_ref, w_ref, b_ref, o_ref, *, nb):
    tn = w_ref.shape[1]
    ct = ct_ref[...]
    st = ct * jax.nn.sigmoid(ct)

    accs = [jnp.zeros((8, tn), jnp.float32) for _ in range(nb)]
    for kc in range(D_MODEL // ADA_KC):
        w = w_ref[kc * ADA_KC:(kc + 1) * ADA_KC, :]
        for b in range(nb):
            s = st[kc * ADA_KC:(kc + 1) * ADA_KC, b:b + 1]
            accs[b] = accs[b] + jnp.sum((w * s).reshape(ADA_KC // 8, 8, tn), axis=0)
    for b in range(nb):
        o_ref[b:b + 1, :] = jnp.sum(accs[b], axis=0, keepdims=True) + b_ref[...]


def _ada(c, w_ada, b_ada):
    nb = c.shape[0]
    n = w_ada.shape[1]
    return pl.pallas_call(
        functools.partial(_ada_kernel, nb=nb),
        out_shape=jax.ShapeDtypeStruct((nb, n), jnp.float32),
        grid=(n // ADA_TN,),
        in_specs=[
            pl.BlockSpec((D_MODEL, nb), lambda j: (0, 0)),
            pl.BlockSpec((D_MODEL, ADA_TN), lambda j: (0, j)),
            pl.BlockSpec((1, ADA_TN), lambda j: (0, j)),
        ],
        out_specs=pl.BlockSpec((nb, ADA_TN), lambda j: (0, j)),
        compiler_params=_cparams(("arbitrary",)),
        name="ada",
    )(c.T, w_ada, b_ada.reshape(1, n))


def _rope_table_kernel(pos_ref, inv_ref, cos_ref, sin_ref):
    ang = pos_ref[0].astype(jnp.float32) * inv_ref[...]
    lane = lax.broadcasted_iota(jnp.int32, ang.shape, 1)
    cos_ref[0] = jnp.cos(ang)
    sin_ref[0] = jnp.where(lane < HEAD_DIM // 2, -1.0, 1.0) * jnp.sin(ang)


def _rope_tables(pos, ts):
    nb, n = pos.shape
    inv = ROPE_THETA ** (-jnp.arange(0, HEAD_DIM, 2, dtype=jnp.float32) / HEAD_DIM)
    inv2 = jnp.concatenate([inv, inv]).reshape(1, HEAD_DIM)
    shp = jax.ShapeDtypeStruct((nb, n, HEAD_DIM), jnp.float32)
    spec = pl.BlockSpec((1, ts, HEAD_DIM), lambda b, i: (b, i, 0))
    return pl.pallas_call(
        _rope_table_kernel,
        out_shape=(shp, shp),
        grid=(nb, n // ts),
        in_specs=[pl.BlockSpec((1, ts, 1), lambda b, i: (b, i, 0)),
                  pl.BlockSpec((1, HEAD_DIM), lambda b, i: (0, 0))],
        out_specs=(spec, spec),
        compiler_params=_cparams(("arbitrary", "arbitrary")),
        name="rope_tables",
    )(pos.reshape(nb, n, 1), inv2)


def _apply_rope(x, cos, sin):
    return x * cos + pltpu.roll(x, HEAD_DIM // 2, axis=1) * sin


IN_TM = 512
IN_TN = 512


def _inproj_kernel(x_ref, g_ref, sc_ref, sh_ref, w_ref, ws_ref, cos_ref, sin_ref,
                   z_ref, zs_ref, u_ref):
    j = pl.program_id(2)

    @pl.when(j == 0)
    def _():
        u = _rms(x_ref[0]) * g_ref[...]
        u = u * (1.0 + sc_ref[0]) + sh_ref[0]
        ub = u.astype(jnp.bfloat16)
        u_ref[...] = ub
        zs_ref[0] = _dot(ub, ws_ref[...])

    acc = _dot(u_ref[...], w_ref[...])
    n_heads = IN_TN // HEAD_DIM
    rope_all = (j >= COL_QN // IN_TN) & (j < COL_KC // IN_TN)
    rope_half = (j == COL_KS // IN_TN) | (j == COL_KW // IN_TN)

    def roped(n_rope):
        cos = cos_ref[0]
        sin = sin_ref[0]
        parts = []
        for h in range(n_heads):
            a = acc[:, h * HEAD_DIM:(h + 1) * HEAD_DIM]
            parts.append(_apply_rope(a, cos, sin) if h < n_rope else a)
        return jnp.concatenate(parts, axis=1).astype(z_ref.dtype)

    @pl.when(rope_all)
    def _():
        z_ref[0] = roped(n_heads)

    @pl.when(rope_half)
    def _():
        z_ref[0] = roped(KV_W // HEAD_DIM)

    @pl.when(jnp.logical_not(rope_all | rope_half))
    def _():
        z_ref[0] = acc.astype(z_ref.dtype)


def _inproj(x, norm_w, sc, sh, w_main, w_small, cos, sin):
    nb, s, d = x.shape
    grid = (nb, s // IN_TM, Z_MAIN // IN_TN)
    return pl.pallas_call(
        _inproj_kernel,
        out_shape=(jax.ShapeDtypeStruct((nb, s, Z_MAIN), jnp.bfloat16),
                   jax.ShapeDtypeStruct((nb, s, Z_SMALL), jnp.float32)),
        grid=grid,
        in_specs=[
            pl.BlockSpec((1, IN_TM, d), lambda b, i, j: (b, i, 0)),
            pl.BlockSpec((1, d), lambda b, i, j: (0, 0)),
            pl.BlockSpec((1, 1, d), lambda b, i, j: (b, 0, 0)),
            pl.BlockSpec((1, 1, d), lambda b, i, j: (b, 0, 0)),
            pl.BlockSpec((d, IN_TN), lambda b, i, j: (0, j)),
            pl.BlockSpec((d, Z_SMALL), lambda b, i, j: (0, 0)),
            pl.BlockSpec((1, IN_TM, HEAD_DIM), lambda b, i, j: (b, i, 0)),
            pl.BlockSpec((1, IN_TM, HEAD_DIM), lambda b, i, j: (b, i, 0)),
        ],
        out_specs=(pl.BlockSpec((1, IN_TM, IN_TN), lambda b, i, j: (b, i, j)),
                   pl.BlockSpec((1, IN_TM, Z_SMALL), lambda b, i, j: (b, i, 0))),
        scratch_shapes=[pltpu.VMEM((IN_TM, d), jnp.bfloat16)],
        compiler_params=_cparams(("arbitrary", "arbitrary", "arbitrary")),
        name="inproj",
    )(x, norm_w.reshape(1, d), sc.reshape(nb, 1, d), sh.reshape(nb, 1, d),
      w_main, w_small, cos, sin)


DEC_T = 512


def _decay_kernel(zs_ref, bf_ref, cum_ref, carry_ref):
    @pl.when(pl.program_id(1) == 0)
    def _():
        carry_ref[...] = jnp.zeros_like(carry_ref)

    xt = zs_ref[0].T[0:FOX_HEADS, :] + bf_ref[...]
    lf = jnp.minimum(xt, 0.0) - jnp.log1p(jnp.exp(-jnp.abs(xt)))
    lane = lax.broadcasted_iota(jnp.int32, lf.shape, 1)
    sh = 1
    while sh < DEC_T:
        lf = lf + jnp.where(lane >= sh, pltpu.roll(lf, sh, axis=1), 0.0)
        sh *= 2
    cum = lf + carry_ref[:, 0:1]
    cum_ref[0] = cum
    carry_ref[...] = jnp.broadcast_to(cum[:, DEC_T - 1:DEC_T], carry_ref.shape)


def _decay(z_small, b_fgate):
    nb, s, _ = z_small.shape
    return pl.pallas_call(
        _decay_kernel,
        out_shape=jax.ShapeDtypeStruct((nb, FOX_HEADS, s), jnp.float32),
        grid=(nb, s // DEC_T),
        in_specs=[pl.BlockSpec((1, DEC_T, Z_SMALL), lambda b, i: (b, i, 0)),
                  pl.BlockSpec((FOX_HEADS, 1), lambda b, i: (0, 0))],
        out_specs=pl.BlockSpec((1, FOX_HEADS, DEC_T), lambda b, i: (b, 0, i)),
        scratch_shapes=[pltpu.VMEM((FOX_HEADS, 128), jnp.float32)],
        compiler_params=_cparams(("arbitrary", "arbitrary")),
        name="decay",
    )(z_small, b_fgate.reshape(FOX_HEADS, 1))


def _compress_kernel(x_ref, w1_ref, w2_ref, pos_ref, cos_ref, sin_ref, o_ref):
    kind = pl.program_id(1)
    half = CMP_STRIDE * HEAD_DIM
    x = x_ref[0, 0]
    nc = x.shape[0]
    p = _dot(x, w1_ref[0, 0:half, :])
    q = _dot(x, w1_ref[0, half:2 * half, :])
    posb = jnp.broadcast_to(pos_ref[...], (8, 2 * half)).astype(jnp.bfloat16)
    pterm = _dot(posb, w1_ref[0])[0:1, :]
    h = p + pltpu.roll(q, nc - 1, axis=0) + pterm
    a = jax.nn.gelu(h).astype(jnp.bfloat16)
    out = _dot(a, w2_ref[0])

    @pl.when(kind < NSA_KV_HEADS)
    def _():
        o_ref[0, 0] = _apply_rope(out, cos_ref[0], sin_ref[0]).astype(o_ref.dtype)

    @pl.when(kind >= NSA_KV_HEADS)
    def _():
        o_ref[0, 0] = out.astype(o_ref.dtype)


def _compress(xc, w1, w2, cmp_pos, cos_c, sin_c):
    nb, nk, nc, width = xc.shape
    return pl.pallas_call(
        _compress_kernel,
        out_shape=jax.ShapeDtypeStruct((nb, nk, nc, HEAD_DIM), jnp.bfloat16),
        grid=(nb, nk),
        in_specs=[
            pl.BlockSpec((1, 1, nc, width), lambda b, k: (b, k, 0, 0)),
            pl.BlockSpec((1, 2 * width, CMP_HIDDEN), lambda b, k: (k // NSA_KV_HEADS, 0, 0)),
            pl.BlockSpec((1, CMP_HIDDEN, HEAD_DIM), lambda b, k: (k // NSA_KV_HEADS, 0, 0)),
            pl.BlockSpec((1, 2 * width), lambda b, k: (0, 0)),
            pl.BlockSpec((1, nc, HEAD_DIM), lambda b, k: (b, 0, 0)),
            pl.BlockSpec((1, nc, HEAD_DIM), lambda b, k: (b, 0, 0)),
        ],
        out_specs=pl.BlockSpec((1, 1, nc, HEAD_DIM), lambda b, k: (b, k, 0, 0)),
        compiler_params=_cparams(("arbitrary", "arbitrary")),
        name="compress",
    )(xc, w1, w2, cmp_pos.reshape(1, 2 * width), cos_c, sin_c)


FOX_T = 256


def _fox_kernel(q_ref, k_ref, v_ref, cum_ref, o_ref):
    h = pl.program_id(1)
    i = pl.program_id(2)
    t = FOX_T
    q = q_ref[0]

    def tile(j, carry, diag):
        m, l, acc = carry
        k0 = pl.multiple_of(j * t, t)
        k = k_ref[0, pl.ds(k0, t), :]
        v = v_ref[0, pl.ds(k0, t), :]
        ck = cum_ref[0, pl.ds(h, 1), pl.ds(k0, t)]
        s = _dot_nt(q, k) * SCALE - ck
        if diag:
            r = lax.broadcasted_iota(jnp.int32, (t, t), 0)
            c = lax.broadcasted_iota(jnp.int32, (t, t), 1)
            s = jnp.where(c <= r, s, MASK_VALUE)
        m_new = jnp.maximum(m, jnp.max(s, axis=1, keepdims=True))
        alpha = jnp.exp(m - m_new)
        p = jnp.exp(s - m_new)
        l = alpha * l + jnp.sum(p, axis=1, keepdims=True)
        acc = alpha * acc + _dot(p.astype(jnp.bfloat16), v)
        return m_new, l, acc

    init = (jnp.full((t, 1), MASK_VALUE, jnp.float32),
            jnp.zeros((t, 1), jnp.float32),
            jnp.zeros((t, HEAD_DIM), jnp.float32))
    carry = lax.fori_loop(0, i, lambda j, c: tile(j, c, False), init)
    m, l, acc = tile(i, carry, True)
    o_ref[0] = (acc / l).astype(o_ref.dtype)


def _fox(z_main, cum):
    nb, s, _ = z_main.shape
    grid = (nb, FOX_HEADS, s // FOX_T)
    return pl.pallas_call(
        _fox_kernel,
        out_shape=jax.ShapeDtypeStruct((nb, s, FOX_W), jnp.bfloat16),
        grid=grid,
        in_specs=[
            pl.BlockSpec((1, FOX_T, HEAD_DIM), lambda b, h, i: (b, i, h)),
            pl.BlockSpec((1, s, HEAD_DIM), lambda b, h, i: (b, 0, FOX_HEADS + h)),
            pl.BlockSpec((1, s, HEAD_DIM), lambda b, h, i: (b, 0, 2 * FOX_HEADS + h)),
            pl.BlockSpec((1, FOX_HEADS, s), lambda b, h, i: (b, 0, 0)),
        ],
        out_specs=pl.BlockSpec((1, FOX_T, HEAD_DIM), lambda b, h, i: (b, i, h)),
        compiler_params=_cparams(("arbitrary", "arbitrary", "arbitrary")),
        name="fox",
    )(z_main, z_main, z_main, cum)


NSA_TQ = 128
NSA_TK = 256
NSA_ROWS = NSA_GROUP * NSA_TQ
NSEL_PAD = 128
SLC_SHIFT = SLC_LEN.bit_length() - 1
assert 1 << SLC_SHIFT == SLC_LEN


def _softmax_masked(s, mask):
    s = jnp.where(mask, s, MASK_VALUE)
    m = jnp.max(s, axis=1, keepdims=True)
    e = jnp.where(mask, jnp.exp(s - m), 0.0)
    return e / jnp.maximum(jnp.sum(e, axis=1, keepdims=True), 1e-30)


def _nsa_kernel(q_ref, zs_ref, kc_ref, vc_ref, ks_ref, vs_ref, kw_ref, vw_ref, ovt_ref,
                o_ref, *, seq):
    g = pl.program_id(1)
    i = pl.program_id(2)
    tq = NSA_TQ
    q0 = i * tq
    qb = q_ref[0]
    q4 = jnp.concatenate([qb[:, r * HEAD_DIM:(r + 1) * HEAD_DIM] for r in range(NSA_GROUP)],
                         axis=0)

    def qpos_of(shape):
        r = lax.broadcasted_iota(jnp.int32, shape, 0)
        return q0 + (r & (tq - 1))

    kc = kc_ref[0, 0]
    vc = vc_ref[0, 0]
    nc = kc.shape[0]
    s_c = _dot_nt(q4, kc) * SCALE
    cend = lax.broadcasted_iota(jnp.int32, s_c.shape, 1) * CMP_STRIDE + (CMP_LEN - 1)
    p_c = _softmax_masked(s_c, cend <= qpos_of(s_c.shape))
    o_c = _dot(p_c.astype(jnp.bfloat16), vc)

    psum = p_c[0:tq] + p_c[tq:2 * tq] + p_c[2 * tq:3 * tq] + p_c[3 * tq:4 * tq]
    p_hi = psum.astype(jnp.bfloat16)
    p_lo = (psum - p_hi.astype(jnp.float32)).astype(jnp.bfloat16)
    ovt = ovt_ref[...]
    imp_t = _dot_nt(ovt, p_hi) + _dot_nt(ovt, p_lo)
    blk = lax.broadcasted_iota(jnp.int32, imp_t.shape, 0)
    cur = (q0 + lax.broadcasted_iota(jnp.int32, imp_t.shape, 1)) >> SLC_SHIFT
    forced = (blk == 0) | (blk == cur) | (blk == cur - 1)
    score = jnp.where(forced, SEL_BONUS, imp_t)
    score = jnp.where(blk <= cur, score, -SEL_BONUS)
    n_slc = seq // SLC_LEN
    score = jnp.where(blk < n_slc, score, -jnp.inf)
    rank = jnp.zeros(score.shape, jnp.int32)
    for mrow in range(n_slc):
        row = score[mrow:mrow + 1, :]
        before = (row > score) | ((row == score) & (blk > mrow))
        rank = rank + before.astype(jnp.int32)
    n_sel = min(SLC_TOPK, n_slc)
    sel = jnp.where(rank < n_sel, 1.0, 0.0).T.astype(jnp.bfloat16)

    tk = NSA_TK

    def sel_tile(j, carry):
        m, l, acc = carry
        k0 = pl.multiple_of(j * tk, tk)
        k = ks_ref[0, pl.ds(k0, tk), :]
        v = vs_ref[0, pl.ds(k0, tk), :]
        s = _dot_nt(q4, k) * SCALE
        nb_i = lax.broadcasted_iota(jnp.int32, (NSEL_PAD, tk), 0)
        tok = k0 + lax.broadcasted_iota(jnp.int32, (NSEL_PAD, tk), 1)
        expand = jnp.where(nb_i == (tok >> SLC_SHIFT), 1.0, 0.0).astype(jnp.bfloat16)
        picked = _dot(sel, expand)
        picked4 = jnp.concatenate([picked] * NSA_GROUP, axis=0)
        kpos = k0 + lax.broadcasted_iota(jnp.int32, s.shape, 1)
        mask = (picked4 > 0.5) & (kpos <= qpos_of(s.shape))
        s = jnp.where(mask, s, MASK_VALUE)
        m_new = jnp.maximum(m, jnp.max(s, axis=1, keepdims=True))
        alpha = jnp.exp(m - m_new)
        p = jnp.where(mask, jnp.exp(s - m_new), 0.0)
        l = alpha * l + jnp.sum(p, axis=1, keepdims=True)
        acc = alpha * acc + _dot(p.astype(jnp.bfloat16), v)
        return m_new, l, acc

    init = (jnp.full((NSA_ROWS, 1), MASK_VALUE, jnp.float32),
            jnp.zeros((NSA_ROWS, 1), jnp.float32),
            jnp.zeros((NSA_ROWS, HEAD_DIM), jnp.float32))
    n_tiles = (q0 + tq - 1) // tk + 1
    _, l_s, acc_s = lax.fori_loop(0, n_tiles, sel_tile, init)
    o_s = acc_s / jnp.maximum(l_s, 1e-30)

    wlen = WINDOW + tq
    w0 = pl.multiple_of(jnp.maximum(q0 - WINDOW, 0), tq)
    kw = kw_ref[0, pl.ds(w0, wlen), :]
    vw = vw_ref[0, pl.ds(w0, wlen), :]
    s_w = _dot_nt(q4, kw) * SCALE
    kpos = w0 + lax.broadcasted_iota(jnp.int32, s_w.shape, 1)
    qpos = qpos_of(s_w.shape)
    p_w = _softmax_masked(s_w, (kpos <= qpos) & (qpos - kpos < WINDOW))
    o_w = _dot(p_w.astype(jnp.bfloat16), vw)

    gates = jax.nn.sigmoid(zs_ref[0])
    outs = []
    for r in range(NSA_GROUP):
        rows = slice(r * tq, (r + 1) * tq)
        o_r = jnp.zeros((tq, HEAD_DIM), jnp.float32)
        for br, o_b in enumerate((o_c, o_s, o_w)):
            col = FOX_HEADS + 3 * (g * NSA_GROUP + r) + br
            gcol = jnp.sum(jnp.where(
                lax.broadcasted_iota(jnp.int32, gates.shape, 1) == col, gates, 0.0),
                axis=1, keepdims=True)
            o_r = o_r + gcol * o_b[rows]
        outs.append(o_r)
    o_ref[0] = jnp.concatenate(outs, axis=1).astype(o_ref.dtype)


def _nsa(z_main, z_small, kvc, ovt):
    nb, s, _ = z_main.shape
    nc = kvc.shape[2]
    gw = NSA_GROUP * HEAD_DIM
    full = lambda col0: pl.BlockSpec(
        (1, s, HEAD_DIM), lambda b, g, i, c=col0 // HEAD_DIM: (b, 0, c + g))
    return pl.pallas_call(
        functools.partial(_nsa_kernel, seq=s),
        out_shape=jax.ShapeDtypeStruct((nb, s, NSA_W), jnp.bfloat16),
        grid=(nb, NSA_KV_HEADS, s // NSA_TQ),
        in_specs=[
            pl.BlockSpec((1, NSA_TQ, gw), lambda b, g, i: (b, i, COL_QN // gw + g)),
            pl.BlockSpec((1, NSA_TQ, Z_SMALL), lambda b, g, i: (b, i, 0)),
            pl.BlockSpec((1, 1, nc, HEAD_DIM), lambda b, g, i: (b, g, 0, 0)),
            pl.BlockSpec((1, 1, nc, HEAD_DIM), lambda b, g, i: (b, NSA_KV_HEADS + g, 0, 0)),
            full(COL_KS), full(COL_KS + KV_W), full(COL_KW), full(COL_KW + KV_W),
            pl.BlockSpec((NSEL_PAD, nc), lambda b, g, i: (0, 0)),
        ],
        out_specs=pl.BlockSpec((1, NSA_TQ, gw), lambda b, g, i: (b, i, g)),
        compiler_params=_cparams(("arbitrary", "arbitrary", "arbitrary")),
        name="nsa",
    )(z_main, z_small, kvc, kvc, z_main, z_main, z_main, z_main, ovt)


OUT_TM = 512
OUT_TN = 512


def _outproj_kernel(of_ref, on_ref, bf_ref, bn_ref, w_ref, x_ref, g_ref, h_ref, y_ref):
    @pl.when(pl.program_id(2) == 0)
    def _():
        yf = _rms(of_ref[0].astype(jnp.float32)) * bf_ref[...]
        yn = _rms(on_ref[0].astype(jnp.float32)) * bn_ref[...]
        y_ref[:, 0:FOX_W] = yf.astype(jnp.bfloat16)
        y_ref[:, FOX_W:FOX_W + NSA_W] = yn.astype(jnp.bfloat16)

    h_ref[0] = x_ref[0] + g_ref[0] * _dot(y_ref[...], w_ref[...])


def _outproj(o_f, o_n, beta_f, beta_n, w_out, x, gate):
    nb, s, d = x.shape
    return pl.pallas_call(
        _outproj_kernel,
        out_shape=jax.ShapeDtypeStruct((nb, s, d), jnp.float32),
        grid=(nb, s // OUT_TM, d // OUT_TN),
        in_specs=[
            pl.BlockSpec((1, OUT_TM, FOX_W), lambda b, i, j: (b, i, 0)),
            pl.BlockSpec((1, OUT_TM, NSA_W), lambda b, i, j: (b, i, 0)),
            pl.BlockSpec((1, FOX_W), lambda b, i, j: (0, 0)),
            pl.BlockSpec((1, NSA_W), lambda b, i, j: (0, 0)),
            pl.BlockSpec((FOX_W + NSA_W, OUT_TN), lambda b, i, j: (0, j)),
            pl.BlockSpec((1, OUT_TM, OUT_TN), lambda b, i, j: (b, i, j)),
            pl.BlockSpec((1, 1, OUT_TN), lambda b, i, j: (b, 0, j)),
        ],
        out_specs=pl.BlockSpec((1, OUT_TM, OUT_TN), lambda b, i, j: (b, i, j)),
        scratch_shapes=[pltpu.VMEM((OUT_TM, FOX_W + NSA_W), jnp.bfloat16)],
        compiler_params=_cparams(("arbitrary", "arbitrary", "arbitrary")),
        name="outproj",
    )(o_f, o_n, beta_f.reshape(1, FOX_W), beta_n.reshape(1, NSA_W), w_out, x,
      gate.reshape(nb, 1, d))


FFN_TM = 512
FFN_TF = 512


def _ffn_kernel(h_ref, nw_ref, sc_ref, sh_ref, g_ref, fw_ref, wg_ref, wu_ref, wd_ref,
                o_ref, u_ref, acc_ref, *, final):
    j = pl.program_id(2)

    @pl.when(j == 0)
    def _():
        u = _rms(h_ref[0]) * nw_ref[...]
        u = u * (1.0 + sc_ref[0]) + sh_ref[0]
        u_ref[...] = u.astype(jnp.bfloat16)
        acc_ref[...] = jnp.zeros_like(acc_ref)

    u = u_ref[...]
    a = _dot(u, wg_ref[...])
    b = _dot(u, wu_ref[...])
    mid = (a * jax.nn.sigmoid(a) * b).astype(jnp.bfloat16)
    acc_ref[...] += _dot(mid, wd_ref[...])

    @pl.when(j == pl.num_programs(2) - 1)
    def _():
        h2 = h_ref[0] + g_ref[0] * acc_ref[...]
        o_ref[0] = _rms(h2) * fw_ref[...] if final else h2


def _ffn(h1, norm_w, sc, sh, gate, final_w, w_gate, w_up, w_down, final):
    nb, s, d = h1.shape
    f = w_gate.shape[1]
    vec = pl.BlockSpec((1, d), lambda b, i, j: (0, 0))
    bvec = pl.BlockSpec((1, 1, d), lambda b, i, j: (b, 0, 0))
    return pl.pallas_call(
        functools.partial(_ffn_kernel, final=final),
        out_shape=jax.ShapeDtypeStruct((nb, s, d), jnp.float32),
        grid=(nb, s // FFN_TM, f // FFN_TF),
        in_specs=[
            pl.BlockSpec((1, FFN_TM, d), lambda b, i, j: (b, i, 0)),
            vec, bvec, bvec, bvec, vec,
            pl.BlockSpec((d, FFN_TF), lambda b, i, j: (0, j)),
            pl.BlockSpec((d, FFN_TF), lambda b, i, j: (0, j)),
            pl.BlockSpec((FFN_TF, d), lambda b, i, j: (j, 0)),
        ],
        out_specs=pl.BlockSpec((1, FFN_TM, d), lambda b, i, j: (b, i, 0)),
        scratch_shapes=[pltpu.VMEM((FFN_TM, d), jnp.bfloat16),
                        pltpu.VMEM((FFN_TM, d), jnp.float32)],
        compiler_params=_cparams(("arbitrary", "arbitrary", "arbitrary")),
        name="ffn",
    )(h1, norm_w.reshape(1, d), sc.reshape(nb, 1, d), sh.reshape(nb, 1, d),
      gate.reshape(nb, 1, d), final_w.reshape(1, d), w_gate, w_up, w_down)


def _split_w_in(w_in):
    sizes = ([FOX_W] * 3 + [FOX_HEADS] + [NSA_W] + [KV_W] * 6 + [3 * NSA_HEADS])
    offs = [0]
    for sz in sizes:
        offs.append(offs[-1] + sz)
    piece = lambda n: w_in[:, offs[n]:offs[n + 1]]
    main = jnp.concatenate([piece(n) for n in (0, 1, 2, 4, 5, 6, 7, 8, 9, 10)], axis=1)
    small = jnp.concatenate(
        [piece(3), piece(11),
         jnp.zeros((w_in.shape[0], Z_SMALL - FOX_HEADS - 3 * NSA_HEADS), w_in.dtype)], axis=1)
    return main.astype(jnp.bfloat16), small.astype(jnp.bfloat16)


def _overlap_t(n_cmp_pad, n_slc):
    cs = jnp.arange(n_cmp_pad) * CMP_STRIDE
    ss = jnp.arange(NSEL_PAD) * SLC_LEN
    ov = (jnp.minimum(cs[None, :] + CMP_LEN, ss[:, None] + SLC_LEN)
          - jnp.maximum(cs[None, :], ss[:, None]))
    ov = jnp.clip(ov, 0).astype(jnp.float32) / CMP_STRIDE
    ov = jnp.where((jnp.arange(NSEL_PAD) < n_slc)[:, None], ov, 0.0)
    return ov.astype(jnp.bfloat16)


def kernel(x, c, positions, w_ada, b_ada, norm_attn, norm_ffn, w_in, b_fgate, cmp_pos,
           w_kc1, w_kc2, w_vc1, w_vc2, beta_fox, beta_nsa, w_out, w_gate, w_up, w_down,
           final_norm):
    nb, s, d = x.shape
    depth = w_ada.shape[0]
    n_cmp_pad = s // CMP_STRIDE
    cmp_end = jnp.arange(n_cmp_pad) * CMP_STRIDE + (CMP_LEN - 1)
    pos_c = jnp.take(positions, jnp.minimum(cmp_end, s - 1), axis=1)
    cos, sin = _rope_tables(positions, 512)
    cos_c, sin_c = _rope_tables(pos_c, n_cmp_pad)
    ovt = _overlap_t(n_cmp_pad, s // SLC_LEN)

    h = x
    for l in range(depth):
        mod = _ada(c, w_ada[l], b_ada[l])
        sh1, sc1, g1, sh2, sc2, g2 = jnp.split(mod, 6, axis=-1)
        w_main, w_small = _split_w_in(w_in[l])
        z_main, z_small = _inproj(h, norm_attn[l], sc1, sh1, w_main, w_small, cos, sin)
        cum = _decay(z_small, b_fgate[l])

        xc = z_main[:, :, COL_KC:COL_KC + 2 * KV_W]
        xc = xc.reshape(nb, n_cmp_pad, CMP_STRIDE, 2 * NSA_KV_HEADS, HEAD_DIM)
        xc = xc.transpose(0, 3, 1, 2, 4).reshape(nb, 2 * NSA_KV_HEADS, n_cmp_pad,
                                                 CMP_STRIDE * HEAD_DIM)
        w1 = jnp.stack([w_kc1[l], w_vc1[l]]).astype(jnp.bfloat16)
        w2 = jnp.stack([w_kc2[l], w_vc2[l]]).astype(jnp.bfloat16)
        kvc = _compress(xc, w1, w2, cmp_pos[l], cos_c, sin_c)

        o_f = _fox(z_main, cum)
        o_n = _nsa(z_main, z_small, kvc, ovt)
        h1 = _outproj(o_f, o_n, beta_fox[l], beta_nsa[l], w_out[l].astype(jnp.bfloat16), h, g1)
        h = _ffn(h1, norm_ffn[l], sc2, sh2, g2, final_norm, w_gate[l].astype(jnp.bfloat16),
                 w_up[l].astype(jnp.bfloat16), w_down[l].astype(jnp.bfloat16),
                 final=(l == depth - 1))
    return h
```

```python
import functools

import jax
import jax.numpy as jnp
from jax import lax
from jax.experimental import pallas as pl
from jax.experimental.pallas import tpu as pltpu

D_MODEL = 2048
HEAD_DIM = 128
FOX_HEADS = 8
NSA_HEADS = 8
NSA_KV_HEADS = 2
NSA_GROUP = NSA_HEADS // NSA_KV_HEADS
CMP_LEN = 32
CMP_STRIDE = 16
CMP_HIDDEN = 2 * HEAD_DIM
SLC_LEN = 64
SLC_TOPK = 16
WINDOW = 512
D_FF = 5632
ROPE_THETA = 10000.0
NORM_EPS = 1e-6
MASK_VALUE = -1e30
SEL_BONUS = 1e6

FOX_W = FOX_HEADS * HEAD_DIM
NSA_W = NSA_HEADS * HEAD_DIM
KV_W = NSA_KV_HEADS * HEAD_DIM
Z_MAIN = 3 * FOX_W + NSA_W + 6 * KV_W
Z_SMALL = 128
COL_QN = 3 * FOX_W
COL_KC = COL_QN + NSA_W
COL_KS = COL_KC + 2 * KV_W
COL_KW = COL_KS + 2 * KV_W

VMEM_LIMIT = 48 * 1024 * 1024
SCALE = HEAD_DIM ** -0.5
LOG2E = 1.4426950408889634


def _cparams(sem):
    return pltpu.CompilerParams(dimension_semantics=sem, vmem_limit_bytes=VMEM_LIMIT)


def _dot(a, b):
    return jnp.dot(a, b, preferred_element_type=jnp.float32)


def _dot_nt(a, b):
    return lax.dot_general(a, b, (((1,), (1,)), ((), ())), preferred_element_type=jnp.float32)


def _rms(x):
    return x * lax.rsqrt(jnp.mean(x * x, axis=-1, keepdims=True) + NORM_EPS)


ADA_TN = 1024
ADA_KC = 256


def _ada_kernel(ct_ref, w_ref, b_ref, o_ref, *, nb):
    tn = w_ref.shape[1]
    ct = ct_ref[...]
    st = ct * jax.nn.sigmoid(ct)

    accs = [jnp.zeros((8, tn), jnp.float32) for _ in range(nb)]
    for kc in range(D_MODEL // ADA_KC):
        w = w_ref[kc * ADA_KC:(kc + 1) * ADA_KC, :]
        for b in range(nb):
            s = st[kc * ADA_KC:(kc + 1) * ADA_KC, b:b + 1]
            accs[b] = accs[b] + jnp.sum((w * s).reshape(ADA_KC // 8, 8, tn), axis=0)
    for b in range(nb):
        o_ref[b:b + 1, :] = jnp.sum(accs[b], axis=0, keepdims=True) + b_ref[...]


def _ada(c, w_ada, b_ada):
    nb = c.shape[0]
    n = w_ada.shape[1]
    return pl.pallas_call(
        functools.partial(_ada_kernel, nb=nb),
        out_shape=jax.ShapeDtypeStruct((nb, n), jnp.float32),
        grid=(n // ADA_TN,),
        in_specs=[
            pl.BlockSpec((D_MODEL, nb), lambda j: (0, 0)),
            pl.BlockSpec((D_MODEL, ADA_TN), lambda j: (0, j)),
            pl.BlockSpec((1, ADA_TN), lambda j: (0, j)),
        ],
        out_specs=pl.BlockSpec((nb, ADA_TN), lambda j: (0, j)),
        compiler_params=_cparams(("arbitrary",)),
        name="ada",
    )(c.T, w_ada, b_ada.reshape(1, n))


def _rope_table_kernel(pos_ref, inv_ref, cos_ref, sin_ref):
    ang = pos_ref[0].astype(jnp.float32) * inv_ref[...]
    lane = lax.broadcasted_iota(jnp.int32, ang.shape, 1)
    cos_ref[0] = jnp.cos(ang)
    sin_ref[0] = jnp.where(lane < HEAD_DIM // 2, -1.0, 1.0) * jnp.sin(ang)


def _rope_tables(pos, ts):
    nb, n = pos.shape
    inv = ROPE_THETA ** (-jnp.arange(0, HEAD_DIM, 2, dtype=jnp.float32) / HEAD_DIM)
    inv2 = jnp.concatenate([inv, inv]).reshape(1, HEAD_DIM)
    shp = jax.ShapeDtypeStruct((nb, n, HEAD_DIM), jnp.float32)
    spec = pl.BlockSpec((1, ts, HEAD_DIM), lambda b, i: (b, i, 0))
    return pl.pallas_call(
        _rope_table_kernel,
        out_shape=(shp, shp),
        grid=(nb, n // ts),
        in_specs=[pl.BlockSpec((1, ts, 1), lambda b, i: (b, i, 0)),
                  pl.BlockSpec((1, HEAD_DIM), lambda b, i: (0, 0))],
        out_specs=(spec, spec),
        compiler_params=_cparams(("arbitrary", "arbitrary")),
        name="rope_tables",
    )(pos.reshape(nb, n, 1), inv2)


def _apply_rope(x, cos, sin):
    return x * cos + pltpu.roll(x, HEAD_DIM // 2, axis=1) * sin


IN_TM = 512
IN_TN = 512


def _inproj_kernel(x_ref, g_ref, sc_ref, sh_ref, w_ref, ws_ref, cos_ref, sin_ref,
                   z_ref, zs_ref, u_ref):
    j = pl.program_id(2)

    @pl.when(j == 0)
    def _():
        u = _rms(x_ref[0]) * g_ref[...]
        u = u * (1.0 + sc_ref[0]) + sh_ref[0]
        ub = u.astype(jnp.bfloat16)
        u_ref[...] = ub
        zs_ref[0] = _dot(ub, ws_ref[...])

    acc = _dot(u_ref[...], w_ref[...])
    n_heads = IN_TN // HEAD_DIM
    rope_all = (j >= COL_QN // IN_TN) & (j < COL_KC // IN_TN)
    rope_half = (j == COL_KS // IN_TN) | (j == COL_KW // IN_TN)

    def roped(n_rope):
        cos = cos_ref[0]
        sin = sin_ref[0]
        parts = []
        for h in range(n_heads):
            a = acc[:, h * HEAD_DIM:(h + 1) * HEAD_DIM]
            parts.append(_apply_rope(a, cos, sin) if h < n_rope else a)
        return jnp.concatenate(parts, axis=1).astype(z_ref.dtype)

    @pl.when(rope_all)
    def _():
        z_ref[0] = roped(n_heads)

    @pl.when(rope_half)
    def _():
        z_ref[0] = roped(KV_W // HEAD_DIM)

    @pl.when(jnp.logical_not(rope_all | rope_half))
    def _():
        z_ref[0] = acc.astype(z_ref.dtype)


def _inproj(x, norm_w, sc, sh, w_main, w_small, cos, sin):
    nb, s, d = x.shape
    grid = (nb, s // IN_TM, Z_MAIN // IN_TN)
    return pl.pallas_call(
        _inproj_kernel,
        out_shape=(jax.ShapeDtypeStruct((nb, s, Z_MAIN), jnp.bfloat16),
                   jax.ShapeDtypeStruct((nb, s, Z_SMALL), jnp.float32)),
        grid=grid,
        in_specs=[
            pl.BlockSpec((1, IN_TM, d), lambda b, i, j: (b, i, 0)),
            pl.BlockSpec((1, d), lambda b, i, j: (0, 0)),
            pl.BlockSpec((1, 1, d), lambda b, i, j: (b, 0, 0)),
            pl.BlockSpec((1, 1, d), lambda b, i, j: (b, 0, 0)),
            pl.BlockSpec((d, IN_TN), lambda b, i, j: (0, j)),
            pl.BlockSpec((d, Z_SMALL), lambda b, i, j: (0, 0)),
            pl.BlockSpec((1, IN_TM, HEAD_DIM), lambda b, i, j: (b, i, 0)),
            pl.BlockSpec((1, IN_TM, HEAD_DIM), lambda b, i, j: (b, i, 0)),
        ],
        out_specs=(pl.BlockSpec((1, IN_TM, IN_TN), lambda b, i, j: (b, i, j)),
                   pl.BlockSpec((1, IN_TM, Z_SMALL), lambda b, i, j: (b, i, 0))),
        scratch_shapes=[pltpu.VMEM((IN_TM, d), jnp.bfloat16)],
        compiler_params=_cparams(("arbitrary", "arbitrary", "arbitrary")),
        name="inproj",
    )(x, norm_w.reshape(1, d), sc.reshape(nb, 1, d), sh.reshape(nb, 1, d),
      w_main, w_small, cos, sin)


DEC_T = 512


def _decay_kernel(zs_ref, bf_ref, cum_ref, carry_ref):
    @pl.when(pl.program_id(1) == 0)
    def _():
        carry_ref[...] = jnp.zeros_like(carry_ref)

    xt = zs_ref[0].T[0:FOX_HEADS, :] + bf_ref[...]
    lf = jnp.minimum(xt, 0.0) - jnp.log1p(jnp.exp(-jnp.abs(xt)))
    lane = lax.broadcasted_iota(jnp.int32, lf.shape, 1)
    sh = 1
    while sh < DEC_T:
        lf = lf + jnp.where(lane >= sh, pltpu.roll(lf, sh, axis=1), 0.0)
        sh *= 2
    cum = lf + carry_ref[:, 0:1]
    cum_ref[0] = cum
    carry_ref[...] = jnp.broadcast_to(cum[:, DEC_T - 1:DEC_T], carry_ref.shape)


def _decay(z_small, b_fgate):
    nb, s, _ = z_small.shape
    return pl.pallas_call(
        _decay_kernel,
        out_shape=jax.ShapeDtypeStruct((nb, FOX_HEADS, s), jnp.float32),
        grid=(nb, s // DEC_T),
        in_specs=[pl.BlockSpec((1, DEC_T, Z_SMALL), lambda b, i: (b, i, 0)),
                  pl.BlockSpec((FOX_HEADS, 1), lambda b, i: (0, 0))],
        out_specs=pl.BlockSpec((1, FOX_HEADS, DEC_T), lambda b, i: (b, 0, i)),
        scratch_shapes=[pltpu.VMEM((FOX_HEADS, 128), jnp.float32)],
        compiler_params=_cparams(("arbitrary", "arbitrary")),
        name="decay",
    )(z_small, b_fgate.reshape(FOX_HEADS, 1))


def _compress_kernel(x_ref, w1_ref, w2_ref, pos_ref, cos_ref, sin_ref, o_ref):
    kind = pl.program_id(1)
    half = CMP_STRIDE * HEAD_DIM
    x = x_ref[0, 0]
    nc = x.shape[0]
    p = _dot(x, w1_ref[0, 0:half, :])
    q = _dot(x, w1_ref[0, half:2 * half, :])
    posb = jnp.broadcast_to(pos_ref[...], (8, 2 * half)).astype(jnp.bfloat16)
    pterm = _dot(posb, w1_ref[0])[0:1, :]
    h = p + pltpu.roll(q, nc - 1, axis=0) + pterm
    a = jax.nn.gelu(h).astype(jnp.bfloat16)
    out = _dot(a, w2_ref[0])

    @pl.when(kind < NSA_KV_HEADS)
    def _():
        o_ref[0, 0] = _apply_rope(out, cos_ref[0], sin_ref[0]).astype(o_ref.dtype)

    @pl.when(kind >= NSA_KV_HEADS)
    def _():
        o_ref[0, 0] = out.astype(o_ref.dtype)


def _compress(xc, w1, w2, cmp_pos, cos_c, sin_c):
    nb, nk, nc, width = xc.shape
    return pl.pallas_call(
        _compress_kernel,
        out_shape=jax.ShapeDtypeStruct((nb, nk, nc, HEAD_DIM), jnp.bfloat16),
        grid=(nb, nk),
        in_specs=[
            pl.BlockSpec((1, 1, nc, width), lambda b, k: (b, k, 0, 0)),
            pl.BlockSpec((1, 2 * width, CMP_HIDDEN), lambda b, k: (k // NSA_KV_HEADS, 0, 0)),
            pl.BlockSpec((1, CMP_HIDDEN, HEAD_DIM), lambda b, k: (k // NSA_KV_HEADS, 0, 0)),
            pl.BlockSpec((1, 2 * width), lambda b, k: (0, 0)),
            pl.BlockSpec((1, nc, HEAD_DIM), lambda b, k: (b, 0, 0)),
            pl.BlockSpec((1, nc, HEAD_DIM), lambda b, k: (b, 0, 0)),
        ],
        out_specs=pl.BlockSpec((1, 1, nc, HEAD_DIM), lambda b, k: (b, k, 0, 0)),
        compiler_params=_cparams(("arbitrary", "arbitrary")),
        name="compress",
    )(xc, w1, w2, cmp_pos.reshape(1, 2 * width), cos_c, sin_c)


FOX_T = 512


def _scaled_q(q):
    return (q.astype(jnp.float32) * (SCALE * LOG2E)).astype(jnp.bfloat16)


def _fox_kernel(q_ref, k_ref, v_ref, cum_ref, o_ref):
    h = pl.program_id(1)
    i = pl.program_id(2)
    t = FOX_T
    q = _scaled_q(q_ref[0])

    def tile(j, carry, diag):
        m, l, acc = carry
        k0 = pl.multiple_of(j * t, t)
        k = k_ref[0, pl.ds(k0, t), :]
        v = v_ref[0, pl.ds(k0, t), :]
        ck = cum_ref[0, pl.ds(h, 1), pl.ds(k0, t)] * LOG2E
        s = _dot_nt(q, k) - ck
        if diag:
            r = lax.broadcasted_iota(jnp.int32, (t, t), 0)
            c = lax.broadcasted_iota(jnp.int32, (t, t), 1)
            s = jnp.where(c <= r, s, MASK_VALUE)
        m_new = jnp.maximum(m, jnp.max(s, axis=1, keepdims=True))
        alpha = jnp.exp2(m - m_new)
        p = jnp.exp2(s - m_new)
        l = alpha * l + jnp.sum(p, axis=1, keepdims=True)
        acc = alpha * acc + _dot(p.astype(jnp.bfloat16), v)
        return m_new, l, acc

    init = (jnp.full((t, 1), MASK_VALUE, jnp.float32),
            jnp.zeros((t, 1), jnp.float32),
            jnp.zeros((t, HEAD_DIM), jnp.float32))
    carry = lax.fori_loop(0, i, lambda j, c: tile(j, c, False), init)
    m, l, acc = tile(i, carry, True)
    o_ref[0] = (acc / l).astype(o_ref.dtype)


def _fox(z_main, cum):
    nb, s, _ = z_main.shape
    grid = (nb, FOX_HEADS, s // FOX_T)
    return pl.pallas_call(
        _fox_kernel,
        out_shape=jax.ShapeDtypeStruct((nb, s, FOX_W), jnp.bfloat16),
        grid=grid,
        in_specs=[
            pl.BlockSpec((1, FOX_T, HEAD_DIM), lambda b, h, i: (b, i, h)),
            pl.BlockSpec((1, s, HEAD_DIM), lambda b, h, i: (b, 0, FOX_HEADS + h)),
            pl.BlockSpec((1, s, HEAD_DIM), lambda b, h, i: (b, 0, 2 * FOX_HEADS + h)),
            pl.BlockSpec((1, FOX_HEADS, s), lambda b, h, i: (b, 0, 0)),
        ],
        out_specs=pl.BlockSpec((1, FOX_T, HEAD_DIM), lambda b, h, i: (b, i, h)),
        compiler_params=_cparams(("arbitrary", "arbitrary", "arbitrary")),
        name="fox",
    )(z_main, z_main, z_main, cum)


NSA_TQ = 256
NSA_ROWS = NSA_GROUP * NSA_TQ
NSEL_PAD = 128
SLC_SHIFT = SLC_LEN.bit_length() - 1
assert 1 << SLC_SHIFT == SLC_LEN
assert WINDOW % NSA_TQ == 0 and NSA_TQ % SLC_LEN == 0
SUBLANES = 8


def _block_ranks(score, n_slc):
    tq = score.shape[1]
    groups = [score[SUBLANES * v:SUBLANES * (v + 1)] for v in range(n_slc // SUBLANES)]
    ranks = [jnp.zeros((SUBLANES, tq), jnp.float32) for _ in groups]
    for m in range(n_slc):
        row = jnp.broadcast_to(score[m:m + 1, :], (SUBLANES, tq))
        for v, grp in enumerate(groups):
            lo = SUBLANES * v
            if lo > m:
                before = row >= grp
            elif lo + SUBLANES - 1 <= m:
                before = row > grp
            else:
                n_i = lo + lax.broadcasted_iota(jnp.int32, grp.shape, 0)
                before = (row > grp) | ((row == grp) & (n_i > m))
            ranks[v] = ranks[v] + jnp.where(before, 1.0, 0.0)
    return jnp.concatenate(ranks, axis=0)


def _nsa_kernel(q_ref, zs_ref, kc_ref, vc_ref, ks_ref, vs_ref, kw_ref, vw_ref, ovt_ref,
                o_ref, *, seq):
    g = pl.program_id(1)
    i = pl.program_id(2)
    tq = NSA_TQ
    q0 = i * tq
    qb = q_ref[0]
    q4 = _scaled_q(jnp.concatenate(
        [qb[:, r * HEAD_DIM:(r + 1) * HEAD_DIM] for r in range(NSA_GROUP)], axis=0))

    def row_query(shape):
        return lax.broadcasted_iota(jnp.int32, shape, 0) & (tq - 1)

    kc = kc_ref[0, 0]
    vc = vc_ref[0, 0]
    s_c = _dot_nt(q4, kc)
    cend = lax.broadcasted_iota(jnp.int32, s_c.shape, 1) * CMP_STRIDE + (CMP_LEN - 1)
    mask_c = cend <= q0 + row_query(s_c.shape)
    s_c = jnp.where(mask_c, s_c, MASK_VALUE)
    e_c = jnp.where(mask_c, jnp.exp2(s_c - jnp.max(s_c, axis=1, keepdims=True)), 0.0)
    p_c = e_c / jnp.maximum(jnp.sum(e_c, axis=1, keepdims=True), 1e-30)
    o_c = _dot(p_c.astype(jnp.bfloat16), vc)

    psum = p_c[0:tq]
    for r in range(1, NSA_GROUP):
        psum = psum + p_c[r * tq:(r + 1) * tq]
    p_hi = psum.astype(jnp.bfloat16)
    p_lo = (psum - p_hi.astype(jnp.float32)).astype(jnp.bfloat16)
    ovt = ovt_ref[...]
    imp_t = _dot_nt(ovt, p_hi) + _dot_nt(ovt, p_lo)
    n_slc = seq // SLC_LEN
    blk = lax.broadcasted_iota(jnp.int32, imp_t.shape, 0)
    cur = (q0 + lax.broadcasted_iota(jnp.int32, imp_t.shape, 1)) >> SLC_SHIFT
    forced = (blk == 0) | (blk == cur) | (blk == cur - 1)
    score = jnp.where(forced, SEL_BONUS, imp_t)
    score = jnp.where(blk <= cur, score, -SEL_BONUS)
    rank = _block_ranks(score, n_slc)
    bias_t = jnp.where(rank < min(SLC_TOPK, n_slc), 0.0, MASK_VALUE)
    bias_t = jnp.concatenate(
        [bias_t, jnp.zeros((NSEL_PAD - n_slc, tq), jnp.float32)], axis=0)
    bias = bias_t.T.astype(jnp.bfloat16)
    qa = jnp.concatenate([q4, jnp.concatenate([bias] * NSA_GROUP, axis=0)], axis=1)

    def sel_tile(j, carry, diag):
        m, l, acc = carry
        k0 = pl.multiple_of(j * tq, tq)
        k = ks_ref[0, pl.ds(k0, tq), :]
        v = vs_ref[0, pl.ds(k0, tq), :]
        tok = k0 + lax.broadcasted_iota(jnp.int32, (tq, NSEL_PAD), 0)
        n_i = lax.broadcasted_iota(jnp.int32, (tq, NSEL_PAD), 1)
        onehot = jnp.where(n_i == (tok >> SLC_SHIFT), 1.0, 0.0).astype(jnp.bfloat16)
        s = _dot_nt(qa, jnp.concatenate([k, onehot], axis=1))
        if diag:
            c = lax.broadcasted_iota(jnp.int32, s.shape, 1)
            s = jnp.where(c <= row_query(s.shape), s, MASK_VALUE)
        m_new = jnp.maximum(m, jnp.max(s, axis=1, keepdims=True))
        alpha = jnp.exp2(m - m_new)
        p = jnp.exp2(s - m_new)
        l = alpha * l + jnp.sum(p, axis=1, keepdims=True)
        acc = alpha * acc + _dot(p.astype(jnp.bfloat16), v)
        return m_new, l, acc

    init = (jnp.full((NSA_ROWS, 1), MASK_VALUE, jnp.float32),
            jnp.zeros((NSA_ROWS, 1), jnp.float32),
            jnp.zeros((NSA_ROWS, HEAD_DIM), jnp.float32))
    carry = lax.fori_loop(0, i, lambda j, c: sel_tile(j, c, False), init)
    _, l_s, acc_s = sel_tile(i, carry, True)
    o_s = acc_s / l_s

    n_w = WINDOW // tq + 1
    s_w, v_w = [], []
    for t in range(n_w):
        start = q0 - WINDOW + t * tq
        st = pl.multiple_of(jnp.maximum(start, 0), tq)
        s = _dot_nt(q4, kw_ref[0, pl.ds(st, tq), :])
        v_w.append(vw_ref[0, pl.ds(st, tq), :])
        c = lax.broadcasted_iota(jnp.int32, s.shape, 1)
        if t == 0:
            s = jnp.where(c > row_query(s.shape), s, MASK_VALUE)
        if t == n_w - 1:
            s = jnp.where(c <= row_query(s.shape), s, MASK_VALUE)
        else:
            s = jnp.where(start >= 0, s, MASK_VALUE)
        s_w.append(s)
    m_w = s_w[0].max(axis=1, keepdims=True)
    for s in s_w[1:]:
        m_w = jnp.maximum(m_w, s.max(axis=1, keepdims=True))
    l_w = jnp.zeros((NSA_ROWS, 1), jnp.float32)
    acc_w = jnp.zeros((NSA_ROWS, HEAD_DIM), jnp.float32)
    for s, v in zip(s_w, v_w):
        p = jnp.exp2(s - m_w)
        l_w = l_w + jnp.sum(p, axis=1, keepdims=True)
        acc_w = acc_w + _dot(p.astype(jnp.bfloat16), v)
    o_w = acc_w / l_w

    gates = jax.nn.sigmoid(zs_ref[0])
    lane = lax.broadcasted_iota(jnp.int32, gates.shape, 1)
    outs = []
    for r in range(NSA_GROUP):
        rows = slice(r * tq, (r + 1) * tq)
        o_r = jnp.zeros((tq, HEAD_DIM), jnp.float32)
        for br, o_b in enumerate((o_c, o_s, o_w)):
            col = FOX_HEADS + 3 * (g * NSA_GROUP + r) + br
            gcol = jnp.sum(jnp.where(lane == col, gates, 0.0), axis=1, keepdims=True)
            o_r = o_r + gcol * o_b[rows]
        outs.append(o_r)
    o_ref[0] = jnp.concatenate(outs, axis=1).astype(o_ref.dtype)


def _nsa(z_main, z_small, kvc, ovt):
    nb, s, _ = z_main.shape
    nc = kvc.shape[2]
    gw = NSA_GROUP * HEAD_DIM
    full = lambda col0: pl.BlockSpec(
        (1, s, HEAD_DIM), lambda b, g, i, c=col0 // HEAD_DIM: (b, 0, c + g))
    return pl.pallas_call(
        functools.partial(_nsa_kernel, seq=s),
        out_shape=jax.ShapeDtypeStruct((nb, s, NSA_W), jnp.bfloat16),
        grid=(nb, NSA_KV_HEADS, s // NSA_TQ),
        in_specs=[
            pl.BlockSpec((1, NSA_TQ, gw), lambda b, g, i: (b, i, COL_QN // gw + g)),
            pl.BlockSpec((1, NSA_TQ, Z_SMALL), lambda b, g, i: (b, i, 0)),
            pl.BlockSpec((1, 1, nc, HEAD_DIM), lambda b, g, i: (b, g, 0, 0)),
            pl.BlockSpec((1, 1, nc, HEAD_DIM), lambda b, g, i: (b, NSA_KV_HEADS + g, 0, 0)),
            full(COL_KS), full(COL_KS + KV_W), full(COL_KW), full(COL_KW + KV_W),
            pl.BlockSpec(ovt.shape, lambda b, g, i: (0, 0)),
        ],
        out_specs=pl.BlockSpec((1, NSA_TQ, gw), lambda b, g, i: (b, i, g)),
        compiler_params=_cparams(("arbitrary", "arbitrary", "arbitrary")),
        name="nsa",
    )(z_main, z_small, kvc, kvc, z_main, z_main, z_main, z_main, ovt)


OUT_TM = 512
OUT_TN = 512


def _outproj_kernel(of_ref, on_ref, bf_ref, bn_ref, w_ref, x_ref, g_ref, h_ref, y_ref):
    @pl.when(pl.program_id(2) == 0)
    def _():
        yf = _rms(of_ref[0].astype(jnp.float32)) * bf_ref[...]
        yn = _rms(on_ref[0].astype(jnp.float32)) * bn_ref[...]
        y_ref[:, 0:FOX_W] = yf.astype(jnp.bfloat16)
        y_ref[:, FOX_W:FOX_W + NSA_W] = yn.astype(jnp.bfloat16)

    h_ref[0] = x_ref[0] + g_ref[0] * _dot(y_ref[...], w_ref[...])


def _outproj(o_f, o_n, beta_f, beta_n, w_out, x, gate):
    nb, s, d = x.shape
    return pl.pallas_call(
        _outproj_kernel,
        out_shape=jax.ShapeDtypeStruct((nb, s, d), jnp.float32),
        grid=(nb, s // OUT_TM, d // OUT_TN),
        in_specs=[
            pl.BlockSpec((1, OUT_TM, FOX_W), lambda b, i, j: (b, i, 0)),
            pl.BlockSpec((1, OUT_TM, NSA_W), lambda b, i, j: (b, i, 0)),
            pl.BlockSpec((1, FOX_W), lambda b, i, j: (0, 0)),
            pl.BlockSpec((1, NSA_W), lambda b, i, j: (0, 0)),
            pl.BlockSpec((FOX_W + NSA_W, OUT_TN), lambda b, i, j: (0, j)),
            pl.BlockSpec((1, OUT_TM, OUT_TN), lambda b, i, j: (b, i, j)),
            pl.BlockSpec((1, 1, OUT_TN), lambda b, i, j: (b, 0, j)),
        ],
        out_specs=pl.BlockSpec((1, OUT_TM, OUT_TN), lambda b, i, j: (b, i, j)),
        scratch_shapes=[pltpu.VMEM((OUT_TM, FOX_W + NSA_W), jnp.bfloat16)],
        compiler_params=_cparams(("arbitrary", "arbitrary", "arbitrary")),
        name="outproj",
    )(o_f, o_n, beta_f.reshape(1, FOX_W), beta_n.reshape(1, NSA_W), w_out, x,
      gate.reshape(nb, 1, d))


FFN_TM = 512
FFN_TF = 512


def _ffn_kernel(h_ref, nw_ref, sc_ref, sh_ref, g_ref, fw_ref, wg_ref, wu_ref, wd_ref,
                o_ref, u_ref, acc_ref, *, final):
    j = pl.program_id(2)

    @pl.when(j == 0)
    def _():
        u = _rms(h_ref[0]) * nw_ref[...]
        u = u * (1.0 + sc_ref[0]) + sh_ref[0]
        u_ref[...] = u.astype(jnp.bfloat16)
        acc_ref[...] = jnp.zeros_like(acc_ref)

    u = u_ref[...]
    a = _dot(u, wg_ref[...])
    b = _dot(u, wu_ref[...])
    mid = (a * jax.nn.sigmoid(a) * b).astype(jnp.bfloat16)
    acc_ref[...] += _dot(mid, wd_ref[...])

    @pl.when(j == pl.num_programs(2) - 1)
    def _():
        h2 = h_ref[0] + g_ref[0] * acc_ref[...]
        o_ref[0] = _rms(h2) * fw_ref[...] if final else h2


def _ffn(h1, norm_w, sc, sh, gate, final_w, w_gate, w_up, w_down, final):
    nb, s, d = h1.shape
    f = w_gate.shape[1]
    vec = pl.BlockSpec((1, d), lambda b, i, j: (0, 0))
    bvec = pl.BlockSpec((1, 1, d), lambda b, i, j: (b, 0, 0))
    return pl.pallas_call(
        functools.partial(_ffn_kernel, final=final),
        out_shape=jax.ShapeDtypeStruct((nb, s, d), jnp.float32),
        grid=(nb, s // FFN_TM, f // FFN_TF),
        in_specs=[
            pl.BlockSpec((1, FFN_TM, d), lambda b, i, j: (b, i, 0)),
            vec, bvec, bvec, bvec, vec,
            pl.BlockSpec((d, FFN_TF), lambda b, i, j: (0, j)),
            pl.BlockSpec((d, FFN_TF), lambda b, i, j: (0, j)),
            pl.BlockSpec((FFN_TF, d), lambda b, i, j: (j, 0)),
        ],
        out_specs=pl.BlockSpec((1, FFN_TM, d), lambda b, i, j: (b, i, 0)),
        scratch_shapes=[pltpu.VMEM((FFN_TM, d), jnp.bfloat16),
                        pltpu.VMEM((FFN_TM, d), jnp.float32)],
        compiler_params=_cparams(("arbitrary", "arbitrary", "arbitrary")),
        name="ffn",
    )(h1, norm_w.reshape(1, d), sc.reshape(nb, 1, d), sh.reshape(nb, 1, d),
      gate.reshape(nb, 1, d), final_w.reshape(1, d), w_gate, w_up, w_down)


def _split_w_in(w_in):
    sizes = ([FOX_W] * 3 + [FOX_HEADS] + [NSA_W] + [KV_W] * 6 + [3 * NSA_HEADS])
    offs = [0]
    for sz in sizes:
        offs.append(offs[-1] + sz)
    piece = lambda n: w_in[:, offs[n]:offs[n + 1]]
    main = jnp.concatenate([piece(n) for n in (0, 1, 2, 4, 5, 6, 7, 8, 9, 10)], axis=1)
    small = jnp.concatenate(
        [piece(3), piece(11),
         jnp.zeros((w_in.shape[0], Z_SMALL - FOX_HEADS - 3 * NSA_HEADS), w_in.dtype)], axis=1)
    return main.astype(jnp.bfloat16), small.astype(jnp.bfloat16)


def _overlap_t(n_cmp_pad, n_slc):
    cs = jnp.arange(n_cmp_pad) * CMP_STRIDE
    ss = jnp.arange(n_slc) * SLC_LEN
    ov = (jnp.minimum(cs[None, :] + CMP_LEN, ss[:, None] + SLC_LEN)
          - jnp.maximum(cs[None, :], ss[:, None]))
    ov = jnp.clip(ov, 0).astype(jnp.float32) / CMP_STRIDE
    return ov.astype(jnp.bfloat16)


def kernel(x, c, positions, w_ada, b_ada, norm_attn, norm_ffn, w_in, b_fgate, cmp_pos,
           w_kc1, w_kc2, w_vc1, w_vc2, beta_fox, beta_nsa, w_out, w_gate, w_up, w_down,
           final_norm):
    nb, s, d = x.shape
    depth = w_ada.shape[0]
    n_cmp_pad = s // CMP_STRIDE
    cmp_end = jnp.arange(n_cmp_pad) * CMP_STRIDE + (CMP_LEN - 1)
    pos_c = jnp.take(positions, jnp.minimum(cmp_end, s - 1), axis=1)
    cos, sin = _rope_tables(positions, 512)
    cos_c, sin_c = _rope_tables(pos_c, n_cmp_pad)
    ovt = _overlap_t(n_cmp_pad, s // SLC_LEN)

    h = x
    for l in range(depth):
        mod = _ada(c, w_ada[l], b_ada[l])
        sh1, sc1, g1, sh2, sc2, g2 = jnp.split(mod, 6, axis=-1)
        w_main, w_small = _split_w_in(w_in[l])
        z_main, z_small = _inproj(h, norm_attn[l], sc1, sh1, w_main, w_small, cos, sin)
        cum = _decay(z_small, b_fgate[l])

        xc = z_main[:, :, COL_KC:COL_KC + 2 * KV_W]
        xc = xc.reshape(nb, n_cmp_pad, CMP_STRIDE, 2 * NSA_KV_HEADS, HEAD_DIM)
        xc = xc.transpose(0, 3, 1, 2, 4).reshape(nb, 2 * NSA_KV_HEADS, n_cmp_pad,
                                                 CMP_STRIDE * HEAD_DIM)
        w1 = jnp.stack([w_kc1[l], w_vc1[l]]).astype(jnp.bfloat16)
        w2 = jnp.stack([w_kc2[l], w_vc2[l]]).astype(jnp.bfloat16)
        kvc = _compress(xc, w1, w2, cmp_pos[l], cos_c, sin_c)

        o_f = _fox(z_main, cum)
        o_n = _nsa(z_main, z_small, kvc, ovt)
        h1 = _outproj(o_f, o_n, beta_fox[l], beta_nsa[l], w_out[l].astype(jnp.bfloat16), h, g1)
        h = _ffn(h1, norm_ffn[l], sc2, sh2, g2, final_norm, w_gate[l].astype(jnp.bfloat16),
                 w_up[l].astype(jnp.bfloat16), w_down[l].astype(jnp.bfloat16),
                 final=(l == depth - 1))
    return h
```

```python
import functools

import jax
import jax.numpy as jnp
from jax import lax
from jax.experimental import pallas as pl
from jax.experimental.pallas import tpu as pltpu

D_MODEL = 2048
HEAD_DIM = 128
FOX_HEADS = 8
NSA_HEADS = 8
NSA_KV_HEADS = 2
NSA_GROUP = NSA_HEADS // NSA_KV_HEADS
CMP_LEN = 32
CMP_STRIDE = 16
CMP_HIDDEN = 2 * HEAD_DIM
SLC_LEN = 64
SLC_TOPK = 16
WINDOW = 512
D_FF = 5632
ROPE_THETA = 10000.0
NORM_EPS = 1e-6
MASK_VALUE = -1e30
SEL_BONUS = 1e6

FOX_W = FOX_HEADS * HEAD_DIM
NSA_W = NSA_HEADS * HEAD_DIM
KV_W = NSA_KV_HEADS * HEAD_DIM
Z_MAIN = 3 * FOX_W + NSA_W + 6 * KV_W
Z_SMALL = 128
COL_QN = 3 * FOX_W
COL_KC = COL_QN + NSA_W
COL_KS = COL_KC + 2 * KV_W
COL_KW = COL_KS + 2 * KV_W

VMEM_LIMIT = 48 * 1024 * 1024
FFN_VMEM_LIMIT = 56 * 1024 * 1024
SCALE = HEAD_DIM ** -0.5
LOG2E = 1.4426950408889634


def _cparams(sem):
    return pltpu.CompilerParams(dimension_semantics=sem, vmem_limit_bytes=VMEM_LIMIT)


def _dot(a, b):
    return jnp.dot(a, b, preferred_element_type=jnp.float32)


def _dot_nt(a, b):
    return lax.dot_general(a, b, (((1,), (1,)), ((), ())), preferred_element_type=jnp.float32)


def _rms(x):
    return x * lax.rsqrt(jnp.mean(x * x, axis=-1, keepdims=True) + NORM_EPS)


ADA_TN = 1024
ADA_KC = 256


def _ada_kernel(ct_ref, w_ref, b_ref, o_ref, *, nb):
    tn = w_ref.shape[1]
    ct = ct_ref[...]
    st = ct * jax.nn.sigmoid(ct)

    accs = [jnp.zeros((8, tn), jnp.float32) for _ in range(nb)]
    for kc in range(D_MODEL // ADA_KC):
        w = w_ref[kc * ADA_KC:(kc + 1) * ADA_KC, :]
        for b in range(nb):
            s = st[kc * ADA_KC:(kc + 1) * ADA_KC, b:b + 1]
            accs[b] = accs[b] + jnp.sum((w * s).reshape(ADA_KC // 8, 8, tn), axis=0)
    for b in range(nb):
        o_ref[b:b + 1, :] = jnp.sum(accs[b], axis=0, keepdims=True) + b_ref[...]


def _ada(c, w_ada, b_ada):
    nb = c.shape[0]
    n = w_ada.shape[1]
    return pl.pallas_call(
        functools.partial(_ada_kernel, nb=nb),
        out_shape=jax.ShapeDtypeStruct((nb, n), jnp.float32),
        grid=(n // ADA_TN,),
        in_specs=[
            pl.BlockSpec((D_MODEL, nb), lambda j: (0, 0)),
            pl.BlockSpec((D_MODEL, ADA_TN), lambda j: (0, j)),
            pl.BlockSpec((1, ADA_TN), lambda j: (0, j)),
        ],
        out_specs=pl.BlockSpec((nb, ADA_TN), lambda j: (0, j)),
        compiler_params=_cparams(("arbitrary",)),
        name="ada",
    )(c.T, w_ada, b_ada.reshape(1, n))


def _rope_table_kernel(pos_ref, inv_ref, cos_ref, sin_ref):
    ang = pos_ref[0].astype(jnp.float32) * inv_ref[...]
    lane = lax.broadcasted_iota(jnp.int32, ang.shape, 1)
    cos_ref[0] = jnp.cos(ang)
    sin_ref[0] = jnp.where(lane < HEAD_DIM // 2, -1.0, 1.0) * jnp.sin(ang)


def _rope_tables(pos, ts):
    nb, n = pos.shape
    inv = ROPE_THETA ** (-jnp.arange(0, HEAD_DIM, 2, dtype=jnp.float32) / HEAD_DIM)
    inv2 = jnp.concatenate([inv, inv]).reshape(1, HEAD_DIM)
    shp = jax.ShapeDtypeStruct((nb, n, HEAD_DIM), jnp.float32)
    spec = pl.BlockSpec((1, ts, HEAD_DIM), lambda b, i: (b, i, 0))
    return pl.pallas_call(
        _rope_table_kernel,
        out_shape=(shp, shp),
        grid=(nb, n // ts),
        in_specs=[pl.BlockSpec((1, ts, 1), lambda b, i: (b, i, 0)),
                  pl.BlockSpec((1, HEAD_DIM), lambda b, i: (0, 0))],
        out_specs=(spec, spec),
        compiler_params=_cparams(("arbitrary", "arbitrary")),
        name="rope_tables",
    )(pos.reshape(nb, n, 1), inv2)


def _apply_rope(x, cos, sin):
    return x * cos + pltpu.roll(x, HEAD_DIM // 2, axis=1) * sin


IN_TM = 1024
IN_TN = 512


def _inproj_kernel(x_ref, g_ref, sc_ref, sh_ref, w_ref, ws_ref, cos_ref, sin_ref,
                   z_ref, zs_ref, u_ref):
    j = pl.program_id(2)

    @pl.when(j == 0)
    def _():
        u = _rms(x_ref[0]) * g_ref[...]
        u = u * (1.0 + sc_ref[0]) + sh_ref[0]
        ub = u.astype(jnp.bfloat16)
        u_ref[...] = ub
        zs_ref[0] = _dot(ub, ws_ref[...])

    acc = _dot(u_ref[...], w_ref[...])
    n_heads = IN_TN // HEAD_DIM
    rope_all = (j >= COL_QN // IN_TN) & (j < COL_KC // IN_TN)
    rope_half = (j == COL_KS // IN_TN) | (j == COL_KW // IN_TN)

    def roped(n_rope):
        cos = cos_ref[0]
        sin = sin_ref[0]
        parts = []
        for h in range(n_heads):
            a = acc[:, h * HEAD_DIM:(h + 1) * HEAD_DIM]
            parts.append(_apply_rope(a, cos, sin) if h < n_rope else a)
        return jnp.concatenate(parts, axis=1).astype(z_ref.dtype)

    @pl.when(rope_all)
    def _():
        z_ref[0] = roped(n_heads)

    @pl.when(rope_half)
    def _():
        z_ref[0] = roped(KV_W // HEAD_DIM)

    @pl.when(jnp.logical_not(rope_all | rope_half))
    def _():
        z_ref[0] = acc.astype(z_ref.dtype)


def _inproj(x, norm_w, sc, sh, w_main, w_small, cos, sin):
    nb, s, d = x.shape
    grid = (nb, s // IN_TM, Z_MAIN // IN_TN)
    return pl.pallas_call(
        _inproj_kernel,
        out_shape=(jax.ShapeDtypeStruct((nb, s, Z_MAIN), jnp.bfloat16),
                   jax.ShapeDtypeStruct((nb, s, Z_SMALL), jnp.float32)),
        grid=grid,
        in_specs=[
            pl.BlockSpec((1, IN_TM, d), lambda b, i, j: (b, i, 0)),
            pl.BlockSpec((1, d), lambda b, i, j: (0, 0)),
            pl.BlockSpec((1, 1, d), lambda b, i, j: (b, 0, 0)),
            pl.BlockSpec((1, 1, d), lambda b, i, j: (b, 0, 0)),
            pl.BlockSpec((d, IN_TN), lambda b, i, j: (0, j)),
            pl.BlockSpec((d, Z_SMALL), lambda b, i, j: (0, 0)),
            pl.BlockSpec((1, IN_TM, HEAD_DIM), lambda b, i, j: (b, i, 0)),
            pl.BlockSpec((1, IN_TM, HEAD_DIM), lambda b, i, j: (b, i, 0)),
        ],
        out_specs=(pl.BlockSpec((1, IN_TM, IN_TN), lambda b, i, j: (b, i, j)),
                   pl.BlockSpec((1, IN_TM, Z_SMALL), lambda b, i, j: (b, i, 0))),
        scratch_shapes=[pltpu.VMEM((IN_TM, d), jnp.bfloat16)],
        compiler_params=_cparams(("arbitrary", "arbitrary", "arbitrary")),
        name="inproj",
    )(x, norm_w.reshape(1, d), sc.reshape(nb, 1, d), sh.reshape(nb, 1, d),
      w_main, w_small, cos, sin)


DEC_T = 512


def _decay_kernel(zs_ref, bf_ref, cum_ref, carry_ref):
    @pl.when(pl.program_id(1) == 0)
    def _():
        carry_ref[...] = jnp.zeros_like(carry_ref)

    xt = zs_ref[0].T[0:FOX_HEADS, :] + bf_ref[...]
    lf = jnp.minimum(xt, 0.0) - jnp.log1p(jnp.exp(-jnp.abs(xt)))
    lane = lax.broadcasted_iota(jnp.int32, lf.shape, 1)
    sh = 1
    while sh < DEC_T:
        lf = lf + jnp.where(lane >= sh, pltpu.roll(lf, sh, axis=1), 0.0)
        sh *= 2
    cum = lf + carry_ref[:, 0:1]
    cum_ref[0] = cum
    carry_ref[...] = jnp.broadcast_to(cum[:, DEC_T - 1:DEC_T], carry_ref.shape)


def _decay(z_small, b_fgate):
    nb, s, _ = z_small.shape
    return pl.pallas_call(
        _decay_kernel,
        out_shape=jax.ShapeDtypeStruct((nb, FOX_HEADS, s), jnp.float32),
        grid=(nb, s // DEC_T),
        in_specs=[pl.BlockSpec((1, DEC_T, Z_SMALL), lambda b, i: (b, i, 0)),
                  pl.BlockSpec((FOX_HEADS, 1), lambda b, i: (0, 0))],
        out_specs=pl.BlockSpec((1, FOX_HEADS, DEC_T), lambda b, i: (b, 0, i)),
        scratch_shapes=[pltpu.VMEM((FOX_HEADS, 128), jnp.float32)],
        compiler_params=_cparams(("arbitrary", "arbitrary")),
        name="decay",
    )(z_small, b_fgate.reshape(FOX_HEADS, 1))


def _compress_kernel(x_ref, w1_ref, w2_ref, pos_ref, cos_ref, sin_ref, o_ref):
    kind = pl.program_id(1)
    half = CMP_STRIDE * HEAD_DIM
    x = x_ref[0, 0]
    nc = x.shape[0]
    p = _dot(x, w1_ref[0, 0:half, :])
    q = _dot(x, w1_ref[0, half:2 * half, :])
    posb = jnp.broadcast_to(pos_ref[...], (8, 2 * half)).astype(jnp.bfloat16)
    pterm = _dot(posb, w1_ref[0])[0:1, :]
    h = p + pltpu.roll(q, nc - 1, axis=0) + pterm
    a = jax.nn.gelu(h).astype(jnp.bfloat16)
    out = _dot(a, w2_ref[0])

    @pl.when(kind < NSA_KV_HEADS)
    def _():
        o_ref[0, 0] = _apply_rope(out, cos_ref[0], sin_ref[0]).astype(o_ref.dtype)

    @pl.when(kind >= NSA_KV_HEADS)
    def _():
        o_ref[0, 0] = out.astype(o_ref.dtype)


def _compress(xc, w1, w2, cmp_pos, cos_c, sin_c):
    nb, nk, nc, width = xc.shape
    return pl.pallas_call(
        _compress_kernel,
        out_shape=jax.ShapeDtypeStruct((nb, nk, nc, HEAD_DIM), jnp.bfloat16),
        grid=(nb, nk),
        in_specs=[
            pl.BlockSpec((1, 1, nc, width), lambda b, k: (b, k, 0, 0)),
            pl.BlockSpec((1, 2 * width, CMP_HIDDEN), lambda b, k: (k // NSA_KV_HEADS, 0, 0)),
            pl.BlockSpec((1, CMP_HIDDEN, HEAD_DIM), lambda b, k: (k // NSA_KV_HEADS, 0, 0)),
            pl.BlockSpec((1, 2 * width), lambda b, k: (0, 0)),
            pl.BlockSpec((1, nc, HEAD_DIM), lambda b, k: (b, 0, 0)),
            pl.BlockSpec((1, nc, HEAD_DIM), lambda b, k: (b, 0, 0)),
        ],
        out_specs=pl.BlockSpec((1, 1, nc, HEAD_DIM), lambda b, k: (b, k, 0, 0)),
        compiler_params=_cparams(("arbitrary", "arbitrary")),
        name="compress",
    )(xc, w1, w2, cmp_pos.reshape(1, 2 * width), cos_c, sin_c)


FOX_T = 512


def _scaled_q(q):
    return (q.astype(jnp.float32) * (SCALE * LOG2E)).astype(jnp.bfloat16)


def _fox_kernel(q_ref, k_ref, v_ref, cum_ref, o_ref):
    h = pl.program_id(1)
    i = pl.program_id(2)
    t = FOX_T
    q = _scaled_q(q_ref[0])

    def tile(j, carry, diag):
        m, l, acc = carry
        k0 = pl.multiple_of(j * t, t)
        k = k_ref[0, pl.ds(k0, t), :]
        v = v_ref[0, pl.ds(k0, t), :]
        ck = cum_ref[0, pl.ds(h, 1), pl.ds(k0, t)] * LOG2E
        s = _dot_nt(q, k) - ck
        if diag:
            r = lax.broadcasted_iota(jnp.int32, (t, t), 0)
            c = lax.broadcasted_iota(jnp.int32, (t, t), 1)
            s = jnp.where(c <= r, s, MASK_VALUE)
        m_new = jnp.maximum(m, jnp.max(s, axis=1, keepdims=True))
        alpha = jnp.exp2(m - m_new)
        p = jnp.exp2(s - m_new)
        l = alpha * l + jnp.sum(p, axis=1, keepdims=True)
        acc = alpha * acc + _dot(p.astype(jnp.bfloat16), v)
        return m_new, l, acc

    init = (jnp.full((t, 1), MASK_VALUE, jnp.float32),
            jnp.zeros((t, 1), jnp.float32),
            jnp.zeros((t, HEAD_DIM), jnp.float32))
    carry = lax.fori_loop(0, i, lambda j, c: tile(j, c, False), init)
    m, l, acc = tile(i, carry, True)
    o_ref[0] = (acc / l).astype(o_ref.dtype)


def _fox(z_main, cum):
    nb, s, _ = z_main.shape
    grid = (nb, FOX_HEADS, s // FOX_T)
    return pl.pallas_call(
        _fox_kernel,
        out_shape=jax.ShapeDtypeStruct((nb, s, FOX_W), jnp.bfloat16),
        grid=grid,
        in_specs=[
            pl.BlockSpec((1, FOX_T, HEAD_DIM), lambda b, h, i: (b, i, h)),
            pl.BlockSpec((1, s, HEAD_DIM), lambda b, h, i: (b, 0, FOX_HEADS + h)),
            pl.BlockSpec((1, s, HEAD_DIM), lambda b, h, i: (b, 0, 2 * FOX_HEADS + h)),
            pl.BlockSpec((1, FOX_HEADS, s), lambda b, h, i: (b, 0, 0)),
        ],
        out_specs=pl.BlockSpec((1, FOX_T, HEAD_DIM), lambda b, h, i: (b, i, h)),
        compiler_params=_cparams(("arbitrary", "arbitrary", "arbitrary")),
        name="fox",
    )(z_main, z_main, z_main, cum)


NSA_TQ = 256
NSA_ROWS = NSA_GROUP * NSA_TQ
NSEL_PAD = 128
SLC_SHIFT = SLC_LEN.bit_length() - 1
assert 1 << SLC_SHIFT == SLC_LEN
assert WINDOW % NSA_TQ == 0 and NSA_TQ % SLC_LEN == 0
SUBLANES = 8


def _block_ranks(score, n_slc):
    tq = score.shape[1]
    groups = [score[SUBLANES * v:SUBLANES * (v + 1)] for v in range(n_slc // SUBLANES)]
    ranks = [jnp.zeros((SUBLANES, tq), jnp.float32) for _ in groups]
    for m in range(n_slc):
        row = jnp.broadcast_to(score[m:m + 1, :], (SUBLANES, tq))
        for v, grp in enumerate(groups):
            lo = SUBLANES * v
            if lo > m:
                before = row >= grp
            elif lo + SUBLANES - 1 <= m:
                before = row > grp
            else:
                n_i = lo + lax.broadcasted_iota(jnp.int32, grp.shape, 0)
                before = (row > grp) | ((row == grp) & (n_i > m))
            ranks[v] = ranks[v] + jnp.where(before, 1.0, 0.0)
    return jnp.concatenate(ranks, axis=0)


def _nsa_kernel(q_ref, zs_ref, kc_ref, vc_ref, ks_ref, vs_ref, kw_ref, vw_ref, ovt_ref,
                o_ref, *, seq):
    g = pl.program_id(1)
    i = pl.program_id(2)
    tq = NSA_TQ
    q0 = i * tq
    qb = q_ref[0]
    q4 = _scaled_q(jnp.concatenate(
        [qb[:, r * HEAD_DIM:(r + 1) * HEAD_DIM] for r in range(NSA_GROUP)], axis=0))

    def row_query(shape):
        return lax.broadcasted_iota(jnp.int32, shape, 0) & (tq - 1)

    kc = kc_ref[0, 0]
    vc = vc_ref[0, 0]
    s_c = _dot_nt(q4, kc)
    cend = lax.broadcasted_iota(jnp.int32, s_c.shape, 1) * CMP_STRIDE + (CMP_LEN - 1)
    mask_c = cend <= q0 + row_query(s_c.shape)
    s_c = jnp.where(mask_c, s_c, MASK_VALUE)
    e_c = jnp.where(mask_c, jnp.exp2(s_c - jnp.max(s_c, axis=1, keepdims=True)), 0.0)
    p_c = e_c / jnp.maximum(jnp.sum(e_c, axis=1, keepdims=True), 1e-30)
    o_c = _dot(p_c.astype(jnp.bfloat16), vc)

    psum = p_c[0:tq]
    for r in range(1, NSA_GROUP):
        psum = psum + p_c[r * tq:(r + 1) * tq]
    p_hi = psum.astype(jnp.bfloat16)
    p_lo = (psum - p_hi.astype(jnp.float32)).astype(jnp.bfloat16)
    ovt = ovt_ref[...]
    imp_t = _dot_nt(ovt, p_hi) + _dot_nt(ovt, p_lo)
    n_slc = seq // SLC_LEN
    blk = lax.broadcasted_iota(jnp.int32, imp_t.shape, 0)
    cur = (q0 + lax.broadcasted_iota(jnp.int32, imp_t.shape, 1)) >> SLC_SHIFT
    forced = (blk == 0) | (blk == cur) | (blk == cur - 1)
    score = jnp.where(forced, SEL_BONUS, imp_t)
    score = jnp.where(blk <= cur, score, -SEL_BONUS)
    rank = _block_ranks(score, n_slc)
    bias_t = jnp.where(rank < min(SLC_TOPK, n_slc), 0.0, MASK_VALUE)
    bias_t = jnp.concatenate(
        [bias_t, jnp.zeros((NSEL_PAD - n_slc, tq), jnp.float32)], axis=0)
    bias = bias_t.T.astype(jnp.bfloat16)
    qa = jnp.concatenate([q4, jnp.concatenate([bias] * NSA_GROUP, axis=0)], axis=1)

    def sel_tile(j, carry, diag):
        m, l, acc = carry
        k0 = pl.multiple_of(j * tq, tq)
        k = ks_ref[0, pl.ds(k0, tq), :]
        v = vs_ref[0, pl.ds(k0, tq), :]
        tok = k0 + lax.broadcasted_iota(jnp.int32, (tq, NSEL_PAD), 0)
        n_i = lax.broadcasted_iota(jnp.int32, (tq, NSEL_PAD), 1)
        onehot = jnp.where(n_i == (tok >> SLC_SHIFT), 1.0, 0.0).astype(jnp.bfloat16)
        s = _dot_nt(qa, jnp.concatenate([k, onehot], axis=1))
        if diag:
            c = lax.broadcasted_iota(jnp.int32, s.shape, 1)
            s = jnp.where(c <= row_query(s.shape), s, MASK_VALUE)
        m_new = jnp.maximum(m, jnp.max(s, axis=1, keepdims=True))
        alpha = jnp.exp2(m - m_new)
        p = jnp.exp2(s - m_new)
        l = alpha * l + jnp.sum(p, axis=1, keepdims=True)
        acc = alpha * acc + _dot(p.astype(jnp.bfloat16), v)
        return m_new, l, acc

    init = (jnp.full((NSA_ROWS, 1), MASK_VALUE, jnp.float32),
            jnp.zeros((NSA_ROWS, 1), jnp.float32),
            jnp.zeros((NSA_ROWS, HEAD_DIM), jnp.float32))
    carry = lax.fori_loop(0, i, lambda j, c: sel_tile(j, c, False), init)
    _, l_s, acc_s = sel_tile(i, carry, True)
    o_s = acc_s / l_s

    n_w = WINDOW // tq + 1
    s_w, v_w = [], []
    for t in range(n_w):
        start = q0 - WINDOW + t * tq
        st = pl.multiple_of(jnp.maximum(start, 0), tq)
        s = _dot_nt(q4, kw_ref[0, pl.ds(st, tq), :])
        v_w.append(vw_ref[0, pl.ds(st, tq), :])
        c = lax.broadcasted_iota(jnp.int32, s.shape, 1)
        if t == 0:
            s = jnp.where(c > row_query(s.shape), s, MASK_VALUE)
        if t == n_w - 1:
            s = jnp.where(c <= row_query(s.shape), s, MASK_VALUE)
        else:
            s = jnp.where(start >= 0, s, MASK_VALUE)
        s_w.append(s)
    m_w = s_w[0].max(axis=1, keepdims=True)
    for s in s_w[1:]:
        m_w = jnp.maximum(m_w, s.max(axis=1, keepdims=True))
    l_w = jnp.zeros((NSA_ROWS, 1), jnp.float32)
    acc_w = jnp.zeros((NSA_ROWS, HEAD_DIM), jnp.float32)
    for s, v in zip(s_w, v_w):
        p = jnp.exp2(s - m_w)
        l_w = l_w + jnp.sum(p, axis=1, keepdims=True)
        acc_w = acc_w + _dot(p.astype(jnp.bfloat16), v)
    o_w = acc_w / l_w

    gates = jax.nn.sigmoid(zs_ref[0])
    lane = lax.broadcasted_iota(jnp.int32, gates.shape, 1)
    outs = []
    for r in range(NSA_GROUP):
        rows = slice(r * tq, (r + 1) * tq)
        o_r = jnp.zeros((tq, HEAD_DIM), jnp.float32)
        for br, o_b in enumerate((o_c, o_s, o_w)):
            col = FOX_HEADS + 3 * (g * NSA_GROUP + r) + br
            gcol = jnp.sum(jnp.where(lane == col, gates, 0.0), axis=1, keepdims=True)
            o_r = o_r + gcol * o_b[rows]
        outs.append(o_r)
    o_ref[0] = jnp.concatenate(outs, axis=1).astype(o_ref.dtype)


def _nsa(z_main, z_small, kvc, ovt):
    nb, s, _ = z_main.shape
    nc = kvc.shape[2]
    gw = NSA_GROUP * HEAD_DIM
    full = lambda col0: pl.BlockSpec(
        (1, s, HEAD_DIM), lambda b, g, i, c=col0 // HEAD_DIM: (b, 0, c + g))
    return pl.pallas_call(
        functools.partial(_nsa_kernel, seq=s),
        out_shape=jax.ShapeDtypeStruct((nb, s, NSA_W), jnp.bfloat16),
        grid=(nb, NSA_KV_HEADS, s // NSA_TQ),
        in_specs=[
            pl.BlockSpec((1, NSA_TQ, gw), lambda b, g, i: (b, i, COL_QN // gw + g)),
            pl.BlockSpec((1, NSA_TQ, Z_SMALL), lambda b, g, i: (b, i, 0)),
            pl.BlockSpec((1, 1, nc, HEAD_DIM), lambda b, g, i: (b, g, 0, 0)),
            pl.BlockSpec((1, 1, nc, HEAD_DIM), lambda b, g, i: (b, NSA_KV_HEADS + g, 0, 0)),
            full(COL_KS), full(COL_KS + KV_W), full(COL_KW), full(COL_KW + KV_W),
            pl.BlockSpec(ovt.shape, lambda b, g, i: (0, 0)),
        ],
        out_specs=pl.BlockSpec((1, NSA_TQ, gw), lambda b, g, i: (b, i, g)),
        compiler_params=_cparams(("arbitrary", "arbitrary", "arbitrary")),
        name="nsa",
    )(z_main, z_small, kvc, kvc, z_main, z_main, z_main, z_main, ovt)


OUT_TM = 512
OUT_RC = 256


def _outproj_kernel(of_ref, on_ref, bf_ref, bn_ref, w_ref, x_ref, g_ref, nw_ref, sc_ref,
                    sh_ref, h_ref, u_ref):
    for r in range(OUT_TM // OUT_RC):
        rows = slice(r * OUT_RC, (r + 1) * OUT_RC)
        yf = _rms(of_ref[0, rows, :].astype(jnp.float32)) * bf_ref[...]
        yn = _rms(on_ref[0, rows, :].astype(jnp.float32)) * bn_ref[...]
        y = jnp.concatenate([yf.astype(jnp.bfloat16), yn.astype(jnp.bfloat16)], axis=1)
        h = x_ref[0, rows, :] + g_ref[0] * _dot(y, w_ref[...])
        h_ref[0, rows, :] = h
        u = _rms(h) * nw_ref[...]
        u_ref[0, rows, :] = (u * (1.0 + sc_ref[0]) + sh_ref[0]).astype(jnp.bfloat16)


def _outproj(o_f, o_n, beta_f, beta_n, w_out, x, gate, norm_w, sc, sh):
    nb, s, d = x.shape
    vec = lambda n: pl.BlockSpec((1, n), lambda b, i: (0, 0))
    bvec = pl.BlockSpec((1, 1, d), lambda b, i: (b, 0, 0))
    rows = lambda n: pl.BlockSpec((1, OUT_TM, n), lambda b, i: (b, i, 0))
    return pl.pallas_call(
        _outproj_kernel,
        out_shape=(jax.ShapeDtypeStruct((nb, s, d), jnp.float32),
                   jax.ShapeDtypeStruct((nb, s, d), jnp.bfloat16)),
        grid=(nb, s // OUT_TM),
        in_specs=[
            rows(FOX_W), rows(NSA_W), vec(FOX_W), vec(NSA_W),
            pl.BlockSpec((FOX_W + NSA_W, d), lambda b, i: (0, 0),
                         pipeline_mode=pl.Buffered(1)),
            rows(d), bvec, vec(d), bvec, bvec,
        ],
        out_specs=(rows(d), rows(d)),
        compiler_params=_cparams(("arbitrary", "arbitrary")),
        name="outproj",
    )(o_f, o_n, beta_f.reshape(1, FOX_W), beta_n.reshape(1, NSA_W), w_out, x,
      gate.reshape(nb, 1, d), norm_w.reshape(1, d), sc.reshape(nb, 1, d),
      sh.reshape(nb, 1, d))


FFN_TM = 1024
FFN_TF = 512
FFN_RC = 512


def _ffn_kernel(u_ref, h_ref, g_ref, fw_ref, wg_ref, wu_ref, wd_ref, o_ref, *, final):
    j = pl.program_id(2)

    @pl.when(j == 0)
    def _():
        o_ref[...] = jnp.zeros_like(o_ref)

    for r in range(FFN_TM // FFN_RC):
        rows = slice(r * FFN_RC, (r + 1) * FFN_RC)
        u = u_ref[0, rows, :]
        a = _dot(u, wg_ref[...])
        b = _dot(u, wu_ref[...])
        mid = (a * jax.nn.sigmoid(a) * b).astype(jnp.bfloat16)
        o_ref[0, rows, :] += _dot(mid, wd_ref[...])

    @pl.when(j == pl.num_programs(2) - 1)
    def _():
        h2 = h_ref[0] + g_ref[0] * o_ref[0]
        o_ref[0] = _rms(h2) * fw_ref[...] if final else h2


def _ffn(u2, h1, gate, final_w, w_gate, w_up, w_down, final):
    nb, s, d = h1.shape
    f = w_gate.shape[1]
    return pl.pallas_call(
        functools.partial(_ffn_kernel, final=final),
        out_shape=jax.ShapeDtypeStruct((nb, s, d), jnp.float32),
        grid=(nb, s // FFN_TM, f // FFN_TF),
        in_specs=[
            pl.BlockSpec((1, FFN_TM, d), lambda b, i, j: (b, i, 0)),
            pl.BlockSpec((1, FFN_TM, d), lambda b, i, j: (b, i, 0),
                         pipeline_mode=pl.Buffered(1)),
            pl.BlockSpec((1, 1, d), lambda b, i, j: (b, 0, 0)),
            pl.BlockSpec((1, d), lambda b, i, j: (0, 0)),
            pl.BlockSpec((d, FFN_TF), lambda b, i, j: (0, j)),
            pl.BlockSpec((d, FFN_TF), lambda b, i, j: (0, j)),
            pl.BlockSpec((FFN_TF, d), lambda b, i, j: (j, 0)),
        ],
        out_specs=pl.BlockSpec((1, FFN_TM, d), lambda b, i, j: (b, i, 0)),
        compiler_params=pltpu.CompilerParams(
            dimension_semantics=("arbitrary", "arbitrary", "arbitrary"),
            vmem_limit_bytes=FFN_VMEM_LIMIT),
        name="ffn",
    )(u2, h1, gate.reshape(nb, 1, d), final_w.reshape(1, d), w_gate, w_up, w_down)


def _split_w_in(w_in):
    sizes = ([FOX_W] * 3 + [FOX_HEADS] + [NSA_W] + [KV_W] * 6 + [3 * NSA_HEADS])
    offs = [0]
    for sz in sizes:
        offs.append(offs[-1] + sz)
    piece = lambda n: w_in[:, offs[n]:offs[n + 1]]
    main = jnp.concatenate([piece(n) for n in (0, 1, 2, 4, 5, 6, 7, 8, 9, 10)], axis=1)
    small = jnp.concatenate(
        [piece(3), piece(11),
         jnp.zeros((w_in.shape[0], Z_SMALL - FOX_HEADS - 3 * NSA_HEADS), w_in.dtype)], axis=1)
    return main.astype(jnp.bfloat16), small.astype(jnp.bfloat16)


def _overlap_t(n_cmp_pad, n_slc):
    cs = jnp.arange(n_cmp_pad) * CMP_STRIDE
    ss = jnp.arange(n_slc) * SLC_LEN
    ov = (jnp.minimum(cs[None, :] + CMP_LEN, ss[:, None] + SLC_LEN)
          - jnp.maximum(cs[None, :], ss[:, None]))
    ov = jnp.clip(ov, 0).astype(jnp.float32) / CMP_STRIDE
    return ov.astype(jnp.bfloat16)


def kernel(x, c, positions, w_ada, b_ada, norm_attn, norm_ffn, w_in, b_fgate, cmp_pos,
           w_kc1, w_kc2, w_vc1, w_vc2, beta_fox, beta_nsa, w_out, w_gate, w_up, w_down,
           final_norm):
    nb, s, d = x.shape
    depth = w_ada.shape[0]
    n_cmp_pad = s // CMP_STRIDE
    cmp_end = jnp.arange(n_cmp_pad) * CMP_STRIDE + (CMP_LEN - 1)
    pos_c = jnp.take(positions, jnp.minimum(cmp_end, s - 1), axis=1)
    cos, sin = _rope_tables(positions, 512)
    cos_c, sin_c = _rope_tables(pos_c, n_cmp_pad)
    ovt = _overlap_t(n_cmp_pad, s // SLC_LEN)

    h = x
    for l in range(depth):
        mod = _ada(c, w_ada[l], b_ada[l])
        sh1, sc1, g1, sh2, sc2, g2 = jnp.split(mod, 6, axis=-1)
        w_main, w_small = _split_w_in(w_in[l])
        z_main, z_small = _inproj(h, norm_attn[l], sc1, sh1, w_main, w_small, cos, sin)
        cum = _decay(z_small, b_fgate[l])

        xc = z_main[:, :, COL_KC:COL_KC + 2 * KV_W]
        xc = xc.reshape(nb, n_cmp_pad, CMP_STRIDE, 2 * NSA_KV_HEADS, HEAD_DIM)
        xc = xc.transpose(0, 3, 1, 2, 4).reshape(nb, 2 * NSA_KV_HEADS, n_cmp_pad,
                                                 CMP_STRIDE * HEAD_DIM)
        w1 = jnp.stack([w_kc1[l], w_vc1[l]]).astype(jnp.bfloat16)
        w2 = jnp.stack([w_kc2[l], w_vc2[l]]).astype(jnp.bfloat16)
        kvc = _compress(xc, w1, w2, cmp_pos[l], cos_c, sin_c)

        o_f = _fox(z_main, cum)
        o_n = _nsa(z_main, z_small, kvc, ovt)
        h1, u2 = _outproj(o_f, o_n, beta_fox[l], beta_nsa[l], w_out[l].astype(jnp.bfloat16),
                          h, g1, norm_ffn[l], sc2, sh2)
        h = _ffn(u2, h1, g2, final_norm, w_gate[l].astype(jnp.bfloat16),
                 w_up[l].astype(jnp.bfloat16), w_down[l].astype(jnp.bfloat16),
                 final=(l == depth - 1))
    return h
```

```python
import functools

import jax
import jax.numpy as jnp
from jax import lax
from jax.experimental import pallas as pl
from jax.experimental.pallas import tpu as pltpu

D_MODEL = 2048
HEAD_DIM = 128
FOX_HEADS = 8
NSA_HEADS = 8
NSA_KV_HEADS = 2
NSA_GROUP = NSA_HEADS // NSA_KV_HEADS
CMP_LEN = 32
CMP_STRIDE = 16
CMP_HIDDEN = 2 * HEAD_DIM
SLC_LEN = 64
SLC_TOPK = 16
WINDOW = 512
D_FF = 5632
ROPE_THETA = 10000.0
NORM_EPS = 1e-6
MASK_VALUE = -1e30
SEL_BONUS = 1e6

FOX_W = FOX_HEADS * HEAD_DIM
NSA_W = NSA_HEADS * HEAD_DIM
KV_W = NSA_KV_HEADS * HEAD_DIM
Z_MAIN = 3 * FOX_W + NSA_W + 6 * KV_W
Z_SMALL = 128
COL_QN = 3 * FOX_W
COL_KC = COL_QN + NSA_W
COL_KS = COL_KC + 2 * KV_W
COL_KW = COL_KS + 2 * KV_W

VMEM_LIMIT = 48 * 1024 * 1024
FFN_VMEM_LIMIT = 56 * 1024 * 1024
SCALE = HEAD_DIM ** -0.5
LOG2E = 1.4426950408889634


def _cparams(sem):
    return pltpu.CompilerParams(dimension_semantics=sem, vmem_limit_bytes=VMEM_LIMIT)


def _dot(a, b):
    return jnp.dot(a, b, preferred_element_type=jnp.float32)


def _dot_nt(a, b):
    return lax.dot_general(a, b, (((1,), (1,)), ((), ())), preferred_element_type=jnp.float32)


def _rms(x):
    return x * lax.rsqrt(jnp.mean(x * x, axis=-1, keepdims=True) + NORM_EPS)


ADA_TN = 1024
ADA_KC = 256


def _ada_kernel(ct_ref, w_ref, b_ref, o_ref, *, nb):
    tn = w_ref.shape[1]
    ct = ct_ref[...]
    st = ct * jax.nn.sigmoid(ct)

    accs = [jnp.zeros((8, tn), jnp.float32) for _ in range(nb)]
    for kc in range(D_MODEL // ADA_KC):
        w = w_ref[kc * ADA_KC:(kc + 1) * ADA_KC, :]
        for b in range(nb):
            s = st[kc * ADA_KC:(kc + 1) * ADA_KC, b:b + 1]
            accs[b] = accs[b] + jnp.sum((w * s).reshape(ADA_KC // 8, 8, tn), axis=0)
    for b in range(nb):
        o_ref[b:b + 1, :] = jnp.sum(accs[b], axis=0, keepdims=True) + b_ref[...]


def _ada(c, w_ada, b_ada):
    nb = c.shape[0]
    n = w_ada.shape[1]
    return pl.pallas_call(
        functools.partial(_ada_kernel, nb=nb),
        out_shape=jax.ShapeDtypeStruct((nb, n), jnp.float32),
        grid=(n // ADA_TN,),
        in_specs=[
            pl.BlockSpec((D_MODEL, nb), lambda j: (0, 0)),
            pl.BlockSpec((D_MODEL, ADA_TN), lambda j: (0, j)),
            pl.BlockSpec((1, ADA_TN), lambda j: (0, j)),
        ],
        out_specs=pl.BlockSpec((nb, ADA_TN), lambda j: (0, j)),
        compiler_params=_cparams(("arbitrary",)),
        name="ada",
    )(c.T, w_ada, b_ada.reshape(1, n))


def _rope_table_kernel(pos_ref, inv_ref, cos_ref, sin_ref):
    ang = pos_ref[0].astype(jnp.float32) * inv_ref[...]
    lane = lax.broadcasted_iota(jnp.int32, ang.shape, 1)
    cos_ref[0] = jnp.cos(ang)
    sin_ref[0] = jnp.where(lane < HEAD_DIM // 2, -1.0, 1.0) * jnp.sin(ang)


def _rope_tables(pos, ts):
    nb, n = pos.shape
    inv = ROPE_THETA ** (-jnp.arange(0, HEAD_DIM, 2, dtype=jnp.float32) / HEAD_DIM)
    inv2 = jnp.concatenate([inv, inv]).reshape(1, HEAD_DIM)
    shp = jax.ShapeDtypeStruct((nb, n, HEAD_DIM), jnp.float32)
    spec = pl.BlockSpec((1, ts, HEAD_DIM), lambda b, i: (b, i, 0))
    return pl.pallas_call(
        _rope_table_kernel,
        out_shape=(shp, shp),
        grid=(nb, n // ts),
        in_specs=[pl.BlockSpec((1, ts, 1), lambda b, i: (b, i, 0)),
                  pl.BlockSpec((1, HEAD_DIM), lambda b, i: (0, 0))],
        out_specs=(spec, spec),
        compiler_params=_cparams(("arbitrary", "arbitrary")),
        name="rope_tables",
    )(pos.reshape(nb, n, 1), inv2)


def _apply_rope(x, cos, sin):
    return x * cos + pltpu.roll(x, HEAD_DIM // 2, axis=1) * sin


IN_TM = 1024
IN_TN = 512


def _inproj_kernel(x_ref, g_ref, sc_ref, sh_ref, w_ref, ws_ref, cos_ref, sin_ref,
                   z_ref, zs_ref, zc_ref, u_ref):
    j = pl.program_id(2)

    @pl.when(j == 0)
    def _():
        u = _rms(x_ref[0]) * g_ref[...]
        u = u * (1.0 + sc_ref[0]) + sh_ref[0]
        ub = u.astype(jnp.bfloat16)
        u_ref[...] = ub
        zs_ref[0] = _dot(ub, ws_ref[...])

    acc = _dot(u_ref[...], w_ref[...])
    n_heads = IN_TN // HEAD_DIM
    rope_all = (j >= COL_QN // IN_TN) & (j < COL_KC // IN_TN)
    rope_half = (j == COL_KS // IN_TN) | (j == COL_KW // IN_TN)

    def roped(n_rope):
        cos = cos_ref[0]
        sin = sin_ref[0]
        parts = []
        for h in range(n_heads):
            a = acc[:, h * HEAD_DIM:(h + 1) * HEAD_DIM]
            parts.append(_apply_rope(a, cos, sin) if h < n_rope else a)
        return jnp.concatenate(parts, axis=1).astype(z_ref.dtype)

    @pl.when(rope_all)
    def _():
        z_ref[0] = roped(n_heads)

    @pl.when(rope_half)
    def _():
        z_ref[0] = roped(KV_W // HEAD_DIM)

    @pl.when(jnp.logical_not(rope_all | rope_half))
    def _():
        z_ref[0] = acc.astype(z_ref.dtype)

    @pl.when(j == COL_KC // IN_TN)
    def _():
        zc_ref[0] = acc


def _inproj(x, norm_w, sc, sh, w_main, w_small, cos, sin):
    nb, s, d = x.shape
    grid = (nb, s // IN_TM, Z_MAIN // IN_TN)
    return pl.pallas_call(
        _inproj_kernel,
        out_shape=(jax.ShapeDtypeStruct((nb, s, Z_MAIN), jnp.bfloat16),
                   jax.ShapeDtypeStruct((nb, s, Z_SMALL), jnp.float32),
                   jax.ShapeDtypeStruct((nb, s, IN_TN), jnp.float32)),
        grid=grid,
        in_specs=[
            pl.BlockSpec((1, IN_TM, d), lambda b, i, j: (b, i, 0)),
            pl.BlockSpec((1, d), lambda b, i, j: (0, 0)),
            pl.BlockSpec((1, 1, d), lambda b, i, j: (b, 0, 0)),
            pl.BlockSpec((1, 1, d), lambda b, i, j: (b, 0, 0)),
            pl.BlockSpec((d, IN_TN), lambda b, i, j: (0, j)),
            pl.BlockSpec((d, Z_SMALL), lambda b, i, j: (0, 0)),
            pl.BlockSpec((1, IN_TM, HEAD_DIM), lambda b, i, j: (b, i, 0)),
            pl.BlockSpec((1, IN_TM, HEAD_DIM), lambda b, i, j: (b, i, 0)),
        ],
        out_specs=(pl.BlockSpec((1, IN_TM, IN_TN), lambda b, i, j: (b, i, j)),
                   pl.BlockSpec((1, IN_TM, Z_SMALL), lambda b, i, j: (b, i, 0)),
                   pl.BlockSpec((1, IN_TM, IN_TN), lambda b, i, j: (b, i, 0))),
        scratch_shapes=[pltpu.VMEM((IN_TM, d), jnp.bfloat16)],
        compiler_params=_cparams(("arbitrary", "arbitrary", "arbitrary")),
        name="inproj",
    )(x, norm_w.reshape(1, d), sc.reshape(nb, 1, d), sh.reshape(nb, 1, d),
      w_main, w_small, cos, sin)


DEC_T = 512


def _decay_kernel(zs_ref, bf_ref, cum_ref, carry_ref):
    @pl.when(pl.program_id(1) == 0)
    def _():
        carry_ref[...] = jnp.zeros_like(carry_ref)

    xt = zs_ref[0].T[0:FOX_HEADS, :] + bf_ref[...]
    lf = jnp.minimum(xt, 0.0) - jnp.log1p(jnp.exp(-jnp.abs(xt)))
    lane = lax.broadcasted_iota(jnp.int32, lf.shape, 1)
    sh = 1
    while sh < DEC_T:
        lf = lf + jnp.where(lane >= sh, pltpu.roll(lf, sh, axis=1), 0.0)
        sh *= 2
    cum = lf + carry_ref[:, 0:1]
    cum_ref[0] = cum
    carry_ref[...] = jnp.broadcast_to(cum[:, DEC_T - 1:DEC_T], carry_ref.shape)


def _decay(z_small, b_fgate):
    nb, s, _ = z_small.shape
    return pl.pallas_call(
        _decay_kernel,
        out_shape=jax.ShapeDtypeStruct((nb, FOX_HEADS, s), jnp.float32),
        grid=(nb, s // DEC_T),
        in_specs=[pl.BlockSpec((1, DEC_T, Z_SMALL), lambda b, i: (b, i, 0)),
                  pl.BlockSpec((FOX_HEADS, 1), lambda b, i: (0, 0))],
        out_specs=pl.BlockSpec((1, FOX_HEADS, DEC_T), lambda b, i: (b, 0, i)),
        scratch_shapes=[pltpu.VMEM((FOX_HEADS, 128), jnp.float32)],
        compiler_params=_cparams(("arbitrary", "arbitrary")),
        name="decay",
    )(z_small, b_fgate.reshape(FOX_HEADS, 1))


def _compress_kernel(x_ref, w1_ref, w2_ref, pos_ref, cos_ref, sin_ref, o_ref):
    kind = pl.program_id(1)
    nc = x_ref.shape[1] // CMP_STRIDE
    p = jnp.zeros((nc, CMP_HIDDEN), jnp.float32)
    q = jnp.zeros((nc, CMP_HIDDEN), jnp.float32)
    for l in range(CMP_STRIDE):
        xl = x_ref[0, pl.ds(l, nc, stride=CMP_STRIDE), :].astype(jnp.bfloat16)
        p = p + _dot(xl, w1_ref[0, l])
        q = q + _dot(xl, w1_ref[0, CMP_STRIDE + l])
    posb = jnp.broadcast_to(pos_ref[...], (8, CMP_LEN * HEAD_DIM)).astype(jnp.bfloat16)
    w1_flat = w1_ref[0].reshape(CMP_LEN * HEAD_DIM, CMP_HIDDEN)
    pterm = _dot(posb, w1_flat)[0:1, :]
    h = p + pltpu.roll(q, nc - 1, axis=0) + pterm
    a = jax.nn.gelu(h).astype(jnp.bfloat16)
    out = _dot(a, w2_ref[0])

    @pl.when(kind < NSA_KV_HEADS)
    def _():
        o_ref[0, 0] = _apply_rope(out, cos_ref[0], sin_ref[0]).astype(o_ref.dtype)

    @pl.when(kind >= NSA_KV_HEADS)
    def _():
        o_ref[0, 0] = out.astype(o_ref.dtype)


def _compress(zc, w1, w2, cmp_pos, cos_c, sin_c):
    nb, s, _ = zc.shape
    nk = 2 * NSA_KV_HEADS
    nc = s // CMP_STRIDE
    return pl.pallas_call(
        _compress_kernel,
        out_shape=jax.ShapeDtypeStruct((nb, nk, nc, HEAD_DIM), jnp.bfloat16),
        grid=(nb, nk),
        in_specs=[
            pl.BlockSpec((1, s, HEAD_DIM), lambda b, k: (b, 0, k)),
            pl.BlockSpec((1, CMP_LEN, HEAD_DIM, CMP_HIDDEN),
                         lambda b, k: (k // NSA_KV_HEADS, 0, 0, 0)),
            pl.BlockSpec((1, CMP_HIDDEN, HEAD_DIM), lambda b, k: (k // NSA_KV_HEADS, 0, 0)),
            pl.BlockSpec((1, CMP_LEN * HEAD_DIM), lambda b, k: (0, 0)),
            pl.BlockSpec((1, nc, HEAD_DIM), lambda b, k: (b, 0, 0)),
            pl.BlockSpec((1, nc, HEAD_DIM), lambda b, k: (b, 0, 0)),
        ],
        out_specs=pl.BlockSpec((1, 1, nc, HEAD_DIM), lambda b, k: (b, k, 0, 0)),
        compiler_params=_cparams(("arbitrary", "arbitrary")),
        name="compress",
    )(zc, w1, w2, cmp_pos.reshape(1, CMP_LEN * HEAD_DIM), cos_c, sin_c)


FOX_T = 512


def _scaled_q(q):
    return (q.astype(jnp.float32) * (SCALE * LOG2E)).astype(jnp.bfloat16)


FOX_HB = 2


def _fox_kernel(q_ref, k_ref, v_ref, cum_ref, o_ref):
    hb = pl.program_id(1)
    i = pl.program_id(2)
    t = FOX_T
    cols = [slice(n * HEAD_DIM, (n + 1) * HEAD_DIM) for n in range(FOX_HB)]
    qs = [_scaled_q(q_ref[0, :, c]) for c in cols]

    def tile(k0, width, carry, diag):
        k0 = pl.multiple_of(k0, t)
        out = []
        for n, (m, l, acc) in enumerate(carry):
            k = k_ref[0, pl.ds(k0, width), cols[n]]
            v = v_ref[0, pl.ds(k0, width), cols[n]]
            ck = cum_ref[0, pl.ds(hb * FOX_HB + n, 1), pl.ds(k0, width)] * LOG2E
            s = _dot_nt(qs[n], k) - ck
            if diag:
                r = lax.broadcasted_iota(jnp.int32, s.shape, 0)
                c = lax.broadcasted_iota(jnp.int32, s.shape, 1)
                s = jnp.where(c <= r + (width - t), s, MASK_VALUE)
            m_new = jnp.maximum(m, jnp.max(s, axis=1, keepdims=True))
            alpha = jnp.exp2(m - m_new)
            p = jnp.exp2(s - m_new)
            l = alpha * l + jnp.sum(p, axis=1, keepdims=True)
            acc = alpha * acc + _dot(p.astype(jnp.bfloat16), v)
            out.append((m_new, l, acc))
        return tuple(out)

    init = tuple((jnp.full((t, 1), MASK_VALUE, jnp.float32),
                  jnp.zeros((t, 1), jnp.float32),
                  jnp.zeros((t, HEAD_DIM), jnp.float32)) for _ in range(FOX_HB))
    n_pairs = i >> 1
    carry = lax.fori_loop(0, n_pairs, lambda p, c: tile(p * (2 * t), 2 * t, c, False), init)
    k_tail = n_pairs * (2 * t)
    carry = lax.cond((i & 1) == 0,
                     lambda c: tile(k_tail, t, c, True),
                     lambda c: tile(k_tail, 2 * t, c, True), carry)
    for n, (_, l, acc) in enumerate(carry):
        o_ref[0, :, cols[n]] = (acc * (1.0 / l)).astype(o_ref.dtype)


def _fox(z_main, cum):
    nb, s, _ = z_main.shape
    w = FOX_HB * HEAD_DIM
    grid = (nb, FOX_HEADS // FOX_HB, s // FOX_T)
    return pl.pallas_call(
        _fox_kernel,
        out_shape=jax.ShapeDtypeStruct((nb, s, FOX_W), jnp.bfloat16),
        grid=grid,
        in_specs=[
            pl.BlockSpec((1, FOX_T, w), lambda b, h, i: (b, i, h)),
            pl.BlockSpec((1, s, w), lambda b, h, i: (b, 0, FOX_W // w + h)),
            pl.BlockSpec((1, s, w), lambda b, h, i: (b, 0, 2 * FOX_W // w + h)),
            pl.BlockSpec((1, FOX_HEADS, s), lambda b, h, i: (b, 0, 0)),
        ],
        out_specs=pl.BlockSpec((1, FOX_T, w), lambda b, h, i: (b, i, h)),
        compiler_params=_cparams(("arbitrary", "arbitrary", "arbitrary")),
        name="fox",
    )(z_main, z_main, z_main, cum)


NSA_TQ = 256
NSA_ROWS = NSA_GROUP * NSA_TQ
NSEL_PAD = 128
SLC_SHIFT = SLC_LEN.bit_length() - 1
assert 1 << SLC_SHIFT == SLC_LEN
assert WINDOW % NSA_TQ == 0 and NSA_TQ % SLC_LEN == 0
SUBLANES = 8


def _block_ranks(score, n_slc):
    tq = score.shape[1]
    groups = [score[SUBLANES * v:SUBLANES * (v + 1)] for v in range(n_slc // SUBLANES)]
    ranks = [jnp.zeros((SUBLANES, tq), jnp.float32) for _ in groups]
    for m in range(n_slc):
        row = jnp.broadcast_to(score[m:m + 1, :], (SUBLANES, tq))
        for v, grp in enumerate(groups):
            lo = SUBLANES * v
            if lo > m:
                before = row >= grp
            elif lo + SUBLANES - 1 <= m:
                before = row > grp
            else:
                n_i = lo + lax.broadcasted_iota(jnp.int32, grp.shape, 0)
                before = (row > grp) | ((row == grp) & (n_i > m))
            ranks[v] = ranks[v] + jnp.where(before, 1.0, 0.0)
    return jnp.concatenate(ranks, axis=0)


def _nsa_kernel(q_ref, zs_ref, kc_ref, vc_ref, ks_ref, vs_ref, kw_ref, vw_ref, ovt_ref,
                o_ref, ka_ref, *, seq):
    g = pl.program_id(1)
    i = pl.program_id(2)
    tq = NSA_TQ
    q0 = i * tq

    @pl.when(i == 0)
    def _():
        ka_ref[:, 0:HEAD_DIM] = ks_ref[0]
        tok = lax.broadcasted_iota(jnp.int32, (seq, NSEL_PAD), 0)
        n_i = lax.broadcasted_iota(jnp.int32, (seq, NSEL_PAD), 1)
        ka_ref[:, HEAD_DIM:HEAD_DIM + NSEL_PAD] = jnp.where(
            n_i == (tok >> SLC_SHIFT), 1.0, 0.0).astype(jnp.bfloat16)

    qb = q_ref[0]
    q4 = _scaled_q(jnp.concatenate(
        [qb[:, r * HEAD_DIM:(r + 1) * HEAD_DIM] for r in range(NSA_GROUP)], axis=0))

    def row_query(shape):
        return lax.broadcasted_iota(jnp.int32, shape, 0) & (tq - 1)

    kc = kc_ref[0, 0]
    vc = vc_ref[0, 0]
    s_c = _dot_nt(q4, kc)
    cend = lax.broadcasted_iota(jnp.int32, s_c.shape, 1) * CMP_STRIDE + (CMP_LEN - 1)
    mask_c = cend <= q0 + row_query(s_c.shape)
    s_c = jnp.where(mask_c, s_c, MASK_VALUE)
    m_c = jnp.maximum(jnp.max(s_c, axis=1, keepdims=True), 0.1 * MASK_VALUE)
    e_c = jnp.exp2(s_c - m_c)
    p_c = e_c * (1.0 / jnp.maximum(jnp.sum(e_c, axis=1, keepdims=True), 1e-30))
    o_c = _dot(p_c.astype(jnp.bfloat16), vc)

    psum = p_c[0:tq]
    for r in range(1, NSA_GROUP):
        psum = psum + p_c[r * tq:(r + 1) * tq]
    p_hi = psum.astype(jnp.bfloat16)
    p_lo = (psum - p_hi.astype(jnp.float32)).astype(jnp.bfloat16)
    ovt = ovt_ref[...]
    imp_t = _dot_nt(ovt, p_hi) + _dot_nt(ovt, p_lo)
    n_slc = seq // SLC_LEN
    blk = lax.broadcasted_iota(jnp.int32, imp_t.shape, 0)
    cur = (q0 + lax.broadcasted_iota(jnp.int32, imp_t.shape, 1)) >> SLC_SHIFT
    forced = (blk == 0) | (blk == cur) | (blk == cur - 1)
    score = jnp.where(forced, SEL_BONUS, imp_t)
    score = jnp.where(blk <= cur, score, -SEL_BONUS)
    rank = _block_ranks(score, n_slc)
    bias_t = jnp.where(rank < min(SLC_TOPK, n_slc), 0.0, MASK_VALUE)
    bias_t = jnp.concatenate(
        [bias_t, jnp.zeros((NSEL_PAD - n_slc, tq), jnp.float32)], axis=0)
    bias = bias_t.T.astype(jnp.bfloat16)
    qa = jnp.concatenate([q4, jnp.concatenate([bias] * NSA_GROUP, axis=0)], axis=1)

    def sel_tile(k0, width, carry, diag):
        m, l, acc = carry
        k0 = pl.multiple_of(k0, tq)
        s = _dot_nt(qa, ka_ref[pl.ds(k0, width), :])
        if diag:
            c = lax.broadcasted_iota(jnp.int32, s.shape, 1)
            s = jnp.where(c <= row_query(s.shape) + (width - tq), s, MASK_VALUE)
        m_new = jnp.maximum(m, jnp.max(s, axis=1, keepdims=True))
        alpha = jnp.exp2(m - m_new)
        p = jnp.exp2(s - m_new)
        l = alpha * l + jnp.sum(p, axis=1, keepdims=True)
        acc = alpha * acc + _dot(p.astype(jnp.bfloat16), vs_ref[0, pl.ds(k0, width), :])
        return m_new, l, acc

    init = (jnp.full((NSA_ROWS, 1), MASK_VALUE, jnp.float32),
            jnp.zeros((NSA_ROWS, 1), jnp.float32),
            jnp.zeros((NSA_ROWS, HEAD_DIM), jnp.float32))
    n_pairs = i >> 1
    carry = lax.fori_loop(0, n_pairs,
                          lambda p, c: sel_tile(p * (2 * tq), 2 * tq, c, False), init)
    k_tail = n_pairs * (2 * tq)
    _, l_s, acc_s = lax.cond((i & 1) == 0,
                             lambda c: sel_tile(k_tail, tq, c, True),
                             lambda c: sel_tile(k_tail, 2 * tq, c, True), carry)
    o_s = acc_s * (1.0 / l_s)

    n_w = WINDOW // tq + 1
    s_w, v_w = [], []
    for t in range(n_w):
        start = q0 - WINDOW + t * tq
        st = pl.multiple_of(jnp.maximum(start, 0), tq)
        s = _dot_nt(q4, kw_ref[0, pl.ds(st, tq), :])
        v_w.append(vw_ref[0, pl.ds(st, tq), :])
        c = lax.broadcasted_iota(jnp.int32, s.shape, 1)
        if t == 0:
            s = jnp.where(c > row_query(s.shape), s, MASK_VALUE)
        if t == n_w - 1:
            s = jnp.where(c <= row_query(s.shape), s, MASK_VALUE)
        else:
            s = jnp.where(start >= 0, s, MASK_VALUE)
        s_w.append(s)
    m_w = s_w[0].max(axis=1, keepdims=True)
    for s in s_w[1:]:
        m_w = jnp.maximum(m_w, s.max(axis=1, keepdims=True))
    l_w = jnp.zeros((NSA_ROWS, 1), jnp.float32)
    acc_w = jnp.zeros((NSA_ROWS, HEAD_DIM), jnp.float32)
    for s, v in zip(s_w, v_w):
        p = jnp.exp2(s - m_w)
        l_w = l_w + jnp.sum(p, axis=1, keepdims=True)
        acc_w = acc_w + _dot(p.astype(jnp.bfloat16), v)
    o_w = acc_w * (1.0 / l_w)

    gates = jax.nn.sigmoid(zs_ref[0])
    lane = lax.broadcasted_iota(jnp.int32, gates.shape, 1)
    outs = []
    for r in range(NSA_GROUP):
        rows = slice(r * tq, (r + 1) * tq)
        o_r = jnp.zeros((tq, HEAD_DIM), jnp.float32)
        for br, o_b in enumerate((o_c, o_s, o_w)):
            col = FOX_HEADS + 3 * (g * NSA_GROUP + r) + br
            gcol = jnp.sum(jnp.where(lane == col, gates, 0.0), axis=1, keepdims=True)
            o_r = o_r + gcol * o_b[rows]
        outs.append(o_r)
    o_ref[0] = jnp.concatenate(outs, axis=1).astype(o_ref.dtype)


def _nsa(z_main, z_small, kvc, ovt):
    nb, s, _ = z_main.shape
    nc = kvc.shape[2]
    gw = NSA_GROUP * HEAD_DIM
    full = lambda col0: pl.BlockSpec(
        (1, s, HEAD_DIM), lambda b, g, i, c=col0 // HEAD_DIM: (b, 0, c + g))
    return pl.pallas_call(
        functools.partial(_nsa_kernel, seq=s),
        out_shape=jax.ShapeDtypeStruct((nb, s, NSA_W), jnp.bfloat16),
        grid=(nb, NSA_KV_HEADS, s // NSA_TQ),
        in_specs=[
            pl.BlockSpec((1, NSA_TQ, gw), lambda b, g, i: (b, i, COL_QN // gw + g)),
            pl.BlockSpec((1, NSA_TQ, Z_SMALL), lambda b, g, i: (b, i, 0)),
            pl.BlockSpec((1, 1, nc, HEAD_DIM), lambda b, g, i: (b, g, 0, 0)),
            pl.BlockSpec((1, 1, nc, HEAD_DIM), lambda b, g, i: (b, NSA_KV_HEADS + g, 0, 0)),
            full(COL_KS), full(COL_KS + KV_W), full(COL_KW), full(COL_KW + KV_W),
            pl.BlockSpec(ovt.shape, lambda b, g, i: (0, 0)),
        ],
        out_specs=pl.BlockSpec((1, NSA_TQ, gw), lambda b, g, i: (b, i, g)),
        scratch_shapes=[pltpu.VMEM((s, HEAD_DIM + NSEL_PAD), jnp.bfloat16)],
        compiler_params=_cparams(("arbitrary", "arbitrary", "arbitrary")),
        name="nsa",
    )(z_main, z_small, kvc, kvc, z_main, z_main, z_main, z_main, ovt)


OUT_TM = 512
OUT_RC = 256


def _outproj_kernel(of_ref, on_ref, bf_ref, bn_ref, w_ref, x_ref, g_ref, nw_ref, sc_ref,
                    sh_ref, h_ref, u_ref):
    for r in range(OUT_TM // OUT_RC):
        rows = slice(r * OUT_RC, (r + 1) * OUT_RC)
        yf = _rms(of_ref[0, rows, :].astype(jnp.float32)) * bf_ref[...]
        yn = _rms(on_ref[0, rows, :].astype(jnp.float32)) * bn_ref[...]
        y = jnp.concatenate([yf.astype(jnp.bfloat16), yn.astype(jnp.bfloat16)], axis=1)
        h = x_ref[0, rows, :] + g_ref[0] * _dot(y, w_ref[...])
        h_ref[0, rows, :] = h
        u = _rms(h) * nw_ref[...]
        u_ref[0, rows, :] = (u * (1.0 + sc_ref[0]) + sh_ref[0]).astype(jnp.bfloat16)


def _outproj(o_f, o_n, beta_f, beta_n, w_out, x, gate, norm_w, sc, sh):
    nb, s, d = x.shape
    vec = lambda n: pl.BlockSpec((1, n), lambda b, i: (0, 0))
    bvec = pl.BlockSpec((1, 1, d), lambda b, i: (b, 0, 0))
    rows = lambda n: pl.BlockSpec((1, OUT_TM, n), lambda b, i: (b, i, 0))
    return pl.pallas_call(
        _outproj_kernel,
        out_shape=(jax.ShapeDtypeStruct((nb, s, d), jnp.float32),
                   jax.ShapeDtypeStruct((nb, s, d), jnp.bfloat16)),
        grid=(nb, s // OUT_TM),
        in_specs=[
            rows(FOX_W), rows(NSA_W), vec(FOX_W), vec(NSA_W),
            pl.BlockSpec((FOX_W + NSA_W, d), lambda b, i: (0, 0),
                         pipeline_mode=pl.Buffered(1)),
            rows(d), bvec, vec(d), bvec, bvec,
        ],
        out_specs=(rows(d), rows(d)),
        compiler_params=_cparams(("arbitrary", "arbitrary")),
        name="outproj",
    )(o_f, o_n, beta_f.reshape(1, FOX_W), beta_n.reshape(1, NSA_W), w_out, x,
      gate.reshape(nb, 1, d), norm_w.reshape(1, d), sc.reshape(nb, 1, d),
      sh.reshape(nb, 1, d))


FFN_TM = 1024
FFN_TF = 512
FFN_RC = 512


def _ffn_kernel(u_ref, h_ref, g_ref, fw_ref, wg_ref, wu_ref, wd_ref, o_ref, *, final):
    j = pl.program_id(2)

    @pl.when(j == 0)
    def _():
        o_ref[...] = jnp.zeros_like(o_ref)

    for r in range(FFN_TM // FFN_RC):
        rows = slice(r * FFN_RC, (r + 1) * FFN_RC)
        u = u_ref[0, rows, :]
        a = _dot(u, wg_ref[...])
        b = _dot(u, wu_ref[...])
        mid = (a * jax.nn.sigmoid(a) * b).astype(jnp.bfloat16)
        o_ref[0, rows, :] += _dot(mid, wd_ref[...])

    @pl.when(j == pl.num_programs(2) - 1)
    def _():
        h2 = h_ref[0] + g_ref[0] * o_ref[0]
        o_ref[0] = _rms(h2) * fw_ref[...] if final else h2


def _ffn(u2, h1, gate, final_w, w_gate, w_up, w_down, final):
    nb, s, d = h1.shape
    f = w_gate.shape[1]
    return pl.pallas_call(
        functools.partial(_ffn_kernel, final=final),
        out_shape=jax.ShapeDtypeStruct((nb, s, d), jnp.float32),
        grid=(nb, s // FFN_TM, f // FFN_TF),
        in_specs=[
            pl.BlockSpec((1, FFN_TM, d), lambda b, i, j: (b, i, 0)),
            pl.BlockSpec((1, FFN_TM, d), lambda b, i, j: (b, i, 0),
                         pipeline_mode=pl.Buffered(1)),
            pl.BlockSpec((1, 1, d), lambda b, i, j: (b, 0, 0)),
            pl.BlockSpec((1, d), lambda b, i, j: (0, 0)),
            pl.BlockSpec((d, FFN_TF), lambda b, i, j: (0, j)),
            pl.BlockSpec((d, FFN_TF), lambda b, i, j: (0, j)),
            pl.BlockSpec((FFN_TF, d), lambda b, i, j: (j, 0)),
        ],
        out_specs=pl.BlockSpec((1, FFN_TM, d), lambda b, i, j: (b, i, 0)),
        compiler_params=pltpu.CompilerParams(
            dimension_semantics=("arbitrary", "arbitrary", "arbitrary"),
            vmem_limit_bytes=FFN_VMEM_LIMIT),
        name="ffn",
    )(u2, h1, gate.reshape(nb, 1, d), final_w.reshape(1, d), w_gate, w_up, w_down)


def _split_w_in(w_in):
    sizes = ([FOX_W] * 3 + [FOX_HEADS] + [NSA_W] + [KV_W] * 6 + [3 * NSA_HEADS])
    offs = [0]
    for sz in sizes:
        offs.append(offs[-1] + sz)
    piece = lambda n: w_in[:, offs[n]:offs[n + 1]]
    main = jnp.concatenate([piece(n) for n in (0, 1, 2, 4, 5, 6, 7, 8, 9, 10)], axis=1)
    small = jnp.concatenate(
        [piece(3), piece(11),
         jnp.zeros((w_in.shape[0], Z_SMALL - FOX_HEADS - 3 * NSA_HEADS), w_in.dtype)], axis=1)
    return main.astype(jnp.bfloat16), small.astype(jnp.bfloat16)


def _overlap_t(n_cmp_pad, n_slc):
    cs = jnp.arange(n_cmp_pad) * CMP_STRIDE
    ss = jnp.arange(n_slc) * SLC_LEN
    ov = (jnp.minimum(cs[None, :] + CMP_LEN, ss[:, None] + SLC_LEN)
          - jnp.maximum(cs[None, :], ss[:, None]))
    ov = jnp.clip(ov, 0).astype(jnp.float32) / CMP_STRIDE
    return ov.astype(jnp.bfloat16)


def kernel(x, c, positions, w_ada, b_ada, norm_attn, norm_ffn, w_in, b_fgate, cmp_pos,
           w_kc1, w_kc2, w_vc1, w_vc2, beta_fox, beta_nsa, w_out, w_gate, w_up, w_down,
           final_norm):
    nb, s, d = x.shape
    depth = w_ada.shape[0]
    n_cmp_pad = s // CMP_STRIDE
    cmp_end = jnp.arange(n_cmp_pad) * CMP_STRIDE + (CMP_LEN - 1)
    pos_c = jnp.take(positions, jnp.minimum(cmp_end, s - 1), axis=1)
    cos, sin = _rope_tables(positions, 512)
    cos_c, sin_c = _rope_tables(pos_c, n_cmp_pad)
    ovt = _overlap_t(n_cmp_pad, s // SLC_LEN)

    h = x
    for l in range(depth):
        mod = _ada(c, w_ada[l], b_ada[l])
        sh1, sc1, g1, sh2, sc2, g2 = jnp.split(mod, 6, axis=-1)
        w_main, w_small = _split_w_in(w_in[l])
        z_main, z_small, zc = _inproj(h, norm_attn[l], sc1, sh1, w_main, w_small, cos, sin)
        cum = _decay(z_small, b_fgate[l])

        w1 = jnp.stack([w_kc1[l], w_vc1[l]]).astype(jnp.bfloat16)
        w1 = w1.reshape(2, CMP_LEN, HEAD_DIM, CMP_HIDDEN)
        w2 = jnp.stack([w_kc2[l], w_vc2[l]]).astype(jnp.bfloat16)
        kvc = _compress(zc, w1, w2, cmp_pos[l], cos_c, sin_c)

        o_f = _fox(z_main, cum)
        o_n = _nsa(z_main, z_small, kvc, ovt)
        h1, u2 = _outproj(o_f, o_n, beta_fox[l], beta_nsa[l], w_out[l].astype(jnp.bfloat16),
                          h, g1, norm_ffn[l], sc2, sh2)
        h = _ffn(u2, h1, g2, final_norm, w_gate[l].astype(jnp.bfloat16),
                 w_up[l].astype(jnp.bfloat16), w_down[l].astype(jnp.bfloat16),
                 final=(l == depth - 1))
    return h
```

```python
import functools

import jax
import jax.numpy as jnp
from jax import lax
from jax.experimental import pallas as pl
from jax.experimental.pallas import tpu as pltpu

D_MODEL = 2048
HEAD_DIM = 128
FOX_HEADS = 8
NSA_HEADS = 8
NSA_KV_HEADS = 2
NSA_GROUP = NSA_HEADS // NSA_KV_HEADS
CMP_LEN = 32
CMP_STRIDE = 16
CMP_HIDDEN = 2 * HEAD_DIM
SLC_LEN = 64
SLC_TOPK = 16
WINDOW = 512
D_FF = 5632
ROPE_THETA = 10000.0
NORM_EPS = 1e-6
MASK_VALUE = -1e30
SEL_BONUS = 1e6

FOX_W = FOX_HEADS * HEAD_DIM
NSA_W = NSA_HEADS * HEAD_DIM
KV_W = NSA_KV_HEADS * HEAD_DIM
Z_MAIN = 3 * FOX_W + NSA_W + 6 * KV_W
Z_SMALL = 128
COL_QN = 3 * FOX_W
COL_KS = COL_QN + NSA_W
COL_KW = COL_KS + KV_W
COL_VS = COL_KW + KV_W
COL_VW = COL_VS + KV_W
COL_KC = COL_VW + KV_W
COL_ROPE_END = COL_VS

VMEM_LIMIT = 48 * 1024 * 1024
FFN_VMEM_LIMIT = 56 * 1024 * 1024
SCALE = HEAD_DIM ** -0.5
LOG2E = 1.4426950408889634


def _cparams(sem):
    return pltpu.CompilerParams(dimension_semantics=sem, vmem_limit_bytes=VMEM_LIMIT)


def _dot(a, b):
    return jnp.dot(a, b, preferred_element_type=jnp.float32)


def _dot_nt(a, b):
    return lax.dot_general(a, b, (((1,), (1,)), ((), ())), preferred_element_type=jnp.float32)


def _rms(x):
    return x * lax.rsqrt(jnp.mean(x * x, axis=-1, keepdims=True) + NORM_EPS)


ADA_TN = 1024
ADA_KC = 256


def _ada_kernel(ct_ref, w_ref, b_ref, o_ref, *, nb):
    tn = w_ref.shape[1]
    ct = ct_ref[...]
    st = ct * jax.nn.sigmoid(ct)

    accs = [jnp.zeros((8, tn), jnp.float32) for _ in range(nb)]
    for kc in range(D_MODEL // ADA_KC):
        w = w_ref[kc * ADA_KC:(kc + 1) * ADA_KC, :]
        for b in range(nb):
            s = st[kc * ADA_KC:(kc + 1) * ADA_KC, b:b + 1]
            accs[b] = accs[b] + jnp.sum((w * s).reshape(ADA_KC // 8, 8, tn), axis=0)
    for b in range(nb):
        o_ref[b:b + 1, :] = jnp.sum(accs[b], axis=0, keepdims=True) + b_ref[...]


def _ada(c, w_ada, b_ada):
    nb = c.shape[0]
    n = w_ada.shape[1]
    return pl.pallas_call(
        functools.partial(_ada_kernel, nb=nb),
        out_shape=jax.ShapeDtypeStruct((nb, n), jnp.float32),
        grid=(n // ADA_TN,),
        in_specs=[
            pl.BlockSpec((D_MODEL, nb), lambda j: (0, 0)),
            pl.BlockSpec((D_MODEL, ADA_TN), lambda j: (0, j)),
            pl.BlockSpec((1, ADA_TN), lambda j: (0, j)),
        ],
        out_specs=pl.BlockSpec((nb, ADA_TN), lambda j: (0, j)),
        compiler_params=_cparams(("arbitrary",)),
        name="ada",
    )(c.T, w_ada, b_ada.reshape(1, n))


def _rope_table_kernel(pos_ref, inv_ref, cos_ref, sin_ref):
    ang = pos_ref[0].astype(jnp.float32) * inv_ref[...]
    lane = lax.broadcasted_iota(jnp.int32, ang.shape, 1)
    cos_ref[0] = jnp.cos(ang)
    sin_ref[0] = jnp.where(lane < HEAD_DIM // 2, -1.0, 1.0) * jnp.sin(ang)


def _rope_tables(pos, ts):
    nb, n = pos.shape
    inv = ROPE_THETA ** (-jnp.arange(0, HEAD_DIM, 2, dtype=jnp.float32) / HEAD_DIM)
    inv2 = jnp.concatenate([inv, inv]).reshape(1, HEAD_DIM)
    shp = jax.ShapeDtypeStruct((nb, n, HEAD_DIM), jnp.float32)
    spec = pl.BlockSpec((1, ts, HEAD_DIM), lambda b, i: (b, i, 0))
    return pl.pallas_call(
        _rope_table_kernel,
        out_shape=(shp, shp),
        grid=(nb, n // ts),
        in_specs=[pl.BlockSpec((1, ts, 1), lambda b, i: (b, i, 0)),
                  pl.BlockSpec((1, HEAD_DIM), lambda b, i: (0, 0))],
        out_specs=(spec, spec),
        compiler_params=_cparams(("arbitrary", "arbitrary")),
        name="rope_tables",
    )(pos.reshape(nb, n, 1), inv2)


def _apply_rope(x, cos, sin):
    return x * cos + pltpu.roll(x, HEAD_DIM // 2, axis=1) * sin


IN_TM = 1024
IN_TN = 512


def _inproj_kernel(x_ref, g_ref, sc_ref, sh_ref, w_ref, ws_ref, cos_ref, sin_ref,
                   z_ref, zs_ref, zc_ref, u_ref):
    j = pl.program_id(2)

    @pl.when(j == 0)
    def _():
        u = _rms(x_ref[0]) * g_ref[...]
        u = u * (1.0 + sc_ref[0]) + sh_ref[0]
        ub = u.astype(jnp.bfloat16)
        u_ref[...] = ub
        zs_ref[0] = _dot_nt(ub, ws_ref[...])

    acc = _dot_nt(u_ref[...], w_ref[...])
    rope = (j >= COL_QN // IN_TN) & (j < COL_ROPE_END // IN_TN)
    cos = jnp.where(rope, cos_ref[0], 1.0)
    sin = jnp.where(rope, sin_ref[0], 0.0)
    parts = [_apply_rope(acc[:, h * HEAD_DIM:(h + 1) * HEAD_DIM], cos, sin)
             for h in range(IN_TN // HEAD_DIM)]
    z_ref[0] = jnp.concatenate(parts, axis=1).astype(z_ref.dtype)

    @pl.when(j == COL_KC // IN_TN)
    def _():
        zc_ref[0] = acc


def _inproj(x, norm_w, sc, sh, w_main, w_small, cos, sin):
    nb, s, d = x.shape
    grid = (nb, s // IN_TM, Z_MAIN // IN_TN)
    return pl.pallas_call(
        _inproj_kernel,
        out_shape=(jax.ShapeDtypeStruct((nb, s, Z_MAIN), jnp.bfloat16),
                   jax.ShapeDtypeStruct((nb, s, Z_SMALL), jnp.float32),
                   jax.ShapeDtypeStruct((nb, s, IN_TN), jnp.float32)),
        grid=grid,
        in_specs=[
            pl.BlockSpec((1, IN_TM, d), lambda b, i, j: (b, i, 0)),
            pl.BlockSpec((1, d), lambda b, i, j: (0, 0)),
            pl.BlockSpec((1, 1, d), lambda b, i, j: (b, 0, 0)),
            pl.BlockSpec((1, 1, d), lambda b, i, j: (b, 0, 0)),
            pl.BlockSpec((IN_TN, d), lambda b, i, j: (j, 0)),
            pl.BlockSpec((Z_SMALL, d), lambda b, i, j: (0, 0)),
            pl.BlockSpec((1, IN_TM, HEAD_DIM), lambda b, i, j: (b, i, 0)),
            pl.BlockSpec((1, IN_TM, HEAD_DIM), lambda b, i, j: (b, i, 0)),
        ],
        out_specs=(pl.BlockSpec((1, IN_TM, IN_TN), lambda b, i, j: (b, i, j)),
                   pl.BlockSpec((1, IN_TM, Z_SMALL), lambda b, i, j: (b, i, 0)),
                   pl.BlockSpec((1, IN_TM, IN_TN), lambda b, i, j: (b, i, 0))),
        scratch_shapes=[pltpu.VMEM((IN_TM, d), jnp.bfloat16)],
        compiler_params=_cparams(("arbitrary", "arbitrary", "arbitrary")),
        name="inproj",
    )(x, norm_w.reshape(1, d), sc.reshape(nb, 1, d), sh.reshape(nb, 1, d),
      w_main, w_small, cos, sin)


DEC_T = 512


def _decay_kernel(zs_ref, bf_ref, cum_ref, carry_ref):
    @pl.when(pl.program_id(1) == 0)
    def _():
        carry_ref[...] = jnp.zeros_like(carry_ref)

    xt = zs_ref[0].T[0:FOX_HEADS, :] + bf_ref[...]
    lf = jnp.minimum(xt, 0.0) - jnp.log1p(jnp.exp(-jnp.abs(xt)))
    lane = lax.broadcasted_iota(jnp.int32, lf.shape, 1)
    sh = 1
    while sh < DEC_T:
        lf = lf + jnp.where(lane >= sh, pltpu.roll(lf, sh, axis=1), 0.0)
        sh *= 2
    cum = lf + carry_ref[:, 0:1]
    cum_ref[0] = cum
    carry_ref[...] = jnp.broadcast_to(cum[:, DEC_T - 1:DEC_T], carry_ref.shape)


def _decay(z_small, b_fgate):
    nb, s, _ = z_small.shape
    return pl.pallas_call(
        _decay_kernel,
        out_shape=jax.ShapeDtypeStruct((nb, FOX_HEADS, s), jnp.float32),
        grid=(nb, s // DEC_T),
        in_specs=[pl.BlockSpec((1, DEC_T, Z_SMALL), lambda b, i: (b, i, 0)),
                  pl.BlockSpec((FOX_HEADS, 1), lambda b, i: (0, 0))],
        out_specs=pl.BlockSpec((1, FOX_HEADS, DEC_T), lambda b, i: (b, 0, i)),
        scratch_shapes=[pltpu.VMEM((FOX_HEADS, 128), jnp.float32)],
        compiler_params=_cparams(("arbitrary", "arbitrary")),
        name="decay",
    )(z_small, b_fgate.reshape(FOX_HEADS, 1))


def _compress_kernel(x_ref, w1_ref, w2_ref, pos_ref, cos_ref, sin_ref, o_ref):
    kind = pl.program_id(1)
    nc = x_ref.shape[1] // CMP_STRIDE
    p = jnp.zeros((nc, CMP_HIDDEN), jnp.float32)
    q = jnp.zeros((nc, CMP_HIDDEN), jnp.float32)
    for l in range(CMP_STRIDE):
        xl = x_ref[0, pl.ds(l, nc, stride=CMP_STRIDE), :].astype(jnp.bfloat16)
        p = p + _dot(xl, w1_ref[0, l])
        q = q + _dot(xl, w1_ref[0, CMP_STRIDE + l])
    posb = jnp.broadcast_to(pos_ref[...], (8, CMP_LEN * HEAD_DIM)).astype(jnp.bfloat16)
    w1_flat = w1_ref[0].reshape(CMP_LEN * HEAD_DIM, CMP_HIDDEN)
    pterm = _dot(posb, w1_flat)[0:1, :]
    h = p + pltpu.roll(q, nc - 1, axis=0) + pterm
    a = jax.nn.gelu(h).astype(jnp.bfloat16)
    out = _dot(a, w2_ref[0])

    @pl.when(kind < NSA_KV_HEADS)
    def _():
        o_ref[0, 0] = _apply_rope(out, cos_ref[0], sin_ref[0]).astype(o_ref.dtype)

    @pl.when(kind >= NSA_KV_HEADS)
    def _():
        o_ref[0, 0] = out.astype(o_ref.dtype)


def _compress(zc, w1, w2, cmp_pos, cos_c, sin_c):
    nb, s, _ = zc.shape
    nk = 2 * NSA_KV_HEADS
    nc = s // CMP_STRIDE
    return pl.pallas_call(
        _compress_kernel,
        out_shape=jax.ShapeDtypeStruct((nb, nk, nc, HEAD_DIM), jnp.bfloat16),
        grid=(nb, nk),
        in_specs=[
            pl.BlockSpec((1, s, HEAD_DIM), lambda b, k: (b, 0, k)),
            pl.BlockSpec((1, CMP_LEN, HEAD_DIM, CMP_HIDDEN),
                         lambda b, k: (k // NSA_KV_HEADS, 0, 0, 0)),
            pl.BlockSpec((1, CMP_HIDDEN, HEAD_DIM), lambda b, k: (k // NSA_KV_HEADS, 0, 0)),
            pl.BlockSpec((1, CMP_LEN * HEAD_DIM), lambda b, k: (0, 0)),
            pl.BlockSpec((1, nc, HEAD_DIM), lambda b, k: (b, 0, 0)),
            pl.BlockSpec((1, nc, HEAD_DIM), lambda b, k: (b, 0, 0)),
        ],
        out_specs=pl.BlockSpec((1, 1, nc, HEAD_DIM), lambda b, k: (b, k, 0, 0)),
        compiler_params=_cparams(("arbitrary", "arbitrary")),
        name="compress",
    )(zc, w1, w2, cmp_pos.reshape(1, CMP_LEN * HEAD_DIM), cos_c, sin_c)


FOX_T = 512


def _scaled_q(q):
    return (q.astype(jnp.float32) * (SCALE * LOG2E)).astype(jnp.bfloat16)


FOX_HB = 2


def _fox_kernel(q_ref, k_ref, v_ref, cum_ref, o_ref):
    hb = pl.program_id(1)
    i = pl.program_id(2)
    t = FOX_T
    cols = [slice(n * HEAD_DIM, (n + 1) * HEAD_DIM) for n in range(FOX_HB)]
    qs = [_scaled_q(q_ref[0, :, c]) for c in cols]

    def tile(k0, width, carry, diag):
        k0 = pl.multiple_of(k0, t)
        out = []
        for n, (m, l, acc) in enumerate(carry):
            k = k_ref[0, pl.ds(k0, width), cols[n]]
            v = v_ref[0, pl.ds(k0, width), cols[n]]
            ck = cum_ref[0, pl.ds(hb * FOX_HB + n, 1), pl.ds(k0, width)] * LOG2E
            s = _dot_nt(qs[n], k) - ck
            if diag:
                r = lax.broadcasted_iota(jnp.int32, s.shape, 0)
                c = lax.broadcasted_iota(jnp.int32, s.shape, 1)
                s = jnp.where(c <= r + (width - t), s, MASK_VALUE)
            m_new = jnp.maximum(m, jnp.max(s, axis=1, keepdims=True))
            alpha = jnp.exp2(m - m_new)
            p = jnp.exp2(s - m_new)
            l = alpha * l + jnp.sum(p, axis=1, keepdims=True)
            acc = alpha * acc + _dot(p.astype(jnp.bfloat16), v)
            out.append((m_new, l, acc))
        return tuple(out)

    init = tuple((jnp.full((t, 1), MASK_VALUE, jnp.float32),
                  jnp.zeros((t, 1), jnp.float32),
                  jnp.zeros((t, HEAD_DIM), jnp.float32)) for _ in range(FOX_HB))
    n_pairs = i >> 1
    carry = lax.fori_loop(0, n_pairs, lambda p, c: tile(p * (2 * t), 2 * t, c, False), init)
    k_tail = n_pairs * (2 * t)
    carry = lax.cond((i & 1) == 0,
                     lambda c: tile(k_tail, t, c, True),
                     lambda c: tile(k_tail, 2 * t, c, True), carry)
    for n, (_, l, acc) in enumerate(carry):
        o_ref[0, :, cols[n]] = (acc * (1.0 / l)).astype(o_ref.dtype)


def _fox(z_main, cum):
    nb, s, _ = z_main.shape
    w = FOX_HB * HEAD_DIM
    grid = (nb, FOX_HEADS // FOX_HB, s // FOX_T)
    return pl.pallas_call(
        _fox_kernel,
        out_shape=jax.ShapeDtypeStruct((nb, s, FOX_W), jnp.bfloat16),
        grid=grid,
        in_specs=[
            pl.BlockSpec((1, FOX_T, w), lambda b, h, i: (b, i, h)),
            pl.BlockSpec((1, s, w), lambda b, h, i: (b, 0, FOX_W // w + h)),
            pl.BlockSpec((1, s, w), lambda b, h, i: (b, 0, 2 * FOX_W // w + h)),
            pl.BlockSpec((1, FOX_HEADS, s), lambda b, h, i: (b, 0, 0)),
        ],
        out_specs=pl.BlockSpec((1, FOX_T, w), lambda b, h, i: (b, i, h)),
        compiler_params=_cparams(("arbitrary", "arbitrary", "arbitrary")),
        name="fox",
    )(z_main, z_main, z_main, cum)


NSA_TQ = 256
NSA_ROWS = NSA_GROUP * NSA_TQ
NSEL_PAD = 128
SLC_SHIFT = SLC_LEN.bit_length() - 1
assert 1 << SLC_SHIFT == SLC_LEN
assert WINDOW % NSA_TQ == 0 and NSA_TQ % SLC_LEN == 0
SUBLANES = 8


def _block_ranks(score, n_slc):
    tq = score.shape[1]
    groups = [score[SUBLANES * v:SUBLANES * (v + 1)] for v in range(n_slc // SUBLANES)]
    ranks = [jnp.zeros((SUBLANES, tq), jnp.float32) for _ in groups]
    for m in range(n_slc):
        row = jnp.broadcast_to(score[m:m + 1, :], (SUBLANES, tq))
        for v, grp in enumerate(groups):
            lo = SUBLANES * v
            if lo > m:
                before = row >= grp
            elif lo + SUBLANES - 1 <= m:
                before = row > grp
            else:
                n_i = lo + lax.broadcasted_iota(jnp.int32, grp.shape, 0)
                before = (row > grp) | ((row == grp) & (n_i > m))
            ranks[v] = ranks[v] + jnp.where(before, 1.0, 0.0)
    return jnp.concatenate(ranks, axis=0)


def _nsa_kernel(q_ref, zs_ref, kc_ref, vc_ref, ks_ref, vs_ref, kw_ref, vw_ref, ovt_ref,
                o_ref, ka_ref, *, seq):
    g = pl.program_id(1)
    i = pl.program_id(2)
    tq = NSA_TQ
    q0 = i * tq

    @pl.when(i == 0)
    def _():
        ka_ref[:, 0:HEAD_DIM] = ks_ref[0]
        tok = lax.broadcasted_iota(jnp.int32, (seq, NSEL_PAD), 0)
        n_i = lax.broadcasted_iota(jnp.int32, (seq, NSEL_PAD), 1)
        ka_ref[:, HEAD_DIM:HEAD_DIM + NSEL_PAD] = jnp.where(
            n_i == (tok >> SLC_SHIFT), 1.0, 0.0).astype(jnp.bfloat16)

    qb = q_ref[0]
    q4 = _scaled_q(jnp.concatenate(
        [qb[:, r * HEAD_DIM:(r + 1) * HEAD_DIM] for r in range(NSA_GROUP)], axis=0))

    def row_query(shape):
        return lax.broadcasted_iota(jnp.int32, shape, 0) & (tq - 1)

    kc = kc_ref[0, 0]
    vc = vc_ref[0, 0]
    s_c = _dot_nt(q4, kc)
    cend = lax.broadcasted_iota(jnp.int32, s_c.shape, 1) * CMP_STRIDE + (CMP_LEN - 1)
    mask_c = cend <= q0 + row_query(s_c.shape)
    s_c = jnp.where(mask_c, s_c, MASK_VALUE)
    m_c = jnp.maximum(jnp.max(s_c, axis=1, keepdims=True), 0.1 * MASK_VALUE)
    e_c = jnp.exp2(s_c - m_c)
    p_c = e_c * (1.0 / jnp.maximum(jnp.sum(e_c, axis=1, keepdims=True), 1e-30))
    o_c = _dot(p_c.astype(jnp.bfloat16), vc)

    psum = p_c[0:tq]
    for r in range(1, NSA_GROUP):
        psum = psum + p_c[r * tq:(r + 1) * tq]
    p_hi = psum.astype(jnp.bfloat16)
    p_lo = (psum - p_hi.astype(jnp.float32)).astype(jnp.bfloat16)
    ovt = ovt_ref[...]
    imp_t = _dot_nt(ovt, p_hi) + _dot_nt(ovt, p_lo)
    n_slc = seq // SLC_LEN
    blk = lax.broadcasted_iota(jnp.int32, imp_t.shape, 0)
    cur = (q0 + lax.broadcasted_iota(jnp.int32, imp_t.shape, 1)) >> SLC_SHIFT
    forced = (blk == 0) | (blk == cur) | (blk == cur - 1)
    score = jnp.where(forced, SEL_BONUS, imp_t)
    score = jnp.where(blk <= cur, score, -SEL_BONUS)
    rank = _block_ranks(score, n_slc)
    bias_t = jnp.where(rank < min(SLC_TOPK, n_slc), 0.0, MASK_VALUE)
    bias_t = jnp.concatenate(
        [bias_t, jnp.zeros((NSEL_PAD - n_slc, tq), jnp.float32)], axis=0)
    bias = bias_t.T.astype(jnp.bfloat16)
    qa = jnp.concatenate([q4, jnp.concatenate([bias] * NSA_GROUP, axis=0)], axis=1)

    def sel_tile(k0, width, carry, diag):
        m, l, acc = carry
        k0 = pl.multiple_of(k0, tq)
        s = _dot_nt(qa, ka_ref[pl.ds(k0, width), :])
        if diag:
            c = lax.broadcasted_iota(jnp.int32, s.shape, 1)
            s = jnp.where(c <= row_query(s.shape) + (width - tq), s, MASK_VALUE)
        m_new = jnp.maximum(m, jnp.max(s, axis=1, keepdims=True))
        alpha = jnp.exp2(m - m_new)
        p = jnp.exp2(s - m_new)
        l = alpha * l + jnp.sum(p, axis=1, keepdims=True)
        acc = alpha * acc + _dot(p.astype(jnp.bfloat16), vs_ref[0, pl.ds(k0, width), :])
        return m_new, l, acc

    init = (jnp.full((NSA_ROWS, 1), MASK_VALUE, jnp.float32),
            jnp.zeros((NSA_ROWS, 1), jnp.float32),
            jnp.zeros((NSA_ROWS, HEAD_DIM), jnp.float32))
    n_pairs = i >> 1
    carry = lax.fori_loop(0, n_pairs,
                          lambda p, c: sel_tile(p * (2 * tq), 2 * tq, c, False), init)
    k_tail = n_pairs * (2 * tq)
    _, l_s, acc_s = lax.cond((i & 1) == 0,
                             lambda c: sel_tile(k_tail, tq, c, True),
                             lambda c: sel_tile(k_tail, 2 * tq, c, True), carry)
    o_s = acc_s * (1.0 / l_s)

    n_w = WINDOW // tq + 1
    s_w, v_w = [], []
    for t in range(n_w):
        start = q0 - WINDOW + t * tq
        st = pl.multiple_of(jnp.maximum(start, 0), tq)
        s = _dot_nt(q4, kw_ref[0, pl.ds(st, tq), :])
        v_w.append(vw_ref[0, pl.ds(st, tq), :])
        c = lax.broadcasted_iota(jnp.int32, s.shape, 1)
        if t == 0:
            s = jnp.where(c > row_query(s.shape), s, MASK_VALUE)
        if t == n_w - 1:
            s = jnp.where(c <= row_query(s.shape), s, MASK_VALUE)
        else:
            s = jnp.where(start >= 0, s, MASK_VALUE)
        s_w.append(s)
    m_w = s_w[0].max(axis=1, keepdims=True)
    for s in s_w[1:]:
        m_w = jnp.maximum(m_w, s.max(axis=1, keepdims=True))
    l_w = jnp.zeros((NSA_ROWS, 1), jnp.float32)
    acc_w = jnp.zeros((NSA_ROWS, HEAD_DIM), jnp.float32)
    for s, v in zip(s_w, v_w):
        p = jnp.exp2(s - m_w)
        l_w = l_w + jnp.sum(p, axis=1, keepdims=True)
        acc_w = acc_w + _dot(p.astype(jnp.bfloat16), v)
    o_w = acc_w * (1.0 / l_w)

    gates = jax.nn.sigmoid(zs_ref[0])
    lane = lax.broadcasted_iota(jnp.int32, gates.shape, 1)
    outs = []
    for r in range(NSA_GROUP):
        rows = slice(r * tq, (r + 1) * tq)
        o_r = jnp.zeros((tq, HEAD_DIM), jnp.float32)
        for br, o_b in enumerate((o_c, o_s, o_w)):
            col = FOX_HEADS + 3 * (g * NSA_GROUP + r) + br
            gcol = jnp.sum(jnp.where(lane == col, gates, 0.0), axis=1, keepdims=True)
            o_r = o_r + gcol * o_b[rows]
        outs.append(o_r)
    o_ref[0] = jnp.concatenate(outs, axis=1).astype(o_ref.dtype)


def _nsa(z_main, z_small, kvc, ovt):
    nb, s, _ = z_main.shape
    nc = kvc.shape[2]
    gw = NSA_GROUP * HEAD_DIM
    full = lambda col0: pl.BlockSpec(
        (1, s, HEAD_DIM), lambda b, g, i, c=col0 // HEAD_DIM: (b, 0, c + g))
    return pl.pallas_call(
        functools.partial(_nsa_kernel, seq=s),
        out_shape=jax.ShapeDtypeStruct((nb, s, NSA_W), jnp.bfloat16),
        grid=(nb, NSA_KV_HEADS, s // NSA_TQ),
        in_specs=[
            pl.BlockSpec((1, NSA_TQ, gw), lambda b, g, i: (b, i, COL_QN // gw + g)),
            pl.BlockSpec((1, NSA_TQ, Z_SMALL), lambda b, g, i: (b, i, 0)),
            pl.BlockSpec((1, 1, nc, HEAD_DIM), lambda b, g, i: (b, g, 0, 0)),
            pl.BlockSpec((1, 1, nc, HEAD_DIM), lambda b, g, i: (b, NSA_KV_HEADS + g, 0, 0)),
            full(COL_KS), full(COL_VS), full(COL_KW), full(COL_VW),
            pl.BlockSpec(ovt.shape, lambda b, g, i: (0, 0)),
        ],
        out_specs=pl.BlockSpec((1, NSA_TQ, gw), lambda b, g, i: (b, i, g)),
        scratch_shapes=[pltpu.VMEM((s, HEAD_DIM + NSEL_PAD), jnp.bfloat16)],
        compiler_params=_cparams(("arbitrary", "arbitrary", "arbitrary")),
        name="nsa",
    )(z_main, z_small, kvc, kvc, z_main, z_main, z_main, z_main, ovt)


OUT_TM = 512
OUT_RC = 256


def _outproj_kernel(of_ref, on_ref, bf_ref, bn_ref, w_ref, x_ref, g_ref, nw_ref, sc_ref,
                    sh_ref, h_ref, u_ref):
    for r in range(OUT_TM // OUT_RC):
        rows = slice(r * OUT_RC, (r + 1) * OUT_RC)
        yf = _rms(of_ref[0, rows, :].astype(jnp.float32)) * bf_ref[...]
        yn = _rms(on_ref[0, rows, :].astype(jnp.float32)) * bn_ref[...]
        y = jnp.concatenate([yf.astype(jnp.bfloat16), yn.astype(jnp.bfloat16)], axis=1)
        h = x_ref[0, rows, :] + g_ref[0] * _dot(y, w_ref[...])
        h_ref[0, rows, :] = h
        u = _rms(h) * nw_ref[...]
        u_ref[0, rows, :] = (u * (1.0 + sc_ref[0]) + sh_ref[0]).astype(jnp.bfloat16)


def _outproj(o_f, o_n, beta_f, beta_n, w_out, x, gate, norm_w, sc, sh):
    nb, s, d = x.shape
    vec = lambda n: pl.BlockSpec((1, n), lambda b, i: (0, 0))
    bvec = pl.BlockSpec((1, 1, d), lambda b, i: (b, 0, 0))
    rows = lambda n: pl.BlockSpec((1, OUT_TM, n), lambda b, i: (b, i, 0))
    return pl.pallas_call(
        _outproj_kernel,
        out_shape=(jax.ShapeDtypeStruct((nb, s, d), jnp.float32),
                   jax.ShapeDtypeStruct((nb, s, d), jnp.bfloat16)),
        grid=(nb, s // OUT_TM),
        in_specs=[
            rows(FOX_W), rows(NSA_W), vec(FOX_W), vec(NSA_W),
            pl.BlockSpec((FOX_W + NSA_W, d), lambda b, i: (0, 0),
                         pipeline_mode=pl.Buffered(1)),
            rows(d), bvec, vec(d), bvec, bvec,
        ],
        out_specs=(rows(d), rows(d)),
        compiler_params=_cparams(("arbitrary", "arbitrary")),
        name="outproj",
    )(o_f, o_n, beta_f.reshape(1, FOX_W), beta_n.reshape(1, NSA_W), w_out, x,
      gate.reshape(nb, 1, d), norm_w.reshape(1, d), sc.reshape(nb, 1, d),
      sh.reshape(nb, 1, d))


FFN_TM = 1024
FFN_TF = 512
FFN_RC = 512


def _ffn_kernel(u_ref, h_ref, g_ref, fw_ref, wg_ref, wu_ref, wd_ref, o_ref, *, final):
    j = pl.program_id(2)

    @pl.when(j == 0)
    def _():
        o_ref[...] = jnp.zeros_like(o_ref)

    for r in range(FFN_TM // FFN_RC):
        rows = slice(r * FFN_RC, (r + 1) * FFN_RC)
        u = u_ref[0, rows, :]
        a = _dot(u, wg_ref[...])
        b = _dot(u, wu_ref[...])
        mid = (a * jax.nn.sigmoid(a) * b).astype(jnp.bfloat16)
        o_ref[0, rows, :] += _dot(mid, wd_ref[...])

    @pl.when(j == pl.num_programs(2) - 1)
    def _():
        h2 = h_ref[0] + g_ref[0] * o_ref[0]
        o_ref[0] = _rms(h2) * fw_ref[...] if final else h2


def _ffn(u2, h1, gate, final_w, w_gate, w_up, w_down, final):
    nb, s, d = h1.shape
    f = w_gate.shape[1]
    return pl.pallas_call(
        functools.partial(_ffn_kernel, final=final),
        out_shape=jax.ShapeDtypeStruct((nb, s, d), jnp.float32),
        grid=(nb, s // FFN_TM, f // FFN_TF),
        in_specs=[
            pl.BlockSpec((1, FFN_TM, d), lambda b, i, j: (b, i, 0)),
            pl.BlockSpec((1, FFN_TM, d), lambda b, i, j: (b, i, 0),
                         pipeline_mode=pl.Buffered(1)),
            pl.BlockSpec((1, 1, d), lambda b, i, j: (b, 0, 0)),
            pl.BlockSpec((1, d), lambda b, i, j: (0, 0)),
            pl.BlockSpec((d, FFN_TF), lambda b, i, j: (0, j)),
            pl.BlockSpec((d, FFN_TF), lambda b, i, j: (0, j)),
            pl.BlockSpec((FFN_TF, d), lambda b, i, j: (j, 0)),
        ],
        out_specs=pl.BlockSpec((1, FFN_TM, d), lambda b, i, j: (b, i, 0)),
        compiler_params=pltpu.CompilerParams(
            dimension_semantics=("arbitrary", "arbitrary", "arbitrary"),
            vmem_limit_bytes=FFN_VMEM_LIMIT),
        name="ffn",
    )(u2, h1, gate.reshape(nb, 1, d), final_w.reshape(1, d), w_gate, w_up, w_down)


def _split_w_in(w_in):
    sizes = ([FOX_W] * 3 + [FOX_HEADS] + [NSA_W] + [KV_W] * 6 + [3 * NSA_HEADS])
    offs = [0]
    for sz in sizes:
        offs.append(offs[-1] + sz)
    wt = w_in.T
    piece = lambda n: wt[offs[n]:offs[n + 1]].astype(jnp.bfloat16)
    main = jnp.concatenate([piece(n) for n in (0, 1, 2, 4, 7, 9, 8, 10, 5, 6)], axis=0)
    small = jnp.concatenate(
        [piece(3), piece(11),
         jnp.zeros((Z_SMALL - FOX_HEADS - 3 * NSA_HEADS, w_in.shape[0]), jnp.bfloat16)], axis=0)
    return main, small


def _overlap_t(n_cmp_pad, n_slc):
    cs = jnp.arange(n_cmp_pad) * CMP_STRIDE
    ss = jnp.arange(n_slc) * SLC_LEN
    ov = (jnp.minimum(cs[None, :] + CMP_LEN, ss[:, None] + SLC_LEN)
          - jnp.maximum(cs[None, :], ss[:, None]))
    ov = jnp.clip(ov, 0).astype(jnp.float32) / CMP_STRIDE
    return ov.astype(jnp.bfloat16)


def kernel(x, c, positions, w_ada, b_ada, norm_attn, norm_ffn, w_in, b_fgate, cmp_pos,
           w_kc1, w_kc2, w_vc1, w_vc2, beta_fox, beta_nsa, w_out, w_gate, w_up, w_down,
           final_norm):
    nb, s, d = x.shape
    depth = w_ada.shape[0]
    n_cmp_pad = s // CMP_STRIDE
    cmp_end = jnp.arange(n_cmp_pad) * CMP_STRIDE + (CMP_LEN - 1)
    pos_c = jnp.take(positions, jnp.minimum(cmp_end, s - 1), axis=1)
    cos, sin = _rope_tables(positions, 512)
    cos_c, sin_c = _rope_tables(pos_c, n_cmp_pad)
    ovt = _overlap_t(n_cmp_pad, s // SLC_LEN)

    h = x
    for l in range(depth):
        mod = _ada(c, w_ada[l], b_ada[l])
        sh1, sc1, g1, sh2, sc2, g2 = jnp.split(mod, 6, axis=-1)
        w_main, w_small = _split_w_in(w_in[l])
        z_main, z_small, zc = _inproj(h, norm_attn[l], sc1, sh1, w_main, w_small, cos, sin)
        cum = _decay(z_small, b_fgate[l])

        w1 = jnp.stack([w_kc1[l], w_vc1[l]]).astype(jnp.bfloat16)
        w1 = w1.reshape(2, CMP_LEN, HEAD_DIM, CMP_HIDDEN)
        w2 = jnp.stack([w_kc2[l], w_vc2[l]]).astype(jnp.bfloat16)
        kvc = _compress(zc, w1, w2, cmp_pos[l], cos_c, sin_c)

        o_f = _fox(z_main, cum)
        o_n = _nsa(z_main, z_small, kvc, ovt)
        h1, u2 = _outproj(o_f, o_n, beta_fox[l], beta_nsa[l], w_out[l].astype(jnp.bfloat16),
                          h, g1, norm_ffn[l], sc2, sh2)
        h = _ffn(u2, h1, g2, final_norm, w_gate[l].astype(jnp.bfloat16),
                 w_up[l].astype(jnp.bfloat16), w_down[l].astype(jnp.bfloat16),
                 final=(l == depth - 1))
    return h
```

```python
import functools

import jax
import jax.numpy as jnp
from jax import lax
from jax.experimental import pallas as pl
from jax.experimental.pallas import tpu as pltpu

D_MODEL = 2048
HEAD_DIM = 128
FOX_HEADS = 8
NSA_HEADS = 8
NSA_KV_HEADS = 2
NSA_GROUP = NSA_HEADS // NSA_KV_HEADS
CMP_LEN = 32
CMP_STRIDE = 16
CMP_HIDDEN = 2 * HEAD_DIM
SLC_LEN = 64
SLC_TOPK = 16
WINDOW = 512
D_FF = 5632
ROPE_THETA = 10000.0
NORM_EPS = 1e-6
MASK_VALUE = -1e30
SEL_BONUS = 1e6

FOX_W = FOX_HEADS * HEAD_DIM
NSA_W = NSA_HEADS * HEAD_DIM
KV_W = NSA_KV_HEADS * HEAD_DIM
Z_MAIN = 3 * FOX_W + NSA_W + 6 * KV_W
Z_SMALL = 128
COL_QN = 3 * FOX_W
COL_KS = COL_QN + NSA_W
COL_KW = COL_KS + KV_W
COL_VS = COL_KW + KV_W
COL_VW = COL_VS + KV_W
COL_KC = COL_VW + KV_W
COL_ROPE_END = COL_VS

VMEM_LIMIT = 48 * 1024 * 1024
FFN_VMEM_LIMIT = 56 * 1024 * 1024
SCALE = HEAD_DIM ** -0.5
LOG2E = 1.4426950408889634


def _cparams(sem):
    return pltpu.CompilerParams(dimension_semantics=sem, vmem_limit_bytes=VMEM_LIMIT)


def _dot(a, b):
    return jnp.dot(a, b, preferred_element_type=jnp.float32)


def _dot_nt(a, b):
    return lax.dot_general(a, b, (((1,), (1,)), ((), ())), preferred_element_type=jnp.float32)


def _rms(x):
    return x * lax.rsqrt(jnp.mean(x * x, axis=-1, keepdims=True) + NORM_EPS)


ADA_TN = 1024
ADA_KC = 256


def _ada_kernel(ct_ref, w_ref, b_ref, o_ref, *, nb):
    tn = w_ref.shape[1]
    ct = ct_ref[...]
    st = ct * jax.nn.sigmoid(ct)

    accs = [jnp.zeros((8, tn), jnp.float32) for _ in range(nb)]
    for kc in range(D_MODEL // ADA_KC):
        w = w_ref[kc * ADA_KC:(kc + 1) * ADA_KC, :]
        for b in range(nb):
            s = st[kc * ADA_KC:(kc + 1) * ADA_KC, b:b + 1]
            accs[b] = accs[b] + jnp.sum((w * s).reshape(ADA_KC // 8, 8, tn), axis=0)
    for b in range(nb):
        o_ref[b:b + 1, :] = jnp.sum(accs[b], axis=0, keepdims=True) + b_ref[...]


def _ada(c, w_ada, b_ada):
    nb = c.shape[0]
    n = w_ada.shape[1]
    return pl.pallas_call(
        functools.partial(_ada_kernel, nb=nb),
        out_shape=jax.ShapeDtypeStruct((nb, n), jnp.float32),
        grid=(n // ADA_TN,),
        in_specs=[
            pl.BlockSpec((D_MODEL, nb), lambda j: (0, 0)),
            pl.BlockSpec((D_MODEL, ADA_TN), lambda j: (0, j)),
            pl.BlockSpec((1, ADA_TN), lambda j: (0, j)),
        ],
        out_specs=pl.BlockSpec((nb, ADA_TN), lambda j: (0, j)),
        compiler_params=_cparams(("arbitrary",)),
        name="ada",
    )(c.T, w_ada, b_ada.reshape(1, n))


def _rope_table_kernel(pos_ref, inv_ref, cos_ref, sin_ref):
    ang = pos_ref[0].astype(jnp.float32) * inv_ref[...]
    lane = lax.broadcasted_iota(jnp.int32, ang.shape, 1)
    cos_ref[0] = jnp.cos(ang)
    sin_ref[0] = jnp.where(lane < HEAD_DIM // 2, -1.0, 1.0) * jnp.sin(ang)


def _rope_tables(pos, ts):
    nb, n = pos.shape
    inv = ROPE_THETA ** (-jnp.arange(0, HEAD_DIM, 2, dtype=jnp.float32) / HEAD_DIM)
    inv2 = jnp.concatenate([inv, inv]).reshape(1, HEAD_DIM)
    shp = jax.ShapeDtypeStruct((nb, n, HEAD_DIM), jnp.float32)
    spec = pl.BlockSpec((1, ts, HEAD_DIM), lambda b, i: (b, i, 0))
    return pl.pallas_call(
        _rope_table_kernel,
        out_shape=(shp, shp),
        grid=(nb, n // ts),
        in_specs=[pl.BlockSpec((1, ts, 1), lambda b, i: (b, i, 0)),
                  pl.BlockSpec((1, HEAD_DIM), lambda b, i: (0, 0))],
        out_specs=(spec, spec),
        compiler_params=_cparams(("arbitrary", "arbitrary")),
        name="rope_tables",
    )(pos.reshape(nb, n, 1), inv2)


def _apply_rope(x, cos, sin):
    return x * cos + pltpu.roll(x, HEAD_DIM // 2, axis=1) * sin


IN_TM = 1024
IN_TN = 512


def _inproj_kernel(x_ref, g_ref, sc_ref, sh_ref, w_ref, ws_ref, cos_ref, sin_ref,
                   z_ref, zs_ref, zc_ref, u_ref):
    j = pl.program_id(2)

    @pl.when(j == 0)
    def _():
        u = _rms(x_ref[0]) * g_ref[...]
        u = u * (1.0 + sc_ref[0]) + sh_ref[0]
        ub = u.astype(jnp.bfloat16)
        u_ref[...] = ub
        zs_ref[0] = _dot_nt(ub, ws_ref[...])

    acc = _dot_nt(u_ref[...], w_ref[...])
    rope = (j >= COL_QN // IN_TN) & (j < COL_ROPE_END // IN_TN)

    @pl.when(rope)
    def _():
        cos = cos_ref[0]
        sin = sin_ref[0]
        parts = [_apply_rope(acc[:, h * HEAD_DIM:(h + 1) * HEAD_DIM], cos, sin)
                 for h in range(IN_TN // HEAD_DIM)]
        z_ref[0] = jnp.concatenate(parts, axis=1).astype(z_ref.dtype)

    @pl.when(jnp.logical_not(rope))
    def _():
        z_ref[0] = acc.astype(z_ref.dtype)

    @pl.when(j == COL_KC // IN_TN)
    def _():
        zc_ref[0] = acc


def _inproj(x, norm_w, sc, sh, w_main, w_small, cos, sin):
    nb, s, d = x.shape
    grid = (nb, s // IN_TM, Z_MAIN // IN_TN)
    return pl.pallas_call(
        _inproj_kernel,
        out_shape=(jax.ShapeDtypeStruct((nb, s, Z_MAIN), jnp.bfloat16),
                   jax.ShapeDtypeStruct((nb, s, Z_SMALL), jnp.float32),
                   jax.ShapeDtypeStruct((nb, s, IN_TN), jnp.float32)),
        grid=grid,
        in_specs=[
            pl.BlockSpec((1, IN_TM, d), lambda b, i, j: (b, i, 0)),
            pl.BlockSpec((1, d), lambda b, i, j: (0, 0)),
            pl.BlockSpec((1, 1, d), lambda b, i, j: (b, 0, 0)),
            pl.BlockSpec((1, 1, d), lambda b, i, j: (b, 0, 0)),
            pl.BlockSpec((IN_TN, d), lambda b, i, j: (j, 0)),
            pl.BlockSpec((Z_SMALL, d), lambda b, i, j: (0, 0)),
            pl.BlockSpec((1, IN_TM, HEAD_DIM), lambda b, i, j: (b, i, 0)),
            pl.BlockSpec((1, IN_TM, HEAD_DIM), lambda b, i, j: (b, i, 0)),
        ],
        out_specs=(pl.BlockSpec((1, IN_TM, IN_TN), lambda b, i, j: (b, i, j)),
                   pl.BlockSpec((1, IN_TM, Z_SMALL), lambda b, i, j: (b, i, 0)),
                   pl.BlockSpec((1, IN_TM, IN_TN), lambda b, i, j: (b, i, 0))),
        scratch_shapes=[pltpu.VMEM((IN_TM, d), jnp.bfloat16)],
        compiler_params=_cparams(("arbitrary", "arbitrary", "arbitrary")),
        name="inproj",
    )(x, norm_w.reshape(1, d), sc.reshape(nb, 1, d), sh.reshape(nb, 1, d),
      w_main, w_small, cos, sin)


DEC_T = 512


def _decay_kernel(zs_ref, bf_ref, cum_ref, carry_ref):
    @pl.when(pl.program_id(1) == 0)
    def _():
        carry_ref[...] = jnp.zeros_like(carry_ref)

    xt = zs_ref[0].T[0:FOX_HEADS, :] + bf_ref[...]
    lf = jnp.minimum(xt, 0.0) - jnp.log1p(jnp.exp(-jnp.abs(xt)))
    lane = lax.broadcasted_iota(jnp.int32, lf.shape, 1)
    sh = 1
    while sh < DEC_T:
        lf = lf + jnp.where(lane >= sh, pltpu.roll(lf, sh, axis=1), 0.0)
        sh *= 2
    cum = lf + carry_ref[:, 0:1]
    cum_ref[0] = cum
    carry_ref[...] = jnp.broadcast_to(cum[:, DEC_T - 1:DEC_T], carry_ref.shape)


def _decay(z_small, b_fgate):
    nb, s, _ = z_small.shape
    return pl.pallas_call(
        _decay_kernel,
        out_shape=jax.ShapeDtypeStruct((nb, FOX_HEADS, s), jnp.float32),
        grid=(nb, s // DEC_T),
        in_specs=[pl.BlockSpec((1, DEC_T, Z_SMALL), lambda b, i: (b, i, 0)),
                  pl.BlockSpec((FOX_HEADS, 1), lambda b, i: (0, 0))],
        out_specs=pl.BlockSpec((1, FOX_HEADS, DEC_T), lambda b, i: (b, 0, i)),
        scratch_shapes=[pltpu.VMEM((FOX_HEADS, 128), jnp.float32)],
        compiler_params=_cparams(("arbitrary", "arbitrary")),
        name="decay",
    )(z_small, b_fgate.reshape(FOX_HEADS, 1))


def _compress_kernel(x_ref, w1_ref, w2_ref, pos_ref, cos_ref, sin_ref, o_ref):
    kind = pl.program_id(1)
    nc = x_ref.shape[1] // CMP_STRIDE
    p = jnp.zeros((nc, CMP_HIDDEN), jnp.float32)
    q = jnp.zeros((nc, CMP_HIDDEN), jnp.float32)
    for l in range(CMP_STRIDE):
        xl = x_ref[0, pl.ds(l, nc, stride=CMP_STRIDE), :].astype(jnp.bfloat16)
        p = p + _dot(xl, w1_ref[0, l])
        q = q + _dot(xl, w1_ref[0, CMP_STRIDE + l])
    posb = jnp.broadcast_to(pos_ref[...], (8, CMP_LEN * HEAD_DIM)).astype(jnp.bfloat16)
    w1_flat = w1_ref[0].reshape(CMP_LEN * HEAD_DIM, CMP_HIDDEN)
    pterm = _dot(posb, w1_flat)[0:1, :]
    h = p + pltpu.roll(q, nc - 1, axis=0) + pterm
    a = jax.nn.gelu(h).astype(jnp.bfloat16)
    out = _dot(a, w2_ref[0])

    @pl.when(kind < NSA_KV_HEADS)
    def _():
        o_ref[0, 0] = _apply_rope(out, cos_ref[0], sin_ref[0]).astype(o_ref.dtype)

    @pl.when(kind >= NSA_KV_HEADS)
    def _():
        o_ref[0, 0] = out.astype(o_ref.dtype)


def _compress(zc, w1, w2, cmp_pos, cos_c, sin_c):
    nb, s, _ = zc.shape
    nk = 2 * NSA_KV_HEADS
    nc = s // CMP_STRIDE
    return pl.pallas_call(
        _compress_kernel,
        out_shape=jax.ShapeDtypeStruct((nb, nk, nc, HEAD_DIM), jnp.bfloat16),
        grid=(nb, nk),
        in_specs=[
            pl.BlockSpec((1, s, HEAD_DIM), lambda b, k: (b, 0, k)),
            pl.BlockSpec((1, CMP_LEN, HEAD_DIM, CMP_HIDDEN),
                         lambda b, k: (k // NSA_KV_HEADS, 0, 0, 0)),
            pl.BlockSpec((1, CMP_HIDDEN, HEAD_DIM), lambda b, k: (k // NSA_KV_HEADS, 0, 0)),
            pl.BlockSpec((1, CMP_LEN * HEAD_DIM), lambda b, k: (0, 0)),
            pl.BlockSpec((1, nc, HEAD_DIM), lambda b, k: (b, 0, 0)),
            pl.BlockSpec((1, nc, HEAD_DIM), lambda b, k: (b, 0, 0)),
        ],
        out_specs=pl.BlockSpec((1, 1, nc, HEAD_DIM), lambda b, k: (b, k, 0, 0)),
        compiler_params=_cparams(("arbitrary", "arbitrary")),
        name="compress",
    )(zc, w1, w2, cmp_pos.reshape(1, CMP_LEN * HEAD_DIM), cos_c, sin_c)


FOX_T = 512


def _softmax_step(s, m, acc, v):
    m_new = jnp.maximum(m, jnp.max(s, axis=1, keepdims=True))
    alpha = jnp.exp2(m - m_new)
    p = jnp.exp2((s - m_new).astype(jnp.bfloat16))
    va = jnp.concatenate([v, jnp.ones_like(v)], axis=1)
    return m_new, alpha * acc + _dot(p, va)


def _softmax_init(rows):
    return (jnp.full((rows, 1), MASK_VALUE, jnp.float32),
            jnp.zeros((rows, 2 * HEAD_DIM), jnp.float32))


def _softmax_result(acc):
    return acc[:, 0:HEAD_DIM] * (1.0 / acc[:, HEAD_DIM:2 * HEAD_DIM])


def _scaled_q(q):
    return (q.astype(jnp.float32) * (SCALE * LOG2E)).astype(jnp.bfloat16)


FOX_HB = 2


def _fox_kernel(q_ref, k_ref, v_ref, cum_ref, o_ref):
    hb = pl.program_id(1)
    i = pl.program_id(2)
    t = FOX_T
    cols = [slice(n * HEAD_DIM, (n + 1) * HEAD_DIM) for n in range(FOX_HB)]
    qs = [_scaled_q(q_ref[0, :, c]) for c in cols]

    def tile(k0, width, carry, diag):
        k0 = pl.multiple_of(k0, t)
        logits = [_dot_nt(qs[n], k_ref[0, pl.ds(k0, width), cols[n]]) for n in range(FOX_HB)]
        out = []
        for n, (m, acc) in enumerate(carry):
            v = v_ref[0, pl.ds(k0, width), cols[n]]
            ck = cum_ref[0, pl.ds(hb * FOX_HB + n, 1), pl.ds(k0, width)] * LOG2E
            s = logits[n] - ck
            if diag:
                r = lax.broadcasted_iota(jnp.int32, s.shape, 0)
                c = lax.broadcasted_iota(jnp.int32, s.shape, 1)
                s = jnp.where(c <= r + (width - t), s, MASK_VALUE)
            out.append(_softmax_step(s, m, acc, v))
        return tuple(out)

    init = tuple(_softmax_init(t) for _ in range(FOX_HB))
    n_pairs = i >> 1
    carry = lax.fori_loop(0, n_pairs, lambda p, c: tile(p * (2 * t), 2 * t, c, False), init)
    k_tail = n_pairs * (2 * t)
    carry = lax.cond((i & 1) == 0,
                     lambda c: tile(k_tail, t, c, True),
                     lambda c: tile(k_tail, 2 * t, c, True), carry)
    for n, (_, acc) in enumerate(carry):
        o_ref[0, :, cols[n]] = _softmax_result(acc).astype(o_ref.dtype)


def _fox(z_main, cum):
    nb, s, _ = z_main.shape
    w = FOX_HB * HEAD_DIM
    grid = (nb, FOX_HEADS // FOX_HB, s // FOX_T)
    return pl.pallas_call(
        _fox_kernel,
        out_shape=jax.ShapeDtypeStruct((nb, s, FOX_W), jnp.bfloat16),
        grid=grid,
        in_specs=[
            pl.BlockSpec((1, FOX_T, w), lambda b, h, i: (b, i, h)),
            pl.BlockSpec((1, s, w), lambda b, h, i: (b, 0, FOX_W // w + h)),
            pl.BlockSpec((1, s, w), lambda b, h, i: (b, 0, 2 * FOX_W // w + h)),
            pl.BlockSpec((1, FOX_HEADS, s), lambda b, h, i: (b, 0, 0)),
        ],
        out_specs=pl.BlockSpec((1, FOX_T, w), lambda b, h, i: (b, i, h)),
        compiler_params=_cparams(("arbitrary", "arbitrary", "arbitrary")),
        name="fox",
    )(z_main, z_main, z_main, cum)


NSA_TQ = 256
NSA_PARTS = 2
NSEL_PAD = 128
SLC_SHIFT = SLC_LEN.bit_length() - 1
assert 1 << SLC_SHIFT == SLC_LEN
assert WINDOW % NSA_TQ == 0 and NSA_TQ % SLC_LEN == 0
SUBLANES = 8


def _block_ranks(score, n_slc):
    tq = score.shape[1]
    groups = [score[SUBLANES * v:SUBLANES * (v + 1)] for v in range(n_slc // SUBLANES)]
    ranks = [jnp.zeros((SUBLANES, tq), jnp.float32) for _ in groups]
    for m in range(n_slc):
        row = jnp.broadcast_to(score[m:m + 1, :], (SUBLANES, tq))
        for v, grp in enumerate(groups):
            lo = SUBLANES * v
            if lo > m:
                before = row >= grp
            elif lo + SUBLANES - 1 <= m:
                before = row > grp
            else:
                n_i = lo + lax.broadcasted_iota(jnp.int32, grp.shape, 0)
                before = (row > grp) | ((row == grp) & (n_i > m))
            ranks[v] = ranks[v] + jnp.where(before, 1.0, 0.0)
    return jnp.concatenate(ranks, axis=0)


def _nsa_kernel(q_ref, zs_ref, kc_ref, vc_ref, ks_ref, vs_ref, kw_ref, vw_ref, ovt_ref,
                o_ref, ka_ref, *, seq):
    g = pl.program_id(1)
    i = pl.program_id(2)
    tq = NSA_TQ
    q0 = i * tq

    @pl.when(i == 0)
    def _():
        ka_ref[:, 0:HEAD_DIM] = ks_ref[0]
        tok = lax.broadcasted_iota(jnp.int32, (seq, NSEL_PAD), 0)
        n_i = lax.broadcasted_iota(jnp.int32, (seq, NSEL_PAD), 1)
        ka_ref[:, HEAD_DIM:HEAD_DIM + NSEL_PAD] = jnp.where(
            n_i == (tok >> SLC_SHIFT), 1.0, 0.0).astype(jnp.bfloat16)

    qb = q_ref[0]
    hpp = NSA_GROUP // NSA_PARTS
    parts = range(NSA_PARTS)
    qh = [_scaled_q(jnp.concatenate(
        [qb[:, r * HEAD_DIM:(r + 1) * HEAD_DIM] for r in range(h * hpp, (h + 1) * hpp)], axis=0))
        for h in parts]
    prow = hpp * tq

    def row_query(shape):
        return lax.broadcasted_iota(jnp.int32, shape, 0) & (tq - 1)

    kc = kc_ref[0, 0]
    vc = vc_ref[0, 0]
    s_cs = [_dot_nt(qh[h], kc) for h in parts]
    p_cs = []
    for s_c in s_cs:
        cend = lax.broadcasted_iota(jnp.int32, s_c.shape, 1) * CMP_STRIDE + (CMP_LEN - 1)
        s_c = jnp.where(cend <= q0 + row_query(s_c.shape), s_c, MASK_VALUE)
        m_c = jnp.maximum(jnp.max(s_c, axis=1, keepdims=True), 0.1 * MASK_VALUE)
        e_c = jnp.exp2(s_c - m_c)
        p_cs.append(e_c * (1.0 / jnp.maximum(jnp.sum(e_c, axis=1, keepdims=True), 1e-30)))
    o_c = [_dot(p_c.astype(jnp.bfloat16), vc) for p_c in p_cs]

    psum = None
    for p_c in p_cs:
        for r in range(hpp):
            term = p_c[r * tq:(r + 1) * tq]
            psum = term if psum is None else psum + term
    p_hi = psum.astype(jnp.bfloat16)
    p_lo = (psum - p_hi.astype(jnp.float32)).astype(jnp.bfloat16)
    ovt = ovt_ref[...]
    imp_t = _dot_nt(ovt, p_hi) + _dot_nt(ovt, p_lo)
    n_slc = seq // SLC_LEN
    blk = lax.broadcasted_iota(jnp.int32, imp_t.shape, 0)
    cur = (q0 + lax.broadcasted_iota(jnp.int32, imp_t.shape, 1)) >> SLC_SHIFT
    forced = (blk == 0) | (blk == cur) | (blk == cur - 1)
    score = jnp.where(forced, SEL_BONUS, imp_t)
    score = jnp.where(blk <= cur, score, -SEL_BONUS)
    rank = _block_ranks(score, n_slc)
    bias_t = jnp.where(rank < min(SLC_TOPK, n_slc), 0.0, MASK_VALUE)
    bias_t = jnp.concatenate(
        [bias_t, jnp.zeros((NSEL_PAD - n_slc, tq), jnp.float32)], axis=0)
    bias = bias_t.T.astype(jnp.bfloat16)
    bias_p = jnp.concatenate([bias] * hpp, axis=0)
    qa = [jnp.concatenate([qh[h], bias_p], axis=1) for h in parts]

    def sel_tile(k0, width, carry, diag):
        k0 = pl.multiple_of(k0, tq)
        logits = [_dot_nt(qa[h], ka_ref[pl.ds(k0, width), :]) for h in parts]
        out = []
        for s, (m, acc) in zip(logits, carry):
            if diag:
                c = lax.broadcasted_iota(jnp.int32, s.shape, 1)
                s = jnp.where(c <= row_query(s.shape) + (width - tq), s, MASK_VALUE)
            out.append(_softmax_step(s, m, acc, vs_ref[0, pl.ds(k0, width), :]))
        return tuple(out)

    init = tuple(_softmax_init(prow) for _ in parts)
    n_pairs = i >> 1
    carry = lax.fori_loop(0, n_pairs,
                          lambda p, c: sel_tile(p * (2 * tq), 2 * tq, c, False), init)
    k_tail = n_pairs * (2 * tq)
    carry = lax.cond((i & 1) == 0,
                     lambda c: sel_tile(k_tail, tq, c, True),
                     lambda c: sel_tile(k_tail, 2 * tq, c, True), carry)
    o_s = [_softmax_result(acc) for _, acc in carry]

    n_w = WINDOW // tq + 1
    starts = [q0 - WINDOW + t * tq for t in range(n_w)]
    offs = [pl.multiple_of(jnp.maximum(st, 0), tq) for st in starts]
    s_w = [[_dot_nt(qh[h], kw_ref[0, pl.ds(offs[t], tq), :]) for h in parts] for t in range(n_w)]
    for t in range(n_w):
        for h in parts:
            s = s_w[t][h]
            c = lax.broadcasted_iota(jnp.int32, s.shape, 1)
            if t == 0:
                s = jnp.where(c > row_query(s.shape), s, MASK_VALUE)
            if t == n_w - 1:
                s = jnp.where(c <= row_query(s.shape), s, MASK_VALUE)
            else:
                s = jnp.where(starts[t] >= 0, s, MASK_VALUE)
            s_w[t][h] = s
    o_w = []
    for h in parts:
        m_w = s_w[0][h].max(axis=1, keepdims=True)
        for t in range(1, n_w):
            m_w = jnp.maximum(m_w, s_w[t][h].max(axis=1, keepdims=True))
        acc_w = jnp.zeros((prow, 2 * HEAD_DIM), jnp.float32)
        for t in range(n_w):
            p = jnp.exp2((s_w[t][h] - m_w).astype(jnp.bfloat16))
            v = vw_ref[0, pl.ds(offs[t], tq), :]
            acc_w = acc_w + _dot(p, jnp.concatenate([v, jnp.ones_like(v)], axis=1))
        o_w.append(_softmax_result(acc_w))

    gates = jax.nn.sigmoid(zs_ref[0])
    lane = lax.broadcasted_iota(jnp.int32, gates.shape, 1)
    outs = []
    for r in range(NSA_GROUP):
        h, rows = r // hpp, slice((r % hpp) * tq, (r % hpp + 1) * tq)
        o_r = jnp.zeros((tq, HEAD_DIM), jnp.float32)
        for br, o_b in enumerate((o_c, o_s, o_w)):
            col = FOX_HEADS + 3 * (g * NSA_GROUP + r) + br
            gcol = jnp.sum(jnp.where(lane == col, gates, 0.0), axis=1, keepdims=True)
            o_r = o_r + gcol * o_b[h][rows]
        outs.append(o_r)
    o_ref[0] = jnp.concatenate(outs, axis=1).astype(o_ref.dtype)


def _nsa(z_main, z_small, kvc, ovt):
    nb, s, _ = z_main.shape
    nc = kvc.shape[2]
    gw = NSA_GROUP * HEAD_DIM
    full = lambda col0: pl.BlockSpec(
        (1, s, HEAD_DIM), lambda b, g, i, c=col0 // HEAD_DIM: (b, 0, c + g))
    return pl.pallas_call(
        functools.partial(_nsa_kernel, seq=s),
        out_shape=jax.ShapeDtypeStruct((nb, s, NSA_W), jnp.bfloat16),
        grid=(nb, NSA_KV_HEADS, s // NSA_TQ),
        in_specs=[
            pl.BlockSpec((1, NSA_TQ, gw), lambda b, g, i: (b, i, COL_QN // gw + g)),
            pl.BlockSpec((1, NSA_TQ, Z_SMALL), lambda b, g, i: (b, i, 0)),
            pl.BlockSpec((1, 1, nc, HEAD_DIM), lambda b, g, i: (b, g, 0, 0)),
            pl.BlockSpec((1, 1, nc, HEAD_DIM), lambda b, g, i: (b, NSA_KV_HEADS + g, 0, 0)),
            full(COL_KS), full(COL_VS), full(COL_KW), full(COL_VW),
            pl.BlockSpec(ovt.shape, lambda b, g, i: (0, 0)),
        ],
        out_specs=pl.BlockSpec((1, NSA_TQ, gw), lambda b, g, i: (b, i, g)),
        scratch_shapes=[pltpu.VMEM((s, HEAD_DIM + NSEL_PAD), jnp.bfloat16)],
        compiler_params=_cparams(("arbitrary", "arbitrary", "arbitrary")),
        name="nsa",
    )(z_main, z_small, kvc, kvc, z_main, z_main, z_main, z_main, ovt)


OUT_TM = 512
OUT_RC = 256


def _outproj_kernel(of_ref, on_ref, bf_ref, bn_ref, w_ref, x_ref, g_ref, nw_ref, sc_ref,
                    sh_ref, h_ref, u_ref):
    for r in range(OUT_TM // OUT_RC):
        rows = slice(r * OUT_RC, (r + 1) * OUT_RC)
        yf = _rms(of_ref[0, rows, :].astype(jnp.float32)) * bf_ref[...]
        yn = _rms(on_ref[0, rows, :].astype(jnp.float32)) * bn_ref[...]
        y = jnp.concatenate([yf.astype(jnp.bfloat16), yn.astype(jnp.bfloat16)], axis=1)
        h = x_ref[0, rows, :] + g_ref[0] * _dot(y, w_ref[...])
        h_ref[0, rows, :] = h
        u = _rms(h) * nw_ref[...]
        u_ref[0, rows, :] = (u * (1.0 + sc_ref[0]) + sh_ref[0]).astype(jnp.bfloat16)


def _outproj(o_f, o_n, beta_f, beta_n, w_out, x, gate, norm_w, sc, sh):
    nb, s, d = x.shape
    vec = lambda n: pl.BlockSpec((1, n), lambda b, i: (0, 0))
    bvec = pl.BlockSpec((1, 1, d), lambda b, i: (b, 0, 0))
    rows = lambda n: pl.BlockSpec((1, OUT_TM, n), lambda b, i: (b, i, 0))
    return pl.pallas_call(
        _outproj_kernel,
        out_shape=(jax.ShapeDtypeStruct((nb, s, d), jnp.float32),
                   jax.ShapeDtypeStruct((nb, s, d), jnp.bfloat16)),
        grid=(nb, s // OUT_TM),
        in_specs=[
            rows(FOX_W), rows(NSA_W), vec(FOX_W), vec(NSA_W),
            pl.BlockSpec((FOX_W + NSA_W, d), lambda b, i: (0, 0),
                         pipeline_mode=pl.Buffered(1)),
            rows(d), bvec, vec(d), bvec, bvec,
        ],
        out_specs=(rows(d), rows(d)),
        compiler_params=_cparams(("arbitrary", "arbitrary")),
        name="outproj",
    )(o_f, o_n, beta_f.reshape(1, FOX_W), beta_n.reshape(1, NSA_W), w_out, x,
      gate.reshape(nb, 1, d), norm_w.reshape(1, d), sc.reshape(nb, 1, d),
      sh.reshape(nb, 1, d))


FFN_TM = 1024
FFN_TF = 512
FFN_RC = 512


def _ffn_kernel(u_ref, h_ref, g_ref, fw_ref, wg_ref, wu_ref, wd_ref, o_ref, *, final):
    j = pl.program_id(2)

    @pl.when(j == 0)
    def _():
        o_ref[...] = jnp.zeros_like(o_ref)

    for r in range(FFN_TM // FFN_RC):
        rows = slice(r * FFN_RC, (r + 1) * FFN_RC)
        u = u_ref[0, rows, :]
        a = _dot(u, wg_ref[...])
        b = _dot(u, wu_ref[...])
        mid = (a * jax.nn.sigmoid(a) * b).astype(jnp.bfloat16)
        o_ref[0, rows, :] += _dot(mid, wd_ref[...])

    @pl.when(j == pl.num_programs(2) - 1)
    def _():
        h2 = h_ref[0] + g_ref[0] * o_ref[0]
        o_ref[0] = _rms(h2) * fw_ref[...] if final else h2


def _ffn(u2, h1, gate, final_w, w_gate, w_up, w_down, final):
    nb, s, d = h1.shape
    f = w_gate.shape[1]
    return pl.pallas_call(
        functools.partial(_ffn_kernel, final=final),
        out_shape=jax.ShapeDtypeStruct((nb, s, d), jnp.float32),
        grid=(nb, s // FFN_TM, f // FFN_TF),
        in_specs=[
            pl.BlockSpec((1, FFN_TM, d), lambda b, i, j: (b, i, 0)),
            pl.BlockSpec((1, FFN_TM, d), lambda b, i, j: (b, i, 0),
                         pipeline_mode=pl.Buffered(1)),
            pl.BlockSpec((1, 1, d), lambda b, i, j: (b, 0, 0)),
            pl.BlockSpec((1, d), lambda b, i, j: (0, 0)),
            pl.BlockSpec((d, FFN_TF), lambda b, i, j: (0, j)),
            pl.BlockSpec((d, FFN_TF), lambda b, i, j: (0, j)),
            pl.BlockSpec((FFN_TF, d), lambda b, i, j: (j, 0)),
        ],
        out_specs=pl.BlockSpec((1, FFN_TM, d), lambda b, i, j: (b, i, 0)),
        compiler_params=pltpu.CompilerParams(
            dimension_semantics=("arbitrary", "arbitrary", "arbitrary"),
            vmem_limit_bytes=FFN_VMEM_LIMIT),
        name="ffn",
    )(u2, h1, gate.reshape(nb, 1, d), final_w.reshape(1, d), w_gate, w_up, w_down)


def _split_w_in(w_in):
    sizes = ([FOX_W] * 3 + [FOX_HEADS] + [NSA_W] + [KV_W] * 6 + [3 * NSA_HEADS])
    offs = [0]
    for sz in sizes:
        offs.append(offs[-1] + sz)
    wt = w_in.T
    piece = lambda n: wt[offs[n]:offs[n + 1]].astype(jnp.bfloat16)
    main = jnp.concatenate([piece(n) for n in (0, 1, 2, 4, 7, 9, 8, 10, 5, 6)], axis=0)
    small = jnp.concatenate(
        [piece(3), piece(11),
         jnp.zeros((Z_SMALL - FOX_HEADS - 3 * NSA_HEADS, w_in.shape[0]), jnp.bfloat16)], axis=0)
    return main, small


def _overlap_t(n_cmp_pad, n_slc):
    cs = jnp.arange(n_cmp_pad) * CMP_STRIDE
    ss = jnp.arange(n_slc) * SLC_LEN
    ov = (jnp.minimum(cs[None, :] + CMP_LEN, ss[:, None] + SLC_LEN)
          - jnp.maximum(cs[None, :], ss[:, None]))
    ov = jnp.clip(ov, 0).astype(jnp.float32) / CMP_STRIDE
    return ov.astype(jnp.bfloat16)


def kernel(x, c, positions, w_ada, b_ada, norm_attn, norm_ffn, w_in, b_fgate, cmp_pos,
           w_kc1, w_kc2, w_vc1, w_vc2, beta_fox, beta_nsa, w_out, w_gate, w_up, w_down,
           final_norm):
    nb, s, d = x.shape
    depth = w_ada.shape[0]
    n_cmp_pad = s // CMP_STRIDE
    cmp_end = jnp.arange(n_cmp_pad) * CMP_STRIDE + (CMP_LEN - 1)
    pos_c = jnp.take(positions, jnp.minimum(cmp_end, s - 1), axis=1)
    cos, sin = _rope_tables(positions, 512)
    cos_c, sin_c = _rope_tables(pos_c, n_cmp_pad)
    ovt = _overlap_t(n_cmp_pad, s // SLC_LEN)

    h = x
    for l in range(depth):
        mod = _ada(c, w_ada[l], b_ada[l])
        sh1, sc1, g1, sh2, sc2, g2 = jnp.split(mod, 6, axis=-1)
        w_main, w_small = _split_w_in(w_in[l])
        z_main, z_small, zc = _inproj(h, norm_attn[l], sc1, sh1, w_main, w_small, cos, sin)
        cum = _decay(z_small, b_fgate[l])

        w1 = jnp.stack([w_kc1[l], w_vc1[l]]).astype(jnp.bfloat16)
        w1 = w1.reshape(2, CMP_LEN, HEAD_DIM, CMP_HIDDEN)
        w2 = jnp.stack([w_kc2[l], w_vc2[l]]).astype(jnp.bfloat16)
        kvc = _compress(zc, w1, w2, cmp_pos[l], cos_c, sin_c)

        o_f = _fox(z_main, cum)
        o_n = _nsa(z_main, z_small, kvc, ovt)
        h1, u2 = _outproj(o_f, o_n, beta_fox[l], beta_nsa[l], w_out[l].astype(jnp.bfloat16),
                          h, g1, norm_ffn[l], sc2, sh2)
        h = _ffn(u2, h1, g2, final_norm, w_gate[l].astype(jnp.bfloat16),
                 w_up[l].astype(jnp.bfloat16), w_down[l].astype(jnp.bfloat16),
                 final=(l == depth - 1))
    return h
```

```python
import functools

import jax
import jax.numpy as jnp
from jax import lax
from jax.experimental import pallas as pl
from jax.experimental.pallas import tpu as pltpu

D_MODEL = 2048
HEAD_DIM = 128
FOX_HEADS = 8
NSA_HEADS = 8
NSA_KV_HEADS = 2
NSA_GROUP = NSA_HEADS // NSA_KV_HEADS
CMP_LEN = 32
CMP_STRIDE = 16
CMP_HIDDEN = 2 * HEAD_DIM
SLC_LEN = 64
SLC_TOPK = 16
WINDOW = 512
D_FF = 5632
ROPE_THETA = 10000.0
NORM_EPS = 1e-6
MASK_VALUE = -1e30
SEL_BONUS = 1e6

FOX_W = FOX_HEADS * HEAD_DIM
NSA_W = NSA_HEADS * HEAD_DIM
KV_W = NSA_KV_HEADS * HEAD_DIM
Z_MAIN = 3 * FOX_W + NSA_W + 6 * KV_W
Z_SMALL = 128
COL_QN = 3 * FOX_W
COL_KS = COL_QN + NSA_W
COL_KW = COL_KS + KV_W
COL_VS = COL_KW + KV_W
COL_VW = COL_VS + KV_W
COL_KC = COL_VW + KV_W
COL_ROPE_END = COL_VS

VMEM_LIMIT = 48 * 1024 * 1024
FFN_VMEM_LIMIT = 56 * 1024 * 1024
SCALE = HEAD_DIM ** -0.5
LOG2E = 1.4426950408889634


def _cparams(sem):
    return pltpu.CompilerParams(dimension_semantics=sem, vmem_limit_bytes=VMEM_LIMIT)


def _dot(a, b):
    return jnp.dot(a, b, preferred_element_type=jnp.float32)


def _dot_nt(a, b):
    return lax.dot_general(a, b, (((1,), (1,)), ((), ())), preferred_element_type=jnp.float32)


def _rms(x):
    return x * lax.rsqrt(jnp.mean(x * x, axis=-1, keepdims=True) + NORM_EPS)


ADA_TN = 1024
ADA_KC = 256


def _ada_kernel(ct_ref, w_ref, b_ref, o_ref, *, nb):
    tn = w_ref.shape[1]
    ct = ct_ref[...]
    st = ct * jax.nn.sigmoid(ct)

    accs = [jnp.zeros((8, tn), jnp.float32) for _ in range(nb)]
    for kc in range(D_MODEL // ADA_KC):
        w = w_ref[kc * ADA_KC:(kc + 1) * ADA_KC, :]
        for b in range(nb):
            s = st[kc * ADA_KC:(kc + 1) * ADA_KC, b:b + 1]
            accs[b] = accs[b] + jnp.sum((w * s).reshape(ADA_KC // 8, 8, tn), axis=0)
    for b in range(nb):
        o_ref[b:b + 1, :] = jnp.sum(accs[b], axis=0, keepdims=True) + b_ref[...]


def _ada(c, w_ada, b_ada):
    nb = c.shape[0]
    n = w_ada.shape[1]
    return pl.pallas_call(
        functools.partial(_ada_kernel, nb=nb),
        out_shape=jax.ShapeDtypeStruct((nb, n), jnp.float32),
        grid=(n // ADA_TN,),
        in_specs=[
            pl.BlockSpec((D_MODEL, nb), lambda j: (0, 0)),
            pl.BlockSpec((D_MODEL, ADA_TN), lambda j: (0, j)),
            pl.BlockSpec((1, ADA_TN), lambda j: (0, j)),
        ],
        out_specs=pl.BlockSpec((nb, ADA_TN), lambda j: (0, j)),
        compiler_params=_cparams(("arbitrary",)),
        name="ada",
    )(c.T, w_ada, b_ada.reshape(1, n))


def _rope_table_kernel(pos_ref, inv_ref, cos_ref, sin_ref):
    ang = pos_ref[0].astype(jnp.float32) * inv_ref[...]
    lane = lax.broadcasted_iota(jnp.int32, ang.shape, 1)
    cos_ref[0] = jnp.cos(ang)
    sin_ref[0] = jnp.where(lane < HEAD_DIM // 2, -1.0, 1.0) * jnp.sin(ang)


def _rope_tables(pos, ts):
    nb, n = pos.shape
    inv = ROPE_THETA ** (-jnp.arange(0, HEAD_DIM, 2, dtype=jnp.float32) / HEAD_DIM)
    inv2 = jnp.concatenate([inv, inv]).reshape(1, HEAD_DIM)
    shp = jax.ShapeDtypeStruct((nb, n, HEAD_DIM), jnp.float32)
    spec = pl.BlockSpec((1, ts, HEAD_DIM), lambda b, i: (b, i, 0))
    return pl.pallas_call(
        _rope_table_kernel,
        out_shape=(shp, shp),
        grid=(nb, n // ts),
        in_specs=[pl.BlockSpec((1, ts, 1), lambda b, i: (b, i, 0)),
                  pl.BlockSpec((1, HEAD_DIM), lambda b, i: (0, 0))],
        out_specs=(spec, spec),
        compiler_params=_cparams(("arbitrary", "arbitrary")),
        name="rope_tables",
    )(pos.reshape(nb, n, 1), inv2)


def _apply_rope(x, cos, sin):
    return x * cos + pltpu.roll(x, HEAD_DIM // 2, axis=1) * sin


IN_TM = 1024
IN_TN = 512


def _inproj_kernel(x_ref, g_ref, sc_ref, sh_ref, w_ref, ws_ref, cos_ref, sin_ref,
                   z_ref, zs_ref, zc_ref, u_ref):
    j = pl.program_id(2)

    @pl.when(j == 0)
    def _():
        u = _rms(x_ref[0]) * g_ref[...]
        u = u * (1.0 + sc_ref[0]) + sh_ref[0]
        ub = u.astype(jnp.bfloat16)
        u_ref[...] = ub
        zs_ref[0] = _dot_nt(ub, ws_ref[...])

    acc = _dot_nt(u_ref[...], w_ref[...])
    rope = (j >= COL_QN // IN_TN) & (j < COL_ROPE_END // IN_TN)

    @pl.when(rope)
    def _():
        cos = cos_ref[0]
        sin = sin_ref[0]
        parts = [_apply_rope(acc[:, h * HEAD_DIM:(h + 1) * HEAD_DIM], cos, sin)
                 for h in range(IN_TN // HEAD_DIM)]
        z_ref[0] = jnp.concatenate(parts, axis=1).astype(z_ref.dtype)

    @pl.when(jnp.logical_not(rope))
    def _():
        z_ref[0] = acc.astype(z_ref.dtype)

    @pl.when(j == COL_KC // IN_TN)
    def _():
        zc_ref[0] = acc


def _inproj(x, norm_w, sc, sh, w_main, w_small, cos, sin):
    nb, s, d = x.shape
    grid = (nb, s // IN_TM, Z_MAIN // IN_TN)
    return pl.pallas_call(
        _inproj_kernel,
        out_shape=(jax.ShapeDtypeStruct((nb, s, Z_MAIN), jnp.bfloat16),
                   jax.ShapeDtypeStruct((nb, s, Z_SMALL), jnp.float32),
                   jax.ShapeDtypeStruct((nb, s, IN_TN), jnp.float32)),
        grid=grid,
        in_specs=[
            pl.BlockSpec((1, IN_TM, d), lambda b, i, j: (b, i, 0)),
            pl.BlockSpec((1, d), lambda b, i, j: (0, 0)),
            pl.BlockSpec((1, 1, d), lambda b, i, j: (b, 0, 0)),
            pl.BlockSpec((1, 1, d), lambda b, i, j: (b, 0, 0)),
            pl.BlockSpec((IN_TN, d), lambda b, i, j: (j, 0)),
            pl.BlockSpec((Z_SMALL, d), lambda b, i, j: (0, 0)),
            pl.BlockSpec((1, IN_TM, HEAD_DIM), lambda b, i, j: (b, i, 0)),
            pl.BlockSpec((1, IN_TM, HEAD_DIM), lambda b, i, j: (b, i, 0)),
        ],
        out_specs=(pl.BlockSpec((1, IN_TM, IN_TN), lambda b, i, j: (b, i, j)),
                   pl.BlockSpec((1, IN_TM, Z_SMALL), lambda b, i, j: (b, i, 0)),
                   pl.BlockSpec((1, IN_TM, IN_TN), lambda b, i, j: (b, i, 0))),
        scratch_shapes=[pltpu.VMEM((IN_TM, d), jnp.bfloat16)],
        compiler_params=_cparams(("arbitrary", "arbitrary", "arbitrary")),
        name="inproj",
    )(x, norm_w.reshape(1, d), sc.reshape(nb, 1, d), sh.reshape(nb, 1, d),
      w_main, w_small, cos, sin)


DEC_T = 512


def _decay_kernel(zs_ref, bf_ref, cum_ref, carry_ref):
    @pl.when(pl.program_id(1) == 0)
    def _():
        carry_ref[...] = jnp.zeros_like(carry_ref)

    xt = zs_ref[0].T[0:FOX_HEADS, :] + bf_ref[...]
    lf = jnp.minimum(xt, 0.0) - jnp.log1p(jnp.exp(-jnp.abs(xt)))
    lane = lax.broadcasted_iota(jnp.int32, lf.shape, 1)
    sh = 1
    while sh < DEC_T:
        lf = lf + jnp.where(lane >= sh, pltpu.roll(lf, sh, axis=1), 0.0)
        sh *= 2
    cum = lf + carry_ref[:, 0:1]
    cum_ref[0] = cum
    carry_ref[...] = jnp.broadcast_to(cum[:, DEC_T - 1:DEC_T], carry_ref.shape)


def _decay(z_small, b_fgate):
    nb, s, _ = z_small.shape
    return pl.pallas_call(
        _decay_kernel,
        out_shape=jax.ShapeDtypeStruct((nb, FOX_HEADS, s), jnp.float32),
        grid=(nb, s // DEC_T),
        in_specs=[pl.BlockSpec((1, DEC_T, Z_SMALL), lambda b, i: (b, i, 0)),
                  pl.BlockSpec((FOX_HEADS, 1), lambda b, i: (0, 0))],
        out_specs=pl.BlockSpec((1, FOX_HEADS, DEC_T), lambda b, i: (b, 0, i)),
        scratch_shapes=[pltpu.VMEM((FOX_HEADS, 128), jnp.float32)],
        compiler_params=_cparams(("arbitrary", "arbitrary")),
        name="decay",
    )(z_small, b_fgate.reshape(FOX_HEADS, 1))


def _compress_kernel(x_ref, w1_ref, w2_ref, pos_ref, cos_ref, sin_ref, o_ref):
    kind = pl.program_id(1)
    nc = x_ref.shape[1] // CMP_STRIDE
    p = jnp.zeros((nc, CMP_HIDDEN), jnp.float32)
    q = jnp.zeros((nc, CMP_HIDDEN), jnp.float32)
    for l in range(CMP_STRIDE):
        xl = x_ref[0, pl.ds(l, nc, stride=CMP_STRIDE), :].astype(jnp.bfloat16)
        p = p + _dot(xl, w1_ref[0, l])
        q = q + _dot(xl, w1_ref[0, CMP_STRIDE + l])
    posb = jnp.broadcast_to(pos_ref[...], (8, CMP_LEN * HEAD_DIM)).astype(jnp.bfloat16)
    w1_flat = w1_ref[0].reshape(CMP_LEN * HEAD_DIM, CMP_HIDDEN)
    pterm = _dot(posb, w1_flat)[0:1, :]
    h = p + pltpu.roll(q, nc - 1, axis=0) + pterm
    a = jax.nn.gelu(h).astype(jnp.bfloat16)
    out = _dot(a, w2_ref[0])

    @pl.when(kind < NSA_KV_HEADS)
    def _():
        o_ref[0, 0] = _apply_rope(out, cos_ref[0], sin_ref[0]).astype(o_ref.dtype)

    @pl.when(kind >= NSA_KV_HEADS)
    def _():
        o_ref[0, 0] = out.astype(o_ref.dtype)


def _compress(zc, w1, w2, cmp_pos, cos_c, sin_c):
    nb, s, _ = zc.shape
    nk = 2 * NSA_KV_HEADS
    nc = s // CMP_STRIDE
    return pl.pallas_call(
        _compress_kernel,
        out_shape=jax.ShapeDtypeStruct((nb, nk, nc, HEAD_DIM), jnp.bfloat16),
        grid=(nb, nk),
        in_specs=[
            pl.BlockSpec((1, s, HEAD_DIM), lambda b, k: (b, 0, k)),
            pl.BlockSpec((1, CMP_LEN, HEAD_DIM, CMP_HIDDEN),
                         lambda b, k: (k // NSA_KV_HEADS, 0, 0, 0)),
            pl.BlockSpec((1, CMP_HIDDEN, HEAD_DIM), lambda b, k: (k // NSA_KV_HEADS, 0, 0)),
            pl.BlockSpec((1, CMP_LEN * HEAD_DIM), lambda b, k: (0, 0)),
            pl.BlockSpec((1, nc, HEAD_DIM), lambda b, k: (b, 0, 0)),
            pl.BlockSpec((1, nc, HEAD_DIM), lambda b, k: (b, 0, 0)),
        ],
        out_specs=pl.BlockSpec((1, 1, nc, HEAD_DIM), lambda b, k: (b, k, 0, 0)),
        compiler_params=_cparams(("arbitrary", "arbitrary")),
        name="compress",
    )(zc, w1, w2, cmp_pos.reshape(1, CMP_LEN * HEAD_DIM), cos_c, sin_c)


FOX_T = 512


def _softmax_step(s, m, acc, v):
    m_new = jnp.maximum(m, jnp.max(s, axis=1, keepdims=True))
    alpha = jnp.exp2(m - m_new)
    p = jnp.exp2((s - m_new).astype(jnp.bfloat16))
    va = jnp.concatenate([v, jnp.ones_like(v)], axis=1)
    return m_new, alpha * acc + _dot(p, va)


def _softmax_init(rows):
    return (jnp.full((rows, 1), MASK_VALUE, jnp.float32),
            jnp.zeros((rows, 2 * HEAD_DIM), jnp.float32))


def _softmax_result(acc):
    return acc[:, 0:HEAD_DIM] * (1.0 / acc[:, HEAD_DIM:2 * HEAD_DIM])


def _scaled_q(q):
    return (q.astype(jnp.float32) * (SCALE * LOG2E)).astype(jnp.bfloat16)


FOX_HB = 2


def _fox_kernel(q_ref, k_ref, v_ref, cum_ref, o_ref):
    hb = pl.program_id(1)
    i = pl.program_id(2)
    t = FOX_T
    cols = [slice(n * HEAD_DIM, (n + 1) * HEAD_DIM) for n in range(FOX_HB)]
    qs = [_scaled_q(q_ref[0, :, c]) for c in cols]

    def tile(k0, width, carry, diag):
        k0 = pl.multiple_of(k0, t)
        logits = [_dot_nt(qs[n], k_ref[0, pl.ds(k0, width), cols[n]]) for n in range(FOX_HB)]
        out = []
        for n, (m, acc) in enumerate(carry):
            v = v_ref[0, pl.ds(k0, width), cols[n]]
            ck = cum_ref[0, pl.ds(hb * FOX_HB + n, 1), pl.ds(k0, width)] * LOG2E
            s = logits[n] - ck
            if diag:
                r = lax.broadcasted_iota(jnp.int32, s.shape, 0)
                c = lax.broadcasted_iota(jnp.int32, s.shape, 1)
                s = jnp.where(c <= r + (width - t), s, MASK_VALUE)
            out.append(_softmax_step(s, m, acc, v))
        return tuple(out)

    init = tuple(_softmax_init(t) for _ in range(FOX_HB))
    n_pairs = i >> 1
    carry = lax.fori_loop(0, n_pairs, lambda p, c: tile(p * (2 * t), 2 * t, c, False), init)
    k_tail = n_pairs * (2 * t)
    carry = lax.cond((i & 1) == 0,
                     lambda c: tile(k_tail, t, c, True),
                     lambda c: tile(k_tail, 2 * t, c, True), carry)
    for n, (_, acc) in enumerate(carry):
        o_ref[0, :, cols[n]] = _softmax_result(acc).astype(o_ref.dtype)


def _fox(z_main, cum):
    nb, s, _ = z_main.shape
    w = FOX_HB * HEAD_DIM
    grid = (nb, FOX_HEADS // FOX_HB, s // FOX_T)
    return pl.pallas_call(
        _fox_kernel,
        out_shape=jax.ShapeDtypeStruct((nb, s, FOX_W), jnp.bfloat16),
        grid=grid,
        in_specs=[
            pl.BlockSpec((1, FOX_T, w), lambda b, h, i: (b, i, h)),
            pl.BlockSpec((1, s, w), lambda b, h, i: (b, 0, FOX_W // w + h)),
            pl.BlockSpec((1, s, w), lambda b, h, i: (b, 0, 2 * FOX_W // w + h)),
            pl.BlockSpec((1, FOX_HEADS, s), lambda b, h, i: (b, 0, 0)),
        ],
        out_specs=pl.BlockSpec((1, FOX_T, w), lambda b, h, i: (b, i, h)),
        compiler_params=_cparams(("arbitrary", "arbitrary", "arbitrary")),
        name="fox",
    )(z_main, z_main, z_main, cum)


NSA_TQ = 512
NSA_PARTS = 2
NSEL_PAD = 128
SLC_SHIFT = SLC_LEN.bit_length() - 1
assert 1 << SLC_SHIFT == SLC_LEN
assert WINDOW % NSA_TQ == 0 and NSA_TQ % SLC_LEN == 0
SUBLANES = 8


def _block_ranks(score, n_slc):
    tq = score.shape[1]
    groups = [score[SUBLANES * v:SUBLANES * (v + 1)] for v in range(n_slc // SUBLANES)]
    ranks = [jnp.zeros((SUBLANES, tq), jnp.float32) for _ in groups]
    for m in range(n_slc):
        row = jnp.broadcast_to(score[m:m + 1, :], (SUBLANES, tq))
        for v, grp in enumerate(groups):
            lo = SUBLANES * v
            if lo > m:
                before = row >= grp
            elif lo + SUBLANES - 1 <= m:
                before = row > grp
            else:
                n_i = lo + lax.broadcasted_iota(jnp.int32, grp.shape, 0)
                before = (row > grp) | ((row == grp) & (n_i > m))
            ranks[v] = ranks[v] + jnp.where(before, 1.0, 0.0)
    return jnp.concatenate(ranks, axis=0)


def _nsa_kernel(q_ref, zs_ref, kc_ref, vc_ref, ks_ref, vs_ref, kw_ref, vw_ref, ovt_ref,
                o_ref, ka_ref, *, seq):
    g = pl.program_id(1)
    i = pl.program_id(2)
    tq = NSA_TQ
    q0 = i * tq

    @pl.when(i == 0)
    def _():
        ka_ref[:, 0:HEAD_DIM] = ks_ref[0]
        tok = lax.broadcasted_iota(jnp.int32, (seq, NSEL_PAD), 0)
        n_i = lax.broadcasted_iota(jnp.int32, (seq, NSEL_PAD), 1)
        ka_ref[:, HEAD_DIM:HEAD_DIM + NSEL_PAD] = jnp.where(
            n_i == (tok >> SLC_SHIFT), 1.0, 0.0).astype(jnp.bfloat16)

    qb = q_ref[0]
    hpp = NSA_GROUP // NSA_PARTS
    parts = range(NSA_PARTS)
    qh = [_scaled_q(jnp.concatenate(
        [qb[:, r * HEAD_DIM:(r + 1) * HEAD_DIM] for r in range(h * hpp, (h + 1) * hpp)], axis=0))
        for h in parts]
    prow = hpp * tq

    def row_query(shape):
        return lax.broadcasted_iota(jnp.int32, shape, 0) & (tq - 1)

    kc = kc_ref[0, 0]
    vc = vc_ref[0, 0]
    s_cs = [_dot_nt(qh[h], kc) for h in parts]
    p_cs = []
    for s_c in s_cs:
        cend = lax.broadcasted_iota(jnp.int32, s_c.shape, 1) * CMP_STRIDE + (CMP_LEN - 1)
        s_c = jnp.where(cend <= q0 + row_query(s_c.shape), s_c, MASK_VALUE)
        m_c = jnp.maximum(jnp.max(s_c, axis=1, keepdims=True), 0.1 * MASK_VALUE)
        e_c = jnp.exp2(s_c - m_c)
        p_cs.append(e_c * (1.0 / jnp.maximum(jnp.sum(e_c, axis=1, keepdims=True), 1e-30)))
    o_c = [_dot(p_c.astype(jnp.bfloat16), vc) for p_c in p_cs]

    psum = None
    for p_c in p_cs:
        for r in range(hpp):
            term = p_c[r * tq:(r + 1) * tq]
            psum = term if psum is None else psum + term
    p_hi = psum.astype(jnp.bfloat16)
    p_lo = (psum - p_hi.astype(jnp.float32)).astype(jnp.bfloat16)
    ovt = ovt_ref[...]
    imp_t = _dot_nt(ovt, p_hi) + _dot_nt(ovt, p_lo)
    n_slc = seq // SLC_LEN
    blk = lax.broadcasted_iota(jnp.int32, imp_t.shape, 0)
    cur = (q0 + lax.broadcasted_iota(jnp.int32, imp_t.shape, 1)) >> SLC_SHIFT
    forced = (blk == 0) | (blk == cur) | (blk == cur - 1)
    score = jnp.where(forced, SEL_BONUS, imp_t)
    score = jnp.where(blk <= cur, score, -SEL_BONUS)
    rank = _block_ranks(score, n_slc)
    bias_t = jnp.where(rank < min(SLC_TOPK, n_slc), 0.0, MASK_VALUE)
    bias_t = jnp.concatenate(
        [bias_t, jnp.zeros((NSEL_PAD - n_slc, tq), jnp.float32)], axis=0)
    bias = bias_t.T.astype(jnp.bfloat16)
    bias_p = jnp.concatenate([bias] * hpp, axis=0)
    qa = [jnp.concatenate([qh[h], bias_p], axis=1) for h in parts]

    def sel_tile(k0, width, carry, diag):
        k0 = pl.multiple_of(k0, tq)
        logits = [_dot_nt(qa[h], ka_ref[pl.ds(k0, width), :]) for h in parts]
        out = []
        for s, (m, acc) in zip(logits, carry):
            if diag:
                c = lax.broadcasted_iota(jnp.int32, s.shape, 1)
                s = jnp.where(c <= row_query(s.shape) + (width - tq), s, MASK_VALUE)
            out.append(_softmax_step(s, m, acc, vs_ref[0, pl.ds(k0, width), :]))
        return tuple(out)

    init = tuple(_softmax_init(prow) for _ in parts)
    n_pairs = i >> 1
    carry = lax.fori_loop(0, n_pairs,
                          lambda p, c: sel_tile(p * (2 * tq), 2 * tq, c, False), init)
    k_tail = n_pairs * (2 * tq)
    carry = lax.cond((i & 1) == 0,
                     lambda c: sel_tile(k_tail, tq, c, True),
                     lambda c: sel_tile(k_tail, 2 * tq, c, True), carry)
    o_s = [_softmax_result(acc) for _, acc in carry]

    n_w = WINDOW // tq + 1
    starts = [q0 - WINDOW + t * tq for t in range(n_w)]
    offs = [pl.multiple_of(jnp.maximum(st, 0), tq) for st in starts]
    s_w = [[_dot_nt(qh[h], kw_ref[0, pl.ds(offs[t], tq), :]) for h in parts] for t in range(n_w)]
    for t in range(n_w):
        for h in parts:
            s = s_w[t][h]
            c = lax.broadcasted_iota(jnp.int32, s.shape, 1)
            if t == 0:
                s = jnp.where(c > row_query(s.shape), s, MASK_VALUE)
            if t == n_w - 1:
                s = jnp.where(c <= row_query(s.shape), s, MASK_VALUE)
            else:
                s = jnp.where(starts[t] >= 0, s, MASK_VALUE)
            s_w[t][h] = s
    o_w = []
    for h in parts:
        m_w = s_w[0][h].max(axis=1, keepdims=True)
        for t in range(1, n_w):
            m_w = jnp.maximum(m_w, s_w[t][h].max(axis=1, keepdims=True))
        acc_w = jnp.zeros((prow, 2 * HEAD_DIM), jnp.float32)
        for t in range(n_w):
            p = jnp.exp2((s_w[t][h] - m_w).astype(jnp.bfloat16))
            v = vw_ref[0, pl.ds(offs[t], tq), :]
            acc_w = acc_w + _dot(p, jnp.concatenate([v, jnp.ones_like(v)], axis=1))
        o_w.append(_softmax_result(acc_w))

    gates = jax.nn.sigmoid(zs_ref[0])
    lane = lax.broadcasted_iota(jnp.int32, gates.shape, 1)
    outs = []
    for r in range(NSA_GROUP):
        h, rows = r // hpp, slice((r % hpp) * tq, (r % hpp + 1) * tq)
        o_r = jnp.zeros((tq, HEAD_DIM), jnp.float32)
        for br, o_b in enumerate((o_c, o_s, o_w)):
            col = FOX_HEADS + 3 * (g * NSA_GROUP + r) + br
            gcol = jnp.sum(jnp.where(lane == col, gates, 0.0), axis=1, keepdims=True)
            o_r = o_r + gcol * o_b[h][rows]
        outs.append(o_r)
    o_ref[0] = jnp.concatenate(outs, axis=1).astype(o_ref.dtype)


def _nsa(z_main, z_small, kvc, ovt):
    nb, s, _ = z_main.shape
    nc = kvc.shape[2]
    gw = NSA_GROUP * HEAD_DIM
    full = lambda col0: pl.BlockSpec(
        (1, s, HEAD_DIM), lambda b, g, i, c=col0 // HEAD_DIM: (b, 0, c + g))
    return pl.pallas_call(
        functools.partial(_nsa_kernel, seq=s),
        out_shape=jax.ShapeDtypeStruct((nb, s, NSA_W), jnp.bfloat16),
        grid=(nb, NSA_KV_HEADS, s // NSA_TQ),
        in_specs=[
            pl.BlockSpec((1, NSA_TQ, gw), lambda b, g, i: (b, i, COL_QN // gw + g)),
            pl.BlockSpec((1, NSA_TQ, Z_SMALL), lambda b, g, i: (b, i, 0)),
            pl.BlockSpec((1, 1, nc, HEAD_DIM), lambda b, g, i: (b, g, 0, 0)),
            pl.BlockSpec((1, 1, nc, HEAD_DIM), lambda b, g, i: (b, NSA_KV_HEADS + g, 0, 0)),
            full(COL_KS), full(COL_VS), full(COL_KW), full(COL_VW),
            pl.BlockSpec(ovt.shape, lambda b, g, i: (0, 0)),
        ],
        out_specs=pl.BlockSpec((1, NSA_TQ, gw), lambda b, g, i: (b, i, g)),
        scratch_shapes=[pltpu.VMEM((s, HEAD_DIM + NSEL_PAD), jnp.bfloat16)],
        compiler_params=_cparams(("arbitrary", "arbitrary", "arbitrary")),
        name="nsa",
    )(z_main, z_small, kvc, kvc, z_main, z_main, z_main, z_main, ovt)


OUT_TM = 512
OUT_RC = 256


def _outproj_kernel(of_ref, on_ref, bf_ref, bn_ref, w_ref, x_ref, g_ref, nw_ref, sc_ref,
                    sh_ref, h_ref, u_ref):
    for r in range(OUT_TM // OUT_RC):
        rows = slice(r * OUT_RC, (r + 1) * OUT_RC)
        yf = _rms(of_ref[0, rows, :].astype(jnp.float32)) * bf_ref[...]
        yn = _rms(on_ref[0, rows, :].astype(jnp.float32)) * bn_ref[...]
        y = jnp.concatenate([yf.astype(jnp.bfloat16), yn.astype(jnp.bfloat16)], axis=1)
        h = x_ref[0, rows, :] + g_ref[0] * _dot(y, w_ref[...])
        h_ref[0, rows, :] = h
        u = _rms(h) * nw_ref[...]
        u_ref[0, rows, :] = (u * (1.0 + sc_ref[0]) + sh_ref[0]).astype(jnp.bfloat16)


def _outproj(o_f, o_n, beta_f, beta_n, w_out, x, gate, norm_w, sc, sh):
    nb, s, d = x.shape
    vec = lambda n: pl.BlockSpec((1, n), lambda b, i: (0, 0))
    bvec = pl.BlockSpec((1, 1, d), lambda b, i: (b, 0, 0))
    rows = lambda n: pl.BlockSpec((1, OUT_TM, n), lambda b, i: (b, i, 0))
    return pl.pallas_call(
        _outproj_kernel,
        out_shape=(jax.ShapeDtypeStruct((nb, s, d), jnp.float32),
                   jax.ShapeDtypeStruct((nb, s, d), jnp.bfloat16)),
        grid=(nb, s // OUT_TM),
        in_specs=[
            rows(FOX_W), rows(NSA_W), vec(FOX_W), vec(NSA_W),
            pl.BlockSpec((FOX_W + NSA_W, d), lambda b, i: (0, 0),
                         pipeline_mode=pl.Buffered(1)),
            rows(d), bvec, vec(d), bvec, bvec,
        ],
        out_specs=(rows(d), rows(d)),
        compiler_params=_cparams(("arbitrary", "arbitrary")),
        name="outproj",
    )(o_f, o_n, beta_f.reshape(1, FOX_W), beta_n.reshape(1, NSA_W), w_out, x,
      gate.reshape(nb, 1, d), norm_w.reshape(1, d), sc.reshape(nb, 1, d),
      sh.reshape(nb, 1, d))


FFN_TM = 1024
FFN_TF = 512
FFN_RC = 512


def _ffn_kernel(u_ref, h_ref, g_ref, fw_ref, wg_ref, wu_ref, wd_ref, o_ref, *, final):
    j = pl.program_id(2)

    @pl.when(j == 0)
    def _():
        o_ref[...] = jnp.zeros_like(o_ref)

    for r in range(FFN_TM // FFN_RC):
        rows = slice(r * FFN_RC, (r + 1) * FFN_RC)
        u = u_ref[0, rows, :]
        a = _dot(u, wg_ref[...])
        b = _dot(u, wu_ref[...])
        mid = (a * jax.nn.sigmoid(a) * b).astype(jnp.bfloat16)
        o_ref[0, rows, :] += _dot(mid, wd_ref[...])

    @pl.when(j == pl.num_programs(2) - 1)
    def _():
        h2 = h_ref[0] + g_ref[0] * o_ref[0]
        o_ref[0] = _rms(h2) * fw_ref[...] if final else h2


def _ffn(u2, h1, gate, final_w, w_gate, w_up, w_down, final):
    nb, s, d = h1.shape
    f = w_gate.shape[1]
    return pl.pallas_call(
        functools.partial(_ffn_kernel, final=final),
        out_shape=jax.ShapeDtypeStruct((nb, s, d), jnp.float32),
        grid=(nb, s // FFN_TM, f // FFN_TF),
        in_specs=[
            pl.BlockSpec((1, FFN_TM, d), lambda b, i, j: (b, i, 0)),
            pl.BlockSpec((1, FFN_TM, d), lambda b, i, j: (b, i, 0),
                         pipeline_mode=pl.Buffered(1)),
            pl.BlockSpec((1, 1, d), lambda b, i, j: (b, 0, 0)),
            pl.BlockSpec((1, d), lambda b, i, j: (0, 0)),
            pl.BlockSpec((d, FFN_TF), lambda b, i, j: (0, j)),
            pl.BlockSpec((d, FFN_TF), lambda b, i, j: (0, j)),
            pl.BlockSpec((FFN_TF, d), lambda b, i, j: (j, 0)),
        ],
        out_specs=pl.BlockSpec((1, FFN_TM, d), lambda b, i, j: (b, i, 0)),
        compiler_params=pltpu.CompilerParams(
            dimension_semantics=("arbitrary", "arbitrary", "arbitrary"),
            vmem_limit_bytes=FFN_VMEM_LIMIT),
        name="ffn",
    )(u2, h1, gate.reshape(nb, 1, d), final_w.reshape(1, d), w_gate, w_up, w_down)


def _split_w_in(w_in):
    sizes = ([FOX_W] * 3 + [FOX_HEADS] + [NSA_W] + [KV_W] * 6 + [3 * NSA_HEADS])
    offs = [0]
    for sz in sizes:
        offs.append(offs[-1] + sz)
    wt = w_in.T
    piece = lambda n: wt[offs[n]:offs[n + 1]].astype(jnp.bfloat16)
    main = jnp.concatenate([piece(n) for n in (0, 1, 2, 4, 7, 9, 8, 10, 5, 6)], axis=0)
    small = jnp.concatenate(
        [piece(3), piece(11),
         jnp.zeros((Z_SMALL - FOX_HEADS - 3 * NSA_HEADS, w_in.shape[0]), jnp.bfloat16)], axis=0)
    return main, small


def _overlap_t(n_cmp_pad, n_slc):
    cs = jnp.arange(n_cmp_pad) * CMP_STRIDE
    ss = jnp.arange(n_slc) * SLC_LEN
    ov = (jnp.minimum(cs[None, :] + CMP_LEN, ss[:, None] + SLC_LEN)
          - jnp.maximum(cs[None, :], ss[:, None]))
    ov = jnp.clip(ov, 0).astype(jnp.float32) / CMP_STRIDE
    return ov.astype(jnp.bfloat16)


def kernel(x, c, positions, w_ada, b_ada, norm_attn, norm_ffn, w_in, b_fgate, cmp_pos,
           w_kc1, w_kc2, w_vc1, w_vc2, beta_fox, beta_nsa, w_out, w_gate, w_up, w_down,
           final_norm):
    nb, s, d = x.shape
    depth = w_ada.shape[0]
    n_cmp_pad = s // CMP_STRIDE
    cmp_end = jnp.arange(n_cmp_pad) * CMP_STRIDE + (CMP_LEN - 1)
    pos_c = jnp.take(positions, jnp.minimum(cmp_end, s - 1), axis=1)
    cos, sin = _rope_tables(positions, 512)
    cos_c, sin_c = _rope_tables(pos_c, n_cmp_pad)
    ovt = _overlap_t(n_cmp_pad, s // SLC_LEN)

    h = x
    for l in range(depth):
        mod = _ada(c, w_ada[l], b_ada[l])
        sh1, sc1, g1, sh2, sc2, g2 = jnp.split(mod, 6, axis=-1)
        w_main, w_small = _split_w_in(w_in[l])
        z_main, z_small, zc = _inproj(h, norm_attn[l], sc1, sh1, w_main, w_small, cos, sin)
        cum = _decay(z_small, b_fgate[l])

        w1 = jnp.stack([w_kc1[l], w_vc1[l]]).astype(jnp.bfloat16)
        w1 = w1.reshape(2, CMP_LEN, HEAD_DIM, CMP_HIDDEN)
        w2 = jnp.stack([w_kc2[l], w_vc2[l]]).astype(jnp.bfloat16)
        kvc = _compress(zc, w1, w2, cmp_pos[l], cos_c, sin_c)

        o_f = _fox(z_main, cum)
        o_n = _nsa(z_main, z_small, kvc, ovt)
        h1, u2 = _outproj(o_f, o_n, beta_fox[l], beta_nsa[l], w_out[l].astype(jnp.bfloat16),
                          h, g1, norm_ffn[l], sc2, sh2)
        h = _ffn(u2, h1, g2, final_norm, w_gate[l].astype(jnp.bfloat16),
                 w_up[l].astype(jnp.bfloat16), w_down[l].astype(jnp.bfloat16),
                 final=(l == depth - 1))
    return h
```

```python
import functools

import jax
import jax.numpy as jnp
from jax import lax
from jax.experimental import pallas as pl
from jax.experimental.pallas import tpu as pltpu

D_MODEL = 2048
HEAD_DIM = 128
FOX_HEADS = 8
NSA_HEADS = 8
NSA_KV_HEADS = 2
NSA_GROUP = NSA_HEADS // NSA_KV_HEADS
CMP_LEN = 32
CMP_STRIDE = 16
CMP_HIDDEN = 2 * HEAD_DIM
SLC_LEN = 64
SLC_TOPK = 16
WINDOW = 512
D_FF = 5632
ROPE_THETA = 10000.0
NORM_EPS = 1e-6
MASK_VALUE = -1e30
SEL_BONUS = 1e6

FOX_W = FOX_HEADS * HEAD_DIM
NSA_W = NSA_HEADS * HEAD_DIM
KV_W = NSA_KV_HEADS * HEAD_DIM
Z_MAIN = 3 * FOX_W + NSA_W + 6 * KV_W
Z_SMALL = 128
COL_QN = 3 * FOX_W
COL_KS = COL_QN + NSA_W
COL_KW = COL_KS + KV_W
COL_VS = COL_KW + KV_W
COL_VW = COL_VS + KV_W
COL_KC = COL_VW + KV_W
COL_ROPE_END = COL_VS

VMEM_LIMIT = 48 * 1024 * 1024
FFN_VMEM_LIMIT = 56 * 1024 * 1024
SCALE = HEAD_DIM ** -0.5
LOG2E = 1.4426950408889634


def _cparams(sem):
    return pltpu.CompilerParams(dimension_semantics=sem, vmem_limit_bytes=VMEM_LIMIT)


def _dot(a, b):
    return jnp.dot(a, b, preferred_element_type=jnp.float32)


def _dot_nt(a, b):
    return lax.dot_general(a, b, (((1,), (1,)), ((), ())), preferred_element_type=jnp.float32)


def _rms(x):
    return x * lax.rsqrt(jnp.mean(x * x, axis=-1, keepdims=True) + NORM_EPS)


ADA_TN = 1024
ADA_KC = 256


def _ada_kernel(ct_ref, w_ref, b_ref, o_ref, *, nb):
    tn = w_ref.shape[1]
    ct = ct_ref[...]
    st = ct * jax.nn.sigmoid(ct)

    accs = [jnp.zeros((8, tn), jnp.float32) for _ in range(nb)]
    for kc in range(D_MODEL // ADA_KC):
        w = w_ref[kc * ADA_KC:(kc + 1) * ADA_KC, :]
        for b in range(nb):
            s = st[kc * ADA_KC:(kc + 1) * ADA_KC, b:b + 1]
            accs[b] = accs[b] + jnp.sum((w * s).reshape(ADA_KC // 8, 8, tn), axis=0)
    for b in range(nb):
        o_ref[b:b + 1, :] = jnp.sum(accs[b], axis=0, keepdims=True) + b_ref[...]


def _ada(c, w_ada, b_ada):
    nb = c.shape[0]
    n = w_ada.shape[1]
    return pl.pallas_call(
        functools.partial(_ada_kernel, nb=nb),
        out_shape=jax.ShapeDtypeStruct((nb, n), jnp.float32),
        grid=(n // ADA_TN,),
        in_specs=[
            pl.BlockSpec((D_MODEL, nb), lambda j: (0, 0)),
            pl.BlockSpec((D_MODEL, ADA_TN), lambda j: (0, j)),
            pl.BlockSpec((1, ADA_TN), lambda j: (0, j)),
        ],
        out_specs=pl.BlockSpec((nb, ADA_TN), lambda j: (0, j)),
        compiler_params=_cparams(("arbitrary",)),
        name="ada",
    )(c.T, w_ada, b_ada.reshape(1, n))


def _rope_table_kernel(pos_ref, inv_ref, cos_ref, sin_ref):
    ang = pos_ref[0].astype(jnp.float32) * inv_ref[...]
    lane = lax.broadcasted_iota(jnp.int32, ang.shape, 1)
    cos_ref[0] = jnp.cos(ang)
    sin_ref[0] = jnp.where(lane < HEAD_DIM // 2, -1.0, 1.0) * jnp.sin(ang)


def _rope_tables(pos, ts):
    nb, n = pos.shape
    inv = ROPE_THETA ** (-jnp.arange(0, HEAD_DIM, 2, dtype=jnp.float32) / HEAD_DIM)
    inv2 = jnp.concatenate([inv, inv]).reshape(1, HEAD_DIM)
    shp = jax.ShapeDtypeStruct((nb, n, HEAD_DIM), jnp.float32)
    spec = pl.BlockSpec((1, ts, HEAD_DIM), lambda b, i: (b, i, 0))
    return pl.pallas_call(
        _rope_table_kernel,
        out_shape=(shp, shp),
        grid=(nb, n // ts),
        in_specs=[pl.BlockSpec((1, ts, 1), lambda b, i: (b, i, 0)),
                  pl.BlockSpec((1, HEAD_DIM), lambda b, i: (0, 0))],
        out_specs=(spec, spec),
        compiler_params=_cparams(("arbitrary", "arbitrary")),
        name="rope_tables",
    )(pos.reshape(nb, n, 1), inv2)


def _apply_rope(x, cos, sin):
    return x * cos + pltpu.roll(x, HEAD_DIM // 2, axis=1) * sin


IN_TM = 1024
IN_TN = 512


def _inproj_kernel(x_ref, g_ref, sc_ref, sh_ref, w_ref, ws_ref, cos_ref, sin_ref,
                   z_ref, zs_ref, zc_ref, u_ref):
    j = pl.program_id(2)

    @pl.when(j == 0)
    def _():
        u = _rms(x_ref[0]) * g_ref[...]
        u = u * (1.0 + sc_ref[0]) + sh_ref[0]
        ub = u.astype(jnp.bfloat16)
        u_ref[...] = ub
        zs_ref[0] = _dot_nt(ub, ws_ref[...])

    acc = _dot_nt(u_ref[...], w_ref[...])
    rope = (j >= COL_QN // IN_TN) & (j < COL_ROPE_END // IN_TN)
    is_q = (j < FOX_W // IN_TN) | ((j >= COL_QN // IN_TN) & (j < COL_KS // IN_TN))
    acc = acc * jnp.where(is_q, SCALE * LOG2E, 1.0)

    @pl.when(rope)
    def _():
        cos = cos_ref[0]
        sin = sin_ref[0]
        parts = [_apply_rope(acc[:, h * HEAD_DIM:(h + 1) * HEAD_DIM], cos, sin)
                 for h in range(IN_TN // HEAD_DIM)]
        z_ref[0] = jnp.concatenate(parts, axis=1).astype(z_ref.dtype)

    @pl.when(jnp.logical_not(rope))
    def _():
        z_ref[0] = acc.astype(z_ref.dtype)

    @pl.when(j == COL_KC // IN_TN)
    def _():
        zc_ref[0] = acc


def _inproj(x, norm_w, sc, sh, w_main, w_small, cos, sin):
    nb, s, d = x.shape
    grid = (nb, s // IN_TM, Z_MAIN // IN_TN)
    return pl.pallas_call(
        _inproj_kernel,
        out_shape=(jax.ShapeDtypeStruct((nb, s, Z_MAIN), jnp.bfloat16),
                   jax.ShapeDtypeStruct((nb, s, Z_SMALL), jnp.float32),
                   jax.ShapeDtypeStruct((nb, s, IN_TN), jnp.float32)),
        grid=grid,
        in_specs=[
            pl.BlockSpec((1, IN_TM, d), lambda b, i, j: (b, i, 0)),
            pl.BlockSpec((1, d), lambda b, i, j: (0, 0)),
            pl.BlockSpec((1, 1, d), lambda b, i, j: (b, 0, 0)),
            pl.BlockSpec((1, 1, d), lambda b, i, j: (b, 0, 0)),
            pl.BlockSpec((IN_TN, d), lambda b, i, j: (j, 0)),
            pl.BlockSpec((Z_SMALL, d), lambda b, i, j: (0, 0)),
            pl.BlockSpec((1, IN_TM, HEAD_DIM), lambda b, i, j: (b, i, 0)),
            pl.BlockSpec((1, IN_TM, HEAD_DIM), lambda b, i, j: (b, i, 0)),
        ],
        out_specs=(pl.BlockSpec((1, IN_TM, IN_TN), lambda b, i, j: (b, i, j)),
                   pl.BlockSpec((1, IN_TM, Z_SMALL), lambda b, i, j: (b, i, 0)),
                   pl.BlockSpec((1, IN_TM, IN_TN), lambda b, i, j: (b, i, 0))),
        scratch_shapes=[pltpu.VMEM((IN_TM, d), jnp.bfloat16)],
        compiler_params=_cparams(("arbitrary", "arbitrary", "arbitrary")),
        name="inproj",
    )(x, norm_w.reshape(1, d), sc.reshape(nb, 1, d), sh.reshape(nb, 1, d),
      w_main, w_small, cos, sin)


DEC_T = 512


def _decay_kernel(zs_ref, bf_ref, cum_ref, carry_ref):
    @pl.when(pl.program_id(1) == 0)
    def _():
        carry_ref[...] = jnp.zeros_like(carry_ref)

    xt = zs_ref[0].T[0:FOX_HEADS, :] + bf_ref[...]
    lf = jnp.minimum(xt, 0.0) - jnp.log1p(jnp.exp(-jnp.abs(xt)))
    lane = lax.broadcasted_iota(jnp.int32, lf.shape, 1)
    sh = 1
    while sh < DEC_T:
        lf = lf + jnp.where(lane >= sh, pltpu.roll(lf, sh, axis=1), 0.0)
        sh *= 2
    cum = lf + carry_ref[:, 0:1]
    cum_ref[0] = cum
    carry_ref[...] = jnp.broadcast_to(cum[:, DEC_T - 1:DEC_T], carry_ref.shape)


def _decay(z_small, b_fgate):
    nb, s, _ = z_small.shape
    return pl.pallas_call(
        _decay_kernel,
        out_shape=jax.ShapeDtypeStruct((nb, FOX_HEADS, s), jnp.float32),
        grid=(nb, s // DEC_T),
        in_specs=[pl.BlockSpec((1, DEC_T, Z_SMALL), lambda b, i: (b, i, 0)),
                  pl.BlockSpec((FOX_HEADS, 1), lambda b, i: (0, 0))],
        out_specs=pl.BlockSpec((1, FOX_HEADS, DEC_T), lambda b, i: (b, 0, i)),
        scratch_shapes=[pltpu.VMEM((FOX_HEADS, 128), jnp.float32)],
        compiler_params=_cparams(("arbitrary", "arbitrary")),
        name="decay",
    )(z_small, b_fgate.reshape(FOX_HEADS, 1))


def _compress_kernel(x_ref, w1_ref, w2_ref, pos_ref, cos_ref, sin_ref, o_ref):
    kind = pl.program_id(1)
    nc = x_ref.shape[1] // CMP_STRIDE
    p = jnp.zeros((nc, CMP_HIDDEN), jnp.float32)
    q = jnp.zeros((nc, CMP_HIDDEN), jnp.float32)
    for l in range(CMP_STRIDE):
        xl = x_ref[0, pl.ds(l, nc, stride=CMP_STRIDE), :].astype(jnp.bfloat16)
        p = p + _dot(xl, w1_ref[0, l])
        q = q + _dot(xl, w1_ref[0, CMP_STRIDE + l])
    posb = jnp.broadcast_to(pos_ref[...], (8, CMP_LEN * HEAD_DIM)).astype(jnp.bfloat16)
    w1_flat = w1_ref[0].reshape(CMP_LEN * HEAD_DIM, CMP_HIDDEN)
    pterm = _dot(posb, w1_flat)[0:1, :]
    h = p + pltpu.roll(q, nc - 1, axis=0) + pterm
    a = jax.nn.gelu(h).astype(jnp.bfloat16)
    out = _dot(a, w2_ref[0])

    @pl.when(kind < NSA_KV_HEADS)
    def _():
        o_ref[0, 0] = _apply_rope(out, cos_ref[0], sin_ref[0]).astype(o_ref.dtype)

    @pl.when(kind >= NSA_KV_HEADS)
    def _():
        o_ref[0, 0] = out.astype(o_ref.dtype)


def _compress(zc, w1, w2, cmp_pos, cos_c, sin_c):
    nb, s, _ = zc.shape
    nk = 2 * NSA_KV_HEADS
    nc = s // CMP_STRIDE
    return pl.pallas_call(
        _compress_kernel,
        out_shape=jax.ShapeDtypeStruct((nb, nk, nc, HEAD_DIM), jnp.bfloat16),
        grid=(nb, nk),
        in_specs=[
            pl.BlockSpec((1, s, HEAD_DIM), lambda b, k: (b, 0, k)),
            pl.BlockSpec((1, CMP_LEN, HEAD_DIM, CMP_HIDDEN),
                         lambda b, k: (k // NSA_KV_HEADS, 0, 0, 0)),
            pl.BlockSpec((1, CMP_HIDDEN, HEAD_DIM), lambda b, k: (k // NSA_KV_HEADS, 0, 0)),
            pl.BlockSpec((1, CMP_LEN * HEAD_DIM), lambda b, k: (0, 0)),
            pl.BlockSpec((1, nc, HEAD_DIM), lambda b, k: (b, 0, 0)),
            pl.BlockSpec((1, nc, HEAD_DIM), lambda b, k: (b, 0, 0)),
        ],
        out_specs=pl.BlockSpec((1, 1, nc, HEAD_DIM), lambda b, k: (b, k, 0, 0)),
        compiler_params=_cparams(("arbitrary", "arbitrary")),
        name="compress",
    )(zc, w1, w2, cmp_pos.reshape(1, CMP_LEN * HEAD_DIM), cos_c, sin_c)


FOX_T = 512


def _softmax_step(s, m, acc, v):
    m_new = jnp.maximum(m, jnp.max(s, axis=1, keepdims=True))
    alpha = jnp.exp2(m - m_new)
    p = jnp.exp2((s - m_new).astype(jnp.bfloat16))
    va = jnp.concatenate([v, jnp.ones_like(v)], axis=1)
    return m_new, alpha * acc + _dot(p, va)


def _softmax_init(rows):
    return (jnp.full((rows, 1), MASK_VALUE, jnp.float32),
            jnp.zeros((rows, 2 * HEAD_DIM), jnp.float32))


def _softmax_result(acc):
    return acc[:, 0:HEAD_DIM] * (1.0 / acc[:, HEAD_DIM:2 * HEAD_DIM])


FOX_HB = 4


def _fox_kernel(q_ref, k_ref, v_ref, cum_ref, o_ref):
    hb = pl.program_id(1)
    i = pl.program_id(2)
    t = FOX_T
    cols = [slice(n * HEAD_DIM, (n + 1) * HEAD_DIM) for n in range(FOX_HB)]
    qs = [q_ref[0, :, c] for c in cols]

    def tile(k0, width, carry, diag):
        k0 = pl.multiple_of(k0, t)
        logits = [_dot_nt(qs[n], k_ref[0, pl.ds(k0, width), cols[n]]) for n in range(FOX_HB)]
        out = []
        for n, (m, acc) in enumerate(carry):
            v = v_ref[0, pl.ds(k0, width), cols[n]]
            ck = cum_ref[0, pl.ds(hb * FOX_HB + n, 1), pl.ds(k0, width)] * LOG2E
            s = logits[n] - ck
            if diag:
                r = lax.broadcasted_iota(jnp.int32, s.shape, 0)
                c = lax.broadcasted_iota(jnp.int32, s.shape, 1)
                s = jnp.where(c <= r + (width - t), s, MASK_VALUE)
            out.append(_softmax_step(s, m, acc, v))
        return tuple(out)

    init = tuple(_softmax_init(t) for _ in range(FOX_HB))
    n_pairs = i >> 1
    carry = lax.fori_loop(0, n_pairs, lambda p, c: tile(p * (2 * t), 2 * t, c, False), init)
    k_tail = n_pairs * (2 * t)
    carry = lax.cond((i & 1) == 0,
                     lambda c: tile(k_tail, t, c, True),
                     lambda c: tile(k_tail, 2 * t, c, True), carry)
    for n, (_, acc) in enumerate(carry):
        o_ref[0, :, cols[n]] = _softmax_result(acc).astype(o_ref.dtype)


def _fox(z_main, cum):
    nb, s, _ = z_main.shape
    w = FOX_HB * HEAD_DIM
    grid = (nb, FOX_HEADS // FOX_HB, s // FOX_T)
    return pl.pallas_call(
        _fox_kernel,
        out_shape=jax.ShapeDtypeStruct((nb, s, FOX_W), jnp.bfloat16),
        grid=grid,
        in_specs=[
            pl.BlockSpec((1, FOX_T, w), lambda b, h, i: (b, i, h)),
            pl.BlockSpec((1, s, w), lambda b, h, i: (b, 0, FOX_W // w + h)),
            pl.BlockSpec((1, s, w), lambda b, h, i: (b, 0, 2 * FOX_W // w + h)),
            pl.BlockSpec((1, FOX_HEADS, s), lambda b, h, i: (b, 0, 0)),
        ],
        out_specs=pl.BlockSpec((1, FOX_T, w), lambda b, h, i: (b, i, h)),
        compiler_params=_cparams(("arbitrary", "arbitrary", "arbitrary")),
        name="fox",
    )(z_main, z_main, z_main, cum)


NSA_TQ = 512
NSA_PARTS = 4
NSEL_PAD = 128
SLC_SHIFT = SLC_LEN.bit_length() - 1
assert 1 << SLC_SHIFT == SLC_LEN
assert WINDOW % NSA_TQ == 0 and NSA_TQ % SLC_LEN == 0
SUBLANES = 8


def _block_ranks(score, n_slc):
    tq = score.shape[1]
    groups = [score[SUBLANES * v:SUBLANES * (v + 1)] for v in range(n_slc // SUBLANES)]
    ranks = [jnp.zeros((SUBLANES, tq), jnp.float32) for _ in groups]
    for m in range(n_slc):
        row = jnp.broadcast_to(score[m:m + 1, :], (SUBLANES, tq))
        for v, grp in enumerate(groups):
            lo = SUBLANES * v
            if lo > m:
                before = row >= grp
            elif lo + SUBLANES - 1 <= m:
                before = row > grp
            else:
                n_i = lo + lax.broadcasted_iota(jnp.int32, grp.shape, 0)
                before = (row > grp) | ((row == grp) & (n_i > m))
            ranks[v] = ranks[v] + jnp.where(before, 1.0, 0.0)
    return jnp.concatenate(ranks, axis=0)


def _nsa_kernel(q_ref, zs_ref, kc_ref, vc_ref, ks_ref, vs_ref, kw_ref, vw_ref, ovt_ref,
                o_ref, ka_ref, *, seq):
    g = pl.program_id(1)
    i = pl.program_id(2)
    tq = NSA_TQ
    q0 = i * tq

    @pl.when(i == 0)
    def _():
        ka_ref[:, 0:HEAD_DIM] = ks_ref[0]
        tok = lax.broadcasted_iota(jnp.int32, (seq, NSEL_PAD), 0)
        n_i = lax.broadcasted_iota(jnp.int32, (seq, NSEL_PAD), 1)
        ka_ref[:, HEAD_DIM:HEAD_DIM + NSEL_PAD] = jnp.where(
            n_i == (tok >> SLC_SHIFT), 1.0, 0.0).astype(jnp.bfloat16)

    qb = q_ref[0]
    hpp = NSA_GROUP // NSA_PARTS
    parts = range(NSA_PARTS)
    qh = [jnp.concatenate(
        [qb[:, r * HEAD_DIM:(r + 1) * HEAD_DIM] for r in range(h * hpp, (h + 1) * hpp)], axis=0)
        for h in parts]
    prow = hpp * tq

    def row_query(shape):
        return lax.broadcasted_iota(jnp.int32, shape, 0) & (tq - 1)

    kc = kc_ref[0, 0]
    vc = vc_ref[0, 0]
    s_cs = [_dot_nt(qh[h], kc) for h in parts]
    p_cs = []
    for s_c in s_cs:
        cend = lax.broadcasted_iota(jnp.int32, s_c.shape, 1) * CMP_STRIDE + (CMP_LEN - 1)
        s_c = jnp.where(cend <= q0 + row_query(s_c.shape), s_c, MASK_VALUE)
        m_c = jnp.maximum(jnp.max(s_c, axis=1, keepdims=True), 0.1 * MASK_VALUE)
        e_c = jnp.exp2(s_c - m_c)
        p_cs.append(e_c * (1.0 / jnp.maximum(jnp.sum(e_c, axis=1, keepdims=True), 1e-30)))
    o_c = [_dot(p_c.astype(jnp.bfloat16), vc) for p_c in p_cs]

    psum = None
    for p_c in p_cs:
        for r in range(hpp):
            term = p_c[r * tq:(r + 1) * tq]
            psum = term if psum is None else psum + term
    p_hi = psum.astype(jnp.bfloat16)
    p_lo = (psum - p_hi.astype(jnp.float32)).astype(jnp.bfloat16)
    ovt = ovt_ref[...]
    imp_t = _dot_nt(ovt, p_hi) + _dot_nt(ovt, p_lo)
    n_slc = seq // SLC_LEN
    blk = lax.broadcasted_iota(jnp.int32, imp_t.shape, 0)
    cur = (q0 + lax.broadcasted_iota(jnp.int32, imp_t.shape, 1)) >> SLC_SHIFT
    forced = (blk == 0) | (blk == cur) | (blk == cur - 1)
    score = jnp.where(forced, SEL_BONUS, imp_t)
    score = jnp.where(blk <= cur, score, -SEL_BONUS)
    rank = _block_ranks(score, n_slc)
    bias_t = jnp.where(rank < min(SLC_TOPK, n_slc), 0.0, MASK_VALUE)
    bias_t = jnp.concatenate(
        [bias_t, jnp.zeros((NSEL_PAD - n_slc, tq), jnp.float32)], axis=0)
    bias = bias_t.T.astype(jnp.bfloat16)
    bias_p = jnp.concatenate([bias] * hpp, axis=0)
    qa = [jnp.concatenate([qh[h], bias_p], axis=1) for h in parts]

    def sel_tile(k0, width, carry, diag):
        k0 = pl.multiple_of(k0, tq)
        logits = [_dot_nt(qa[h], ka_ref[pl.ds(k0, width), :]) for h in parts]
        out = []
        for s, (m, acc) in zip(logits, carry):
            if diag:
                c = lax.broadcasted_iota(jnp.int32, s.shape, 1)
                s = jnp.where(c <= row_query(s.shape) + (width - tq), s, MASK_VALUE)
            out.append(_softmax_step(s, m, acc, vs_ref[0, pl.ds(k0, width), :]))
        return tuple(out)

    init = tuple(_softmax_init(prow) for _ in parts)
    n_pairs = i >> 1
    carry = lax.fori_loop(0, n_pairs,
                          lambda p, c: sel_tile(p * (2 * tq), 2 * tq, c, False), init)
    k_tail = n_pairs * (2 * tq)
    carry = lax.cond((i & 1) == 0,
                     lambda c: sel_tile(k_tail, tq, c, True),
                     lambda c: sel_tile(k_tail, 2 * tq, c, True), carry)
    o_s = [_softmax_result(acc) for _, acc in carry]

    n_w = WINDOW // tq + 1
    starts = [q0 - WINDOW + t * tq for t in range(n_w)]
    offs = [pl.multiple_of(jnp.maximum(st, 0), tq) for st in starts]
    s_w = [[_dot_nt(qh[h], kw_ref[0, pl.ds(offs[t], tq), :]) for h in parts] for t in range(n_w)]
    for t in range(n_w):
        for h in parts:
            s = s_w[t][h]
            c = lax.broadcasted_iota(jnp.int32, s.shape, 1)
            if t == 0:
                s = jnp.where(c > row_query(s.shape), s, MASK_VALUE)
            if t == n_w - 1:
                s = jnp.where(c <= row_query(s.shape), s, MASK_VALUE)
            else:
                s = jnp.where(starts[t] >= 0, s, MASK_VALUE)
            s_w[t][h] = s
    o_w = []
    for h in parts:
        m_w = s_w[0][h].max(axis=1, keepdims=True)
        for t in range(1, n_w):
            m_w = jnp.maximum(m_w, s_w[t][h].max(axis=1, keepdims=True))
        acc_w = jnp.zeros((prow, 2 * HEAD_DIM), jnp.float32)
        for t in range(n_w):
            p = jnp.exp2((s_w[t][h] - m_w).astype(jnp.bfloat16))
            v = vw_ref[0, pl.ds(offs[t], tq), :]
            acc_w = acc_w + _dot(p, jnp.concatenate([v, jnp.ones_like(v)], axis=1))
        o_w.append(_softmax_result(acc_w))

    gates = jax.nn.sigmoid(zs_ref[0])
    lane = lax.broadcasted_iota(jnp.int32, gates.shape, 1)
    outs = []
    for r in range(NSA_GROUP):
        h, rows = r // hpp, slice((r % hpp) * tq, (r % hpp + 1) * tq)
        o_r = jnp.zeros((tq, HEAD_DIM), jnp.float32)
        for br, o_b in enumerate((o_c, o_s, o_w)):
            col = FOX_HEADS + 3 * (g * NSA_GROUP + r) + br
            gcol = jnp.sum(jnp.where(lane == col, gates, 0.0), axis=1, keepdims=True)
            o_r = o_r + gcol * o_b[h][rows]
        outs.append(o_r)
    o_ref[0] = jnp.concatenate(outs, axis=1).astype(o_ref.dtype)


def _nsa(z_main, z_small, kvc, ovt):
    nb, s, _ = z_main.shape
    nc = kvc.shape[2]
    gw = NSA_GROUP * HEAD_DIM
    full = lambda col0: pl.BlockSpec(
        (1, s, HEAD_DIM), lambda b, g, i, c=col0 // HEAD_DIM: (b, 0, c + g))
    return pl.pallas_call(
        functools.partial(_nsa_kernel, seq=s),
        out_shape=jax.ShapeDtypeStruct((nb, s, NSA_W), jnp.bfloat16),
        grid=(nb, NSA_KV_HEADS, s // NSA_TQ),
        in_specs=[
            pl.BlockSpec((1, NSA_TQ, gw), lambda b, g, i: (b, i, COL_QN // gw + g)),
            pl.BlockSpec((1, NSA_TQ, Z_SMALL), lambda b, g, i: (b, i, 0)),
            pl.BlockSpec((1, 1, nc, HEAD_DIM), lambda b, g, i: (b, g, 0, 0)),
            pl.BlockSpec((1, 1, nc, HEAD_DIM), lambda b, g, i: (b, NSA_KV_HEADS + g, 0, 0)),
            full(COL_KS), full(COL_VS), full(COL_KW), full(COL_VW),
            pl.BlockSpec(ovt.shape, lambda b, g, i: (0, 0)),
        ],
        out_specs=pl.BlockSpec((1, NSA_TQ, gw), lambda b, g, i: (b, i, g)),
        scratch_shapes=[pltpu.VMEM((s, HEAD_DIM + NSEL_PAD), jnp.bfloat16)],
        compiler_params=_cparams(("arbitrary", "arbitrary", "arbitrary")),
        name="nsa",
    )(z_main, z_small, kvc, kvc, z_main, z_main, z_main, z_main, ovt)


OUT_TM = 512
OUT_RC = 256


def _outproj_kernel(of_ref, on_ref, bf_ref, bn_ref, w_ref, x_ref, g_ref, nw_ref, sc_ref,
                    sh_ref, h_ref, u_ref):
    for r in range(OUT_TM // OUT_RC):
        rows = slice(r * OUT_RC, (r + 1) * OUT_RC)
        yf = _rms(of_ref[0, rows, :].astype(jnp.float32)) * bf_ref[...]
        yn = _rms(on_ref[0, rows, :].astype(jnp.float32)) * bn_ref[...]
        y = jnp.concatenate([yf.astype(jnp.bfloat16), yn.astype(jnp.bfloat16)], axis=1)
        h = x_ref[0, rows, :] + g_ref[0] * _dot(y, w_ref[...])
        h_ref[0, rows, :] = h
        u = _rms(h) * nw_ref[...]
        u_ref[0, rows, :] = (u * (1.0 + sc_ref[0]) + sh_ref[0]).astype(jnp.bfloat16)


def _outproj(o_f, o_n, beta_f, beta_n, w_out, x, gate, norm_w, sc, sh):
    nb, s, d = x.shape
    vec = lambda n: pl.BlockSpec((1, n), lambda b, i: (0, 0))
    bvec = pl.BlockSpec((1, 1, d), lambda b, i: (b, 0, 0))
    rows = lambda n: pl.BlockSpec((1, OUT_TM, n), lambda b, i: (b, i, 0))
    return pl.pallas_call(
        _outproj_kernel,
        out_shape=(jax.ShapeDtypeStruct((nb, s, d), jnp.float32),
                   jax.ShapeDtypeStruct((nb, s, d), jnp.bfloat16)),
        grid=(nb, s // OUT_TM),
        in_specs=[
            rows(FOX_W), rows(NSA_W), vec(FOX_W), vec(NSA_W),
            pl.BlockSpec((FOX_W + NSA_W, d), lambda b, i: (0, 0),
                         pipeline_mode=pl.Buffered(1)),
            rows(d), bvec, vec(d), bvec, bvec,
        ],
        out_specs=(rows(d), rows(d)),
        compiler_params=_cparams(("arbitrary", "arbitrary")),
        name="outproj",
    )(o_f, o_n, beta_f.reshape(1, FOX_W), beta_n.reshape(1, NSA_W), w_out, x,
      gate.reshape(nb, 1, d), norm_w.reshape(1, d), sc.reshape(nb, 1, d),
      sh.reshape(nb, 1, d))


FFN_TM = 1024
FFN_TF = 512
FFN_RC = 512


def _ffn_kernel(u_ref, h_ref, g_ref, fw_ref, wg_ref, wu_ref, wd_ref, o_ref, *, final):
    j = pl.program_id(2)

    @pl.when(j == 0)
    def _():
        o_ref[...] = jnp.zeros_like(o_ref)

    for r in range(FFN_TM // FFN_RC):
        rows = slice(r * FFN_RC, (r + 1) * FFN_RC)
        u = u_ref[0, rows, :]
        a = _dot(u, wg_ref[...])
        b = _dot(u, wu_ref[...])
        mid = (a * jax.nn.sigmoid(a) * b).astype(jnp.bfloat16)
        o_ref[0, rows, :] += _dot(mid, wd_ref[...])

    @pl.when(j == pl.num_programs(2) - 1)
    def _():
        h2 = h_ref[0] + g_ref[0] * o_ref[0]
        o_ref[0] = _rms(h2) * fw_ref[...] if final else h2


def _ffn(u2, h1, gate, final_w, w_gate, w_up, w_down, final):
    nb, s, d = h1.shape
    f = w_gate.shape[1]
    return pl.pallas_call(
        functools.partial(_ffn_kernel, final=final),
        out_shape=jax.ShapeDtypeStruct((nb, s, d), jnp.float32),
        grid=(nb, s // FFN_TM, f // FFN_TF),
        in_specs=[
            pl.BlockSpec((1, FFN_TM, d), lambda b, i, j: (b, i, 0)),
            pl.BlockSpec((1, FFN_TM, d), lambda b, i, j: (b, i, 0),
                         pipeline_mode=pl.Buffered(1)),
            pl.BlockSpec((1, 1, d), lambda b, i, j: (b, 0, 0)),
            pl.BlockSpec((1, d), lambda b, i, j: (0, 0)),
            pl.BlockSpec((d, FFN_TF), lambda b, i, j: (0, j)),
            pl.BlockSpec((d, FFN_TF), lambda b, i, j: (0, j)),
            pl.BlockSpec((FFN_TF, d), lambda b, i, j: (j, 0)),
        ],
        out_specs=pl.BlockSpec((1, FFN_TM, d), lambda b, i, j: (b, i, 0)),
        compiler_params=pltpu.CompilerParams(
            dimension_semantics=("arbitrary", "arbitrary", "arbitrary"),
            vmem_limit_bytes=FFN_VMEM_LIMIT),
        name="ffn",
    )(u2, h1, gate.reshape(nb, 1, d), final_w.reshape(1, d), w_gate, w_up, w_down)


def _split_w_in(w_in):
    sizes = ([FOX_W] * 3 + [FOX_HEADS] + [NSA_W] + [KV_W] * 6 + [3 * NSA_HEADS])
    offs = [0]
    for sz in sizes:
        offs.append(offs[-1] + sz)
    wt = w_in.T
    piece = lambda n: wt[offs[n]:offs[n + 1]].astype(jnp.bfloat16)
    main = jnp.concatenate([piece(n) for n in (0, 1, 2, 4, 7, 9, 8, 10, 5, 6)], axis=0)
    small = jnp.concatenate(
        [piece(3), piece(11),
         jnp.zeros((Z_SMALL - FOX_HEADS - 3 * NSA_HEADS, w_in.shape[0]), jnp.bfloat16)], axis=0)
    return main, small


def _overlap_t(n_cmp_pad, n_slc):
    cs = jnp.arange(n_cmp_pad) * CMP_STRIDE
    ss = jnp.arange(n_slc) * SLC_LEN
    ov = (jnp.minimum(cs[None, :] + CMP_LEN, ss[:, None] + SLC_LEN)
          - jnp.maximum(cs[None, :], ss[:, None]))
    ov = jnp.clip(ov, 0).astype(jnp.float32) / CMP_STRIDE
    return ov.astype(jnp.bfloat16)


def kernel(x, c, positions, w_ada, b_ada, norm_attn, norm_ffn, w_in, b_fgate, cmp_pos,
           w_kc1, w_kc2, w_vc1, w_vc2, beta_fox, beta_nsa, w_out, w_gate, w_up, w_down,
           final_norm):
    nb, s, d = x.shape
    depth = w_ada.shape[0]
    n_cmp_pad = s // CMP_STRIDE
    cmp_end = jnp.arange(n_cmp_pad) * CMP_STRIDE + (CMP_LEN - 1)
    pos_c = jnp.take(positions, jnp.minimum(cmp_end, s - 1), axis=1)
    cos, sin = _rope_tables(positions, 512)
    cos_c, sin_c = _rope_tables(pos_c, n_cmp_pad)
    ovt = _overlap_t(n_cmp_pad, s // SLC_LEN)

    h = x
    for l in range(depth):
        mod = _ada(c, w_ada[l], b_ada[l])
        sh1, sc1, g1, sh2, sc2, g2 = jnp.split(mod, 6, axis=-1)
        w_main, w_small = _split_w_in(w_in[l])
        z_main, z_small, zc = _inproj(h, norm_attn[l], sc1, sh1, w_main, w_small, cos, sin)
        cum = _decay(z_small, b_fgate[l])

        w1 = jnp.stack([w_kc1[l], w_vc1[l]]).astype(jnp.bfloat16)
        w1 = w1.reshape(2, CMP_LEN, HEAD_DIM, CMP_HIDDEN)
        w2 = jnp.stack([w_kc2[l], w_vc2[l]]).astype(jnp.bfloat16)
        kvc = _compress(zc, w1, w2, cmp_pos[l], cos_c, sin_c)

        o_f = _fox(z_main, cum)
        o_n = _nsa(z_main, z_small, kvc, ovt)
        h1, u2 = _outproj(o_f, o_n, beta_fox[l], beta_nsa[l], w_out[l].astype(jnp.bfloat16),
                          h, g1, norm_ffn[l], sc2, sh2)
        h = _ffn(u2, h1, g2, final_norm, w_gate[l].astype(jnp.bfloat16),
                 w_up[l].astype(jnp.bfloat16), w_down[l].astype(jnp.bfloat16),
                 final=(l == depth - 1))
    return h
```

```python
import functools

import jax
import jax.numpy as jnp
from jax import lax
from jax.experimental import pallas as pl
from jax.experimental.pallas import tpu as pltpu

D_MODEL = 2048
HEAD_DIM = 128
FOX_HEADS = 8
NSA_HEADS = 8
NSA_KV_HEADS = 2
NSA_GROUP = NSA_HEADS // NSA_KV_HEADS
CMP_LEN = 32
CMP_STRIDE = 16
CMP_HIDDEN = 2 * HEAD_DIM
SLC_LEN = 64
SLC_TOPK = 16
WINDOW = 512
D_FF = 5632
ROPE_THETA = 10000.0
NORM_EPS = 1e-6
MASK_VALUE = -1e30
SEL_BONUS = 1e6

FOX_W = FOX_HEADS * HEAD_DIM
NSA_W = NSA_HEADS * HEAD_DIM
KV_W = NSA_KV_HEADS * HEAD_DIM
Z_MAIN = 3 * FOX_W + NSA_W + 6 * KV_W
Z_SMALL = 128
COL_QN = 3 * FOX_W
COL_KS = COL_QN + NSA_W
COL_KW = COL_KS + KV_W
COL_VS = COL_KW + KV_W
COL_VW = COL_VS + KV_W
COL_KC = COL_VW + KV_W
COL_ROPE_END = COL_VS

VMEM_LIMIT = 48 * 1024 * 1024
FFN_VMEM_LIMIT = 56 * 1024 * 1024
SCALE = HEAD_DIM ** -0.5
LOG2E = 1.4426950408889634


def _cparams(sem):
    return pltpu.CompilerParams(dimension_semantics=sem, vmem_limit_bytes=VMEM_LIMIT)


def _dot(a, b):
    return jnp.dot(a, b, preferred_element_type=jnp.float32)


def _dot_nt(a, b):
    return lax.dot_general(a, b, (((1,), (1,)), ((), ())), preferred_element_type=jnp.float32)


def _rms(x):
    return x * lax.rsqrt(jnp.mean(x * x, axis=-1, keepdims=True) + NORM_EPS)


ADA_TN = 1024
ADA_KC = 256


def _ada_kernel(ct_ref, w_ref, b_ref, o_ref, *, nb):
    tn = w_ref.shape[1]
    ct = ct_ref[...]
    st = ct * jax.nn.sigmoid(ct)

    accs = [jnp.zeros((8, tn), jnp.float32) for _ in range(nb)]
    for kc in range(D_MODEL // ADA_KC):
        w = w_ref[kc * ADA_KC:(kc + 1) * ADA_KC, :]
        for b in range(nb):
            s = st[kc * ADA_KC:(kc + 1) * ADA_KC, b:b + 1]
            accs[b] = accs[b] + jnp.sum((w * s).reshape(ADA_KC // 8, 8, tn), axis=0)
    for b in range(nb):
        o_ref[b:b + 1, :] = jnp.sum(accs[b], axis=0, keepdims=True) + b_ref[...]


def _ada(c, w_ada, b_ada):
    nb = c.shape[0]
    n = w_ada.shape[1]
    return pl.pallas_call(
        functools.partial(_ada_kernel, nb=nb),
        out_shape=jax.ShapeDtypeStruct((nb, n), jnp.float32),
        grid=(n // ADA_TN,),
        in_specs=[
            pl.BlockSpec((D_MODEL, nb), lambda j: (0, 0)),
            pl.BlockSpec((D_MODEL, ADA_TN), lambda j: (0, j)),
            pl.BlockSpec((1, ADA_TN), lambda j: (0, j)),
        ],
        out_specs=pl.BlockSpec((nb, ADA_TN), lambda j: (0, j)),
        compiler_params=_cparams(("arbitrary",)),
        name="ada",
    )(c.T, w_ada, b_ada.reshape(1, n))


def _rope_table_kernel(pos_ref, inv_ref, cos_ref, sin_ref):
    ang = pos_ref[0].astype(jnp.float32) * inv_ref[...]
    lane = lax.broadcasted_iota(jnp.int32, ang.shape, 1)
    cos_ref[0] = jnp.cos(ang)
    sin_ref[0] = jnp.where(lane < HEAD_DIM // 2, -1.0, 1.0) * jnp.sin(ang)


def _rope_tables(pos, ts):
    nb, n = pos.shape
    inv = ROPE_THETA ** (-jnp.arange(0, HEAD_DIM, 2, dtype=jnp.float32) / HEAD_DIM)
    inv2 = jnp.concatenate([inv, inv]).reshape(1, HEAD_DIM)
    shp = jax.ShapeDtypeStruct((nb, n, HEAD_DIM), jnp.float32)
    spec = pl.BlockSpec((1, ts, HEAD_DIM), lambda b, i: (b, i, 0))
    return pl.pallas_call(
        _rope_table_kernel,
        out_shape=(shp, shp),
        grid=(nb, n // ts),
        in_specs=[pl.BlockSpec((1, ts, 1), lambda b, i: (b, i, 0)),
                  pl.BlockSpec((1, HEAD_DIM), lambda b, i: (0, 0))],
        out_specs=(spec, spec),
        compiler_params=_cparams(("arbitrary", "arbitrary")),
        name="rope_tables",
    )(pos.reshape(nb, n, 1), inv2)


def _apply_rope(x, cos, sin):
    return x * cos + pltpu.roll(x, HEAD_DIM // 2, axis=1) * sin


IN_TM = 1024
IN_TN = 512


def _inproj_kernel(x_ref, g_ref, sc_ref, sh_ref, w_ref, ws_ref, cos_ref, sin_ref,
                   z_ref, zs_ref, zc_ref, u_ref):
    j = pl.program_id(2)

    @pl.when(j == 0)
    def _():
        u = _rms(x_ref[0]) * g_ref[...]
        u = u * (1.0 + sc_ref[0]) + sh_ref[0]
        ub = u.astype(jnp.bfloat16)
        u_ref[...] = ub
        zs_ref[0] = _dot_nt(ub, ws_ref[...])

    acc = _dot_nt(u_ref[...], w_ref[...])
    rope = (j >= COL_QN // IN_TN) & (j < COL_ROPE_END // IN_TN)
    is_q = (j < FOX_W // IN_TN) | ((j >= COL_QN // IN_TN) & (j < COL_KS // IN_TN))
    acc = acc * jnp.where(is_q, SCALE * LOG2E, 1.0)

    @pl.when(rope)
    def _():
        cos = cos_ref[0]
        sin = sin_ref[0]
        parts = [_apply_rope(acc[:, h * HEAD_DIM:(h + 1) * HEAD_DIM], cos, sin)
                 for h in range(IN_TN // HEAD_DIM)]
        z_ref[0] = jnp.concatenate(parts, axis=1).astype(z_ref.dtype)

    @pl.when(jnp.logical_not(rope))
    def _():
        z_ref[0] = acc.astype(z_ref.dtype)

    @pl.when(j == COL_KC // IN_TN)
    def _():
        zc_ref[0] = acc


def _inproj(x, norm_w, sc, sh, w_main, w_small, cos, sin):
    nb, s, d = x.shape
    grid = (nb, s // IN_TM, Z_MAIN // IN_TN)
    return pl.pallas_call(
        _inproj_kernel,
        out_shape=(jax.ShapeDtypeStruct((nb, s, Z_MAIN), jnp.bfloat16),
                   jax.ShapeDtypeStruct((nb, s, Z_SMALL), jnp.float32),
                   jax.ShapeDtypeStruct((nb, s, IN_TN), jnp.float32)),
        grid=grid,
        in_specs=[
            pl.BlockSpec((1, IN_TM, d), lambda b, i, j: (b, i, 0)),
            pl.BlockSpec((1, d), lambda b, i, j: (0, 0)),
            pl.BlockSpec((1, 1, d), lambda b, i, j: (b, 0, 0)),
            pl.BlockSpec((1, 1, d), lambda b, i, j: (b, 0, 0)),
            pl.BlockSpec((IN_TN, d), lambda b, i, j: (j, 0)),
            pl.BlockSpec((Z_SMALL, d), lambda b, i, j: (0, 0)),
            pl.BlockSpec((1, IN_TM, HEAD_DIM), lambda b, i, j: (b, i, 0)),
            pl.BlockSpec((1, IN_TM, HEAD_DIM), lambda b, i, j: (b, i, 0)),
        ],
        out_specs=(pl.BlockSpec((1, IN_TM, IN_TN), lambda b, i, j: (b, i, j)),
                   pl.BlockSpec((1, IN_TM, Z_SMALL), lambda b, i, j: (b, i, 0)),
                   pl.BlockSpec((1, IN_TM, IN_TN), lambda b, i, j: (b, i, 0))),
        scratch_shapes=[pltpu.VMEM((IN_TM, d), jnp.bfloat16)],
        compiler_params=_cparams(("arbitrary", "arbitrary", "arbitrary")),
        name="inproj",
    )(x, norm_w.reshape(1, d), sc.reshape(nb, 1, d), sh.reshape(nb, 1, d),
      w_main, w_small, cos, sin)


DEC_T = 512


def _decay_kernel(zs_ref, bf_ref, cum_ref, carry_ref):
    @pl.when(pl.program_id(1) == 0)
    def _():
        carry_ref[...] = jnp.zeros_like(carry_ref)

    xt = zs_ref[0].T[0:FOX_HEADS, :] + bf_ref[...]
    lf = jnp.minimum(xt, 0.0) - jnp.log1p(jnp.exp(-jnp.abs(xt)))
    lane = lax.broadcasted_iota(jnp.int32, lf.shape, 1)
    sh = 1
    while sh < DEC_T:
        lf = lf + jnp.where(lane >= sh, pltpu.roll(lf, sh, axis=1), 0.0)
        sh *= 2
    cum = lf + carry_ref[:, 0:1]
    cum_ref[0] = cum
    carry_ref[...] = jnp.broadcast_to(cum[:, DEC_T - 1:DEC_T], carry_ref.shape)


def _decay(z_small, b_fgate):
    nb, s, _ = z_small.shape
    return pl.pallas_call(
        _decay_kernel,
        out_shape=jax.ShapeDtypeStruct((nb, FOX_HEADS, s), jnp.float32),
        grid=(nb, s // DEC_T),
        in_specs=[pl.BlockSpec((1, DEC_T, Z_SMALL), lambda b, i: (b, i, 0)),
                  pl.BlockSpec((FOX_HEADS, 1), lambda b, i: (0, 0))],
        out_specs=pl.BlockSpec((1, FOX_HEADS, DEC_T), lambda b, i: (b, 0, i)),
        scratch_shapes=[pltpu.VMEM((FOX_HEADS, 128), jnp.float32)],
        compiler_params=_cparams(("arbitrary", "arbitrary")),
        name="decay",
    )(z_small, b_fgate.reshape(FOX_HEADS, 1))


def _compress_kernel(x_ref, w1_ref, w2_ref, pos_ref, cos_ref, sin_ref, o_ref):
    kind = pl.program_id(1)
    nc = x_ref.shape[1] // CMP_STRIDE
    p = jnp.zeros((nc, CMP_HIDDEN), jnp.float32)
    q = jnp.zeros((nc, CMP_HIDDEN), jnp.float32)
    for l in range(CMP_STRIDE):
        xl = x_ref[0, pl.ds(l, nc, stride=CMP_STRIDE), :].astype(jnp.bfloat16)
        p = p + _dot(xl, w1_ref[0, l])
        q = q + _dot(xl, w1_ref[0, CMP_STRIDE + l])
    posb = jnp.broadcast_to(pos_ref[...], (8, CMP_LEN * HEAD_DIM)).astype(jnp.bfloat16)
    w1_flat = w1_ref[0].reshape(CMP_LEN * HEAD_DIM, CMP_HIDDEN)
    pterm = _dot(posb, w1_flat)[0:1, :]
    h = p + pltpu.roll(q, nc - 1, axis=0) + pterm
    a = jax.nn.gelu(h).astype(jnp.bfloat16)
    out = _dot(a, w2_ref[0])

    @pl.when(kind < NSA_KV_HEADS)
    def _():
        o_ref[0, 0] = _apply_rope(out, cos_ref[0], sin_ref[0]).astype(o_ref.dtype)

    @pl.when(kind >= NSA_KV_HEADS)
    def _():
        o_ref[0, 0] = out.astype(o_ref.dtype)


def _compress(zc, w1, w2, cmp_pos, cos_c, sin_c):
    nb, s, _ = zc.shape
    nk = 2 * NSA_KV_HEADS
    nc = s // CMP_STRIDE
    return pl.pallas_call(
        _compress_kernel,
        out_shape=jax.ShapeDtypeStruct((nb, nk, nc, HEAD_DIM), jnp.bfloat16),
        grid=(nb, nk),
        in_specs=[
            pl.BlockSpec((1, s, HEAD_DIM), lambda b, k: (b, 0, k)),
            pl.BlockSpec((1, CMP_LEN, HEAD_DIM, CMP_HIDDEN),
                         lambda b, k: (k // NSA_KV_HEADS, 0, 0, 0)),
            pl.BlockSpec((1, CMP_HIDDEN, HEAD_DIM), lambda b, k: (k // NSA_KV_HEADS, 0, 0)),
            pl.BlockSpec((1, CMP_LEN * HEAD_DIM), lambda b, k: (0, 0)),
            pl.BlockSpec((1, nc, HEAD_DIM), lambda b, k: (b, 0, 0)),
            pl.BlockSpec((1, nc, HEAD_DIM), lambda b, k: (b, 0, 0)),
        ],
        out_specs=pl.BlockSpec((1, 1, nc, HEAD_DIM), lambda b, k: (b, k, 0, 0)),
        compiler_params=_cparams(("arbitrary", "arbitrary")),
        name="compress",
    )(zc, w1, w2, cmp_pos.reshape(1, CMP_LEN * HEAD_DIM), cos_c, sin_c)


FOX_T = 512


def _softmax_step(s, m, acc, v):
    m_new = jnp.maximum(m, jnp.max(s, axis=1, keepdims=True))
    alpha = jnp.exp2(m - m_new)
    p = jnp.exp2((s - m_new).astype(jnp.bfloat16))
    va = jnp.concatenate([v, jnp.ones_like(v)], axis=1)
    return m_new, alpha * acc + _dot(p, va)


def _softmax_init(rows):
    return (jnp.full((rows, 1), MASK_VALUE, jnp.float32),
            jnp.zeros((rows, 2 * HEAD_DIM), jnp.float32))


def _softmax_result(acc):
    return acc[:, 0:HEAD_DIM] * (1.0 / acc[:, HEAD_DIM:2 * HEAD_DIM])


FOX_HB = 4


def _fox_kernel(q_ref, k_ref, v_ref, cum_ref, *rest, n_cast):
    o_ref = rest[n_cast]
    for w_ref, wb_ref in zip(rest[:n_cast], rest[n_cast + 1:]):
        wb_ref[...] = w_ref[...].astype(wb_ref.dtype)
    hb = pl.program_id(1)
    i = pl.program_id(2)
    t = FOX_T
    cols = [slice(n * HEAD_DIM, (n + 1) * HEAD_DIM) for n in range(FOX_HB)]
    qs = [q_ref[0, :, c] for c in cols]

    def tile(k0, width, carry, diag):
        k0 = pl.multiple_of(k0, t)
        logits = [_dot_nt(qs[n], k_ref[0, pl.ds(k0, width), cols[n]]) for n in range(FOX_HB)]
        out = []
        for n, (m, acc) in enumerate(carry):
            v = v_ref[0, pl.ds(k0, width), cols[n]]
            ck = cum_ref[0, pl.ds(hb * FOX_HB + n, 1), pl.ds(k0, width)] * LOG2E
            s = logits[n] - ck
            if diag:
                r = lax.broadcasted_iota(jnp.int32, s.shape, 0)
                c = lax.broadcasted_iota(jnp.int32, s.shape, 1)
                s = jnp.where(c <= r + (width - t), s, MASK_VALUE)
            out.append(_softmax_step(s, m, acc, v))
        return tuple(out)

    init = tuple(_softmax_init(t) for _ in range(FOX_HB))
    n_pairs = i >> 1
    carry = lax.fori_loop(0, n_pairs, lambda p, c: tile(p * (2 * t), 2 * t, c, False), init)
    k_tail = n_pairs * (2 * t)
    carry = lax.cond((i & 1) == 0,
                     lambda c: tile(k_tail, t, c, True),
                     lambda c: tile(k_tail, 2 * t, c, True), carry)
    for n, (_, acc) in enumerate(carry):
        o_ref[0, :, cols[n]] = _softmax_result(acc).astype(o_ref.dtype)


def _fox(z_main, cum, weights):
    nb, s, _ = z_main.shape
    w = FOX_HB * HEAD_DIM
    grid = (nb, FOX_HEADS // FOX_HB, s // FOX_T)
    n_steps = grid[0] * grid[1] * grid[2]
    step = lambda b, h, i: ((b * grid[1] + h) * grid[2] + i, 0)
    slabs = []
    for wt in weights:
        rows = wt.shape[0] // n_steps
        assert rows * n_steps == wt.shape[0] and rows % BF16_SUBLANES == 0, wt.shape
        slabs.append(pl.BlockSpec((rows, wt.shape[1]), step))
    outs = pl.pallas_call(
        functools.partial(_fox_kernel, n_cast=len(weights)),
        out_shape=[jax.ShapeDtypeStruct((nb, s, FOX_W), jnp.bfloat16)]
        + [jax.ShapeDtypeStruct(wt.shape, jnp.bfloat16) for wt in weights],
        grid=grid,
        in_specs=[
            pl.BlockSpec((1, FOX_T, w), lambda b, h, i: (b, i, h)),
            pl.BlockSpec((1, s, w), lambda b, h, i: (b, 0, FOX_W // w + h)),
            pl.BlockSpec((1, s, w), lambda b, h, i: (b, 0, 2 * FOX_W // w + h)),
            pl.BlockSpec((1, FOX_HEADS, s), lambda b, h, i: (b, 0, 0)),
        ] + slabs,
        out_specs=[pl.BlockSpec((1, FOX_T, w), lambda b, h, i: (b, i, h))] + slabs,
        compiler_params=_cparams(("arbitrary", "arbitrary", "arbitrary")),
        name="fox",
    )(z_main, z_main, z_main, cum, *weights)
    return outs[0], outs[1:]


NSA_TQ = 512
NSA_PARTS = 4
NSEL_PAD = 128
SLC_SHIFT = SLC_LEN.bit_length() - 1
assert 1 << SLC_SHIFT == SLC_LEN
assert WINDOW % NSA_TQ == 0 and NSA_TQ % SLC_LEN == 0
SUBLANES = 8
BF16_SUBLANES = 16


def _block_ranks(score, n_slc):
    tq = score.shape[1]
    groups = [score[SUBLANES * v:SUBLANES * (v + 1)] for v in range(n_slc // SUBLANES)]
    ranks = [jnp.zeros((SUBLANES, tq), jnp.float32) for _ in groups]
    for m in range(n_slc):
        row = jnp.broadcast_to(score[m:m + 1, :], (SUBLANES, tq))
        for v, grp in enumerate(groups):
            lo = SUBLANES * v
            if lo > m:
                before = row >= grp
            elif lo + SUBLANES - 1 <= m:
                before = row > grp
            else:
                n_i = lo + lax.broadcasted_iota(jnp.int32, grp.shape, 0)
                before = (row > grp) | ((row == grp) & (n_i > m))
            ranks[v] = ranks[v] + jnp.where(before, 1.0, 0.0)
    return jnp.concatenate(ranks, axis=0)


def _nsa_kernel(q_ref, zs_ref, kc_ref, vc_ref, ks_ref, vs_ref, kw_ref, vw_ref, ovt_ref,
                o_ref, ka_ref, *, seq):
    g = pl.program_id(1)
    i = pl.program_id(2)
    tq = NSA_TQ
    q0 = i * tq

    @pl.when(i == 0)
    def _():
        ka_ref[:, 0:HEAD_DIM] = ks_ref[0]
        tok = lax.broadcasted_iota(jnp.int32, (seq, NSEL_PAD), 0)
        n_i = lax.broadcasted_iota(jnp.int32, (seq, NSEL_PAD), 1)
        ka_ref[:, HEAD_DIM:HEAD_DIM + NSEL_PAD] = jnp.where(
            n_i == (tok >> SLC_SHIFT), 1.0, 0.0).astype(jnp.bfloat16)

    qb = q_ref[0]
    hpp = NSA_GROUP // NSA_PARTS
    parts = range(NSA_PARTS)
    qh = [jnp.concatenate(
        [qb[:, r * HEAD_DIM:(r + 1) * HEAD_DIM] for r in range(h * hpp, (h + 1) * hpp)], axis=0)
        for h in parts]
    prow = hpp * tq

    def row_query(shape):
        return lax.broadcasted_iota(jnp.int32, shape, 0) & (tq - 1)

    kc = kc_ref[0, 0]
    vc = vc_ref[0, 0]
    s_cs = [_dot_nt(qh[h], kc) for h in parts]
    p_cs = []
    for s_c in s_cs:
        cend = lax.broadcasted_iota(jnp.int32, s_c.shape, 1) * CMP_STRIDE + (CMP_LEN - 1)
        s_c = jnp.where(cend <= q0 + row_query(s_c.shape), s_c, MASK_VALUE)
        m_c = jnp.maximum(jnp.max(s_c, axis=1, keepdims=True), 0.1 * MASK_VALUE)
        e_c = jnp.exp2(s_c - m_c)
        p_cs.append(e_c * (1.0 / jnp.maximum(jnp.sum(e_c, axis=1, keepdims=True), 1e-30)))
    o_c = [_dot(p_c.astype(jnp.bfloat16), vc) for p_c in p_cs]

    psum = None
    for p_c in p_cs:
        for r in range(hpp):
            term = p_c[r * tq:(r + 1) * tq]
            psum = term if psum is None else psum + term
    p_hi = psum.astype(jnp.bfloat16)
    p_lo = (psum - p_hi.astype(jnp.float32)).astype(jnp.bfloat16)
    ovt = ovt_ref[...]
    imp_t = _dot_nt(ovt, p_hi) + _dot_nt(ovt, p_lo)
    n_slc = seq // SLC_LEN
    blk = lax.broadcasted_iota(jnp.int32, imp_t.shape, 0)
    cur = (q0 + lax.broadcasted_iota(jnp.int32, imp_t.shape, 1)) >> SLC_SHIFT
    forced = (blk == 0) | (blk == cur) | (blk == cur - 1)
    score = jnp.where(forced, SEL_BONUS, imp_t)
    score = jnp.where(blk <= cur, score, -SEL_BONUS)
    rank = _block_ranks(score, n_slc)
    bias_t = jnp.where(rank < min(SLC_TOPK, n_slc), 0.0, MASK_VALUE)
    bias_t = jnp.concatenate(
        [bias_t, jnp.zeros((NSEL_PAD - n_slc, tq), jnp.float32)], axis=0)
    bias = bias_t.T.astype(jnp.bfloat16)
    bias_p = jnp.concatenate([bias] * hpp, axis=0)
    qa = [jnp.concatenate([qh[h], bias_p], axis=1) for h in parts]

    def sel_tile(k0, width, carry, diag):
        k0 = pl.multiple_of(k0, tq)
        logits = [_dot_nt(qa[h], ka_ref[pl.ds(k0, width), :]) for h in parts]
        out = []
        for s, (m, acc) in zip(logits, carry):
            if diag:
                c = lax.broadcasted_iota(jnp.int32, s.shape, 1)
                s = jnp.where(c <= row_query(s.shape) + (width - tq), s, MASK_VALUE)
            out.append(_softmax_step(s, m, acc, vs_ref[0, pl.ds(k0, width), :]))
        return tuple(out)

    init = tuple(_softmax_init(prow) for _ in parts)
    n_pairs = i >> 1
    carry = lax.fori_loop(0, n_pairs,
                          lambda p, c: sel_tile(p * (2 * tq), 2 * tq, c, False), init)
    k_tail = n_pairs * (2 * tq)
    carry = lax.cond((i & 1) == 0,
                     lambda c: sel_tile(k_tail, tq, c, True),
                     lambda c: sel_tile(k_tail, 2 * tq, c, True), carry)
    o_s = [_softmax_result(acc) for _, acc in carry]

    n_w = WINDOW // tq + 1
    starts = [q0 - WINDOW + t * tq for t in range(n_w)]
    offs = [pl.multiple_of(jnp.maximum(st, 0), tq) for st in starts]
    s_w = [[_dot_nt(qh[h], kw_ref[0, pl.ds(offs[t], tq), :]) for h in parts] for t in range(n_w)]
    for t in range(n_w):
        for h in parts:
            s = s_w[t][h]
            c = lax.broadcasted_iota(jnp.int32, s.shape, 1)
            if t == 0:
                s = jnp.where(c > row_query(s.shape), s, MASK_VALUE)
            if t == n_w - 1:
                s = jnp.where(c <= row_query(s.shape), s, MASK_VALUE)
            else:
                s = jnp.where(starts[t] >= 0, s, MASK_VALUE)
            s_w[t][h] = s
    o_w = []
    for h in parts:
        m_w = s_w[0][h].max(axis=1, keepdims=True)
        for t in range(1, n_w):
            m_w = jnp.maximum(m_w, s_w[t][h].max(axis=1, keepdims=True))
        acc_w = jnp.zeros((prow, 2 * HEAD_DIM), jnp.float32)
        for t in range(n_w):
            p = jnp.exp2((s_w[t][h] - m_w).astype(jnp.bfloat16))
            v = vw_ref[0, pl.ds(offs[t], tq), :]
            acc_w = acc_w + _dot(p, jnp.concatenate([v, jnp.ones_like(v)], axis=1))
        o_w.append(_softmax_result(acc_w))

    gates = jax.nn.sigmoid(zs_ref[0])
    lane = lax.broadcasted_iota(jnp.int32, gates.shape, 1)
    outs = []
    for r in range(NSA_GROUP):
        h, rows = r // hpp, slice((r % hpp) * tq, (r % hpp + 1) * tq)
        o_r = jnp.zeros((tq, HEAD_DIM), jnp.float32)
        for br, o_b in enumerate((o_c, o_s, o_w)):
            col = FOX_HEADS + 3 * (g * NSA_GROUP + r) + br
            gcol = jnp.sum(jnp.where(lane == col, gates, 0.0), axis=1, keepdims=True)
            o_r = o_r + gcol * o_b[h][rows]
        outs.append(o_r)
    o_ref[0] = jnp.concatenate(outs, axis=1).astype(o_ref.dtype)


def _nsa(z_main, z_small, kvc, ovt):
    nb, s, _ = z_main.shape
    nc = kvc.shape[2]
    gw = NSA_GROUP * HEAD_DIM
    full = lambda col0: pl.BlockSpec(
        (1, s, HEAD_DIM), lambda b, g, i, c=col0 // HEAD_DIM: (b, 0, c + g))
    return pl.pallas_call(
        functools.partial(_nsa_kernel, seq=s),
        out_shape=jax.ShapeDtypeStruct((nb, s, NSA_W), jnp.bfloat16),
        grid=(nb, NSA_KV_HEADS, s // NSA_TQ),
        in_specs=[
            pl.BlockSpec((1, NSA_TQ, gw), lambda b, g, i: (b, i, COL_QN // gw + g)),
            pl.BlockSpec((1, NSA_TQ, Z_SMALL), lambda b, g, i: (b, i, 0)),
            pl.BlockSpec((1, 1, nc, HEAD_DIM), lambda b, g, i: (b, g, 0, 0)),
            pl.BlockSpec((1, 1, nc, HEAD_DIM), lambda b, g, i: (b, NSA_KV_HEADS + g, 0, 0)),
            full(COL_KS), full(COL_VS), full(COL_KW), full(COL_VW),
            pl.BlockSpec(ovt.shape, lambda b, g, i: (0, 0)),
        ],
        out_specs=pl.BlockSpec((1, NSA_TQ, gw), lambda b, g, i: (b, i, g)),
        scratch_shapes=[pltpu.VMEM((s, HEAD_DIM + NSEL_PAD), jnp.bfloat16)],
        compiler_params=_cparams(("arbitrary", "arbitrary", "arbitrary")),
        name="nsa",
    )(z_main, z_small, kvc, kvc, z_main, z_main, z_main, z_main, ovt)


OUT_TM = 512
OUT_RC = 256


def _outproj_kernel(of_ref, on_ref, bf_ref, bn_ref, w_ref, x_ref, g_ref, nw_ref, sc_ref,
                    sh_ref, h_ref, u_ref):
    for r in range(OUT_TM // OUT_RC):
        rows = slice(r * OUT_RC, (r + 1) * OUT_RC)
        yf = _rms(of_ref[0, rows, :].astype(jnp.float32)) * bf_ref[...]
        yn = _rms(on_ref[0, rows, :].astype(jnp.float32)) * bn_ref[...]
        y = jnp.concatenate([yf.astype(jnp.bfloat16), yn.astype(jnp.bfloat16)], axis=1)
        h = x_ref[0, rows, :] + g_ref[0] * _dot(y, w_ref[...])
        h_ref[0, rows, :] = h
        u = _rms(h) * nw_ref[...]
        u_ref[0, rows, :] = (u * (1.0 + sc_ref[0]) + sh_ref[0]).astype(jnp.bfloat16)


def _outproj(o_f, o_n, beta_f, beta_n, w_out, x, gate, norm_w, sc, sh):
    nb, s, d = x.shape
    vec = lambda n: pl.BlockSpec((1, n), lambda b, i: (0, 0))
    bvec = pl.BlockSpec((1, 1, d), lambda b, i: (b, 0, 0))
    rows = lambda n: pl.BlockSpec((1, OUT_TM, n), lambda b, i: (b, i, 0))
    return pl.pallas_call(
        _outproj_kernel,
        out_shape=(jax.ShapeDtypeStruct((nb, s, d), jnp.float32),
                   jax.ShapeDtypeStruct((nb, s, d), jnp.bfloat16)),
        grid=(nb, s // OUT_TM),
        in_specs=[
            rows(FOX_W), rows(NSA_W), vec(FOX_W), vec(NSA_W),
            pl.BlockSpec((FOX_W + NSA_W, d), lambda b, i: (0, 0),
                         pipeline_mode=pl.Buffered(1)),
            rows(d), bvec, vec(d), bvec, bvec,
        ],
        out_specs=(rows(d), rows(d)),
        compiler_params=_cparams(("arbitrary", "arbitrary")),
        name="outproj",
    )(o_f, o_n, beta_f.reshape(1, FOX_W), beta_n.reshape(1, NSA_W), w_out, x,
      gate.reshape(nb, 1, d), norm_w.reshape(1, d), sc.reshape(nb, 1, d),
      sh.reshape(nb, 1, d))


FFN_TM = 1024
FFN_TF = 512
FFN_RC = 512


def _ffn_kernel(u_ref, h_ref, g_ref, fw_ref, wg_ref, wu_ref, wd_ref, o_ref, *, final):
    j = pl.program_id(2)

    @pl.when(j == 0)
    def _():
        o_ref[...] = jnp.zeros_like(o_ref)

    for r in range(FFN_TM // FFN_RC):
        rows = slice(r * FFN_RC, (r + 1) * FFN_RC)
        u = u_ref[0, rows, :]
        a = _dot(u, wg_ref[...])
        b = _dot(u, wu_ref[...])
        mid = (a * jax.nn.sigmoid(a) * b).astype(jnp.bfloat16)
        o_ref[0, rows, :] += _dot(mid, wd_ref[...])

    @pl.when(j == pl.num_programs(2) - 1)
    def _():
        h2 = h_ref[0] + g_ref[0] * o_ref[0]
        o_ref[0] = _rms(h2) * fw_ref[...] if final else h2


def _ffn(u2, h1, gate, final_w, w_gate, w_up, w_down, final):
    nb, s, d = h1.shape
    f = w_gate.shape[1]
    return pl.pallas_call(
        functools.partial(_ffn_kernel, final=final),
        out_shape=jax.ShapeDtypeStruct((nb, s, d), jnp.float32),
        grid=(nb, s // FFN_TM, f // FFN_TF),
        in_specs=[
            pl.BlockSpec((1, FFN_TM, d), lambda b, i, j: (b, i, 0)),
            pl.BlockSpec((1, FFN_TM, d), lambda b, i, j: (b, i, 0),
                         pipeline_mode=pl.Buffered(1)),
            pl.BlockSpec((1, 1, d), lambda b, i, j: (b, 0, 0)),
            pl.BlockSpec((1, d), lambda b, i, j: (0, 0)),
            pl.BlockSpec((d, FFN_TF), lambda b, i, j: (0, j)),
            pl.BlockSpec((d, FFN_TF), lambda b, i, j: (0, j)),
            pl.BlockSpec((FFN_TF, d), lambda b, i, j: (j, 0)),
        ],
        out_specs=pl.BlockSpec((1, FFN_TM, d), lambda b, i, j: (b, i, 0)),
        compiler_params=pltpu.CompilerParams(
            dimension_semantics=("arbitrary", "arbitrary", "arbitrary"),
            vmem_limit_bytes=FFN_VMEM_LIMIT),
        name="ffn",
    )(u2, h1, gate.reshape(nb, 1, d), final_w.reshape(1, d), w_gate, w_up, w_down)


def _split_w_in(w_in):
    sizes = ([FOX_W] * 3 + [FOX_HEADS] + [NSA_W] + [KV_W] * 6 + [3 * NSA_HEADS])
    offs = [0]
    for sz in sizes:
        offs.append(offs[-1] + sz)
    wt = w_in.T
    piece = lambda n: wt[offs[n]:offs[n + 1]].astype(jnp.bfloat16)
    main = jnp.concatenate([piece(n) for n in (0, 1, 2, 4, 7, 9, 8, 10, 5, 6)], axis=0)
    small = jnp.concatenate(
        [piece(3), piece(11),
         jnp.zeros((Z_SMALL - FOX_HEADS - 3 * NSA_HEADS, w_in.shape[0]), jnp.bfloat16)], axis=0)
    return main, small


def _overlap_t(n_cmp_pad, n_slc):
    cs = jnp.arange(n_cmp_pad) * CMP_STRIDE
    ss = jnp.arange(n_slc) * SLC_LEN
    ov = (jnp.minimum(cs[None, :] + CMP_LEN, ss[:, None] + SLC_LEN)
          - jnp.maximum(cs[None, :], ss[:, None]))
    ov = jnp.clip(ov, 0).astype(jnp.float32) / CMP_STRIDE
    return ov.astype(jnp.bfloat16)


def kernel(x, c, positions, w_ada, b_ada, norm_attn, norm_ffn, w_in, b_fgate, cmp_pos,
           w_kc1, w_kc2, w_vc1, w_vc2, beta_fox, beta_nsa, w_out, w_gate, w_up, w_down,
           final_norm):
    nb, s, d = x.shape
    depth = w_ada.shape[0]
    n_cmp_pad = s // CMP_STRIDE
    cmp_end = jnp.arange(n_cmp_pad) * CMP_STRIDE + (CMP_LEN - 1)
    pos_c = jnp.take(positions, jnp.minimum(cmp_end, s - 1), axis=1)
    cos, sin = _rope_tables(positions, 512)
    cos_c, sin_c = _rope_tables(pos_c, n_cmp_pad)
    ovt = _overlap_t(n_cmp_pad, s // SLC_LEN)

    h = x
    for l in range(depth):
        mod = _ada(c, w_ada[l], b_ada[l])
        sh1, sc1, g1, sh2, sc2, g2 = jnp.split(mod, 6, axis=-1)
        w_main, w_small = _split_w_in(w_in[l])
        z_main, z_small, zc = _inproj(h, norm_attn[l], sc1, sh1, w_main, w_small, cos, sin)
        cum = _decay(z_small, b_fgate[l])

        w1 = jnp.stack([w_kc1[l], w_vc1[l]]).astype(jnp.bfloat16)
        w1 = w1.reshape(2, CMP_LEN, HEAD_DIM, CMP_HIDDEN)
        w2 = jnp.stack([w_kc2[l], w_vc2[l]]).astype(jnp.bfloat16)
        kvc = _compress(zc, w1, w2, cmp_pos[l], cos_c, sin_c)

        o_f, (wo, wg, wu, wd) = _fox(z_main, cum, (w_out[l], w_gate[l], w_up[l], w_down[l]))
        o_n = _nsa(z_main, z_small, kvc, ovt)
        h1, u2 = _outproj(o_f, o_n, beta_fox[l], beta_nsa[l], wo, h, g1, norm_ffn[l], sc2, sh2)
        h = _ffn(u2, h1, g2, final_norm, wg, wu, wd, final=(l == depth - 1))
    return h
```

```python
import functools

import jax
import jax.numpy as jnp
from jax import lax
from jax.experimental import pallas as pl
from jax.experimental.pallas import tpu as pltpu

D_MODEL = 2048
HEAD_DIM = 128
FOX_HEADS = 8
NSA_HEADS = 8
NSA_KV_HEADS = 2
NSA_GROUP = NSA_HEADS // NSA_KV_HEADS
CMP_LEN = 32
CMP_STRIDE = 16
CMP_HIDDEN = 2 * HEAD_DIM
SLC_LEN = 64
SLC_TOPK = 16
WINDOW = 512
D_FF = 5632
ROPE_THETA = 10000.0
NORM_EPS = 1e-6
MASK_VALUE = -1e30
SEL_BONUS = 1e6

FOX_W = FOX_HEADS * HEAD_DIM
NSA_W = NSA_HEADS * HEAD_DIM
KV_W = NSA_KV_HEADS * HEAD_DIM
Z_MAIN = 3 * FOX_W + NSA_W + 6 * KV_W
Z_SMALL = 128
COL_QN = 3 * FOX_W
COL_KS = COL_QN + NSA_W
COL_KW = COL_KS + KV_W
COL_VS = COL_KW + KV_W
COL_VW = COL_VS + KV_W
COL_KC = COL_VW + KV_W
COL_ROPE_END = COL_VS

VMEM_LIMIT = 48 * 1024 * 1024
FFN_VMEM_LIMIT = 56 * 1024 * 1024
SCALE = HEAD_DIM ** -0.5
LOG2E = 1.4426950408889634


def _cparams(sem):
    return pltpu.CompilerParams(dimension_semantics=sem, vmem_limit_bytes=VMEM_LIMIT)


def _dot(a, b):
    return jnp.dot(a, b, preferred_element_type=jnp.float32)


def _dot_nt(a, b):
    return lax.dot_general(a, b, (((1,), (1,)), ((), ())), preferred_element_type=jnp.float32)


def _rms(x):
    return x * lax.rsqrt(jnp.mean(x * x, axis=-1, keepdims=True) + NORM_EPS)


ADA_TN = 1024
ADA_KC = 256


def _ada_kernel(ct_ref, w_ref, b_ref, o_ref):
    nb = ct_ref.shape[1]
    tn = w_ref.shape[1]
    ct = ct_ref[...]
    st = ct * jax.nn.sigmoid(ct)

    accs = [jnp.zeros((8, tn), jnp.float32) for _ in range(nb)]
    for kc in range(D_MODEL // ADA_KC):
        w = w_ref[kc * ADA_KC:(kc + 1) * ADA_KC, :]
        for b in range(nb):
            s = st[kc * ADA_KC:(kc + 1) * ADA_KC, b:b + 1]
            accs[b] = accs[b] + jnp.sum((w * s).reshape(ADA_KC // 8, 8, tn), axis=0)
    for b in range(nb):
        o_ref[b:b + 1, :] = jnp.sum(accs[b], axis=0, keepdims=True) + b_ref[...]


def _ada(ct, w_ada, b_ada, n):
    nb = ct.shape[1]
    return pl.pallas_call(
        _ada_kernel,
        out_shape=jax.ShapeDtypeStruct((nb, n), jnp.float32),
        grid=(n // ADA_TN,),
        in_specs=[
            pl.BlockSpec((D_MODEL, nb), lambda j: (0, 0)),
            pl.BlockSpec((D_MODEL, ADA_TN), lambda j: (0, j)),
            pl.BlockSpec((1, ADA_TN), lambda j: (0, j)),
        ],
        out_specs=pl.BlockSpec((nb, ADA_TN), lambda j: (0, j)),
        compiler_params=_cparams(("arbitrary",)),
        name="ada",
    )(ct, w_ada, b_ada)


def _rope_table_kernel(pos_ref, inv_ref, cos_ref, sin_ref):
    ang = pos_ref[0].astype(jnp.float32) * inv_ref[...]
    lane = lax.broadcasted_iota(jnp.int32, ang.shape, 1)
    cos_ref[0] = jnp.cos(ang)
    sin_ref[0] = jnp.where(lane < HEAD_DIM // 2, -1.0, 1.0) * jnp.sin(ang)


def _rope_tables(pos, ts):
    nb, n = pos.shape
    inv = ROPE_THETA ** (-jnp.arange(0, HEAD_DIM, 2, dtype=jnp.float32) / HEAD_DIM)
    inv2 = jnp.concatenate([inv, inv]).reshape(1, HEAD_DIM)
    shp = jax.ShapeDtypeStruct((nb, n, HEAD_DIM), jnp.float32)
    spec = pl.BlockSpec((1, ts, HEAD_DIM), lambda b, i: (b, i, 0))
    return pl.pallas_call(
        _rope_table_kernel,
        out_shape=(shp, shp),
        grid=(nb, n // ts),
        in_specs=[pl.BlockSpec((1, ts, 1), lambda b, i: (b, i, 0)),
                  pl.BlockSpec((1, HEAD_DIM), lambda b, i: (0, 0))],
        out_specs=(spec, spec),
        compiler_params=_cparams(("arbitrary", "arbitrary")),
        name="rope_tables",
    )(pos.reshape(nb, n, 1), inv2)


def _apply_rope(x, cos, sin):
    return x * cos + pltpu.roll(x, HEAD_DIM // 2, axis=1) * sin


IN_TM = 1024
IN_TN = 512


def _inproj_kernel(x_ref, g_ref, sc_ref, sh_ref, w_ref, ws_ref, cos_ref, sin_ref,
                   z_ref, zs_ref, zc_ref, u_ref):
    j = pl.program_id(2)

    @pl.when(j == 0)
    def _():
        u = _rms(x_ref[0]) * g_ref[...]
        u = u * (1.0 + sc_ref[0]) + sh_ref[0]
        ub = u.astype(jnp.bfloat16)
        u_ref[...] = ub
        zs_ref[0] = _dot_nt(ub, ws_ref[...])

    acc = _dot_nt(u_ref[...], w_ref[...])
    rope = (j >= COL_QN // IN_TN) & (j < COL_ROPE_END // IN_TN)
    is_q = (j < FOX_W // IN_TN) | ((j >= COL_QN // IN_TN) & (j < COL_KS // IN_TN))
    acc = acc * jnp.where(is_q, SCALE * LOG2E, 1.0)

    @pl.when(rope)
    def _():
        cos = cos_ref[0]
        sin = sin_ref[0]
        parts = [_apply_rope(acc[:, h * HEAD_DIM:(h + 1) * HEAD_DIM], cos, sin)
                 for h in range(IN_TN // HEAD_DIM)]
        z_ref[0] = jnp.concatenate(parts, axis=1).astype(z_ref.dtype)

    @pl.when(jnp.logical_not(rope))
    def _():
        z_ref[0] = acc.astype(z_ref.dtype)

    @pl.when(j == COL_KC // IN_TN)
    def _():
        zc_ref[0] = acc


def _inproj(x, norm_w, sc, sh, w_main, w_small, cos, sin):
    nb, s, d = x.shape
    grid = (nb, s // IN_TM, Z_MAIN // IN_TN)
    return pl.pallas_call(
        _inproj_kernel,
        out_shape=(jax.ShapeDtypeStruct((nb, s, Z_MAIN), jnp.bfloat16),
                   jax.ShapeDtypeStruct((nb, s, Z_SMALL), jnp.float32),
                   jax.ShapeDtypeStruct((nb, s, IN_TN), jnp.float32)),
        grid=grid,
        in_specs=[
            pl.BlockSpec((1, IN_TM, d), lambda b, i, j: (b, i, 0)),
            pl.BlockSpec((1, d), lambda b, i, j: (0, 0)),
            pl.BlockSpec((1, 1, d), lambda b, i, j: (b, 0, 0)),
            pl.BlockSpec((1, 1, d), lambda b, i, j: (b, 0, 0)),
            pl.BlockSpec((IN_TN, d), lambda b, i, j: (j, 0)),
            pl.BlockSpec((Z_SMALL, d), lambda b, i, j: (0, 0)),
            pl.BlockSpec((1, IN_TM, HEAD_DIM), lambda b, i, j: (b, i, 0)),
            pl.BlockSpec((1, IN_TM, HEAD_DIM), lambda b, i, j: (b, i, 0)),
        ],
        out_specs=(pl.BlockSpec((1, IN_TM, IN_TN), lambda b, i, j: (b, i, j)),
                   pl.BlockSpec((1, IN_TM, Z_SMALL), lambda b, i, j: (b, i, 0)),
                   pl.BlockSpec((1, IN_TM, IN_TN), lambda b, i, j: (b, i, 0))),
        scratch_shapes=[pltpu.VMEM((IN_TM, d), jnp.bfloat16)],
        compiler_params=_cparams(("arbitrary", "arbitrary", "arbitrary")),
        name="inproj",
    )(x, norm_w.reshape(1, d), sc.reshape(nb, 1, d), sh.reshape(nb, 1, d),
      w_main, w_small, cos, sin)


DEC_T = 512


def _decay_kernel(zs_ref, bf_ref, cum_ref, carry_ref):
    @pl.when(pl.program_id(1) == 0)
    def _():
        carry_ref[...] = jnp.zeros_like(carry_ref)

    xt = zs_ref[0].T[0:FOX_HEADS, :] + bf_ref[...]
    lf = jnp.minimum(xt, 0.0) - jnp.log1p(jnp.exp(-jnp.abs(xt)))
    lane = lax.broadcasted_iota(jnp.int32, lf.shape, 1)
    sh = 1
    while sh < DEC_T:
        lf = lf + jnp.where(lane >= sh, pltpu.roll(lf, sh, axis=1), 0.0)
        sh *= 2
    cum = lf + carry_ref[:, 0:1]
    cum_ref[0] = cum
    carry_ref[...] = jnp.broadcast_to(cum[:, DEC_T - 1:DEC_T], carry_ref.shape)


def _decay(z_small, b_fgate):
    nb, s, _ = z_small.shape
    return pl.pallas_call(
        _decay_kernel,
        out_shape=jax.ShapeDtypeStruct((nb, FOX_HEADS, s), jnp.float32),
        grid=(nb, s // DEC_T),
        in_specs=[pl.BlockSpec((1, DEC_T, Z_SMALL), lambda b, i: (b, i, 0)),
                  pl.BlockSpec((FOX_HEADS, 1), lambda b, i: (0, 0))],
        out_specs=pl.BlockSpec((1, FOX_HEADS, DEC_T), lambda b, i: (b, 0, i)),
        scratch_shapes=[pltpu.VMEM((FOX_HEADS, 128), jnp.float32)],
        compiler_params=_cparams(("arbitrary", "arbitrary")),
        name="decay",
    )(z_small, b_fgate.reshape(FOX_HEADS, 1))


def _compress_kernel(x_ref, w1_ref, w2_ref, pos_ref, cos_ref, sin_ref, o_ref):
    kind = pl.program_id(1)
    nc = x_ref.shape[1] // CMP_STRIDE
    p = jnp.zeros((nc, CMP_HIDDEN), jnp.float32)
    q = jnp.zeros((nc, CMP_HIDDEN), jnp.float32)
    for l in range(CMP_STRIDE):
        xl = x_ref[0, pl.ds(l, nc, stride=CMP_STRIDE), :].astype(jnp.bfloat16)
        p = p + _dot(xl, w1_ref[0, l])
        q = q + _dot(xl, w1_ref[0, CMP_STRIDE + l])
    posb = jnp.broadcast_to(pos_ref[...], (8, CMP_LEN * HEAD_DIM)).astype(jnp.bfloat16)
    w1_flat = w1_ref[0].reshape(CMP_LEN * HEAD_DIM, CMP_HIDDEN)
    pterm = _dot(posb, w1_flat)[0:1, :]
    h = p + pltpu.roll(q, nc - 1, axis=0) + pterm
    a = jax.nn.gelu(h).astype(jnp.bfloat16)
    out = _dot(a, w2_ref[0])

    @pl.when(kind < NSA_KV_HEADS)
    def _():
        o_ref[0, 0] = _apply_rope(out, cos_ref[0], sin_ref[0]).astype(o_ref.dtype)

    @pl.when(kind >= NSA_KV_HEADS)
    def _():
        o_ref[0, 0] = out.astype(o_ref.dtype)


def _compress(zc, w1, w2, cmp_pos, cos_c, sin_c):
    nb, s, _ = zc.shape
    nk = 2 * NSA_KV_HEADS
    nc = s // CMP_STRIDE
    return pl.pallas_call(
        _compress_kernel,
        out_shape=jax.ShapeDtypeStruct((nb, nk, nc, HEAD_DIM), jnp.bfloat16),
        grid=(nb, nk),
        in_specs=[
            pl.BlockSpec((1, s, HEAD_DIM), lambda b, k: (b, 0, k)),
            pl.BlockSpec((1, CMP_LEN, HEAD_DIM, CMP_HIDDEN),
                         lambda b, k: (k // NSA_KV_HEADS, 0, 0, 0)),
            pl.BlockSpec((1, CMP_HIDDEN, HEAD_DIM), lambda b, k: (k // NSA_KV_HEADS, 0, 0)),
            pl.BlockSpec((1, CMP_LEN * HEAD_DIM), lambda b, k: (0, 0)),
            pl.BlockSpec((1, nc, HEAD_DIM), lambda b, k: (b, 0, 0)),
            pl.BlockSpec((1, nc, HEAD_DIM), lambda b, k: (b, 0, 0)),
        ],
        out_specs=pl.BlockSpec((1, 1, nc, HEAD_DIM), lambda b, k: (b, k, 0, 0)),
        compiler_params=_cparams(("arbitrary", "arbitrary")),
        name="compress",
    )(zc, w1, w2, cmp_pos.reshape(1, CMP_LEN * HEAD_DIM), cos_c, sin_c)


FOX_T = 512


def _softmax_step(s, m, acc, v):
    m_new = jnp.maximum(m, jnp.max(s, axis=1, keepdims=True))
    alpha = jnp.exp2(m - m_new)
    p = jnp.exp2((s - m_new).astype(jnp.bfloat16))
    va = jnp.concatenate([v, jnp.ones_like(v)], axis=1)
    return m_new, alpha * acc + _dot(p, va)


def _softmax_init(rows):
    return (jnp.full((rows, 1), MASK_VALUE, jnp.float32),
            jnp.zeros((rows, 2 * HEAD_DIM), jnp.float32))


def _softmax_result(acc):
    return acc[:, 0:HEAD_DIM] * (1.0 / acc[:, HEAD_DIM:2 * HEAD_DIM])


FOX_HB = 4


def _fox_kernel(q_ref, k_ref, v_ref, cum_ref, *rest, n_cast):
    o_ref = rest[n_cast]
    for w_ref, wb_ref in zip(rest[:n_cast], rest[n_cast + 1:]):
        wb_ref[...] = w_ref[...].astype(wb_ref.dtype)
    hb = pl.program_id(1)
    i = pl.program_id(2)
    t = FOX_T
    cols = [slice(n * HEAD_DIM, (n + 1) * HEAD_DIM) for n in range(FOX_HB)]
    qs = [q_ref[0, :, c] for c in cols]

    def tile(k0, width, carry, diag):
        k0 = pl.multiple_of(k0, t)
        logits = [_dot_nt(qs[n], k_ref[0, pl.ds(k0, width), cols[n]]) for n in range(FOX_HB)]
        out = []
        for n, (m, acc) in enumerate(carry):
            v = v_ref[0, pl.ds(k0, width), cols[n]]
            ck = cum_ref[0, pl.ds(hb * FOX_HB + n, 1), pl.ds(k0, width)] * LOG2E
            s = logits[n] - ck
            if diag:
                r = lax.broadcasted_iota(jnp.int32, s.shape, 0)
                c = lax.broadcasted_iota(jnp.int32, s.shape, 1)
                s = jnp.where(c <= r + (width - t), s, MASK_VALUE)
            out.append(_softmax_step(s, m, acc, v))
        return tuple(out)

    init = tuple(_softmax_init(t) for _ in range(FOX_HB))
    n_pairs = i >> 1
    carry = lax.fori_loop(0, n_pairs, lambda p, c: tile(p * (2 * t), 2 * t, c, False), init)
    k_tail = n_pairs * (2 * t)
    carry = lax.cond((i & 1) == 0,
                     lambda c: tile(k_tail, t, c, True),
                     lambda c: tile(k_tail, 2 * t, c, True), carry)
    for n, (_, acc) in enumerate(carry):
        o_ref[0, :, cols[n]] = _softmax_result(acc).astype(o_ref.dtype)


def _fox(z_main, cum, weights):
    nb, s, _ = z_main.shape
    w = FOX_HB * HEAD_DIM
    grid = (nb, FOX_HEADS // FOX_HB, s // FOX_T)
    n_steps = grid[0] * grid[1] * grid[2]
    step = lambda b, h, i: ((b * grid[1] + h) * grid[2] + i, 0)
    slabs = []
    for wt in weights:
        rows = wt.shape[0] // n_steps
        assert rows * n_steps == wt.shape[0] and rows % BF16_SUBLANES == 0, wt.shape
        slabs.append(pl.BlockSpec((rows, wt.shape[1]), step))
    outs = pl.pallas_call(
        functools.partial(_fox_kernel, n_cast=len(weights)),
        out_shape=[jax.ShapeDtypeStruct((nb, s, FOX_W), jnp.bfloat16)]
        + [jax.ShapeDtypeStruct(wt.shape, jnp.bfloat16) for wt in weights],
        grid=grid,
        in_specs=[
            pl.BlockSpec((1, FOX_T, w), lambda b, h, i: (b, i, h)),
            pl.BlockSpec((1, s, w), lambda b, h, i: (b, 0, FOX_W // w + h)),
            pl.BlockSpec((1, s, w), lambda b, h, i: (b, 0, 2 * FOX_W // w + h)),
            pl.BlockSpec((1, FOX_HEADS, s), lambda b, h, i: (b, 0, 0)),
        ] + slabs,
        out_specs=[pl.BlockSpec((1, FOX_T, w), lambda b, h, i: (b, i, h))] + slabs,
        compiler_params=_cparams(("arbitrary", "arbitrary", "arbitrary")),
        name="fox",
    )(z_main, z_main, z_main, cum, *weights)
    return outs[0], outs[1:]


NSA_TQ = 512
NSA_PARTS = 4
NSEL_PAD = 128
SLC_SHIFT = SLC_LEN.bit_length() - 1
assert 1 << SLC_SHIFT == SLC_LEN
assert WINDOW % NSA_TQ == 0 and NSA_TQ % SLC_LEN == 0
SUBLANES = 8
BF16_SUBLANES = 16


def _block_ranks(score, n_slc):
    tq = score.shape[1]
    groups = [score[SUBLANES * v:SUBLANES * (v + 1)] for v in range(n_slc // SUBLANES)]
    ranks = [jnp.zeros((SUBLANES, tq), jnp.float32) for _ in groups]
    for m in range(n_slc):
        row = jnp.broadcast_to(score[m:m + 1, :], (SUBLANES, tq))
        for v, grp in enumerate(groups):
            lo = SUBLANES * v
            if lo > m:
                before = row >= grp
            elif lo + SUBLANES - 1 <= m:
                before = row > grp
            else:
                n_i = lo + lax.broadcasted_iota(jnp.int32, grp.shape, 0)
                before = (row > grp) | ((row == grp) & (n_i > m))
            ranks[v] = ranks[v] + jnp.where(before, 1.0, 0.0)
    return jnp.concatenate(ranks, axis=0)


def _nsa_kernel(q_ref, zs_ref, kc_ref, vc_ref, ks_ref, vs_ref, kw_ref, vw_ref, ovt_ref,
                ct_ref, wada_ref, bada_ref, o_ref, mod_ref, ka_ref, *, seq):
    _ada_kernel(ct_ref, wada_ref, bada_ref, mod_ref)
    g = pl.program_id(1)
    i = pl.program_id(2)
    tq = NSA_TQ
    q0 = i * tq

    @pl.when(i == 0)
    def _():
        ka_ref[:, 0:HEAD_DIM] = ks_ref[0]
        tok = lax.broadcasted_iota(jnp.int32, (seq, NSEL_PAD), 0)
        n_i = lax.broadcasted_iota(jnp.int32, (seq, NSEL_PAD), 1)
        ka_ref[:, HEAD_DIM:HEAD_DIM + NSEL_PAD] = jnp.where(
            n_i == (tok >> SLC_SHIFT), 1.0, 0.0).astype(jnp.bfloat16)

    qb = q_ref[0]
    hpp = NSA_GROUP // NSA_PARTS
    parts = range(NSA_PARTS)
    qh = [jnp.concatenate(
        [qb[:, r * HEAD_DIM:(r + 1) * HEAD_DIM] for r in range(h * hpp, (h + 1) * hpp)], axis=0)
        for h in parts]
    prow = hpp * tq

    def row_query(shape):
        return lax.broadcasted_iota(jnp.int32, shape, 0) & (tq - 1)

    kc = kc_ref[0, 0]
    vc = vc_ref[0, 0]
    s_cs = [_dot_nt(qh[h], kc) for h in parts]
    p_cs = []
    for s_c in s_cs:
        cend = lax.broadcasted_iota(jnp.int32, s_c.shape, 1) * CMP_STRIDE + (CMP_LEN - 1)
        s_c = jnp.where(cend <= q0 + row_query(s_c.shape), s_c, MASK_VALUE)
        m_c = jnp.maximum(jnp.max(s_c, axis=1, keepdims=True), 0.1 * MASK_VALUE)
        e_c = jnp.exp2(s_c - m_c)
        p_cs.append(e_c * (1.0 / jnp.maximum(jnp.sum(e_c, axis=1, keepdims=True), 1e-30)))
    o_c = [_dot(p_c.astype(jnp.bfloat16), vc) for p_c in p_cs]

    psum = None
    for p_c in p_cs:
        for r in range(hpp):
            term = p_c[r * tq:(r + 1) * tq]
            psum = term if psum is None else psum + term
    p_hi = psum.astype(jnp.bfloat16)
    p_lo = (psum - p_hi.astype(jnp.float32)).astype(jnp.bfloat16)
    ovt = ovt_ref[...]
    imp_t = _dot_nt(ovt, p_hi) + _dot_nt(ovt, p_lo)
    n_slc = seq // SLC_LEN
    blk = lax.broadcasted_iota(jnp.int32, imp_t.shape, 0)
    cur = (q0 + lax.broadcasted_iota(jnp.int32, imp_t.shape, 1)) >> SLC_SHIFT
    forced = (blk == 0) | (blk == cur) | (blk == cur - 1)
    score = jnp.where(forced, SEL_BONUS, imp_t)
    score = jnp.where(blk <= cur, score, -SEL_BONUS)
    rank = _block_ranks(score, n_slc)
    bias_t = jnp.where(rank < min(SLC_TOPK, n_slc), 0.0, MASK_VALUE)
    bias_t = jnp.concatenate(
        [bias_t, jnp.zeros((NSEL_PAD - n_slc, tq), jnp.float32)], axis=0)
    bias = bias_t.T.astype(jnp.bfloat16)
    bias_p = jnp.concatenate([bias] * hpp, axis=0)
    qa = [jnp.concatenate([qh[h], bias_p], axis=1) for h in parts]

    def sel_tile(k0, width, carry, diag):
        k0 = pl.multiple_of(k0, tq)
        logits = [_dot_nt(qa[h], ka_ref[pl.ds(k0, width), :]) for h in parts]
        out = []
        for s, (m, acc) in zip(logits, carry):
            if diag:
                c = lax.broadcasted_iota(jnp.int32, s.shape, 1)
                s = jnp.where(c <= row_query(s.shape) + (width - tq), s, MASK_VALUE)
            out.append(_softmax_step(s, m, acc, vs_ref[0, pl.ds(k0, width), :]))
        return tuple(out)

    init = tuple(_softmax_init(prow) for _ in parts)
    n_pairs = i >> 1
    carry = lax.fori_loop(0, n_pairs,
                          lambda p, c: sel_tile(p * (2 * tq), 2 * tq, c, False), init)
    k_tail = n_pairs * (2 * tq)
    carry = lax.cond((i & 1) == 0,
                     lambda c: sel_tile(k_tail, tq, c, True),
                     lambda c: sel_tile(k_tail, 2 * tq, c, True), carry)
    o_s = [_softmax_result(acc) for _, acc in carry]

    n_w = WINDOW // tq + 1
    starts = [q0 - WINDOW + t * tq for t in range(n_w)]
    offs = [pl.multiple_of(jnp.maximum(st, 0), tq) for st in starts]
    s_w = [[_dot_nt(qh[h], kw_ref[0, pl.ds(offs[t], tq), :]) for h in parts] for t in range(n_w)]
    for t in range(n_w):
        for h in parts:
            s = s_w[t][h]
            c = lax.broadcasted_iota(jnp.int32, s.shape, 1)
            if t == 0:
                s = jnp.where(c > row_query(s.shape), s, MASK_VALUE)
            if t == n_w - 1:
                s = jnp.where(c <= row_query(s.shape), s, MASK_VALUE)
            else:
                s = jnp.where(starts[t] >= 0, s, MASK_VALUE)
            s_w[t][h] = s
    o_w = []
    for h in parts:
        m_w = s_w[0][h].max(axis=1, keepdims=True)
        for t in range(1, n_w):
            m_w = jnp.maximum(m_w, s_w[t][h].max(axis=1, keepdims=True))
        acc_w = jnp.zeros((prow, 2 * HEAD_DIM), jnp.float32)
        for t in range(n_w):
            p = jnp.exp2((s_w[t][h] - m_w).astype(jnp.bfloat16))
            v = vw_ref[0, pl.ds(offs[t], tq), :]
            acc_w = acc_w + _dot(p, jnp.concatenate([v, jnp.ones_like(v)], axis=1))
        o_w.append(_softmax_result(acc_w))

    gates = jax.nn.sigmoid(zs_ref[0])
    lane = lax.broadcasted_iota(jnp.int32, gates.shape, 1)
    outs = []
    for r in range(NSA_GROUP):
        h, rows = r // hpp, slice((r % hpp) * tq, (r % hpp + 1) * tq)
        o_r = jnp.zeros((tq, HEAD_DIM), jnp.float32)
        for br, o_b in enumerate((o_c, o_s, o_w)):
            col = FOX_HEADS + 3 * (g * NSA_GROUP + r) + br
            gcol = jnp.sum(jnp.where(lane == col, gates, 0.0), axis=1, keepdims=True)
            o_r = o_r + gcol * o_b[h][rows]
        outs.append(o_r)
    o_ref[0] = jnp.concatenate(outs, axis=1).astype(o_ref.dtype)


def _nsa(z_main, z_small, kvc, ovt, ct, w_ada, b_ada, n_done):
    nb, s, _ = z_main.shape
    nc = kvc.shape[2]
    gw = NSA_GROUP * HEAD_DIM
    full = lambda col0: pl.BlockSpec(
        (1, s, HEAD_DIM), lambda b, g, i, c=col0 // HEAD_DIM: (b, 0, c + g))
    grid = (nb, NSA_KV_HEADS, s // NSA_TQ)
    n_rest = w_ada.shape[1] - n_done
    tn = n_rest // (grid[0] * grid[1] * grid[2])
    assert tn % HEAD_DIM == 0 and n_done % tn == 0, (n_rest, tn)
    step = lambda b, g, i: (b * grid[1] + g) * grid[2] + i
    return pl.pallas_call(
        functools.partial(_nsa_kernel, seq=s),
        out_shape=(jax.ShapeDtypeStruct((nb, s, NSA_W), jnp.bfloat16),
                   jax.ShapeDtypeStruct((nb, n_rest), jnp.float32)),
        grid=grid,
        in_specs=[
            pl.BlockSpec((1, NSA_TQ, gw), lambda b, g, i: (b, i, COL_QN // gw + g)),
            pl.BlockSpec((1, NSA_TQ, Z_SMALL), lambda b, g, i: (b, i, 0)),
            pl.BlockSpec((1, 1, nc, HEAD_DIM), lambda b, g, i: (b, g, 0, 0)),
            pl.BlockSpec((1, 1, nc, HEAD_DIM), lambda b, g, i: (b, NSA_KV_HEADS + g, 0, 0)),
            full(COL_KS), full(COL_VS), full(COL_KW), full(COL_VW),
            pl.BlockSpec(ovt.shape, lambda b, g, i: (0, 0)),
            pl.BlockSpec(ct.shape, lambda b, g, i: (0, 0)),
            pl.BlockSpec((D_MODEL, tn), lambda b, g, i: (0, n_done // tn + step(b, g, i))),
            pl.BlockSpec((1, tn), lambda b, g, i: (0, n_done // tn + step(b, g, i))),
        ],
        out_specs=(pl.BlockSpec((1, NSA_TQ, gw), lambda b, g, i: (b, i, g)),
                   pl.BlockSpec((nb, tn), lambda b, g, i: (0, step(b, g, i)))),
        scratch_shapes=[pltpu.VMEM((s, HEAD_DIM + NSEL_PAD), jnp.bfloat16)],
        compiler_params=_cparams(("arbitrary", "arbitrary", "arbitrary")),
        name="nsa",
    )(z_main, z_small, kvc, kvc, z_main, z_main, z_main, z_main, ovt, ct, w_ada, b_ada)


OUT_TM = 512
OUT_RC = 256


def _outproj_kernel(of_ref, on_ref, bf_ref, bn_ref, w_ref, x_ref, g_ref, nw_ref, sc_ref,
                    sh_ref, h_ref, u_ref):
    for r in range(OUT_TM // OUT_RC):
        rows = slice(r * OUT_RC, (r + 1) * OUT_RC)
        yf = _rms(of_ref[0, rows, :].astype(jnp.float32)) * bf_ref[...]
        yn = _rms(on_ref[0, rows, :].astype(jnp.float32)) * bn_ref[...]
        y = jnp.concatenate([yf.astype(jnp.bfloat16), yn.astype(jnp.bfloat16)], axis=1)
        h = x_ref[0, rows, :] + g_ref[0] * _dot(y, w_ref[...])
        h_ref[0, rows, :] = h
        u = _rms(h) * nw_ref[...]
        u_ref[0, rows, :] = (u * (1.0 + sc_ref[0]) + sh_ref[0]).astype(jnp.bfloat16)


def _outproj(o_f, o_n, beta_f, beta_n, w_out, x, gate, norm_w, sc, sh):
    nb, s, d = x.shape
    vec = lambda n: pl.BlockSpec((1, n), lambda b, i: (0, 0))
    bvec = pl.BlockSpec((1, 1, d), lambda b, i: (b, 0, 0))
    rows = lambda n: pl.BlockSpec((1, OUT_TM, n), lambda b, i: (b, i, 0))
    return pl.pallas_call(
        _outproj_kernel,
        out_shape=(jax.ShapeDtypeStruct((nb, s, d), jnp.float32),
                   jax.ShapeDtypeStruct((nb, s, d), jnp.bfloat16)),
        grid=(nb, s // OUT_TM),
        in_specs=[
            rows(FOX_W), rows(NSA_W), vec(FOX_W), vec(NSA_W),
            pl.BlockSpec((FOX_W + NSA_W, d), lambda b, i: (0, 0),
                         pipeline_mode=pl.Buffered(1)),
            rows(d), bvec, vec(d), bvec, bvec,
        ],
        out_specs=(rows(d), rows(d)),
        compiler_params=_cparams(("arbitrary", "arbitrary")),
        name="outproj",
    )(o_f, o_n, beta_f.reshape(1, FOX_W), beta_n.reshape(1, NSA_W), w_out, x,
      gate.reshape(nb, 1, d), norm_w.reshape(1, d), sc.reshape(nb, 1, d),
      sh.reshape(nb, 1, d))


FFN_TM = 1024
FFN_TF = 512
FFN_RC = 512


def _ffn_kernel(u_ref, h_ref, g_ref, fw_ref, wg_ref, wu_ref, wd_ref, o_ref, *, final):
    j = pl.program_id(2)

    @pl.when(j == 0)
    def _():
        o_ref[...] = jnp.zeros_like(o_ref)

    for r in range(FFN_TM // FFN_RC):
        rows = slice(r * FFN_RC, (r + 1) * FFN_RC)
        u = u_ref[0, rows, :]
        a = _dot(u, wg_ref[...])
        b = _dot(u, wu_ref[...])
        mid = (a * jax.nn.sigmoid(a) * b).astype(jnp.bfloat16)
        o_ref[0, rows, :] += _dot(mid, wd_ref[...])

    @pl.when(j == pl.num_programs(2) - 1)
    def _():
        h2 = h_ref[0] + g_ref[0] * o_ref[0]
        o_ref[0] = _rms(h2) * fw_ref[...] if final else h2


def _ffn(u2, h1, gate, final_w, w_gate, w_up, w_down, final):
    nb, s, d = h1.shape
    f = w_gate.shape[1]
    return pl.pallas_call(
        functools.partial(_ffn_kernel, final=final),
        out_shape=jax.ShapeDtypeStruct((nb, s, d), jnp.float32),
        grid=(nb, s // FFN_TM, f // FFN_TF),
        in_specs=[
            pl.BlockSpec((1, FFN_TM, d), lambda b, i, j: (b, i, 0)),
            pl.BlockSpec((1, FFN_TM, d), lambda b, i, j: (b, i, 0),
                         pipeline_mode=pl.Buffered(1)),
            pl.BlockSpec((1, 1, d), lambda b, i, j: (b, 0, 0)),
            pl.BlockSpec((1, d), lambda b, i, j: (0, 0)),
            pl.BlockSpec((d, FFN_TF), lambda b, i, j: (0, j)),
            pl.BlockSpec((d, FFN_TF), lambda b, i, j: (0, j)),
            pl.BlockSpec((FFN_TF, d), lambda b, i, j: (j, 0)),
        ],
        out_specs=pl.BlockSpec((1, FFN_TM, d), lambda b, i, j: (b, i, 0)),
        compiler_params=pltpu.CompilerParams(
            dimension_semantics=("arbitrary", "arbitrary", "arbitrary"),
            vmem_limit_bytes=FFN_VMEM_LIMIT),
        name="ffn",
    )(u2, h1, gate.reshape(nb, 1, d), final_w.reshape(1, d), w_gate, w_up, w_down)


def _split_w_in(w_in):
    sizes = ([FOX_W] * 3 + [FOX_HEADS] + [NSA_W] + [KV_W] * 6 + [3 * NSA_HEADS])
    offs = [0]
    for sz in sizes:
        offs.append(offs[-1] + sz)
    wt = w_in.T
    piece = lambda n: wt[offs[n]:offs[n + 1]].astype(jnp.bfloat16)
    main = jnp.concatenate([piece(n) for n in (0, 1, 2, 4, 7, 9, 8, 10, 5, 6)], axis=0)
    small = jnp.concatenate(
        [piece(3), piece(11),
         jnp.zeros((Z_SMALL - FOX_HEADS - 3 * NSA_HEADS, w_in.shape[0]), jnp.bfloat16)], axis=0)
    return main, small


def _overlap_t(n_cmp_pad, n_slc):
    cs = jnp.arange(n_cmp_pad) * CMP_STRIDE
    ss = jnp.arange(n_slc) * SLC_LEN
    ov = (jnp.minimum(cs[None, :] + CMP_LEN, ss[:, None] + SLC_LEN)
          - jnp.maximum(cs[None, :], ss[:, None]))
    ov = jnp.clip(ov, 0).astype(jnp.float32) / CMP_STRIDE
    return ov.astype(jnp.bfloat16)


def kernel(x, c, positions, w_ada, b_ada, norm_attn, norm_ffn, w_in, b_fgate, cmp_pos,
           w_kc1, w_kc2, w_vc1, w_vc2, beta_fox, beta_nsa, w_out, w_gate, w_up, w_down,
           final_norm):
    nb, s, d = x.shape
    depth = w_ada.shape[0]
    n_cmp_pad = s // CMP_STRIDE
    cmp_end = jnp.arange(n_cmp_pad) * CMP_STRIDE + (CMP_LEN - 1)
    pos_c = jnp.take(positions, jnp.minimum(cmp_end, s - 1), axis=1)
    cos, sin = _rope_tables(positions, 512)
    cos_c, sin_c = _rope_tables(pos_c, n_cmp_pad)
    ovt = _overlap_t(n_cmp_pad, s // SLC_LEN)

    h = x
    for l in range(depth):
        ct = c.T
        b_ada_l = b_ada[l].reshape(1, 6 * d)
        sh1, sc1 = jnp.split(_ada(ct, w_ada[l], b_ada_l, 2 * d), 2, axis=-1)
        w_main, w_small = _split_w_in(w_in[l])
        z_main, z_small, zc = _inproj(h, norm_attn[l], sc1, sh1, w_main, w_small, cos, sin)
        cum = _decay(z_small, b_fgate[l])

        w1 = jnp.stack([w_kc1[l], w_vc1[l]]).astype(jnp.bfloat16)
        w1 = w1.reshape(2, CMP_LEN, HEAD_DIM, CMP_HIDDEN)
        w2 = jnp.stack([w_kc2[l], w_vc2[l]]).astype(jnp.bfloat16)
        kvc = _compress(zc, w1, w2, cmp_pos[l], cos_c, sin_c)

        o_f, (wo, wg, wu, wd) = _fox(z_main, cum, (w_out[l], w_gate[l], w_up[l], w_down[l]))
        o_n, mod_rest = _nsa(z_main, z_small, kvc, ovt, ct, w_ada[l], b_ada_l, 2 * d)
        g1, sh2, sc2, g2 = jnp.split(mod_rest, 4, axis=-1)
        h1, u2 = _outproj(o_f, o_n, beta_fox[l], beta_nsa[l], wo, h, g1, norm_ffn[l], sc2, sh2)
        h = _ffn(u2, h1, g2, final_norm, wg, wu, wd, final=(l == depth - 1))
    return h
```

```python
import functools

import jax
import jax.numpy as jnp
from jax import lax
from jax.experimental import pallas as pl
from jax.experimental.pallas import tpu as pltpu

D_MODEL = 2048
HEAD_DIM = 128
FOX_HEADS = 8
NSA_HEADS = 8
NSA_KV_HEADS = 2
NSA_GROUP = NSA_HEADS // NSA_KV_HEADS
CMP_LEN = 32
CMP_STRIDE = 16
CMP_HIDDEN = 2 * HEAD_DIM
SLC_LEN = 64
SLC_TOPK = 16
WINDOW = 512
D_FF = 5632
ROPE_THETA = 10000.0
NORM_EPS = 1e-6
MASK_VALUE = -1e30
SEL_BONUS = 1e6

FOX_W = FOX_HEADS * HEAD_DIM
NSA_W = NSA_HEADS * HEAD_DIM
KV_W = NSA_KV_HEADS * HEAD_DIM
Z_MAIN = 3 * FOX_W + NSA_W + 6 * KV_W
Z_SMALL = 128
COL_QN = 3 * FOX_W
COL_KC = COL_QN + NSA_W
COL_KS = COL_KC + 2 * KV_W
COL_VS = COL_KS + KV_W
COL_KW = COL_VS + KV_W
COL_VW = COL_KW + KV_W

VMEM_LIMIT = 48 * 1024 * 1024
FFN_VMEM_LIMIT = 56 * 1024 * 1024
SCALE = HEAD_DIM ** -0.5
LOG2E = 1.4426950408889634


def _cparams(sem):
    return pltpu.CompilerParams(dimension_semantics=sem, vmem_limit_bytes=VMEM_LIMIT)


def _dot(a, b):
    return jnp.dot(a, b, preferred_element_type=jnp.float32)


def _dot_nt(a, b):
    return lax.dot_general(a, b, (((1,), (1,)), ((), ())), preferred_element_type=jnp.float32)


def _rms(x):
    return x * lax.rsqrt(jnp.mean(x * x, axis=-1, keepdims=True) + NORM_EPS)


ADA_TN = 1024
ADA_KC = 256


def _ada_kernel(ct_ref, w_ref, b_ref, o_ref, *, nb):
    tn = w_ref.shape[1]
    ct = ct_ref[...]
    st = ct * jax.nn.sigmoid(ct)

    accs = [jnp.zeros((8, tn), jnp.float32) for _ in range(nb)]
    for kc in range(D_MODEL // ADA_KC):
        w = w_ref[kc * ADA_KC:(kc + 1) * ADA_KC, :]
        for b in range(nb):
            s = st[kc * ADA_KC:(kc + 1) * ADA_KC, b:b + 1]
            accs[b] = accs[b] + jnp.sum((w * s).reshape(ADA_KC // 8, 8, tn), axis=0)
    for b in range(nb):
        o_ref[b:b + 1, :] = jnp.sum(accs[b], axis=0, keepdims=True) + b_ref[...]


def _ada(c, w_ada, b_ada):
    nb = c.shape[0]
    n = w_ada.shape[1]
    return pl.pallas_call(
        functools.partial(_ada_kernel, nb=nb),
        out_shape=jax.ShapeDtypeStruct((nb, n), jnp.float32),
        grid=(n // ADA_TN,),
        in_specs=[
            pl.BlockSpec((D_MODEL, nb), lambda j: (0, 0)),
            pl.BlockSpec((D_MODEL, ADA_TN), lambda j: (0, j)),
            pl.BlockSpec((1, ADA_TN), lambda j: (0, j)),
        ],
        out_specs=pl.BlockSpec((nb, ADA_TN), lambda j: (0, j)),
        compiler_params=_cparams(("arbitrary",)),
        name="ada",
    )(c.T, w_ada, b_ada.reshape(1, n))


def _rope_table_kernel(pos_ref, inv_ref, cos_ref, sin_ref):
    half = pos_ref.shape[1] // 2
    lane = lax.broadcasted_iota(jnp.int32, (half, HEAD_DIM), 1)
    low = lane < HEAD_DIM // 2
    pos = jnp.where(low, pos_ref[0, 0:half, :], pos_ref[0, half:2 * half, :])
    ang = pos.astype(jnp.float32) * inv_ref[...]
    cos = jnp.cos(ang)
    sin = jnp.sin(ang)
    cos_sw = pltpu.roll(cos, HEAD_DIM // 2, axis=1)
    sin_sw = pltpu.roll(sin, HEAD_DIM // 2, axis=1)
    cos_ref[0, 0:half, :] = jnp.where(low, cos, cos_sw)
    cos_ref[0, half:2 * half, :] = jnp.where(low, cos_sw, cos)
    sin_ref[0, 0:half, :] = jnp.where(low, -sin, sin_sw)
    sin_ref[0, half:2 * half, :] = jnp.where(low, -sin_sw, sin)


def _rope_tables(pos, ts):
    nb, n = pos.shape
    inv = ROPE_THETA ** (-jnp.arange(0, HEAD_DIM, 2, dtype=jnp.float32) / HEAD_DIM)
    inv2 = jnp.concatenate([inv, inv]).reshape(1, HEAD_DIM)
    shp = jax.ShapeDtypeStruct((nb, n, HEAD_DIM), jnp.float32)
    spec = pl.BlockSpec((1, ts, HEAD_DIM), lambda b, i: (b, i, 0))
    return pl.pallas_call(
        _rope_table_kernel,
        out_shape=(shp, shp),
        grid=(nb, n // ts),
        in_specs=[pl.BlockSpec((1, ts, 1), lambda b, i: (b, i, 0)),
                  pl.BlockSpec((1, HEAD_DIM), lambda b, i: (0, 0))],
        out_specs=(spec, spec),
        compiler_params=_cparams(("arbitrary", "arbitrary")),
        name="rope_tables",
    )(pos.reshape(nb, n, 1), inv2)


def _apply_rope(x, cos, sin):
    return x * cos + pltpu.roll(x, HEAD_DIM // 2, axis=1) * sin


IN_TM = 1024
IN_TN = 512


def _inproj_kernel(x_ref, g_ref, sc_ref, sh_ref, w_ref, ws_ref, cos_ref, sin_ref,
                   z_ref, zs_ref, zc_ref, u_ref):
    j = pl.program_id(2)

    @pl.when(j == 0)
    def _():
        u = _rms(x_ref[0]) * g_ref[...]
        u = u * (1.0 + sc_ref[0]) + sh_ref[0]
        ub = u.astype(jnp.bfloat16)
        u_ref[...] = ub
        zs_ref[0] = _dot_nt(ub, ws_ref[...])

    acc = _dot_nt(u_ref[...], w_ref[...])
    rope_all = (j >= COL_QN // IN_TN) & (j < COL_KC // IN_TN)
    rope_half = (j == COL_KS // IN_TN) | (j == COL_KW // IN_TN)
    is_q = (j < FOX_W // IN_TN) | rope_all
    acc = acc * jnp.where(is_q, SCALE * LOG2E, 1.0)
    n_heads = IN_TN // HEAD_DIM

    def roped(n_rope):
        cos = cos_ref[0]
        sin = sin_ref[0]
        parts = []
        for h in range(n_heads):
            a = acc[:, h * HEAD_DIM:(h + 1) * HEAD_DIM]
            parts.append(_apply_rope(a, cos, sin) if h < n_rope else a)
        return jnp.concatenate(parts, axis=1).astype(z_ref.dtype)

    @pl.when(rope_all)
    def _():
        z_ref[0] = roped(n_heads)

    @pl.when(rope_half)
    def _():
        z_ref[0] = roped(KV_W // HEAD_DIM)

    @pl.when(jnp.logical_not(rope_all | rope_half))
    def _():
        z_ref[0] = acc.astype(z_ref.dtype)

    @pl.when(j == COL_KC // IN_TN)
    def _():
        zc_ref[0] = acc


def _inproj(x, norm_w, sc, sh, w_main, w_small, cos, sin):
    nb, s, d = x.shape
    grid = (nb, s // IN_TM, Z_MAIN // IN_TN)
    return pl.pallas_call(
        _inproj_kernel,
        out_shape=(jax.ShapeDtypeStruct((nb, s, Z_MAIN), jnp.bfloat16),
                   jax.ShapeDtypeStruct((nb, s, Z_SMALL), jnp.float32),
                   jax.ShapeDtypeStruct((nb, s, IN_TN), jnp.float32)),
        grid=grid,
        in_specs=[
            pl.BlockSpec((1, IN_TM, d), lambda b, i, j: (b, i, 0)),
            pl.BlockSpec((1, d), lambda b, i, j: (0, 0)),
            pl.BlockSpec((1, 1, d), lambda b, i, j: (b, 0, 0)),
            pl.BlockSpec((1, 1, d), lambda b, i, j: (b, 0, 0)),
            pl.BlockSpec((IN_TN, d), lambda b, i, j: (j, 0)),
            pl.BlockSpec((Z_SMALL, d), lambda b, i, j: (0, 0)),
            pl.BlockSpec((1, IN_TM, HEAD_DIM), lambda b, i, j: (b, i, 0)),
            pl.BlockSpec((1, IN_TM, HEAD_DIM), lambda b, i, j: (b, i, 0)),
        ],
        out_specs=(pl.BlockSpec((1, IN_TM, IN_TN), lambda b, i, j: (b, i, j)),
                   pl.BlockSpec((1, IN_TM, Z_SMALL), lambda b, i, j: (b, i, 0)),
                   pl.BlockSpec((1, IN_TM, IN_TN), lambda b, i, j: (b, i, 0))),
        scratch_shapes=[pltpu.VMEM((IN_TM, d), jnp.bfloat16)],
        compiler_params=_cparams(("arbitrary", "arbitrary", "arbitrary")),
        name="inproj",
    )(x, norm_w.reshape(1, d), sc.reshape(nb, 1, d), sh.reshape(nb, 1, d),
      w_main, w_small, cos, sin)


DEC_T = 2048


def _decay_kernel(zs_ref, bf_ref, cum_ref, carry_ref):
    @pl.when(pl.program_id(1) == 0)
    def _():
        carry_ref[...] = jnp.zeros_like(carry_ref)

    xt = zs_ref[0].T[0:FOX_HEADS, :] + bf_ref[...]
    lf = jnp.minimum(xt, 0.0) - jnp.log1p(jnp.exp(-jnp.abs(xt)))
    lane = lax.broadcasted_iota(jnp.int32, lf.shape, 1)
    sh = 1
    while sh < DEC_T:
        lf = lf + jnp.where(lane >= sh, pltpu.roll(lf, sh, axis=1), 0.0)
        sh *= 2
    cum = lf + carry_ref[:, 0:1]
    cum_ref[0] = cum
    carry_ref[...] = jnp.broadcast_to(cum[:, DEC_T - 1:DEC_T], carry_ref.shape)


def _decay(z_small, b_fgate):
    nb, s, _ = z_small.shape
    return pl.pallas_call(
        _decay_kernel,
        out_shape=jax.ShapeDtypeStruct((nb, FOX_HEADS, s), jnp.float32),
        grid=(nb, s // DEC_T),
        in_specs=[pl.BlockSpec((1, DEC_T, Z_SMALL), lambda b, i: (b, i, 0)),
                  pl.BlockSpec((FOX_HEADS, 1), lambda b, i: (0, 0))],
        out_specs=pl.BlockSpec((1, FOX_HEADS, DEC_T), lambda b, i: (b, 0, i)),
        scratch_shapes=[pltpu.VMEM((FOX_HEADS, 128), jnp.float32)],
        compiler_params=_cparams(("arbitrary", "arbitrary")),
        name="decay",
    )(z_small, b_fgate.reshape(FOX_HEADS, 1))


def _compress_kernel(x_ref, w1_ref, w2_ref, pos_ref, cos_ref, sin_ref, o_ref):
    kind = pl.program_id(1)
    nc = x_ref.shape[1] // CMP_STRIDE
    p = jnp.zeros((nc, CMP_HIDDEN), jnp.float32)
    q = jnp.zeros((nc, CMP_HIDDEN), jnp.float32)
    for l in range(CMP_STRIDE):
        xl = x_ref[0, pl.ds(l, nc, stride=CMP_STRIDE), :].astype(jnp.bfloat16)
        p = p + _dot(xl, w1_ref[0, l])
        q = q + _dot(xl, w1_ref[0, CMP_STRIDE + l])
    posb = jnp.broadcast_to(pos_ref[...], (8, CMP_LEN * HEAD_DIM)).astype(jnp.bfloat16)
    w1_flat = w1_ref[0].reshape(CMP_LEN * HEAD_DIM, CMP_HIDDEN)
    pterm = _dot(posb, w1_flat)[0:1, :]
    h = p + pltpu.roll(q, nc - 1, axis=0) + pterm
    a = jax.nn.gelu(h).astype(jnp.bfloat16)
    out = _dot(a, w2_ref[0])

    @pl.when(kind < NSA_KV_HEADS)
    def _():
        o_ref[0, 0] = _apply_rope(out, cos_ref[0], sin_ref[0]).astype(o_ref.dtype)

    @pl.when(kind >= NSA_KV_HEADS)
    def _():
        o_ref[0, 0] = out.astype(o_ref.dtype)


def _compress(zc, w1, w2, cmp_pos, cos_c, sin_c):
    nb, s, _ = zc.shape
    nk = 2 * NSA_KV_HEADS
    nc = s // CMP_STRIDE
    return pl.pallas_call(
        _compress_kernel,
        out_shape=jax.ShapeDtypeStruct((nb, nk, nc, HEAD_DIM), jnp.bfloat16),
        grid=(nb, nk),
        in_specs=[
            pl.BlockSpec((1, s, HEAD_DIM), lambda b, k: (b, 0, k)),
            pl.BlockSpec((1, CMP_LEN, HEAD_DIM, CMP_HIDDEN),
                         lambda b, k: (k // NSA_KV_HEADS, 0, 0, 0)),
            pl.BlockSpec((1, CMP_HIDDEN, HEAD_DIM), lambda b, k: (k // NSA_KV_HEADS, 0, 0)),
            pl.BlockSpec((1, CMP_LEN * HEAD_DIM), lambda b, k: (0, 0)),
            pl.BlockSpec((1, nc, HEAD_DIM), lambda b, k: (b, 0, 0)),
            pl.BlockSpec((1, nc, HEAD_DIM), lambda b, k: (b, 0, 0)),
        ],
        out_specs=pl.BlockSpec((1, 1, nc, HEAD_DIM), lambda b, k: (b, k, 0, 0)),
        compiler_params=_cparams(("arbitrary", "arbitrary")),
        name="compress",
    )(zc, w1, w2, cmp_pos.reshape(1, CMP_LEN * HEAD_DIM), cos_c, sin_c)


FOX_T = 512


def _softmax_step(s, m, acc, v):
    m_new = jnp.maximum(m, jnp.max(s, axis=1, keepdims=True))
    alpha = jnp.exp2(m - m_new)
    p = jnp.exp2((s - m_new).astype(jnp.bfloat16))
    va = jnp.concatenate([v, jnp.ones_like(v)], axis=1)
    return m_new, alpha * acc + _dot(p, va)


def _softmax_init(rows):
    return (jnp.full((rows, 1), MASK_VALUE, jnp.float32),
            jnp.zeros((rows, 2 * HEAD_DIM), jnp.float32))


def _softmax_result(acc):
    return acc[:, 0:HEAD_DIM] * (1.0 / acc[:, HEAD_DIM:2 * HEAD_DIM])


FOX_HB = 4


def _fox_kernel(q_ref, k_ref, v_ref, cum_ref, *rest, n_cast):
    o_ref = rest[n_cast]
    for w_ref, wb_ref in zip(rest[:n_cast], rest[n_cast + 1:]):
        wb_ref[...] = w_ref[...].astype(wb_ref.dtype)
    hb = pl.program_id(1)
    i = pl.program_id(2)
    t = FOX_T
    cols = [slice(n * HEAD_DIM, (n + 1) * HEAD_DIM) for n in range(FOX_HB)]
    qs = [q_ref[0, :, c] for c in cols]

    def tile(k0, width, carry, diag):
        k0 = pl.multiple_of(k0, t)
        logits = [_dot_nt(qs[n], k_ref[0, pl.ds(k0, width), cols[n]]) for n in range(FOX_HB)]
        out = []
        for n, (m, acc) in enumerate(carry):
            v = v_ref[0, pl.ds(k0, width), cols[n]]
            ck = cum_ref[0, pl.ds(hb * FOX_HB + n, 1), pl.ds(k0, width)] * LOG2E
            s = logits[n] - ck
            if diag:
                r = lax.broadcasted_iota(jnp.int32, s.shape, 0)
                c = lax.broadcasted_iota(jnp.int32, s.shape, 1)
                s = jnp.where(c <= r + (width - t), s, MASK_VALUE)
            out.append(_softmax_step(s, m, acc, v))
        return tuple(out)

    init = tuple(_softmax_init(t) for _ in range(FOX_HB))
    n_pairs = i >> 1
    carry = lax.fori_loop(0, n_pairs, lambda p, c: tile(p * (2 * t), 2 * t, c, False), init)
    k_tail = n_pairs * (2 * t)
    carry = lax.cond((i & 1) == 0,
                     lambda c: tile(k_tail, t, c, True),
                     lambda c: tile(k_tail, 2 * t, c, True), carry)
    for n, (_, acc) in enumerate(carry):
        o_ref[0, :, cols[n]] = _softmax_result(acc).astype(o_ref.dtype)


def _fox(z_main, cum, weights):
    nb, s, _ = z_main.shape
    w = FOX_HB * HEAD_DIM
    grid = (nb, FOX_HEADS // FOX_HB, s // FOX_T)
    n_steps = grid[0] * grid[1] * grid[2]
    step = lambda b, h, i: ((b * grid[1] + h) * grid[2] + i, 0)
    slabs = []
    for wt in weights:
        rows = wt.shape[0] // n_steps
        assert rows * n_steps == wt.shape[0] and rows % BF16_SUBLANES == 0, wt.shape
        slabs.append(pl.BlockSpec((rows, wt.shape[1]), step))
    outs = pl.pallas_call(
        functools.partial(_fox_kernel, n_cast=len(weights)),
        out_shape=[jax.ShapeDtypeStruct((nb, s, FOX_W), jnp.bfloat16)]
        + [jax.ShapeDtypeStruct(wt.shape, jnp.bfloat16) for wt in weights],
        grid=grid,
        in_specs=[
            pl.BlockSpec((1, FOX_T, w), lambda b, h, i: (b, i, h)),
            pl.BlockSpec((1, s, w), lambda b, h, i: (b, 0, FOX_W // w + h)),
            pl.BlockSpec((1, s, w), lambda b, h, i: (b, 0, 2 * FOX_W // w + h)),
            pl.BlockSpec((1, FOX_HEADS, s), lambda b, h, i: (b, 0, 0)),
        ] + slabs,
        out_specs=[pl.BlockSpec((1, FOX_T, w), lambda b, h, i: (b, i, h))] + slabs,
        compiler_params=_cparams(("arbitrary", "arbitrary", "arbitrary")),
        name="fox",
    )(z_main, z_main, z_main, cum, *weights)
    return outs[0], outs[1:]


NSA_TQ = 512
NSA_PARTS = 4
NSEL_PAD = 128
SLC_SHIFT = SLC_LEN.bit_length() - 1
assert 1 << SLC_SHIFT == SLC_LEN
assert WINDOW % NSA_TQ == 0 and NSA_TQ % SLC_LEN == 0
SUBLANES = 8
BF16_SUBLANES = 16


def _block_ranks(score, n_slc):
    tq = score.shape[1]
    groups = [score[SUBLANES * v:SUBLANES * (v + 1)] for v in range(n_slc // SUBLANES)]
    ranks = [jnp.zeros((SUBLANES, tq), jnp.float32) for _ in groups]
    for m in range(n_slc):
        row = jnp.broadcast_to(score[m:m + 1, :], (SUBLANES, tq))
        for v, grp in enumerate(groups):
            lo = SUBLANES * v
            if lo > m:
                before = row >= grp
            elif lo + SUBLANES - 1 <= m:
                before = row > grp
            else:
                n_i = lo + lax.broadcasted_iota(jnp.int32, grp.shape, 0)
                before = (row > grp) | ((row == grp) & (n_i > m))
            ranks[v] = ranks[v] + jnp.where(before, 1.0, 0.0)
    return jnp.concatenate(ranks, axis=0)


def _nsa_kernel(q_ref, zs_ref, kc_ref, vc_ref, ks_ref, vs_ref, kw_ref, vw_ref, ovt_ref,
                o_ref, ka_ref, *, seq):
    g = pl.program_id(1)
    i = pl.program_id(2)
    tq = NSA_TQ
    q0 = i * tq

    @pl.when(i == 0)
    def _():
        ka_ref[:, 0:HEAD_DIM] = ks_ref[0]
        tok = lax.broadcasted_iota(jnp.int32, (seq, NSEL_PAD), 0)
        n_i = lax.broadcasted_iota(jnp.int32, (seq, NSEL_PAD), 1)
        ka_ref[:, HEAD_DIM:HEAD_DIM + NSEL_PAD] = jnp.where(
            n_i == (tok >> SLC_SHIFT), 1.0, 0.0).astype(jnp.bfloat16)

    qb = q_ref[0]
    hpp = NSA_GROUP // NSA_PARTS
    parts = range(NSA_PARTS)
    qh = [jnp.concatenate(
        [qb[:, r * HEAD_DIM:(r + 1) * HEAD_DIM] for r in range(h * hpp, (h + 1) * hpp)], axis=0)
        for h in parts]
    prow = hpp * tq

    def row_query(shape):
        return lax.broadcasted_iota(jnp.int32, shape, 0) & (tq - 1)

    kc = kc_ref[0, 0]
    vc = vc_ref[0, 0]
    s_cs = [_dot_nt(qh[h], kc) for h in parts]
    p_cs = []
    for s_c in s_cs:
        cend = lax.broadcasted_iota(jnp.int32, s_c.shape, 1) * CMP_STRIDE + (CMP_LEN - 1)
        s_c = jnp.where(cend <= q0 + row_query(s_c.shape), s_c, MASK_VALUE)
        m_c = jnp.maximum(jnp.max(s_c, axis=1, keepdims=True), 0.1 * MASK_VALUE)
        e_c = jnp.exp2(s_c - m_c)
        p_cs.append(e_c * (1.0 / jnp.maximum(jnp.sum(e_c, axis=1, keepdims=True), 1e-30)))
    o_c = [_dot(p_c.astype(jnp.bfloat16), vc) for p_c in p_cs]

    psum = None
    for p_c in p_cs:
        for r in range(hpp):
            term = p_c[r * tq:(r + 1) * tq]
            psum = term if psum is None else psum + term
    p_hi = psum.astype(jnp.bfloat16)
    p_lo = (psum - p_hi.astype(jnp.float32)).astype(jnp.bfloat16)
    ovt = ovt_ref[...]
    imp_t = _dot_nt(ovt, p_hi) + _dot_nt(ovt, p_lo)
    n_slc = seq // SLC_LEN
    blk = lax.broadcasted_iota(jnp.int32, imp_t.shape, 0)
    cur = (q0 + lax.broadcasted_iota(jnp.int32, imp_t.shape, 1)) >> SLC_SHIFT
    forced = (blk == 0) | (blk == cur) | (blk == cur - 1)
    score = jnp.where(forced, SEL_BONUS, imp_t)
    score = jnp.where(blk <= cur, score, -SEL_BONUS)
    rank = _block_ranks(score, n_slc)
    bias_t = jnp.where(rank < min(SLC_TOPK, n_slc), 0.0, MASK_VALUE)
    bias_t = jnp.concatenate(
        [bias_t, jnp.zeros((NSEL_PAD - n_slc, tq), jnp.float32)], axis=0)
    bias = bias_t.T.astype(jnp.bfloat16)
    bias_p = jnp.concatenate([bias] * hpp, axis=0)
    qa = [jnp.concatenate([qh[h], bias_p], axis=1) for h in parts]

    def sel_tile(k0, width, carry, diag):
        k0 = pl.multiple_of(k0, tq)
        logits = [_dot_nt(qa[h], ka_ref[pl.ds(k0, width), :]) for h in parts]
        out = []
        for s, (m, acc) in zip(logits, carry):
            if diag:
                c = lax.broadcasted_iota(jnp.int32, s.shape, 1)
                s = jnp.where(c <= row_query(s.shape) + (width - tq), s, MASK_VALUE)
            out.append(_softmax_step(s, m, acc, vs_ref[0, pl.ds(k0, width), :]))
        return tuple(out)

    init = tuple(_softmax_init(prow) for _ in parts)
    n_pairs = i >> 1
    carry = lax.fori_loop(0, n_pairs,
                          lambda p, c: sel_tile(p * (2 * tq), 2 * tq, c, False), init)
    k_tail = n_pairs * (2 * tq)
    carry = lax.cond((i & 1) == 0,
                     lambda c: sel_tile(k_tail, tq, c, True),
                     lambda c: sel_tile(k_tail, 2 * tq, c, True), carry)
    o_s = [_softmax_result(acc) for _, acc in carry]

    n_w = WINDOW // tq + 1
    starts = [q0 - WINDOW + t * tq for t in range(n_w)]
    offs = [pl.multiple_of(jnp.maximum(st, 0), tq) for st in starts]
    s_w = [[_dot_nt(qh[h], kw_ref[0, pl.ds(offs[t], tq), :]) for h in parts] for t in range(n_w)]
    for t in range(n_w):
        for h in parts:
            s = s_w[t][h]
            c = lax.broadcasted_iota(jnp.int32, s.shape, 1)
            if t == 0:
                s = jnp.where(c > row_query(s.shape), s, MASK_VALUE)
            if t == n_w - 1:
                s = jnp.where(c <= row_query(s.shape), s, MASK_VALUE)
            else:
                s = jnp.where(starts[t] >= 0, s, MASK_VALUE)
            s_w[t][h] = s
    o_w = []
    for h in parts:
        m_w = s_w[0][h].max(axis=1, keepdims=True)
        for t in range(1, n_w):
            m_w = jnp.maximum(m_w, s_w[t][h].max(axis=1, keepdims=True))
        acc_w = jnp.zeros((prow, 2 * HEAD_DIM), jnp.float32)
        for t in range(n_w):
            p = jnp.exp2((s_w[t][h] - m_w).astype(jnp.bfloat16))
            v = vw_ref[0, pl.ds(offs[t], tq), :]
            acc_w = acc_w + _dot(p, jnp.concatenate([v, jnp.ones_like(v)], axis=1))
        o_w.append(_softmax_result(acc_w))

    gates = jax.nn.sigmoid(zs_ref[0])
    lane = lax.broadcasted_iota(jnp.int32, gates.shape, 1)
    outs = []
    for r in range(NSA_GROUP):
        h, rows = r // hpp, slice((r % hpp) * tq, (r % hpp + 1) * tq)
        o_r = jnp.zeros((tq, HEAD_DIM), jnp.float32)
        for br, o_b in enumerate((o_c, o_s, o_w)):
            col = FOX_HEADS + 3 * (g * NSA_GROUP + r) + br
            gcol = jnp.sum(jnp.where(lane == col, gates, 0.0), axis=1, keepdims=True)
            o_r = o_r + gcol * o_b[h][rows]
        outs.append(o_r)
    o_ref[0] = jnp.concatenate(outs, axis=1).astype(o_ref.dtype)


def _nsa(z_main, z_small, kvc, ovt):
    nb, s, _ = z_main.shape
    nc = kvc.shape[2]
    gw = NSA_GROUP * HEAD_DIM
    full = lambda col0: pl.BlockSpec(
        (1, s, HEAD_DIM), lambda b, g, i, c=col0 // HEAD_DIM: (b, 0, c + g))
    return pl.pallas_call(
        functools.partial(_nsa_kernel, seq=s),
        out_shape=jax.ShapeDtypeStruct((nb, s, NSA_W), jnp.bfloat16),
        grid=(nb, NSA_KV_HEADS, s // NSA_TQ),
        in_specs=[
            pl.BlockSpec((1, NSA_TQ, gw), lambda b, g, i: (b, i, COL_QN // gw + g)),
            pl.BlockSpec((1, NSA_TQ, Z_SMALL), lambda b, g, i: (b, i, 0)),
            pl.BlockSpec((1, 1, nc, HEAD_DIM), lambda b, g, i: (b, g, 0, 0)),
            pl.BlockSpec((1, 1, nc, HEAD_DIM), lambda b, g, i: (b, NSA_KV_HEADS + g, 0, 0)),
            full(COL_KS), full(COL_VS), full(COL_KW), full(COL_VW),
            pl.BlockSpec(ovt.shape, lambda b, g, i: (0, 0)),
        ],
        out_specs=pl.BlockSpec((1, NSA_TQ, gw), lambda b, g, i: (b, i, g)),
        scratch_shapes=[pltpu.VMEM((s, HEAD_DIM + NSEL_PAD), jnp.bfloat16)],
        compiler_params=_cparams(("arbitrary", "arbitrary", "arbitrary")),
        name="nsa",
    )(z_main, z_small, kvc, kvc, z_main, z_main, z_main, z_main, ovt)


OUT_TM = 512
OUT_RC = 256


def _outproj_kernel(of_ref, on_ref, bf_ref, bn_ref, w_ref, x_ref, g_ref, nw_ref, sc_ref,
                    sh_ref, h_ref, u_ref):
    for r in range(OUT_TM // OUT_RC):
        rows = slice(r * OUT_RC, (r + 1) * OUT_RC)
        yf = _rms(of_ref[0, rows, :].astype(jnp.float32)) * bf_ref[...]
        yn = _rms(on_ref[0, rows, :].astype(jnp.float32)) * bn_ref[...]
        y = jnp.concatenate([yf.astype(jnp.bfloat16), yn.astype(jnp.bfloat16)], axis=1)
        h = x_ref[0, rows, :] + g_ref[0] * _dot(y, w_ref[...])
        h_ref[0, rows, :] = h
        u = _rms(h) * nw_ref[...]
        u_ref[0, rows, :] = (u * (1.0 + sc_ref[0]) + sh_ref[0]).astype(jnp.bfloat16)


def _outproj(o_f, o_n, beta_f, beta_n, w_out, x, gate, norm_w, sc, sh):
    nb, s, d = x.shape
    vec = lambda n: pl.BlockSpec((1, n), lambda b, i: (0, 0))
    bvec = pl.BlockSpec((1, 1, d), lambda b, i: (b, 0, 0))
    rows = lambda n: pl.BlockSpec((1, OUT_TM, n), lambda b, i: (b, i, 0))
    return pl.pallas_call(
        _outproj_kernel,
        out_shape=(jax.ShapeDtypeStruct((nb, s, d), jnp.float32),
                   jax.ShapeDtypeStruct((nb, s, d), jnp.bfloat16)),
        grid=(nb, s // OUT_TM),
        in_specs=[
            rows(FOX_W), rows(NSA_W), vec(FOX_W), vec(NSA_W),
            pl.BlockSpec((FOX_W + NSA_W, d), lambda b, i: (0, 0),
                         pipeline_mode=pl.Buffered(1)),
            rows(d), bvec, vec(d), bvec, bvec,
        ],
        out_specs=(rows(d), rows(d)),
        compiler_params=_cparams(("arbitrary", "arbitrary")),
        name="outproj",
    )(o_f, o_n, beta_f.reshape(1, FOX_W), beta_n.reshape(1, NSA_W), w_out, x,
      gate.reshape(nb, 1, d), norm_w.reshape(1, d), sc.reshape(nb, 1, d),
      sh.reshape(nb, 1, d))


FFN_TM = 1024
FFN_TF = 512
FFN_RC = 512


def _ffn_kernel(u_ref, h_ref, g_ref, fw_ref, wg_ref, wu_ref, wd_ref, o_ref, *, final):
    j = pl.program_id(2)

    @pl.when(j == 0)
    def _():
        o_ref[...] = jnp.zeros_like(o_ref)

    for r in range(FFN_TM // FFN_RC):
        rows = slice(r * FFN_RC, (r + 1) * FFN_RC)
        u = u_ref[0, rows, :]
        a = _dot(u, wg_ref[...])
        b = _dot(u, wu_ref[...])
        mid = (a * jax.nn.sigmoid(a) * b).astype(jnp.bfloat16)
        o_ref[0, rows, :] += _dot(mid, wd_ref[...])

    @pl.when(j == pl.num_programs(2) - 1)
    def _():
        h2 = h_ref[0] + g_ref[0] * o_ref[0]
        o_ref[0] = _rms(h2) * fw_ref[...] if final else h2


def _ffn(u2, h1, gate, final_w, w_gate, w_up, w_down, final):
    nb, s, d = h1.shape
    f = w_gate.shape[1]
    return pl.pallas_call(
        functools.partial(_ffn_kernel, final=final),
        out_shape=jax.ShapeDtypeStruct((nb, s, d), jnp.float32),
        grid=(nb, s // FFN_TM, f // FFN_TF),
        in_specs=[
            pl.BlockSpec((1, FFN_TM, d), lambda b, i, j: (b, i, 0)),
            pl.BlockSpec((1, FFN_TM, d), lambda b, i, j: (b, i, 0),
                         pipeline_mode=pl.Buffered(1)),
            pl.BlockSpec((1, 1, d), lambda b, i, j: (b, 0, 0)),
            pl.BlockSpec((1, d), lambda b, i, j: (0, 0)),
            pl.BlockSpec((d, FFN_TF), lambda b, i, j: (0, j)),
            pl.BlockSpec((d, FFN_TF), lambda b, i, j: (0, j)),
            pl.BlockSpec((FFN_TF, d), lambda b, i, j: (j, 0)),
        ],
        out_specs=pl.BlockSpec((1, FFN_TM, d), lambda b, i, j: (b, i, 0)),
        compiler_params=pltpu.CompilerParams(
            dimension_semantics=("arbitrary", "arbitrary", "arbitrary"),
            vmem_limit_bytes=FFN_VMEM_LIMIT),
        name="ffn",
    )(u2, h1, gate.reshape(nb, 1, d), final_w.reshape(1, d), w_gate, w_up, w_down)


def _split_w_in(w_in):
    sizes = ([FOX_W] * 3 + [FOX_HEADS] + [NSA_W] + [KV_W] * 6 + [3 * NSA_HEADS])
    offs = [0]
    for sz in sizes:
        offs.append(offs[-1] + sz)
    wt = w_in.T.astype(jnp.bfloat16)
    piece = lambda n: wt[offs[n]:offs[n + 1]]
    main = jnp.concatenate([wt[offs[0]:offs[3]], wt[offs[4]:offs[11]]], axis=0)
    small = jnp.concatenate(
        [piece(3), piece(11),
         jnp.zeros((Z_SMALL - FOX_HEADS - 3 * NSA_HEADS, w_in.shape[0]), jnp.bfloat16)], axis=0)
    return main, small


def _overlap_t(n_cmp_pad, n_slc):
    cs = jnp.arange(n_cmp_pad) * CMP_STRIDE
    ss = jnp.arange(n_slc) * SLC_LEN
    ov = (jnp.minimum(cs[None, :] + CMP_LEN, ss[:, None] + SLC_LEN)
          - jnp.maximum(cs[None, :], ss[:, None]))
    ov = jnp.clip(ov, 0).astype(jnp.float32) / CMP_STRIDE
    return ov.astype(jnp.bfloat16)


def kernel(x, c, positions, w_ada, b_ada, norm_attn, norm_ffn, w_in, b_fgate, cmp_pos,
           w_kc1, w_kc2, w_vc1, w_vc2, beta_fox, beta_nsa, w_out, w_gate, w_up, w_down,
           final_norm):
    nb, s, d = x.shape
    depth = w_ada.shape[0]
    n_cmp_pad = s // CMP_STRIDE
    cmp_end = jnp.arange(n_cmp_pad) * CMP_STRIDE + (CMP_LEN - 1)
    pos_c = jnp.take(positions, jnp.minimum(cmp_end, s - 1), axis=1)
    cos, sin = _rope_tables(positions, 512)
    cos_c, sin_c = _rope_tables(pos_c, n_cmp_pad)
    ovt = _overlap_t(n_cmp_pad, s // SLC_LEN)

    h = x
    for l in range(depth):
        mod = _ada(c, w_ada[l], b_ada[l])
        sh1, sc1, g1, sh2, sc2, g2 = jnp.split(mod, 6, axis=-1)
        w_main, w_small = _split_w_in(w_in[l])
        z_main, z_small, zc = _inproj(h, norm_attn[l], sc1, sh1, w_main, w_small, cos, sin)
        cum = _decay(z_small, b_fgate[l])

        w1 = jnp.stack([w_kc1[l], w_vc1[l]]).astype(jnp.bfloat16)
        w1 = w1.reshape(2, CMP_LEN, HEAD_DIM, CMP_HIDDEN)
        w2 = jnp.stack([w_kc2[l], w_vc2[l]]).astype(jnp.bfloat16)
        kvc = _compress(zc, w1, w2, cmp_pos[l], cos_c, sin_c)

        o_f, (wo, wg, wu, wd) = _fox(z_main, cum, (w_out[l], w_gate[l], w_up[l], w_down[l]))
        o_n = _nsa(z_main, z_small, kvc, ovt)
        h1, u2 = _outproj(o_f, o_n, beta_fox[l], beta_nsa[l], wo, h, g1, norm_ffn[l], sc2, sh2)
        h = _ffn(u2, h1, g2, final_norm, wg, wu, wd, final=(l == depth - 1))
    return h
```

```python
import functools

import jax
import jax.numpy as jnp
from jax import lax
from jax.experimental import pallas as pl
from jax.experimental.pallas import tpu as pltpu

D_MODEL = 2048
HEAD_DIM = 128
FOX_HEADS = 8
NSA_HEADS = 8
NSA_KV_HEADS = 2
NSA_GROUP = NSA_HEADS // NSA_KV_HEADS
CMP_LEN = 32
CMP_STRIDE = 16
CMP_HIDDEN = 2 * HEAD_DIM
SLC_LEN = 64
SLC_TOPK = 16
WINDOW = 512
D_FF = 5632
ROPE_THETA = 10000.0
NORM_EPS = 1e-6
MASK_VALUE = -1e30
SEL_BONUS = 1e6

FOX_W = FOX_HEADS * HEAD_DIM
NSA_W = NSA_HEADS * HEAD_DIM
KV_W = NSA_KV_HEADS * HEAD_DIM
Z_MAIN = 3 * FOX_W + NSA_W + 6 * KV_W
Z_SMALL = 128
COL_QN = 3 * FOX_W
COL_KS = COL_QN + NSA_W
COL_KW = COL_KS + KV_W
COL_VS = COL_KW + KV_W
COL_VW = COL_VS + KV_W
COL_KC = COL_VW + KV_W

VMEM_LIMIT = 48 * 1024 * 1024
FFN_VMEM_LIMIT = 56 * 1024 * 1024
SCALE = HEAD_DIM ** -0.5
LOG2E = 1.4426950408889634


def _cparams(sem):
    return pltpu.CompilerParams(dimension_semantics=sem, vmem_limit_bytes=VMEM_LIMIT)


def _dot(a, b):
    return jnp.dot(a, b, preferred_element_type=jnp.float32)


def _dot_nt(a, b):
    return lax.dot_general(a, b, (((1,), (1,)), ((), ())), preferred_element_type=jnp.float32)


def _rms(x):
    return x * lax.rsqrt(jnp.mean(x * x, axis=-1, keepdims=True) + NORM_EPS)


ADA_TN = 1024
ADA_KC = 256


def _ada_kernel(ct_ref, w_ref, b_ref, o_ref, *, nb):
    tn = w_ref.shape[1]
    ct = ct_ref[...]
    st = ct * jax.nn.sigmoid(ct)

    accs = [jnp.zeros((8, tn), jnp.float32) for _ in range(nb)]
    for kc in range(D_MODEL // ADA_KC):
        w = w_ref[kc * ADA_KC:(kc + 1) * ADA_KC, :]
        for b in range(nb):
            s = st[kc * ADA_KC:(kc + 1) * ADA_KC, b:b + 1]
            accs[b] = accs[b] + jnp.sum((w * s).reshape(ADA_KC // 8, 8, tn), axis=0)
    for b in range(nb):
        o_ref[b:b + 1, :] = jnp.sum(accs[b], axis=0, keepdims=True) + b_ref[...]


def _ada(c, w_ada, b_ada):
    nb = c.shape[0]
    n = w_ada.shape[1]
    return pl.pallas_call(
        functools.partial(_ada_kernel, nb=nb),
        out_shape=jax.ShapeDtypeStruct((nb, n), jnp.float32),
        grid=(n // ADA_TN,),
        in_specs=[
            pl.BlockSpec((D_MODEL, nb), lambda j: (0, 0)),
            pl.BlockSpec((D_MODEL, ADA_TN), lambda j: (0, j)),
            pl.BlockSpec((1, ADA_TN), lambda j: (0, j)),
        ],
        out_specs=pl.BlockSpec((nb, ADA_TN), lambda j: (0, j)),
        compiler_params=_cparams(("arbitrary",)),
        name="ada",
    )(c.T, w_ada, b_ada.reshape(1, n))


def _rope_table_kernel(pos_ref, inv_ref, cos_ref, sin_ref):
    half = pos_ref.shape[1] // 2
    lane = lax.broadcasted_iota(jnp.int32, (half, HEAD_DIM), 1)
    low = lane < HEAD_DIM // 2
    pos = jnp.where(low, pos_ref[0, 0:half, :], pos_ref[0, half:2 * half, :])
    ang = pos.astype(jnp.float32) * inv_ref[...]
    cos = jnp.cos(ang)
    sin = jnp.sin(ang)
    cos_sw = pltpu.roll(cos, HEAD_DIM // 2, axis=1)
    sin_sw = pltpu.roll(sin, HEAD_DIM // 2, axis=1)
    cos_ref[0, 0:half, :] = jnp.where(low, cos, cos_sw)
    cos_ref[0, half:2 * half, :] = jnp.where(low, cos_sw, cos)
    sin_ref[0, 0:half, :] = jnp.where(low, -sin, sin_sw)
    sin_ref[0, half:2 * half, :] = jnp.where(low, -sin_sw, sin)


def _rope_tables(pos, ts):
    nb, n = pos.shape
    inv = ROPE_THETA ** (-jnp.arange(0, HEAD_DIM, 2, dtype=jnp.float32) / HEAD_DIM)
    inv2 = jnp.concatenate([inv, inv]).reshape(1, HEAD_DIM)
    shp = jax.ShapeDtypeStruct((nb, n, HEAD_DIM), jnp.float32)
    spec = pl.BlockSpec((1, ts, HEAD_DIM), lambda b, i: (b, i, 0))
    return pl.pallas_call(
        _rope_table_kernel,
        out_shape=(shp, shp),
        grid=(nb, n // ts),
        in_specs=[pl.BlockSpec((1, ts, 1), lambda b, i: (b, i, 0)),
                  pl.BlockSpec((1, HEAD_DIM), lambda b, i: (0, 0))],
        out_specs=(spec, spec),
        compiler_params=_cparams(("arbitrary", "arbitrary")),
        name="rope_tables",
    )(pos.reshape(nb, n, 1), inv2)


def _apply_rope(x, cos, sin):
    return x * cos + pltpu.roll(x, HEAD_DIM // 2, axis=1) * sin


IN_TM = 1024
IN_TN = 512


def _inproj_kernel(x_ref, g_ref, sc_ref, sh_ref, w_ref, ws_ref, cos_ref, sin_ref,
                   z_ref, zs_ref, zc_ref, u_ref):
    j = pl.program_id(2)

    @pl.when(j == 0)
    def _():
        u = _rms(x_ref[0]) * g_ref[...]
        u = u * (1.0 + sc_ref[0]) + sh_ref[0]
        ub = u.astype(jnp.bfloat16)
        u_ref[...] = ub
        zs_ref[0] = _dot_nt(ub, ws_ref[...])

    fox_q = j < FOX_W // IN_TN
    nsa_q = (j >= COL_QN // IN_TN) & (j < COL_KS // IN_TN)
    nsa_k = j == COL_KS // IN_TN
    cmp_kv = j == COL_KC // IN_TN

    def matmul():
        return _dot_nt(u_ref[...], w_ref[...])

    def roped(acc):
        cos = cos_ref[0]
        sin = sin_ref[0]
        parts = [_apply_rope(acc[:, h * HEAD_DIM:(h + 1) * HEAD_DIM], cos, sin)
                 for h in range(IN_TN // HEAD_DIM)]
        return jnp.concatenate(parts, axis=1).astype(z_ref.dtype)

    @pl.when(fox_q)
    def _():
        z_ref[0] = (matmul() * (SCALE * LOG2E)).astype(z_ref.dtype)

    @pl.when(nsa_q)
    def _():
        z_ref[0] = roped(matmul() * (SCALE * LOG2E))

    @pl.when(nsa_k)
    def _():
        z_ref[0] = roped(matmul())

    @pl.when(cmp_kv)
    def _():
        acc = matmul()
        z_ref[0] = acc.astype(z_ref.dtype)
        zc_ref[0] = acc

    @pl.when(jnp.logical_not(fox_q | nsa_q | nsa_k | cmp_kv))
    def _():
        z_ref[0] = matmul().astype(z_ref.dtype)


def _inproj(x, norm_w, sc, sh, w_main, w_small, cos, sin):
    nb, s, d = x.shape
    grid = (nb, s // IN_TM, Z_MAIN // IN_TN)
    return pl.pallas_call(
        _inproj_kernel,
        out_shape=(jax.ShapeDtypeStruct((nb, s, Z_MAIN), jnp.bfloat16),
                   jax.ShapeDtypeStruct((nb, s, Z_SMALL), jnp.float32),
                   jax.ShapeDtypeStruct((nb, s, IN_TN), jnp.float32)),
        grid=grid,
        in_specs=[
            pl.BlockSpec((1, IN_TM, d), lambda b, i, j: (b, i, 0)),
            pl.BlockSpec((1, d), lambda b, i, j: (0, 0)),
            pl.BlockSpec((1, 1, d), lambda b, i, j: (b, 0, 0)),
            pl.BlockSpec((1, 1, d), lambda b, i, j: (b, 0, 0)),
            pl.BlockSpec((IN_TN, d), lambda b, i, j: (j, 0)),
            pl.BlockSpec((Z_SMALL, d), lambda b, i, j: (0, 0)),
            pl.BlockSpec((1, IN_TM, HEAD_DIM), lambda b, i, j: (b, i, 0)),
            pl.BlockSpec((1, IN_TM, HEAD_DIM), lambda b, i, j: (b, i, 0)),
        ],
        out_specs=(pl.BlockSpec((1, IN_TM, IN_TN), lambda b, i, j: (b, i, j)),
                   pl.BlockSpec((1, IN_TM, Z_SMALL), lambda b, i, j: (b, i, 0)),
                   pl.BlockSpec((1, IN_TM, IN_TN), lambda b, i, j: (b, i, 0))),
        scratch_shapes=[pltpu.VMEM((IN_TM, d), jnp.bfloat16)],
        compiler_params=_cparams(("arbitrary", "arbitrary", "arbitrary")),
        name="inproj",
    )(x, norm_w.reshape(1, d), sc.reshape(nb, 1, d), sh.reshape(nb, 1, d),
      w_main, w_small, cos, sin)


DEC_T = 2048


def _decay_kernel(zs_ref, bf_ref, cum_ref, carry_ref):
    @pl.when(pl.program_id(1) == 0)
    def _():
        carry_ref[...] = jnp.zeros_like(carry_ref)

    xt = zs_ref[0].T[0:FOX_HEADS, :] + bf_ref[...]
    lf = jnp.minimum(xt, 0.0) - jnp.log1p(jnp.exp(-jnp.abs(xt)))
    lane = lax.broadcasted_iota(jnp.int32, lf.shape, 1)
    sh = 1
    while sh < DEC_T:
        lf = lf + jnp.where(lane >= sh, pltpu.roll(lf, sh, axis=1), 0.0)
        sh *= 2
    cum = lf + carry_ref[:, 0:1]
    cum_ref[0] = cum
    carry_ref[...] = jnp.broadcast_to(cum[:, DEC_T - 1:DEC_T], carry_ref.shape)


def _decay(z_small, b_fgate):
    nb, s, _ = z_small.shape
    return pl.pallas_call(
        _decay_kernel,
        out_shape=jax.ShapeDtypeStruct((nb, FOX_HEADS, s), jnp.float32),
        grid=(nb, s // DEC_T),
        in_specs=[pl.BlockSpec((1, DEC_T, Z_SMALL), lambda b, i: (b, i, 0)),
                  pl.BlockSpec((FOX_HEADS, 1), lambda b, i: (0, 0))],
        out_specs=pl.BlockSpec((1, FOX_HEADS, DEC_T), lambda b, i: (b, 0, i)),
        scratch_shapes=[pltpu.VMEM((FOX_HEADS, 128), jnp.float32)],
        compiler_params=_cparams(("arbitrary", "arbitrary")),
        name="decay",
    )(z_small, b_fgate.reshape(FOX_HEADS, 1))


def _compress_kernel(x_ref, w1_ref, w2_ref, pos_ref, cos_ref, sin_ref, o_ref):
    kind = pl.program_id(1)
    nc = x_ref.shape[1] // CMP_STRIDE
    p = jnp.zeros((nc, CMP_HIDDEN), jnp.float32)
    q = jnp.zeros((nc, CMP_HIDDEN), jnp.float32)
    for l in range(CMP_STRIDE):
        xl = x_ref[0, pl.ds(l, nc, stride=CMP_STRIDE), :].astype(jnp.bfloat16)
        p = p + _dot(xl, w1_ref[0, l])
        q = q + _dot(xl, w1_ref[0, CMP_STRIDE + l])
    posb = jnp.broadcast_to(pos_ref[...], (8, CMP_LEN * HEAD_DIM)).astype(jnp.bfloat16)
    w1_flat = w1_ref[0].reshape(CMP_LEN * HEAD_DIM, CMP_HIDDEN)
    pterm = _dot(posb, w1_flat)[0:1, :]
    h = p + pltpu.roll(q, nc - 1, axis=0) + pterm
    a = jax.nn.gelu(h).astype(jnp.bfloat16)
    out = _dot(a, w2_ref[0])

    @pl.when(kind < NSA_KV_HEADS)
    def _():
        o_ref[0, 0] = _apply_rope(out, cos_ref[0], sin_ref[0]).astype(o_ref.dtype)

    @pl.when(kind >= NSA_KV_HEADS)
    def _():
        o_ref[0, 0] = out.astype(o_ref.dtype)


def _compress(zc, w1, w2, cmp_pos, cos_c, sin_c):
    nb, s, _ = zc.shape
    nk = 2 * NSA_KV_HEADS
    nc = s // CMP_STRIDE
    return pl.pallas_call(
        _compress_kernel,
        out_shape=jax.ShapeDtypeStruct((nb, nk, nc, HEAD_DIM), jnp.bfloat16),
        grid=(nb, nk),
        in_specs=[
            pl.BlockSpec((1, s, HEAD_DIM), lambda b, k: (b, 0, k)),
            pl.BlockSpec((1, CMP_LEN, HEAD_DIM, CMP_HIDDEN),
                         lambda b, k: (k // NSA_KV_HEADS, 0, 0, 0)),
            pl.BlockSpec((1, CMP_HIDDEN, HEAD_DIM), lambda b, k: (k // NSA_KV_HEADS, 0, 0)),
            pl.BlockSpec((1, CMP_LEN * HEAD_DIM), lambda b, k: (0, 0)),
            pl.BlockSpec((1, nc, HEAD_DIM), lambda b, k: (b, 0, 0)),
            pl.BlockSpec((1, nc, HEAD_DIM), lambda b, k: (b, 0, 0)),
        ],
        out_specs=pl.BlockSpec((1, 1, nc, HEAD_DIM), lambda b, k: (b, k, 0, 0)),
        compiler_params=_cparams(("arbitrary", "arbitrary")),
        name="compress",
    )(zc, w1, w2, cmp_pos.reshape(1, CMP_LEN * HEAD_DIM), cos_c, sin_c)


FOX_T = 512


def _softmax_step(s, m, acc, v):
    m_new = jnp.maximum(m, jnp.max(s, axis=1, keepdims=True))
    alpha = jnp.exp2(m - m_new)
    p = jnp.exp2((s - m_new).astype(jnp.bfloat16))
    va = jnp.concatenate([v, jnp.ones_like(v)], axis=1)
    return m_new, alpha * acc + _dot(p, va)


def _softmax_init(rows):
    return (jnp.full((rows, 1), MASK_VALUE, jnp.float32),
            jnp.zeros((rows, 2 * HEAD_DIM), jnp.float32))


def _softmax_result(acc):
    return acc[:, 0:HEAD_DIM] * (1.0 / acc[:, HEAD_DIM:2 * HEAD_DIM])


FOX_HB = 4


def _fox_kernel(q_ref, k_ref, v_ref, cum_ref, *rest, n_cast):
    o_ref = rest[n_cast]
    for w_ref, wb_ref in zip(rest[:n_cast], rest[n_cast + 1:]):
        wb_ref[...] = w_ref[...].astype(wb_ref.dtype)
    hb = pl.program_id(1)
    i = pl.program_id(2)
    t = FOX_T
    cols = [slice(n * HEAD_DIM, (n + 1) * HEAD_DIM) for n in range(FOX_HB)]
    qs = [q_ref[0, :, c] for c in cols]

    def tile(k0, width, carry, diag):
        k0 = pl.multiple_of(k0, t)
        logits = [_dot_nt(qs[n], k_ref[0, pl.ds(k0, width), cols[n]]) for n in range(FOX_HB)]
        out = []
        for n, (m, acc) in enumerate(carry):
            v = v_ref[0, pl.ds(k0, width), cols[n]]
            ck = cum_ref[0, pl.ds(hb * FOX_HB + n, 1), pl.ds(k0, width)] * LOG2E
            s = logits[n] - ck
            if diag:
                r = lax.broadcasted_iota(jnp.int32, s.shape, 0)
                c = lax.broadcasted_iota(jnp.int32, s.shape, 1)
                s = jnp.where(c <= r + (width - t), s, MASK_VALUE)
            out.append(_softmax_step(s, m, acc, v))
        return tuple(out)

    init = tuple(_softmax_init(t) for _ in range(FOX_HB))
    n_pairs = i >> 1
    carry = lax.fori_loop(0, n_pairs, lambda p, c: tile(p * (2 * t), 2 * t, c, False), init)
    k_tail = n_pairs * (2 * t)
    carry = lax.cond((i & 1) == 0,
                     lambda c: tile(k_tail, t, c, True),
                     lambda c: tile(k_tail, 2 * t, c, True), carry)
    for n, (_, acc) in enumerate(carry):
        o_ref[0, :, cols[n]] = _softmax_result(acc).astype(o_ref.dtype)


def _fox(z_main, cum, weights):
    nb, s, _ = z_main.shape
    w = FOX_HB * HEAD_DIM
    grid = (nb, FOX_HEADS // FOX_HB, s // FOX_T)
    n_steps = grid[0] * grid[1] * grid[2]
    step = lambda b, h, i: ((b * grid[1] + h) * grid[2] + i, 0)
    slabs = []
    for wt in weights:
        rows = wt.shape[0] // n_steps
        assert rows * n_steps == wt.shape[0] and rows % BF16_SUBLANES == 0, wt.shape
        slabs.append(pl.BlockSpec((rows, wt.shape[1]), step))
    outs = pl.pallas_call(
        functools.partial(_fox_kernel, n_cast=len(weights)),
        out_shape=[jax.ShapeDtypeStruct((nb, s, FOX_W), jnp.bfloat16)]
        + [jax.ShapeDtypeStruct(wt.shape, jnp.bfloat16) for wt in weights],
        grid=grid,
        in_specs=[
            pl.BlockSpec((1, FOX_T, w), lambda b, h, i: (b, i, h)),
            pl.BlockSpec((1, s, w), lambda b, h, i: (b, 0, FOX_W // w + h)),
            pl.BlockSpec((1, s, w), lambda b, h, i: (b, 0, 2 * FOX_W // w + h)),
            pl.BlockSpec((1, FOX_HEADS, s), lambda b, h, i: (b, 0, 0)),
        ] + slabs,
        out_specs=[pl.BlockSpec((1, FOX_T, w), lambda b, h, i: (b, i, h))] + slabs,
        compiler_params=_cparams(("arbitrary", "arbitrary", "arbitrary")),
        name="fox",
    )(z_main, z_main, z_main, cum, *weights)
    return outs[0], outs[1:]


NSA_TQ = 512
NSA_PARTS = 4
NSEL_PAD = 128
SLC_SHIFT = SLC_LEN.bit_length() - 1
assert 1 << SLC_SHIFT == SLC_LEN
assert WINDOW % NSA_TQ == 0 and NSA_TQ % SLC_LEN == 0
SUBLANES = 8
BF16_SUBLANES = 16


def _block_ranks(score, n_slc):
    tq = score.shape[1]
    groups = [score[SUBLANES * v:SUBLANES * (v + 1)] for v in range(n_slc // SUBLANES)]
    ranks = [jnp.zeros((SUBLANES, tq), jnp.float32) for _ in groups]
    for m in range(n_slc):
        row = jnp.broadcast_to(score[m:m + 1, :], (SUBLANES, tq))
        for v, grp in enumerate(groups):
            lo = SUBLANES * v
            if lo > m:
                before = row >= grp
            elif lo + SUBLANES - 1 <= m:
                before = row > grp
            else:
                n_i = lo + lax.broadcasted_iota(jnp.int32, grp.shape, 0)
                before = (row > grp) | ((row == grp) & (n_i > m))
            ranks[v] = ranks[v] + jnp.where(before, 1.0, 0.0)
    return jnp.concatenate(ranks, axis=0)


def _nsa_kernel(q_ref, zs_ref, kc_ref, vc_ref, ks_ref, vs_ref, kw_ref, vw_ref, ovt_ref,
                o_ref, ka_ref, *, seq):
    g = pl.program_id(1)
    i = pl.program_id(2)
    tq = NSA_TQ
    q0 = i * tq

    @pl.when(i == 0)
    def _():
        ka_ref[:, 0:HEAD_DIM] = ks_ref[0]
        tok = lax.broadcasted_iota(jnp.int32, (seq, NSEL_PAD), 0)
        n_i = lax.broadcasted_iota(jnp.int32, (seq, NSEL_PAD), 1)
        ka_ref[:, HEAD_DIM:HEAD_DIM + NSEL_PAD] = jnp.where(
            n_i == (tok >> SLC_SHIFT), 1.0, 0.0).astype(jnp.bfloat16)

    qb = q_ref[0]
    hpp = NSA_GROUP // NSA_PARTS
    parts = range(NSA_PARTS)
    qh = [jnp.concatenate(
        [qb[:, r * HEAD_DIM:(r + 1) * HEAD_DIM] for r in range(h * hpp, (h + 1) * hpp)], axis=0)
        for h in parts]
    prow = hpp * tq

    def row_query(shape):
        return lax.broadcasted_iota(jnp.int32, shape, 0) & (tq - 1)

    kc = kc_ref[0, 0]
    vc = vc_ref[0, 0]
    s_cs = [_dot_nt(qh[h], kc) for h in parts]
    p_cs = []
    for s_c in s_cs:
        cend = lax.broadcasted_iota(jnp.int32, s_c.shape, 1) * CMP_STRIDE + (CMP_LEN - 1)
        s_c = jnp.where(cend <= q0 + row_query(s_c.shape), s_c, MASK_VALUE)
        m_c = jnp.maximum(jnp.max(s_c, axis=1, keepdims=True), 0.1 * MASK_VALUE)
        e_c = jnp.exp2(s_c - m_c)
        p_cs.append(e_c * (1.0 / jnp.maximum(jnp.sum(e_c, axis=1, keepdims=True), 1e-30)))
    o_c = [_dot(p_c.astype(jnp.bfloat16), vc) for p_c in p_cs]

    psum = None
    for p_c in p_cs:
        for r in range(hpp):
            term = p_c[r * tq:(r + 1) * tq]
            psum = term if psum is None else psum + term
    p_hi = psum.astype(jnp.bfloat16)
    p_lo = (psum - p_hi.astype(jnp.float32)).astype(jnp.bfloat16)
    ovt = ovt_ref[...]
    imp_t = _dot_nt(ovt, p_hi) + _dot_nt(ovt, p_lo)
    n_slc = seq // SLC_LEN
    blk = lax.broadcasted_iota(jnp.int32, imp_t.shape, 0)
    cur = (q0 + lax.broadcasted_iota(jnp.int32, imp_t.shape, 1)) >> SLC_SHIFT
    forced = (blk == 0) | (blk == cur) | (blk == cur - 1)
    score = jnp.where(forced, SEL_BONUS, imp_t)
    score = jnp.where(blk <= cur, score, -SEL_BONUS)
    rank = _block_ranks(score, n_slc)
    bias_t = jnp.where(rank < min(SLC_TOPK, n_slc), 0.0, MASK_VALUE)
    bias_t = jnp.concatenate(
        [bias_t, jnp.zeros((NSEL_PAD - n_slc, tq), jnp.float32)], axis=0)
    bias = bias_t.T.astype(jnp.bfloat16)
    bias_p = jnp.concatenate([bias] * hpp, axis=0)
    qa = [jnp.concatenate([qh[h], bias_p], axis=1) for h in parts]

    def sel_tile(k0, width, carry, diag):
        k0 = pl.multiple_of(k0, tq)
        logits = [_dot_nt(qa[h], ka_ref[pl.ds(k0, width), :]) for h in parts]
        out = []
        for s, (m, acc) in zip(logits, carry):
            if diag:
                c = lax.broadcasted_iota(jnp.int32, s.shape, 1)
                s = jnp.where(c <= row_query(s.shape) + (width - tq), s, MASK_VALUE)
            out.append(_softmax_step(s, m, acc, vs_ref[0, pl.ds(k0, width), :]))
        return tuple(out)

    init = tuple(_softmax_init(prow) for _ in parts)
    n_pairs = i >> 1
    carry = lax.fori_loop(0, n_pairs,
                          lambda p, c: sel_tile(p * (2 * tq), 2 * tq, c, False), init)
    k_tail = n_pairs * (2 * tq)
    carry = lax.cond((i & 1) == 0,
                     lambda c: sel_tile(k_tail, tq, c, True),
                     lambda c: sel_tile(k_tail, 2 * tq, c, True), carry)
    o_s = [_softmax_result(acc) for _, acc in carry]

    n_w = WINDOW // tq + 1
    starts = [q0 - WINDOW + t * tq for t in range(n_w)]
    offs = [pl.multiple_of(jnp.maximum(st, 0), tq) for st in starts]
    s_w = [[_dot_nt(qh[h], kw_ref[0, pl.ds(offs[t], tq), :]) for h in parts] for t in range(n_w)]
    for t in range(n_w):
        for h in parts:
            s = s_w[t][h]
            c = lax.broadcasted_iota(jnp.int32, s.shape, 1)
            if t == 0:
                s = jnp.where(c > row_query(s.shape), s, MASK_VALUE)
            if t == n_w - 1:
                s = jnp.where(c <= row_query(s.shape), s, MASK_VALUE)
            else:
                s = jnp.where(starts[t] >= 0, s, MASK_VALUE)
            s_w[t][h] = s
    o_w = []
    for h in parts:
        m_w = s_w[0][h].max(axis=1, keepdims=True)
        for t in range(1, n_w):
            m_w = jnp.maximum(m_w, s_w[t][h].max(axis=1, keepdims=True))
        acc_w = jnp.zeros((prow, 2 * HEAD_DIM), jnp.float32)
        for t in range(n_w):
            p = jnp.exp2((s_w[t][h] - m_w).astype(jnp.bfloat16))
            v = vw_ref[0, pl.ds(offs[t], tq), :]
            acc_w = acc_w + _dot(p, jnp.concatenate([v, jnp.ones_like(v)], axis=1))
        o_w.append(_softmax_result(acc_w))

    gates = jax.nn.sigmoid(zs_ref[0])
    lane = lax.broadcasted_iota(jnp.int32, gates.shape, 1)
    outs = []
    for r in range(NSA_GROUP):
        h, rows = r // hpp, slice((r % hpp) * tq, (r % hpp + 1) * tq)
        o_r = jnp.zeros((tq, HEAD_DIM), jnp.float32)
        for br, o_b in enumerate((o_c, o_s, o_w)):
            col = FOX_HEADS + 3 * (g * NSA_GROUP + r) + br
            gcol = jnp.sum(jnp.where(lane == col, gates, 0.0), axis=1, keepdims=True)
            o_r = o_r + gcol * o_b[h][rows]
        outs.append(o_r)
    o_ref[0] = jnp.concatenate(outs, axis=1).astype(o_ref.dtype)


def _nsa(z_main, z_small, kvc, ovt):
    nb, s, _ = z_main.shape
    nc = kvc.shape[2]
    gw = NSA_GROUP * HEAD_DIM
    full = lambda col0: pl.BlockSpec(
        (1, s, HEAD_DIM), lambda b, g, i, c=col0 // HEAD_DIM: (b, 0, c + g))
    return pl.pallas_call(
        functools.partial(_nsa_kernel, seq=s),
        out_shape=jax.ShapeDtypeStruct((nb, s, NSA_W), jnp.bfloat16),
        grid=(nb, NSA_KV_HEADS, s // NSA_TQ),
        in_specs=[
            pl.BlockSpec((1, NSA_TQ, gw), lambda b, g, i: (b, i, COL_QN // gw + g)),
            pl.BlockSpec((1, NSA_TQ, Z_SMALL), lambda b, g, i: (b, i, 0)),
            pl.BlockSpec((1, 1, nc, HEAD_DIM), lambda b, g, i: (b, g, 0, 0)),
            pl.BlockSpec((1, 1, nc, HEAD_DIM), lambda b, g, i: (b, NSA_KV_HEADS + g, 0, 0)),
            full(COL_KS), full(COL_VS), full(COL_KW), full(COL_VW),
            pl.BlockSpec(ovt.shape, lambda b, g, i: (0, 0)),
        ],
        out_specs=pl.BlockSpec((1, NSA_TQ, gw), lambda b, g, i: (b, i, g)),
        scratch_shapes=[pltpu.VMEM((s, HEAD_DIM + NSEL_PAD), jnp.bfloat16)],
        compiler_params=_cparams(("arbitrary", "arbitrary", "arbitrary")),
        name="nsa",
    )(z_main, z_small, kvc, kvc, z_main, z_main, z_main, z_main, ovt)


OUT_TM = 512
OUT_RC = 256


def _outproj_kernel(of_ref, on_ref, bf_ref, bn_ref, w_ref, x_ref, g_ref, nw_ref, sc_ref,
                    sh_ref, h_ref, u_ref):
    for r in range(OUT_TM // OUT_RC):
        rows = slice(r * OUT_RC, (r + 1) * OUT_RC)
        yf = _rms(of_ref[0, rows, :].astype(jnp.float32)) * bf_ref[...]
        yn = _rms(on_ref[0, rows, :].astype(jnp.float32)) * bn_ref[...]
        y = jnp.concatenate([yf.astype(jnp.bfloat16), yn.astype(jnp.bfloat16)], axis=1)
        h = x_ref[0, rows, :] + g_ref[0] * _dot(y, w_ref[...])
        h_ref[0, rows, :] = h
        u = _rms(h) * nw_ref[...]
        u_ref[0, rows, :] = (u * (1.0 + sc_ref[0]) + sh_ref[0]).astype(jnp.bfloat16)


def _outproj(o_f, o_n, beta_f, beta_n, w_out, x, gate, norm_w, sc, sh):
    nb, s, d = x.shape
    vec = lambda n: pl.BlockSpec((1, n), lambda b, i: (0, 0))
    bvec = pl.BlockSpec((1, 1, d), lambda b, i: (b, 0, 0))
    rows = lambda n: pl.BlockSpec((1, OUT_TM, n), lambda b, i: (b, i, 0))
    return pl.pallas_call(
        _outproj_kernel,
        out_shape=(jax.ShapeDtypeStruct((nb, s, d), jnp.float32),
                   jax.ShapeDtypeStruct((nb, s, d), jnp.bfloat16)),
        grid=(nb, s // OUT_TM),
        in_specs=[
            rows(FOX_W), rows(NSA_W), vec(FOX_W), vec(NSA_W),
            pl.BlockSpec((FOX_W + NSA_W, d), lambda b, i: (0, 0),
                         pipeline_mode=pl.Buffered(1)),
            rows(d), bvec, vec(d), bvec, bvec,
        ],
        out_specs=(rows(d), rows(d)),
        compiler_params=_cparams(("arbitrary", "arbitrary")),
        name="outproj",
    )(o_f, o_n, beta_f.reshape(1, FOX_W), beta_n.reshape(1, NSA_W), w_out, x,
      gate.reshape(nb, 1, d), norm_w.reshape(1, d), sc.reshape(nb, 1, d),
      sh.reshape(nb, 1, d))


FFN_TM = 1024
FFN_TF = 512
FFN_RC = 512
FFN_RC_LAST = 256


def _ffn_kernel(u_ref, h_ref, g_ref, fw_ref, wg_ref, wu_ref, wd_ref, o_ref, *, final):
    j = pl.program_id(2)
    last = pl.num_programs(2) - 1

    def partial_sum(rows):
        u = u_ref[0, rows, :]
        a = _dot(u, wg_ref[...])
        b = _dot(u, wu_ref[...])
        mid = (a * jax.nn.sigmoid(a) * b).astype(jnp.bfloat16)
        return _dot(mid, wd_ref[...])

    def chunks(rc):
        return [slice(r * rc, (r + 1) * rc) for r in range(FFN_TM // rc)]

    @pl.when(j == 0)
    def _():
        for rows in chunks(FFN_RC):
            o_ref[0, rows, :] = partial_sum(rows)

    @pl.when((j > 0) & (j < last))
    def _():
        for rows in chunks(FFN_RC):
            o_ref[0, rows, :] += partial_sum(rows)

    @pl.when(j == last)
    def _():
        for rows in chunks(FFN_RC_LAST):
            h2 = h_ref[0, rows, :] + g_ref[0] * (o_ref[0, rows, :] + partial_sum(rows))
            o_ref[0, rows, :] = _rms(h2) * fw_ref[...] if final else h2


def _ffn(u2, h1, gate, final_w, w_gate, w_up, w_down, final):
    nb, s, d = h1.shape
    f = w_gate.shape[1]
    return pl.pallas_call(
        functools.partial(_ffn_kernel, final=final),
        out_shape=jax.ShapeDtypeStruct((nb, s, d), jnp.float32),
        grid=(nb, s // FFN_TM, f // FFN_TF),
        in_specs=[
            pl.BlockSpec((1, FFN_TM, d), lambda b, i, j: (b, i, 0)),
            pl.BlockSpec((1, FFN_TM, d), lambda b, i, j: (b, i, 0),
                         pipeline_mode=pl.Buffered(1)),
            pl.BlockSpec((1, 1, d), lambda b, i, j: (b, 0, 0)),
            pl.BlockSpec((1, d), lambda b, i, j: (0, 0)),
            pl.BlockSpec((d, FFN_TF), lambda b, i, j: (0, j)),
            pl.BlockSpec((d, FFN_TF), lambda b, i, j: (0, j)),
            pl.BlockSpec((FFN_TF, d), lambda b, i, j: (j, 0)),
        ],
        out_specs=pl.BlockSpec((1, FFN_TM, d), lambda b, i, j: (b, i, 0)),
        compiler_params=pltpu.CompilerParams(
            dimension_semantics=("arbitrary", "arbitrary", "arbitrary"),
            vmem_limit_bytes=FFN_VMEM_LIMIT),
        name="ffn",
    )(u2, h1, gate.reshape(nb, 1, d), final_w.reshape(1, d), w_gate, w_up, w_down)


def _split_w_in(w_in):
    sizes = ([FOX_W] * 3 + [FOX_HEADS] + [NSA_W] + [KV_W] * 6 + [3 * NSA_HEADS])
    offs = [0]
    for sz in sizes:
        offs.append(offs[-1] + sz)
    wt = w_in.T
    piece = lambda n: wt[offs[n]:offs[n + 1]].astype(jnp.bfloat16)
    main = jnp.concatenate([piece(n) for n in (0, 1, 2, 4, 7, 9, 8, 10, 5, 6)], axis=0)
    small = jnp.concatenate(
        [piece(3), piece(11),
         jnp.zeros((Z_SMALL - FOX_HEADS - 3 * NSA_HEADS, w_in.shape[0]), jnp.bfloat16)], axis=0)
    return main, small


def _overlap_t(n_cmp_pad, n_slc):
    cs = jnp.arange(n_cmp_pad) * CMP_STRIDE
    ss = jnp.arange(n_slc) * SLC_LEN
    ov = (jnp.minimum(cs[None, :] + CMP_LEN, ss[:, None] + SLC_LEN)
          - jnp.maximum(cs[None, :], ss[:, None]))
    ov = jnp.clip(ov, 0).astype(jnp.float32) / CMP_STRIDE
    return ov.astype(jnp.bfloat16)


def kernel(x, c, positions, w_ada, b_ada, norm_attn, norm_ffn, w_in, b_fgate, cmp_pos,
           w_kc1, w_kc2, w_vc1, w_vc2, beta_fox, beta_nsa, w_out, w_gate, w_up, w_down,
           final_norm):
    nb, s, d = x.shape
    depth = w_ada.shape[0]
    n_cmp_pad = s // CMP_STRIDE
    cmp_end = jnp.arange(n_cmp_pad) * CMP_STRIDE + (CMP_LEN - 1)
    pos_c = jnp.take(positions, jnp.minimum(cmp_end, s - 1), axis=1)
    cos, sin = _rope_tables(positions, 512)
    cos_c, sin_c = _rope_tables(pos_c, n_cmp_pad)
    ovt = _overlap_t(n_cmp_pad, s // SLC_LEN)

    h = x
    for l in range(depth):
        mod = _ada(c, w_ada[l], b_ada[l])
        sh1, sc1, g1, sh2, sc2, g2 = jnp.split(mod, 6, axis=-1)
        w_main, w_small = _split_w_in(w_in[l])
        z_main, z_small, zc = _inproj(h, norm_attn[l], sc1, sh1, w_main, w_small, cos, sin)
        cum = _decay(z_small, b_fgate[l])

        w1 = jnp.stack([w_kc1[l], w_vc1[l]]).astype(jnp.bfloat16)
        w1 = w1.reshape(2, CMP_LEN, HEAD_DIM, CMP_HIDDEN)
        w2 = jnp.stack([w_kc2[l], w_vc2[l]]).astype(jnp.bfloat16)
        kvc = _compress(zc, w1, w2, cmp_pos[l], cos_c, sin_c)

        o_f, (wo, wg, wu, wd) = _fox(z_main, cum, (w_out[l], w_gate[l], w_up[l], w_down[l]))
        o_n = _nsa(z_main, z_small, kvc, ovt)
        h1, u2 = _outproj(o_f, o_n, beta_fox[l], beta_nsa[l], wo, h, g1, norm_ffn[l], sc2, sh2)
        h = _ffn(u2, h1, g2, final_norm, wg, wu, wd, final=(l == depth - 1))
    return h
```

```python
import functools

import jax
import jax.numpy as jnp
from jax import lax
from jax.experimental import pallas as pl
from jax.experimental.pallas import tpu as pltpu

D_MODEL = 2048
HEAD_DIM = 128
FOX_HEADS = 8
NSA_HEADS = 8
NSA_KV_HEADS = 2
NSA_GROUP = NSA_HEADS // NSA_KV_HEADS
CMP_LEN = 32
CMP_STRIDE = 16
CMP_HIDDEN = 2 * HEAD_DIM
SLC_LEN = 64
SLC_TOPK = 16
WINDOW = 512
D_FF = 5632
ROPE_THETA = 10000.0
NORM_EPS = 1e-6
MASK_VALUE = -1e30
SEL_BONUS = 1e6

FOX_W = FOX_HEADS * HEAD_DIM
NSA_W = NSA_HEADS * HEAD_DIM
KV_W = NSA_KV_HEADS * HEAD_DIM
Z_MAIN = 3 * FOX_W + NSA_W + 6 * KV_W
Z_SMALL = 128
COL_QN = 3 * FOX_W
COL_KS = COL_QN + NSA_W
COL_KW = COL_KS + KV_W
COL_VS = COL_KW + KV_W
COL_VW = COL_VS + KV_W
COL_KC = COL_VW + KV_W

VMEM_LIMIT = 48 * 1024 * 1024
FFN_VMEM_LIMIT = 56 * 1024 * 1024
SCALE = HEAD_DIM ** -0.5
LOG2E = 1.4426950408889634


def _cparams(sem):
    return pltpu.CompilerParams(dimension_semantics=sem, vmem_limit_bytes=VMEM_LIMIT)


def _dot(a, b):
    return jnp.dot(a, b, preferred_element_type=jnp.float32)


def _dot_nt(a, b):
    return lax.dot_general(a, b, (((1,), (1,)), ((), ())), preferred_element_type=jnp.float32)


def _rms(x):
    return x * lax.rsqrt(jnp.mean(x * x, axis=-1, keepdims=True) + NORM_EPS)


ADA_TN = 1024
ADA_KC = 256


def _ada_kernel(ct_ref, w_ref, b_ref, o_ref, *, nb):
    tn = w_ref.shape[1]
    ct = ct_ref[...]
    st = ct * jax.nn.sigmoid(ct)

    accs = [jnp.zeros((8, tn), jnp.float32) for _ in range(nb)]
    for kc in range(D_MODEL // ADA_KC):
        w = w_ref[kc * ADA_KC:(kc + 1) * ADA_KC, :]
        for b in range(nb):
            s = st[kc * ADA_KC:(kc + 1) * ADA_KC, b:b + 1]
            accs[b] = accs[b] + jnp.sum((w * s).reshape(ADA_KC // 8, 8, tn), axis=0)
    for b in range(nb):
        o_ref[b:b + 1, :] = jnp.sum(accs[b], axis=0, keepdims=True) + b_ref[...]


def _ada(c, w_ada, b_ada):
    nb = c.shape[0]
    n = w_ada.shape[1]
    return pl.pallas_call(
        functools.partial(_ada_kernel, nb=nb),
        out_shape=jax.ShapeDtypeStruct((nb, n), jnp.float32),
        grid=(n // ADA_TN,),
        in_specs=[
            pl.BlockSpec((D_MODEL, nb), lambda j: (0, 0)),
            pl.BlockSpec((D_MODEL, ADA_TN), lambda j: (0, j)),
            pl.BlockSpec((1, ADA_TN), lambda j: (0, j)),
        ],
        out_specs=pl.BlockSpec((nb, ADA_TN), lambda j: (0, j)),
        compiler_params=_cparams(("arbitrary",)),
        name="ada",
    )(c.T, w_ada, b_ada.reshape(1, n))


def _rope_table_kernel(pos_ref, inv_ref, cos_ref, sin_ref):
    half = pos_ref.shape[1] // 2
    lane = lax.broadcasted_iota(jnp.int32, (half, HEAD_DIM), 1)
    low = lane < HEAD_DIM // 2
    pos = jnp.where(low, pos_ref[0, 0:half, :], pos_ref[0, half:2 * half, :])
    ang = pos.astype(jnp.float32) * inv_ref[...]
    cos = jnp.cos(ang)
    sin = jnp.sin(ang)
    cos_sw = pltpu.roll(cos, HEAD_DIM // 2, axis=1)
    sin_sw = pltpu.roll(sin, HEAD_DIM // 2, axis=1)
    cos_ref[0, 0:half, :] = jnp.where(low, cos, cos_sw)
    cos_ref[0, half:2 * half, :] = jnp.where(low, cos_sw, cos)
    sin_ref[0, 0:half, :] = jnp.where(low, -sin, sin_sw)
    sin_ref[0, half:2 * half, :] = jnp.where(low, -sin_sw, sin)


def _rope_tables(pos, ts):
    nb, n = pos.shape
    inv = ROPE_THETA ** (-jnp.arange(0, HEAD_DIM, 2, dtype=jnp.float32) / HEAD_DIM)
    inv2 = jnp.concatenate([inv, inv]).reshape(1, HEAD_DIM)
    shp = jax.ShapeDtypeStruct((nb, n, HEAD_DIM), jnp.float32)
    spec = pl.BlockSpec((1, ts, HEAD_DIM), lambda b, i: (b, i, 0))
    return pl.pallas_call(
        _rope_table_kernel,
        out_shape=(shp, shp),
        grid=(nb, n // ts),
        in_specs=[pl.BlockSpec((1, ts, 1), lambda b, i: (b, i, 0)),
                  pl.BlockSpec((1, HEAD_DIM), lambda b, i: (0, 0))],
        out_specs=(spec, spec),
        compiler_params=_cparams(("arbitrary", "arbitrary")),
        name="rope_tables",
    )(pos.reshape(nb, n, 1), inv2)


def _apply_rope(x, cos, sin):
    return x * cos + pltpu.roll(x, HEAD_DIM // 2, axis=1) * sin


IN_TM = 1024
IN_TN = 512


def _inproj_kernel(x_ref, g_ref, sc_ref, sh_ref, w_ref, ws_ref, cos_ref, sin_ref,
                   z_ref, zs_ref, zc_ref, u_ref):
    j = pl.program_id(2)

    @pl.when(j == 0)
    def _():
        u = _rms(x_ref[0]) * g_ref[...]
        u = u * (1.0 + sc_ref[0]) + sh_ref[0]
        ub = u.astype(jnp.bfloat16)
        u_ref[...] = ub
        zs_ref[0] = _dot_nt(ub, ws_ref[...])

    fox_q = j < FOX_W // IN_TN
    nsa_q = (j >= COL_QN // IN_TN) & (j < COL_KS // IN_TN)
    nsa_k = j == COL_KS // IN_TN
    cmp_kv = j == COL_KC // IN_TN

    def matmul():
        return _dot_nt(u_ref[...], w_ref[...])

    def roped(acc):
        cos = cos_ref[0]
        sin = sin_ref[0]
        parts = [_apply_rope(acc[:, h * HEAD_DIM:(h + 1) * HEAD_DIM], cos, sin)
                 for h in range(IN_TN // HEAD_DIM)]
        return jnp.concatenate(parts, axis=1).astype(z_ref.dtype)

    @pl.when(fox_q)
    def _():
        z_ref[0] = (matmul() * (SCALE * LOG2E)).astype(z_ref.dtype)

    @pl.when(nsa_q)
    def _():
        z_ref[0] = roped(matmul() * (SCALE * LOG2E))

    @pl.when(nsa_k)
    def _():
        z_ref[0] = roped(matmul())

    @pl.when(cmp_kv)
    def _():
        acc = matmul()
        z_ref[0] = acc.astype(z_ref.dtype)
        zc_ref[0] = acc

    @pl.when(jnp.logical_not(fox_q | nsa_q | nsa_k | cmp_kv))
    def _():
        z_ref[0] = matmul().astype(z_ref.dtype)


def _inproj(x, norm_w, sc, sh, w_main, w_small, cos, sin):
    nb, s, d = x.shape
    grid = (nb, s // IN_TM, Z_MAIN // IN_TN)
    return pl.pallas_call(
        _inproj_kernel,
        out_shape=(jax.ShapeDtypeStruct((nb, s, Z_MAIN), jnp.bfloat16),
                   jax.ShapeDtypeStruct((nb, s, Z_SMALL), jnp.float32),
                   jax.ShapeDtypeStruct((nb, s, IN_TN), jnp.float32)),
        grid=grid,
        in_specs=[
            pl.BlockSpec((1, IN_TM, d), lambda b, i, j: (b, i, 0)),
            pl.BlockSpec((1, d), lambda b, i, j: (0, 0)),
            pl.BlockSpec((1, 1, d), lambda b, i, j: (b, 0, 0)),
            pl.BlockSpec((1, 1, d), lambda b, i, j: (b, 0, 0)),
            pl.BlockSpec((IN_TN, d), lambda b, i, j: (j, 0)),
            pl.BlockSpec((Z_SMALL, d), lambda b, i, j: (0, 0)),
            pl.BlockSpec((1, IN_TM, HEAD_DIM), lambda b, i, j: (b, i, 0)),
            pl.BlockSpec((1, IN_TM, HEAD_DIM), lambda b, i, j: (b, i, 0)),
        ],
        out_specs=(pl.BlockSpec((1, IN_TM, IN_TN), lambda b, i, j: (b, i, j)),
                   pl.BlockSpec((1, IN_TM, Z_SMALL), lambda b, i, j: (b, i, 0)),
                   pl.BlockSpec((1, IN_TM, IN_TN), lambda b, i, j: (b, i, 0))),
        scratch_shapes=[pltpu.VMEM((IN_TM, d), jnp.bfloat16)],
        compiler_params=_cparams(("arbitrary", "arbitrary", "arbitrary")),
        name="inproj",
    )(x, norm_w.reshape(1, d), sc.reshape(nb, 1, d), sh.reshape(nb, 1, d),
      w_main, w_small, cos, sin)


DEC_T = 2048


def _decay_kernel(zs_ref, bf_ref, cum_ref, carry_ref):
    @pl.when(pl.program_id(1) == 0)
    def _():
        carry_ref[...] = jnp.zeros_like(carry_ref)

    xt = zs_ref[0].T[0:FOX_HEADS, :] + bf_ref[...]
    lf = jnp.minimum(xt, 0.0) - jnp.log1p(jnp.exp(-jnp.abs(xt)))
    lane = lax.broadcasted_iota(jnp.int32, lf.shape, 1)
    sh = 1
    while sh < DEC_T:
        lf = lf + jnp.where(lane >= sh, pltpu.roll(lf, sh, axis=1), 0.0)
        sh *= 2
    cum = lf + carry_ref[:, 0:1]
    cum_ref[0] = cum
    carry_ref[...] = jnp.broadcast_to(cum[:, DEC_T - 1:DEC_T], carry_ref.shape)


def _decay(z_small, b_fgate):
    nb, s, _ = z_small.shape
    return pl.pallas_call(
        _decay_kernel,
        out_shape=jax.ShapeDtypeStruct((nb, FOX_HEADS, s), jnp.float32),
        grid=(nb, s // DEC_T),
        in_specs=[pl.BlockSpec((1, DEC_T, Z_SMALL), lambda b, i: (b, i, 0)),
                  pl.BlockSpec((FOX_HEADS, 1), lambda b, i: (0, 0))],
        out_specs=pl.BlockSpec((1, FOX_HEADS, DEC_T), lambda b, i: (b, 0, i)),
        scratch_shapes=[pltpu.VMEM((FOX_HEADS, 128), jnp.float32)],
        compiler_params=_cparams(("arbitrary", "arbitrary")),
        name="decay",
    )(z_small, b_fgate.reshape(FOX_HEADS, 1))


def _compress_kernel(x_ref, w1_ref, w2_ref, pos_ref, cos_ref, sin_ref, o_ref):
    kind = pl.program_id(1)
    nc = x_ref.shape[1] // CMP_STRIDE
    p = jnp.zeros((nc, CMP_HIDDEN), jnp.float32)
    q = jnp.zeros((nc, CMP_HIDDEN), jnp.float32)
    for l in range(CMP_STRIDE):
        xl = x_ref[0, pl.ds(l, nc, stride=CMP_STRIDE), :].astype(jnp.bfloat16)
        p = p + _dot(xl, w1_ref[0, l])
        q = q + _dot(xl, w1_ref[0, CMP_STRIDE + l])
    posb = jnp.broadcast_to(pos_ref[...], (8, CMP_LEN * HEAD_DIM)).astype(jnp.bfloat16)
    w1_flat = w1_ref[0].reshape(CMP_LEN * HEAD_DIM, CMP_HIDDEN)
    pterm = _dot(posb, w1_flat)[0:1, :]
    h = p + pltpu.roll(q, nc - 1, axis=0) + pterm
    a = jax.nn.gelu(h).astype(jnp.bfloat16)
    out = _dot(a, w2_ref[0])

    @pl.when(kind < NSA_KV_HEADS)
    def _():
        o_ref[0, 0] = _apply_rope(out, cos_ref[0], sin_ref[0]).astype(o_ref.dtype)

    @pl.when(kind >= NSA_KV_HEADS)
    def _():
        o_ref[0, 0] = out.astype(o_ref.dtype)


def _compress(zc, w1, w2, cmp_pos, cos_c, sin_c):
    nb, s, _ = zc.shape
    nk = 2 * NSA_KV_HEADS
    nc = s // CMP_STRIDE
    return pl.pallas_call(
        _compress_kernel,
        out_shape=jax.ShapeDtypeStruct((nb, nk, nc, HEAD_DIM), jnp.bfloat16),
        grid=(nb, nk),
        in_specs=[
            pl.BlockSpec((1, s, HEAD_DIM), lambda b, k: (b, 0, k)),
            pl.BlockSpec((1, CMP_LEN, HEAD_DIM, CMP_HIDDEN),
                         lambda b, k: (k // NSA_KV_HEADS, 0, 0, 0)),
            pl.BlockSpec((1, CMP_HIDDEN, HEAD_DIM), lambda b, k: (k // NSA_KV_HEADS, 0, 0)),
            pl.BlockSpec((1, CMP_LEN * HEAD_DIM), lambda b, k: (0, 0)),
            pl.BlockSpec((1, nc, HEAD_DIM), lambda b, k: (b, 0, 0)),
            pl.BlockSpec((1, nc, HEAD_DIM), lambda b, k: (b, 0, 0)),
        ],
        out_specs=pl.BlockSpec((1, 1, nc, HEAD_DIM), lambda b, k: (b, k, 0, 0)),
        compiler_params=_cparams(("arbitrary", "arbitrary")),
        name="compress",
    )(zc, w1, w2, cmp_pos.reshape(1, CMP_LEN * HEAD_DIM), cos_c, sin_c)


FOX_T = 512


def _softmax_step(s, m, acc, v):
    m_new = jnp.maximum(m, jnp.max(s, axis=1, keepdims=True))
    alpha = jnp.exp2(m - m_new)
    p = jnp.exp2((s - m_new).astype(jnp.bfloat16))
    va = jnp.concatenate([v, jnp.ones_like(v)], axis=1)
    return m_new, alpha * acc + _dot(p, va)


def _softmax_init(rows):
    return (jnp.full((rows, 1), MASK_VALUE, jnp.float32),
            jnp.zeros((rows, 2 * HEAD_DIM), jnp.float32))


def _softmax_result(acc):
    return acc[:, 0:HEAD_DIM] * (1.0 / acc[:, HEAD_DIM:2 * HEAD_DIM])


FOX_HB = 4


def _fox_kernel(q_ref, k_ref, v_ref, cum_ref, *rest, n_cast):
    o_ref = rest[n_cast]
    for w_ref, wb_ref in zip(rest[:n_cast], rest[n_cast + 1:]):
        wb_ref[...] = w_ref[...].astype(wb_ref.dtype)
    hb = pl.program_id(1)
    i = pl.program_id(2)
    t = FOX_T
    cols = [slice(n * HEAD_DIM, (n + 1) * HEAD_DIM) for n in range(FOX_HB)]
    qs = [q_ref[0, :, c] for c in cols]

    def tile(k0, width, carry, diag):
        k0 = pl.multiple_of(k0, t)
        logits = [_dot_nt(qs[n], k_ref[0, pl.ds(k0, width), cols[n]]) for n in range(FOX_HB)]
        out = []
        for n, (m, acc) in enumerate(carry):
            v = v_ref[0, pl.ds(k0, width), cols[n]]
            ck = cum_ref[0, pl.ds(hb * FOX_HB + n, 1), pl.ds(k0, width)] * LOG2E
            s = logits[n] - ck
            if diag:
                r = lax.broadcasted_iota(jnp.int32, s.shape, 0)
                c = lax.broadcasted_iota(jnp.int32, s.shape, 1)
                s = jnp.where(c <= r + (width - t), s, MASK_VALUE)
            out.append(_softmax_step(s, m, acc, v))
        return tuple(out)

    init = tuple(_softmax_init(t) for _ in range(FOX_HB))
    n_pairs = i >> 1
    carry = lax.fori_loop(0, n_pairs, lambda p, c: tile(p * (2 * t), 2 * t, c, False), init)
    k_tail = n_pairs * (2 * t)
    carry = lax.cond((i & 1) == 0,
                     lambda c: tile(k_tail, t, c, True),
                     lambda c: tile(k_tail, 2 * t, c, True), carry)
    for n, (_, acc) in enumerate(carry):
        o_ref[0, :, cols[n]] = _softmax_result(acc).astype(o_ref.dtype)


def _fox(z_main, cum, weights):
    nb, s, _ = z_main.shape
    w = FOX_HB * HEAD_DIM
    grid = (nb, FOX_HEADS // FOX_HB, s // FOX_T)
    n_steps = grid[0] * grid[1] * grid[2]
    step = lambda b, h, i: ((b * grid[1] + h) * grid[2] + i, 0)
    slabs = []
    for wt in weights:
        rows = wt.shape[0] // n_steps
        assert rows * n_steps == wt.shape[0] and rows % BF16_SUBLANES == 0, wt.shape
        slabs.append(pl.BlockSpec((rows, wt.shape[1]), step))
    outs = pl.pallas_call(
        functools.partial(_fox_kernel, n_cast=len(weights)),
        out_shape=[jax.ShapeDtypeStruct((nb, s, FOX_W), jnp.bfloat16)]
        + [jax.ShapeDtypeStruct(wt.shape, jnp.bfloat16) for wt in weights],
        grid=grid,
        in_specs=[
            pl.BlockSpec((1, FOX_T, w), lambda b, h, i: (b, i, h)),
            pl.BlockSpec((1, s, w), lambda b, h, i: (b, 0, FOX_W // w + h)),
            pl.BlockSpec((1, s, w), lambda b, h, i: (b, 0, 2 * FOX_W // w + h)),
            pl.BlockSpec((1, FOX_HEADS, s), lambda b, h, i: (b, 0, 0)),
        ] + slabs,
        out_specs=[pl.BlockSpec((1, FOX_T, w), lambda b, h, i: (b, i, h))] + slabs,
        compiler_params=_cparams(("arbitrary", "arbitrary", "arbitrary")),
        name="fox",
    )(z_main, z_main, z_main, cum, *weights)
    return outs[0], outs[1:]


NSA_TQ = 512
NSA_PARTS = 4
NSEL_PAD = 128
SLC_SHIFT = SLC_LEN.bit_length() - 1
assert 1 << SLC_SHIFT == SLC_LEN
assert WINDOW % NSA_TQ == 0 and NSA_TQ % SLC_LEN == 0
SUBLANES = 8
BF16_SUBLANES = 16


def _block_ranks(score, n_slc):
    tq = score.shape[1]
    groups = [score[SUBLANES * v:SUBLANES * (v + 1)] for v in range(n_slc // SUBLANES)]
    ranks = [jnp.zeros((SUBLANES, tq), jnp.float32) for _ in groups]
    for m in range(n_slc):
        row = jnp.broadcast_to(score[m:m + 1, :], (SUBLANES, tq))
        for v, grp in enumerate(groups):
            lo = SUBLANES * v
            if lo > m:
                before = row >= grp
            elif lo + SUBLANES - 1 <= m:
                before = row > grp
            else:
                n_i = lo + lax.broadcasted_iota(jnp.int32, grp.shape, 0)
                before = (row > grp) | ((row == grp) & (n_i > m))
            ranks[v] = ranks[v] + jnp.where(before, 1.0, 0.0)
    return jnp.concatenate(ranks, axis=0)


def _nsa_kernel(q_ref, zs_ref, kc_ref, vc_ref, ks_ref, vs_ref, kw_ref, vw_ref, ovt_ref,
                o_ref, ka_ref, *, seq):
    g = pl.program_id(1)
    i = pl.program_id(2)
    tq = NSA_TQ
    q0 = i * tq

    @pl.when(i == 0)
    def _():
        ka_ref[:, 0:HEAD_DIM] = ks_ref[0]
        tok = lax.broadcasted_iota(jnp.int32, (seq, NSEL_PAD), 0)
        n_i = lax.broadcasted_iota(jnp.int32, (seq, NSEL_PAD), 1)
        ka_ref[:, HEAD_DIM:HEAD_DIM + NSEL_PAD] = jnp.where(
            n_i == (tok >> SLC_SHIFT), 1.0, 0.0).astype(jnp.bfloat16)

    qb = q_ref[0]
    hpp = NSA_GROUP // NSA_PARTS
    parts = range(NSA_PARTS)
    qh = [jnp.concatenate(
        [qb[:, r * HEAD_DIM:(r + 1) * HEAD_DIM] for r in range(h * hpp, (h + 1) * hpp)], axis=0)
        for h in parts]
    prow = hpp * tq

    def row_query(shape):
        return lax.broadcasted_iota(jnp.int32, shape, 0) & (tq - 1)

    kc = kc_ref[0, 0]
    vc = vc_ref[0, 0]
    s_cs = [_dot_nt(qh[h], kc) for h in parts]
    p_cs = []
    for s_c in s_cs:
        cend = lax.broadcasted_iota(jnp.int32, s_c.shape, 1) * CMP_STRIDE + (CMP_LEN - 1)
        s_c = jnp.where(cend <= q0 + row_query(s_c.shape), s_c, MASK_VALUE)
        m_c = jnp.maximum(jnp.max(s_c, axis=1, keepdims=True), 0.1 * MASK_VALUE)
        e_c = jnp.exp2(s_c - m_c)
        p_cs.append(e_c * (1.0 / jnp.maximum(jnp.sum(e_c, axis=1, keepdims=True), 1e-30)))
    o_c = [_dot(p_c.astype(jnp.bfloat16), vc) for p_c in p_cs]

    psum = None
    for p_c in p_cs:
        for r in range(hpp):
            term = p_c[r * tq:(r + 1) * tq]
            psum = term if psum is None else psum + term
    p_hi = psum.astype(jnp.bfloat16)
    p_lo = (psum - p_hi.astype(jnp.float32)).astype(jnp.bfloat16)
    ovt = ovt_ref[...]
    imp_t = _dot_nt(ovt, p_hi) + _dot_nt(ovt, p_lo)
    n_slc = seq // SLC_LEN
    blk = lax.broadcasted_iota(jnp.int32, imp_t.shape, 0)
    cur = (q0 + lax.broadcasted_iota(jnp.int32, imp_t.shape, 1)) >> SLC_SHIFT
    forced = (blk == 0) | (blk == cur) | (blk == cur - 1)
    score = jnp.where(forced, SEL_BONUS, imp_t)
    score = jnp.where(blk <= cur, score, -SEL_BONUS)
    rank = _block_ranks(score, n_slc)
    bias_t = jnp.where(rank < min(SLC_TOPK, n_slc), 0.0, MASK_VALUE)
    bias_t = jnp.concatenate(
        [bias_t, jnp.zeros((NSEL_PAD - n_slc, tq), jnp.float32)], axis=0)
    bias = bias_t.T.astype(jnp.bfloat16)
    bias_p = jnp.concatenate([bias] * hpp, axis=0)
    qa = [jnp.concatenate([qh[h], bias_p], axis=1) for h in parts]

    def sel_tile(k0, width, carry, diag):
        k0 = pl.multiple_of(k0, tq)
        logits = [_dot_nt(qa[h], ka_ref[pl.ds(k0, width), :]) for h in parts]
        out = []
        for s, (m, acc) in zip(logits, carry):
            if diag:
                c = lax.broadcasted_iota(jnp.int32, s.shape, 1)
                s = jnp.where(c <= row_query(s.shape) + (width - tq), s, MASK_VALUE)
            out.append(_softmax_step(s, m, acc, vs_ref[0, pl.ds(k0, width), :]))
        return tuple(out)

    init = tuple(_softmax_init(prow) for _ in parts)
    n_pairs = i >> 1
    carry = lax.fori_loop(0, n_pairs,
                          lambda p, c: sel_tile(p * (2 * tq), 2 * tq, c, False), init)
    k_tail = n_pairs * (2 * tq)
    carry = lax.cond((i & 1) == 0,
                     lambda c: sel_tile(k_tail, tq, c, True),
                     lambda c: sel_tile(k_tail, 2 * tq, c, True), carry)
    o_s = [_softmax_result(acc) for _, acc in carry]

    n_w = WINDOW // tq + 1
    starts = [q0 - WINDOW + t * tq for t in range(n_w)]
    offs = [pl.multiple_of(jnp.maximum(st, 0), tq) for st in starts]
    s_w = [[_dot_nt(qh[h], kw_ref[0, pl.ds(offs[t], tq), :]) for h in parts] for t in range(n_w)]
    for t in range(n_w):
        for h in parts:
            s = s_w[t][h]
            c = lax.broadcasted_iota(jnp.int32, s.shape, 1)
            if t == 0:
                s = jnp.where(c > row_query(s.shape), s, MASK_VALUE)
            if t == n_w - 1:
                s = jnp.where(c <= row_query(s.shape), s, MASK_VALUE)
            else:
                s = jnp.where(starts[t] >= 0, s, MASK_VALUE)
            s_w[t][h] = s
    o_w = []
    for h in parts:
        m_w = s_w[0][h].max(axis=1, keepdims=True)
        for t in range(1, n_w):
            m_w = jnp.maximum(m_w, s_w[t][h].max(axis=1, keepdims=True))
        acc_w = jnp.zeros((prow, 2 * HEAD_DIM), jnp.float32)
        for t in range(n_w):
            p = jnp.exp2((s_w[t][h] - m_w).astype(jnp.bfloat16))
            v = vw_ref[0, pl.ds(offs[t], tq), :]
            acc_w = acc_w + _dot(p, jnp.concatenate([v, jnp.ones_like(v)], axis=1))
        o_w.append(_softmax_result(acc_w))

    gates = jax.nn.sigmoid(zs_ref[0])
    lane = lax.broadcasted_iota(jnp.int32, gates.shape, 1)
    outs = []
    for r in range(NSA_GROUP):
        h, rows = r // hpp, slice((r % hpp) * tq, (r % hpp + 1) * tq)
        o_r = jnp.zeros((tq, HEAD_DIM), jnp.float32)
        for br, o_b in enumerate((o_c, o_s, o_w)):
            col = FOX_HEADS + 3 * (g * NSA_GROUP + r) + br
            gcol = jnp.sum(jnp.where(lane == col, gates, 0.0), axis=1, keepdims=True)
            o_r = o_r + gcol * o_b[h][rows]
        outs.append(o_r)
    o_ref[0] = jnp.concatenate(outs, axis=1).astype(o_ref.dtype)


def _nsa(z_main, z_small, kvc, ovt):
    nb, s, _ = z_main.shape
    nc = kvc.shape[2]
    gw = NSA_GROUP * HEAD_DIM
    full = lambda col0: pl.BlockSpec(
        (1, s, HEAD_DIM), lambda b, g, i, c=col0 // HEAD_DIM: (b, 0, c + g))
    return pl.pallas_call(
        functools.partial(_nsa_kernel, seq=s),
        out_shape=jax.ShapeDtypeStruct((nb, s, NSA_W), jnp.bfloat16),
        grid=(nb, NSA_KV_HEADS, s // NSA_TQ),
        in_specs=[
            pl.BlockSpec((1, NSA_TQ, gw), lambda b, g, i: (b, i, COL_QN // gw + g)),
            pl.BlockSpec((1, NSA_TQ, Z_SMALL), lambda b, g, i: (b, i, 0)),
            pl.BlockSpec((1, 1, nc, HEAD_DIM), lambda b, g, i: (b, g, 0, 0)),
            pl.BlockSpec((1, 1, nc, HEAD_DIM), lambda b, g, i: (b, NSA_KV_HEADS + g, 0, 0)),
            full(COL_KS), full(COL_VS), full(COL_KW), full(COL_VW),
            pl.BlockSpec(ovt.shape, lambda b, g, i: (0, 0)),
        ],
        out_specs=pl.BlockSpec((1, NSA_TQ, gw), lambda b, g, i: (b, i, g)),
        scratch_shapes=[pltpu.VMEM((s, HEAD_DIM + NSEL_PAD), jnp.bfloat16)],
        compiler_params=_cparams(("arbitrary", "arbitrary", "arbitrary")),
        name="nsa",
    )(z_main, z_small, kvc, kvc, z_main, z_main, z_main, z_main, ovt)


OUT_TM = 512
OUT_RC = 256


def _outproj_kernel(of_ref, on_ref, bf_ref, bn_ref, w_ref, x_ref, g_ref, nw_ref, sc_ref,
                    sh_ref, h_ref, u_ref):
    for r in range(OUT_TM // OUT_RC):
        rows = slice(r * OUT_RC, (r + 1) * OUT_RC)
        yf = _rms(of_ref[0, rows, :].astype(jnp.float32)) * bf_ref[...]
        yn = _rms(on_ref[0, rows, :].astype(jnp.float32)) * bn_ref[...]
        y = jnp.concatenate([yf.astype(jnp.bfloat16), yn.astype(jnp.bfloat16)], axis=1)
        h = x_ref[0, rows, :] + g_ref[0] * _dot(y, w_ref[...])
        h_ref[0, rows, :] = h
        u = _rms(h) * nw_ref[...]
        u_ref[0, rows, :] = (u * (1.0 + sc_ref[0]) + sh_ref[0]).astype(jnp.bfloat16)


def _outproj(o_f, o_n, beta_f, beta_n, w_out, x, gate, norm_w, sc, sh):
    nb, s, d = x.shape
    vec = lambda n: pl.BlockSpec((1, n), lambda b, i: (0, 0))
    bvec = pl.BlockSpec((1, 1, d), lambda b, i: (b, 0, 0))
    rows = lambda n: pl.BlockSpec((1, OUT_TM, n), lambda b, i: (b, i, 0))
    return pl.pallas_call(
        _outproj_kernel,
        out_shape=(jax.ShapeDtypeStruct((nb, s, d), jnp.float32),
                   jax.ShapeDtypeStruct((nb, s, d), jnp.bfloat16)),
        grid=(nb, s // OUT_TM),
        in_specs=[
            rows(FOX_W), rows(NSA_W), vec(FOX_W), vec(NSA_W),
            pl.BlockSpec((FOX_W + NSA_W, d), lambda b, i: (0, 0),
                         pipeline_mode=pl.Buffered(1)),
            rows(d), bvec, vec(d), bvec, bvec,
        ],
        out_specs=(rows(d), rows(d)),
        compiler_params=_cparams(("arbitrary", "arbitrary")),
        name="outproj",
    )(o_f, o_n, beta_f.reshape(1, FOX_W), beta_n.reshape(1, NSA_W), w_out, x,
      gate.reshape(nb, 1, d), norm_w.reshape(1, d), sc.reshape(nb, 1, d),
      sh.reshape(nb, 1, d))


FFN_TM = 1024
FFN_TF = 512
FFN_RC = 512
FFN_RC_LAST = 256


def _ffn_kernel(u_ref, h_ref, g_ref, fw_ref, wg_ref, wu_ref, wd_ref, o_ref, *, final):
    j = pl.program_id(2)
    last = pl.num_programs(2) - 1

    def partial_sum(rows):
        u = u_ref[0, rows, :]
        a = _dot(u, wg_ref[...])
        b = _dot(u, wu_ref[...])
        mid = (a * jax.nn.sigmoid(a) * b).astype(jnp.bfloat16)
        return _dot(mid, wd_ref[...])

    def chunks(rc):
        return [slice(r * rc, (r + 1) * rc) for r in range(FFN_TM // rc)]

    @pl.when(j == 0)
    def _():
        for rows in chunks(FFN_RC):
            o_ref[0, rows, :] = partial_sum(rows)

    @pl.when((j > 0) & (j < last))
    def _():
        for rows in chunks(FFN_RC):
            o_ref[0, rows, :] += partial_sum(rows)

    @pl.when(j == last)
    def _():
        for rows in chunks(FFN_RC_LAST):
            h2 = h_ref[0, rows, :] + g_ref[0] * (o_ref[0, rows, :] + partial_sum(rows))
            o_ref[0, rows, :] = _rms(h2) * fw_ref[...] if final else h2


def _ffn(u2, h1, gate, final_w, w_gate, w_up, w_down, final):
    nb, s, d = h1.shape
    f = w_gate.shape[1]
    return pl.pallas_call(
        functools.partial(_ffn_kernel, final=final),
        out_shape=jax.ShapeDtypeStruct((nb, s, d), jnp.float32),
        grid=(nb, s // FFN_TM, f // FFN_TF),
        in_specs=[
            pl.BlockSpec((1, FFN_TM, d), lambda b, i, j: (b, i, 0)),
            pl.BlockSpec((1, FFN_TM, d), lambda b, i, j: (b, i, 0),
                         pipeline_mode=pl.Buffered(1)),
            pl.BlockSpec((1, 1, d), lambda b, i, j: (b, 0, 0)),
            pl.BlockSpec((1, d), lambda b, i, j: (0, 0)),
            pl.BlockSpec((d, FFN_TF), lambda b, i, j: (0, j)),
            pl.BlockSpec((d, FFN_TF), lambda b, i, j: (0, j)),
            pl.BlockSpec((FFN_TF, d), lambda b, i, j: (j, 0)),
        ],
        out_specs=pl.BlockSpec((1, FFN_TM, d), lambda b, i, j: (b, i, 0)),
        compiler_params=pltpu.CompilerParams(
            dimension_semantics=("arbitrary", "arbitrary", "arbitrary"),
            vmem_limit_bytes=FFN_VMEM_LIMIT),
        name="ffn",
    )(u2, h1, gate.reshape(nb, 1, d), final_w.reshape(1, d), w_gate, w_up, w_down)


def _split_w_in(w_in):
    sizes = ([FOX_W] * 3 + [FOX_HEADS] + [NSA_W] + [KV_W] * 6 + [3 * NSA_HEADS])
    offs = [0]
    for sz in sizes:
        offs.append(offs[-1] + sz)
    wt = w_in.T
    piece = lambda n: wt[offs[n]:offs[n + 1]].astype(jnp.bfloat16)
    order = (0, 1, 2, 4, 7, 9, 8, 10, 5, 6)
    src_rows = [offs[n] + r for n in order for r in range(0, sizes[n], KV_W)]
    assert all(r % SUBLANES == 0 for r in src_rows), src_rows
    main = _gather_cast_rows(wt, jnp.asarray([r // SUBLANES for r in src_rows], jnp.int32))
    small = jnp.concatenate(
        [piece(3), piece(11),
         jnp.zeros((Z_SMALL - FOX_HEADS - 3 * NSA_HEADS, w_in.shape[0]), jnp.bfloat16)], axis=0)
    return main, small


def _cast_kernel(rows_ref, w_ref, o_ref):
    del rows_ref
    o_ref[...] = w_ref[...].astype(o_ref.dtype)


def _gather_cast_rows(wt, src_tiles):
    src_rows = src_tiles
    n_units = src_rows.shape[0]
    d = wt.shape[1]
    return pl.pallas_call(
        _cast_kernel,
        out_shape=jax.ShapeDtypeStruct((n_units * KV_W, d), jnp.bfloat16),
        grid_spec=pltpu.PrefetchScalarGridSpec(
            num_scalar_prefetch=1,
            grid=(n_units,),
            in_specs=[pl.BlockSpec((pl.Element(KV_W), pl.Element(d)),
                                   lambda u, rows: (rows[u] * SUBLANES, 0))],
            out_specs=pl.BlockSpec((KV_W, d), lambda u, rows: (u, 0)),
        ),
        compiler_params=_cparams(("arbitrary",)),
        name="w_in_prep",
    )(src_rows, wt)


def _overlap_t(n_cmp_pad, n_slc):
    cs = jnp.arange(n_cmp_pad) * CMP_STRIDE
    ss = jnp.arange(n_slc) * SLC_LEN
    ov = (jnp.minimum(cs[None, :] + CMP_LEN, ss[:, None] + SLC_LEN)
          - jnp.maximum(cs[None, :], ss[:, None]))
    ov = jnp.clip(ov, 0).astype(jnp.float32) / CMP_STRIDE
    return ov.astype(jnp.bfloat16)


def kernel(x, c, positions, w_ada, b_ada, norm_attn, norm_ffn, w_in, b_fgate, cmp_pos,
           w_kc1, w_kc2, w_vc1, w_vc2, beta_fox, beta_nsa, w_out, w_gate, w_up, w_down,
           final_norm):
    nb, s, d = x.shape
    depth = w_ada.shape[0]
    n_cmp_pad = s // CMP_STRIDE
    cmp_end = jnp.arange(n_cmp_pad) * CMP_STRIDE + (CMP_LEN - 1)
    pos_c = jnp.take(positions, jnp.minimum(cmp_end, s - 1), axis=1)
    cos, sin = _rope_tables(positions, 512)
    cos_c, sin_c = _rope_tables(pos_c, n_cmp_pad)
    ovt = _overlap_t(n_cmp_pad, s // SLC_LEN)

    h = x
    for l in range(depth):
        mod = _ada(c, w_ada[l], b_ada[l])
        sh1, sc1, g1, sh2, sc2, g2 = jnp.split(mod, 6, axis=-1)
        w_main, w_small = _split_w_in(w_in[l])
        z_main, z_small, zc = _inproj(h, norm_attn[l], sc1, sh1, w_main, w_small, cos, sin)
        cum = _decay(z_small, b_fgate[l])

        w1 = jnp.stack([w_kc1[l], w_vc1[l]]).astype(jnp.bfloat16)
        w1 = w1.reshape(2, CMP_LEN, HEAD_DIM, CMP_HIDDEN)
        w2 = jnp.stack([w_kc2[l], w_vc2[l]]).astype(jnp.bfloat16)
        kvc = _compress(zc, w1, w2, cmp_pos[l], cos_c, sin_c)

        o_f, (wo, wg, wu, wd) = _fox(z_main, cum, (w_out[l], w_gate[l], w_up[l], w_down[l]))
        o_n = _nsa(z_main, z_small, kvc, ovt)
        h1, u2 = _outproj(o_f, o_n, beta_fox[l], beta_nsa[l], wo, h, g1, norm_ffn[l], sc2, sh2)
        h = _ffn(u2, h1, g2, final_norm, wg, wu, wd, final=(l == depth - 1))
    return h
```

```python
import functools

import jax
import jax.numpy as jnp
from jax import lax
from jax.experimental import pallas as pl
from jax.experimental.pallas import tpu as pltpu

D_MODEL = 2048
HEAD_DIM = 128
FOX_HEADS = 8
NSA_HEADS = 8
NSA_KV_HEADS = 2
NSA_GROUP = NSA_HEADS // NSA_KV_HEADS
CMP_LEN = 32
CMP_STRIDE = 16
CMP_HIDDEN = 2 * HEAD_DIM
SLC_LEN = 64
SLC_TOPK = 16
WINDOW = 512
ROPE_THETA = 10000.0
NORM_EPS = 1e-6
MASK_VALUE = -1e30
SEL_BONUS = 1e6

FOX_W = FOX_HEADS * HEAD_DIM
NSA_W = NSA_HEADS * HEAD_DIM
KV_W = NSA_KV_HEADS * HEAD_DIM
Z_MAIN = 3 * FOX_W + NSA_W + 6 * KV_W
Z_SMALL = 128
COL_QN = 3 * FOX_W
COL_KS = COL_QN + NSA_W
COL_KW = COL_KS + KV_W
COL_VS = COL_KW + KV_W
COL_VW = COL_VS + KV_W
COL_KC = COL_VW + KV_W

VMEM_LIMIT = 48 * 1024 * 1024
FFN_VMEM_LIMIT = 56 * 1024 * 1024
SCALE = HEAD_DIM ** -0.5
LOG2E = 1.4426950408889634


def _cparams(sem):
    return pltpu.CompilerParams(dimension_semantics=sem, vmem_limit_bytes=VMEM_LIMIT)


def _dot(a, b):
    return jnp.dot(a, b, preferred_element_type=jnp.float32)


def _dot_nt(a, b):
    return lax.dot_general(a, b, (((1,), (1,)), ((), ())), preferred_element_type=jnp.float32)


def _rms(x):
    return x * lax.rsqrt(jnp.mean(x * x, axis=-1, keepdims=True) + NORM_EPS)


ADA_TN = 1024
ADA_KC = 256


def _ada_kernel(ct_ref, w_ref, b_ref, o_ref, *, nb):
    tn = w_ref.shape[1]
    ct = ct_ref[...]
    st = ct * jax.nn.sigmoid(ct)

    accs = [jnp.zeros((8, tn), jnp.float32) for _ in range(nb)]
    for kc in range(D_MODEL // ADA_KC):
        w = w_ref[kc * ADA_KC:(kc + 1) * ADA_KC, :]
        for b in range(nb):
            s = st[kc * ADA_KC:(kc + 1) * ADA_KC, b:b + 1]
            accs[b] = accs[b] + jnp.sum(
                (w * s).reshape(ADA_KC // SUBLANES, SUBLANES, tn), axis=0)
    for b in range(nb):
        o_ref[b:b + 1, :] = jnp.sum(accs[b], axis=0, keepdims=True) + b_ref[...]


def _ada(c, w_ada, b_ada):
    nb = c.shape[0]
    n = w_ada.shape[1]
    return pl.pallas_call(
        functools.partial(_ada_kernel, nb=nb),
        out_shape=jax.ShapeDtypeStruct((nb, n), jnp.float32),
        grid=(n // ADA_TN,),
        in_specs=[
            pl.BlockSpec((D_MODEL, nb), lambda j: (0, 0)),
            pl.BlockSpec((D_MODEL, ADA_TN), lambda j: (0, j)),
            pl.BlockSpec((1, ADA_TN), lambda j: (0, j)),
        ],
        out_specs=pl.BlockSpec((nb, ADA_TN), lambda j: (0, j)),
        compiler_params=_cparams(("arbitrary",)),
        name="ada",
    )(c.T, w_ada, b_ada.reshape(1, n))


def _rope_table_kernel(pos_ref, inv_ref, cos_ref, sin_ref):
    half = pos_ref.shape[1] // 2
    lane = lax.broadcasted_iota(jnp.int32, (half, HEAD_DIM), 1)
    low = lane < HEAD_DIM // 2
    pos = jnp.where(low, pos_ref[0, 0:half, :], pos_ref[0, half:2 * half, :])
    ang = pos.astype(jnp.float32) * inv_ref[...]
    cos = jnp.cos(ang)
    sin = jnp.sin(ang)
    cos_sw = pltpu.roll(cos, HEAD_DIM // 2, axis=1)
    sin_sw = pltpu.roll(sin, HEAD_DIM // 2, axis=1)
    cos_ref[0, 0:half, :] = jnp.where(low, cos, cos_sw)
    cos_ref[0, half:2 * half, :] = jnp.where(low, cos_sw, cos)
    sin_ref[0, 0:half, :] = jnp.where(low, -sin, sin_sw)
    sin_ref[0, half:2 * half, :] = jnp.where(low, -sin_sw, sin)


def _rope_tables(pos, ts):
    nb, n = pos.shape
    inv = ROPE_THETA ** (-jnp.arange(0, HEAD_DIM, 2, dtype=jnp.float32) / HEAD_DIM)
    inv2 = jnp.concatenate([inv, inv]).reshape(1, HEAD_DIM)
    shp = jax.ShapeDtypeStruct((nb, n, HEAD_DIM), jnp.float32)
    spec = pl.BlockSpec((1, ts, HEAD_DIM), lambda b, i: (b, i, 0))
    return pl.pallas_call(
        _rope_table_kernel,
        out_shape=(shp, shp),
        grid=(nb, n // ts),
        in_specs=[pl.BlockSpec((1, ts, 1), lambda b, i: (b, i, 0)),
                  pl.BlockSpec((1, HEAD_DIM), lambda b, i: (0, 0))],
        out_specs=(spec, spec),
        compiler_params=_cparams(("arbitrary", "arbitrary")),
        name="rope_tables",
    )(pos.reshape(nb, n, 1), inv2)


def _apply_rope(x, cos, sin):
    return x * cos + pltpu.roll(x, HEAD_DIM // 2, axis=1) * sin


IN_TM = 1024
IN_TN = 512


def _inproj_kernel(x_ref, g_ref, sc_ref, sh_ref, w_ref, ws_ref, cos_ref, sin_ref,
                   z_ref, zs_ref, zc_ref, u_ref):
    j = pl.program_id(2)

    @pl.when(j == 0)
    def _():
        u = _rms(x_ref[0]) * g_ref[...]
        u = u * (1.0 + sc_ref[0]) + sh_ref[0]
        ub = u.astype(jnp.bfloat16)
        u_ref[...] = ub
        zs_ref[0] = _dot_nt(ub, ws_ref[...])

    fox_q = j < FOX_W // IN_TN
    nsa_q = (j >= COL_QN // IN_TN) & (j < COL_KS // IN_TN)
    nsa_k = j == COL_KS // IN_TN
    cmp_kv = j == COL_KC // IN_TN

    def matmul():
        return _dot_nt(u_ref[...], w_ref[...])

    def roped(acc):
        cos = cos_ref[0]
        sin = sin_ref[0]
        parts = [_apply_rope(acc[:, h * HEAD_DIM:(h + 1) * HEAD_DIM], cos, sin)
                 for h in range(IN_TN // HEAD_DIM)]
        return jnp.concatenate(parts, axis=1).astype(z_ref.dtype)

    @pl.when(fox_q)
    def _():
        z_ref[0] = (matmul() * (SCALE * LOG2E)).astype(z_ref.dtype)

    @pl.when(nsa_q)
    def _():
        z_ref[0] = roped(matmul() * (SCALE * LOG2E))

    @pl.when(nsa_k)
    def _():
        z_ref[0] = roped(matmul())

    @pl.when(cmp_kv)
    def _():
        acc = matmul()
        z_ref[0] = acc.astype(z_ref.dtype)
        zc_ref[0] = acc

    @pl.when(jnp.logical_not(fox_q | nsa_q | nsa_k | cmp_kv))
    def _():
        z_ref[0] = matmul().astype(z_ref.dtype)


def _inproj(x, norm_w, sc, sh, w_main, w_small, cos, sin):
    nb, s, d = x.shape
    grid = (nb, s // IN_TM, Z_MAIN // IN_TN)
    return pl.pallas_call(
        _inproj_kernel,
        out_shape=(jax.ShapeDtypeStruct((nb, s, Z_MAIN), jnp.bfloat16),
                   jax.ShapeDtypeStruct((nb, s, Z_SMALL), jnp.float32),
                   jax.ShapeDtypeStruct((nb, s, IN_TN), jnp.float32)),
        grid=grid,
        in_specs=[
            pl.BlockSpec((1, IN_TM, d), lambda b, i, j: (b, i, 0)),
            pl.BlockSpec((1, d), lambda b, i, j: (0, 0)),
            pl.BlockSpec((1, 1, d), lambda b, i, j: (b, 0, 0)),
            pl.BlockSpec((1, 1, d), lambda b, i, j: (b, 0, 0)),
            pl.BlockSpec((IN_TN, d), lambda b, i, j: (j, 0)),
            pl.BlockSpec((Z_SMALL, d), lambda b, i, j: (0, 0)),
            pl.BlockSpec((1, IN_TM, HEAD_DIM), lambda b, i, j: (b, i, 0)),
            pl.BlockSpec((1, IN_TM, HEAD_DIM), lambda b, i, j: (b, i, 0)),
        ],
        out_specs=(pl.BlockSpec((1, IN_TM, IN_TN), lambda b, i, j: (b, i, j)),
                   pl.BlockSpec((1, IN_TM, Z_SMALL), lambda b, i, j: (b, i, 0)),
                   pl.BlockSpec((1, IN_TM, IN_TN), lambda b, i, j: (b, i, 0))),
        scratch_shapes=[pltpu.VMEM((IN_TM, d), jnp.bfloat16)],
        compiler_params=_cparams(("arbitrary", "arbitrary", "arbitrary")),
        name="inproj",
    )(x, norm_w.reshape(1, d), sc.reshape(nb, 1, d), sh.reshape(nb, 1, d),
      w_main, w_small, cos, sin)


DEC_T = 2048


def _decay_kernel(zs_ref, bf_ref, cum_ref, carry_ref):
    @pl.when(pl.program_id(1) == 0)
    def _():
        carry_ref[...] = jnp.zeros_like(carry_ref)

    xt = zs_ref[0].T[0:FOX_HEADS, :] + bf_ref[...]
    lf = jnp.minimum(xt, 0.0) - jnp.log1p(jnp.exp(-jnp.abs(xt)))
    lane = lax.broadcasted_iota(jnp.int32, lf.shape, 1)
    sh = 1
    while sh < DEC_T:
        lf = lf + jnp.where(lane >= sh, pltpu.roll(lf, sh, axis=1), 0.0)
        sh *= 2
    cum = lf + carry_ref[:, 0:1]
    cum_ref[0] = cum
    carry_ref[...] = jnp.broadcast_to(cum[:, DEC_T - 1:DEC_T], carry_ref.shape)


def _decay(z_small, b_fgate):
    nb, s, _ = z_small.shape
    return pl.pallas_call(
        _decay_kernel,
        out_shape=jax.ShapeDtypeStruct((nb, FOX_HEADS, s), jnp.float32),
        grid=(nb, s // DEC_T),
        in_specs=[pl.BlockSpec((1, DEC_T, Z_SMALL), lambda b, i: (b, i, 0)),
                  pl.BlockSpec((FOX_HEADS, 1), lambda b, i: (0, 0))],
        out_specs=pl.BlockSpec((1, FOX_HEADS, DEC_T), lambda b, i: (b, 0, i)),
        scratch_shapes=[pltpu.VMEM((FOX_HEADS, 128), jnp.float32)],
        compiler_params=_cparams(("arbitrary", "arbitrary")),
        name="decay",
    )(z_small, b_fgate.reshape(FOX_HEADS, 1))


def _compress_kernel(x_ref, w1_ref, w2_ref, pos_ref, cos_ref, sin_ref, o_ref):
    kind = pl.program_id(1)
    nc = x_ref.shape[1] // CMP_STRIDE
    p = jnp.zeros((nc, CMP_HIDDEN), jnp.float32)
    q = jnp.zeros((nc, CMP_HIDDEN), jnp.float32)
    for l in range(0, CMP_STRIDE, 2):
        xl = jnp.concatenate(
            [x_ref[0, pl.ds(l + k, nc, stride=CMP_STRIDE), :].astype(jnp.bfloat16)
             for k in range(2)], axis=1)
        pair = lambda l0: w1_ref[0, l0:l0 + 2].reshape(2 * HEAD_DIM, CMP_HIDDEN)
        p = p + _dot(xl, pair(l))
        q = q + _dot(xl, pair(CMP_STRIDE + l))
    posb = jnp.broadcast_to(pos_ref[...], (8, CMP_LEN * HEAD_DIM)).astype(jnp.bfloat16)
    w1_flat = w1_ref[0].reshape(CMP_LEN * HEAD_DIM, CMP_HIDDEN)
    pterm = _dot(posb, w1_flat)[0:1, :]
    h = p + pltpu.roll(q, nc - 1, axis=0) + pterm
    a = jax.nn.gelu(h).astype(jnp.bfloat16)
    out = _dot(a, w2_ref[0])

    @pl.when(kind < NSA_KV_HEADS)
    def _():
        o_ref[0, 0] = _apply_rope(out, cos_ref[0], sin_ref[0]).astype(o_ref.dtype)

    @pl.when(kind >= NSA_KV_HEADS)
    def _():
        o_ref[0, 0] = out.astype(o_ref.dtype)


def _compress(zc, w1, w2, cmp_pos, cos_c, sin_c):
    nb, s, _ = zc.shape
    nk = 2 * NSA_KV_HEADS
    nc = s // CMP_STRIDE
    return pl.pallas_call(
        _compress_kernel,
        out_shape=jax.ShapeDtypeStruct((nb, nk, nc, HEAD_DIM), jnp.bfloat16),
        grid=(nb, nk),
        in_specs=[
            pl.BlockSpec((1, s, HEAD_DIM), lambda b, k: (b, 0, k)),
            pl.BlockSpec((1, CMP_LEN, HEAD_DIM, CMP_HIDDEN),
                         lambda b, k: (k // NSA_KV_HEADS, 0, 0, 0)),
            pl.BlockSpec((1, CMP_HIDDEN, HEAD_DIM), lambda b, k: (k // NSA_KV_HEADS, 0, 0)),
            pl.BlockSpec((1, CMP_LEN * HEAD_DIM), lambda b, k: (0, 0)),
            pl.BlockSpec((1, nc, HEAD_DIM), lambda b, k: (b, 0, 0)),
            pl.BlockSpec((1, nc, HEAD_DIM), lambda b, k: (b, 0, 0)),
        ],
        out_specs=pl.BlockSpec((1, 1, nc, HEAD_DIM), lambda b, k: (b, k, 0, 0)),
        compiler_params=_cparams(("arbitrary", "arbitrary")),
        name="compress",
    )(zc, w1, w2, cmp_pos.reshape(1, CMP_LEN * HEAD_DIM), cos_c, sin_c)


FOX_T = 512


def _softmax_step(s, m, acc, v):
    m_new = jnp.maximum(m, jnp.max(s, axis=1, keepdims=True))
    alpha = jnp.exp2(m - m_new)
    p = jnp.exp2((s - m_new).astype(jnp.bfloat16))
    va = jnp.concatenate([v, jnp.ones_like(v)], axis=1)
    return m_new, alpha * acc + _dot(p, va)


def _softmax_init(rows):
    return (jnp.full((rows, 1), MASK_VALUE, jnp.float32),
            jnp.zeros((rows, 2 * HEAD_DIM), jnp.float32))


def _softmax_result(acc):
    return acc[:, 0:HEAD_DIM] * (1.0 / acc[:, HEAD_DIM:2 * HEAD_DIM])


FOX_HB = 4


def _fox_kernel(q_ref, k_ref, v_ref, cum_ref, *rest, n_cast):
    o_ref = rest[n_cast]
    for w_ref, wb_ref in zip(rest[:n_cast], rest[n_cast + 1:]):
        wb_ref[...] = w_ref[...].astype(wb_ref.dtype)
    hb = pl.program_id(1)
    i = pl.program_id(2)
    t = FOX_T
    cols = [slice(n * HEAD_DIM, (n + 1) * HEAD_DIM) for n in range(FOX_HB)]
    qs = [q_ref[0, :, c] for c in cols]

    def tile(k0, width, carry, diag):
        k0 = pl.multiple_of(k0, t)
        logits = [_dot_nt(qs[n], k_ref[0, pl.ds(k0, width), cols[n]]) for n in range(FOX_HB)]
        out = []
        for n, (m, acc) in enumerate(carry):
            v = v_ref[0, pl.ds(k0, width), cols[n]]
            ck = cum_ref[0, pl.ds(hb * FOX_HB + n, 1), pl.ds(k0, width)] * LOG2E
            s = logits[n] - ck
            if diag:
                r = lax.broadcasted_iota(jnp.int32, s.shape, 0)
                c = lax.broadcasted_iota(jnp.int32, s.shape, 1)
                s = jnp.where(c <= r + (width - t), s, MASK_VALUE)
            out.append(_softmax_step(s, m, acc, v))
        return tuple(out)

    init = tuple(_softmax_init(t) for _ in range(FOX_HB))
    n_pairs = i >> 1
    carry = lax.fori_loop(0, n_pairs, lambda p, c: tile(p * (2 * t), 2 * t, c, False), init)
    k_tail = n_pairs * (2 * t)
    carry = lax.cond((i & 1) == 0,
                     lambda c: tile(k_tail, t, c, True),
                     lambda c: tile(k_tail, 2 * t, c, True), carry)
    for n, (_, acc) in enumerate(carry):
        o_ref[0, :, cols[n]] = _softmax_result(acc).astype(o_ref.dtype)


def _fox(z_main, cum, weights):
    nb, s, _ = z_main.shape
    w = FOX_HB * HEAD_DIM
    grid = (nb, FOX_HEADS // FOX_HB, s // FOX_T)
    n_steps = grid[0] * grid[1] * grid[2]
    step = lambda b, h, i: ((b * grid[1] + h) * grid[2] + i, 0)
    slabs = []
    for wt in weights:
        rows = wt.shape[0] // n_steps
        assert rows * n_steps == wt.shape[0] and rows % BF16_SUBLANES == 0, wt.shape
        slabs.append(pl.BlockSpec((rows, wt.shape[1]), step))
    outs = pl.pallas_call(
        functools.partial(_fox_kernel, n_cast=len(weights)),
        out_shape=[jax.ShapeDtypeStruct((nb, s, FOX_W), jnp.bfloat16)]
        + [jax.ShapeDtypeStruct(wt.shape, jnp.bfloat16) for wt in weights],
        grid=grid,
        in_specs=[
            pl.BlockSpec((1, FOX_T, w), lambda b, h, i: (b, i, h)),
            pl.BlockSpec((1, s, w), lambda b, h, i: (b, 0, FOX_W // w + h)),
            pl.BlockSpec((1, s, w), lambda b, h, i: (b, 0, 2 * FOX_W // w + h)),
            pl.BlockSpec((1, FOX_HEADS, s), lambda b, h, i: (b, 0, 0)),
        ] + slabs,
        out_specs=[pl.BlockSpec((1, FOX_T, w), lambda b, h, i: (b, i, h))] + slabs,
        compiler_params=_cparams(("arbitrary", "arbitrary", "arbitrary")),
        name="fox",
    )(z_main, z_main, z_main, cum, *weights)
    return outs[0], outs[1:]


NSA_TQ = 512
NSA_PARTS = 4
NSEL_PAD = 128
SLC_SHIFT = SLC_LEN.bit_length() - 1
assert 1 << SLC_SHIFT == SLC_LEN
assert WINDOW % NSA_TQ == 0 and NSA_TQ % SLC_LEN == 0
SUBLANES = 8
BF16_SUBLANES = 16


def _block_ranks(score, n_slc):
    tq = score.shape[1]
    groups = [score[SUBLANES * v:SUBLANES * (v + 1)] for v in range(n_slc // SUBLANES)]
    ranks = [jnp.zeros((SUBLANES, tq), jnp.float32) for _ in groups]
    for m in range(n_slc):
        row = jnp.broadcast_to(score[m:m + 1, :], (SUBLANES, tq))
        for v, grp in enumerate(groups):
            lo = SUBLANES * v
            if lo > m:
                before = row >= grp
            elif lo + SUBLANES - 1 <= m:
                before = row > grp
            else:
                n_i = lo + lax.broadcasted_iota(jnp.int32, grp.shape, 0)
                before = (row > grp) | ((row == grp) & (n_i > m))
            ranks[v] = ranks[v] + jnp.where(before, 1.0, 0.0)
    return jnp.concatenate(ranks, axis=0)


def _nsa_kernel(q_ref, zs_ref, kc_ref, vc_ref, ks_ref, vs_ref, kw_ref, vw_ref, ovt_ref,
                o_ref, ka_ref, *, seq):
    g = pl.program_id(1)
    i = pl.program_id(2)
    tq = NSA_TQ
    q0 = i * tq

    @pl.when(i == 0)
    def _():
        ka_ref[:, 0:HEAD_DIM] = ks_ref[0]
        tok = lax.broadcasted_iota(jnp.int32, (seq, NSEL_PAD), 0)
        n_i = lax.broadcasted_iota(jnp.int32, (seq, NSEL_PAD), 1)
        ka_ref[:, HEAD_DIM:HEAD_DIM + NSEL_PAD] = jnp.where(
            n_i == (tok >> SLC_SHIFT), 1.0, 0.0).astype(jnp.bfloat16)

    qb = q_ref[0]
    hpp = NSA_GROUP // NSA_PARTS
    parts = range(NSA_PARTS)
    qh = [jnp.concatenate(
        [qb[:, r * HEAD_DIM:(r + 1) * HEAD_DIM] for r in range(h * hpp, (h + 1) * hpp)], axis=0)
        for h in parts]
    prow = hpp * tq

    def row_query(shape):
        return lax.broadcasted_iota(jnp.int32, shape, 0) & (tq - 1)

    kc = kc_ref[0, 0]
    vc = vc_ref[0, 0]
    s_cs = [_dot_nt(qh[h], kc) for h in parts]
    p_cs = []
    for s_c in s_cs:
        cend = lax.broadcasted_iota(jnp.int32, s_c.shape, 1) * CMP_STRIDE + (CMP_LEN - 1)
        s_c = jnp.where(cend <= q0 + row_query(s_c.shape), s_c, MASK_VALUE)
        m_c = jnp.maximum(jnp.max(s_c, axis=1, keepdims=True), 0.1 * MASK_VALUE)
        e_c = jnp.exp2(s_c - m_c)
        p_cs.append(e_c * (1.0 / jnp.maximum(jnp.sum(e_c, axis=1, keepdims=True), 1e-30)))
    o_c = [_dot(p_c.astype(jnp.bfloat16), vc) for p_c in p_cs]

    psum = None
    for p_c in p_cs:
        for r in range(hpp):
            term = p_c[r * tq:(r + 1) * tq]
            psum = term if psum is None else psum + term
    p_hi = psum.astype(jnp.bfloat16)
    p_lo = (psum - p_hi.astype(jnp.float32)).astype(jnp.bfloat16)
    ovt = ovt_ref[...]
    imp_t = _dot_nt(ovt, p_hi) + _dot_nt(ovt, p_lo)
    n_slc = seq // SLC_LEN
    blk = lax.broadcasted_iota(jnp.int32, imp_t.shape, 0)
    cur = (q0 + lax.broadcasted_iota(jnp.int32, imp_t.shape, 1)) >> SLC_SHIFT
    forced = (blk == 0) | (blk == cur) | (blk == cur - 1)
    score = jnp.where(forced, SEL_BONUS, imp_t)
    score = jnp.where(blk <= cur, score, -SEL_BONUS)
    rank = _block_ranks(score, n_slc)
    bias_t = jnp.where(rank < min(SLC_TOPK, n_slc), 0.0, MASK_VALUE)
    bias_t = jnp.concatenate(
        [bias_t, jnp.zeros((NSEL_PAD - n_slc, tq), jnp.float32)], axis=0)
    bias = bias_t.T.astype(jnp.bfloat16)
    bias_p = jnp.concatenate([bias] * hpp, axis=0)
    qa = [jnp.concatenate([qh[h], bias_p], axis=1) for h in parts]

    def sel_tile(k0, width, carry, diag):
        k0 = pl.multiple_of(k0, tq)
        logits = [_dot_nt(qa[h], ka_ref[pl.ds(k0, width), :]) for h in parts]
        out = []
        for s, (m, acc) in zip(logits, carry):
            if diag:
                c = lax.broadcasted_iota(jnp.int32, s.shape, 1)
                s = jnp.where(c <= row_query(s.shape) + (width - tq), s, MASK_VALUE)
            out.append(_softmax_step(s, m, acc, vs_ref[0, pl.ds(k0, width), :]))
        return tuple(out)

    init = tuple(_softmax_init(prow) for _ in parts)
    n_pairs = i >> 1
    carry = lax.fori_loop(0, n_pairs,
                          lambda p, c: sel_tile(p * (2 * tq), 2 * tq, c, False), init)
    k_tail = n_pairs * (2 * tq)
    carry = lax.cond((i & 1) == 0,
                     lambda c: sel_tile(k_tail, tq, c, True),
                     lambda c: sel_tile(k_tail, 2 * tq, c, True), carry)
    o_s = [_softmax_result(acc) for _, acc in carry]

    n_w = WINDOW // tq + 1
    starts = [q0 - WINDOW + t * tq for t in range(n_w)]
    offs = [pl.multiple_of(jnp.maximum(st, 0), tq) for st in starts]
    s_w = [[_dot_nt(qh[h], kw_ref[0, pl.ds(offs[t], tq), :]) for h in parts] for t in range(n_w)]
    for t in range(n_w):
        for h in parts:
            s = s_w[t][h]
            c = lax.broadcasted_iota(jnp.int32, s.shape, 1)
            if t == 0:
                s = jnp.where(c > row_query(s.shape), s, MASK_VALUE)
            if t == n_w - 1:
                s = jnp.where(c <= row_query(s.shape), s, MASK_VALUE)
            else:
                s = jnp.where(starts[t] >= 0, s, MASK_VALUE)
            s_w[t][h] = s
    o_w = []
    for h in parts:
        m_w = s_w[0][h].max(axis=1, keepdims=True)
        for t in range(1, n_w):
            m_w = jnp.maximum(m_w, s_w[t][h].max(axis=1, keepdims=True))
        acc_w = jnp.zeros((prow, 2 * HEAD_DIM), jnp.float32)
        for t in range(n_w):
            p = jnp.exp2((s_w[t][h] - m_w).astype(jnp.bfloat16))
            v = vw_ref[0, pl.ds(offs[t], tq), :]
            acc_w = acc_w + _dot(p, jnp.concatenate([v, jnp.ones_like(v)], axis=1))
        o_w.append(_softmax_result(acc_w))

    gates = jax.nn.sigmoid(zs_ref[0])
    lane = lax.broadcasted_iota(jnp.int32, gates.shape, 1)
    outs = []
    for r in range(NSA_GROUP):
        h, rows = r // hpp, slice((r % hpp) * tq, (r % hpp + 1) * tq)
        o_r = jnp.zeros((tq, HEAD_DIM), jnp.float32)
        for br, o_b in enumerate((o_c, o_s, o_w)):
            col = FOX_HEADS + 3 * (g * NSA_GROUP + r) + br
            gcol = jnp.sum(jnp.where(lane == col, gates, 0.0), axis=1, keepdims=True)
            o_r = o_r + gcol * o_b[h][rows]
        outs.append(o_r)
    o_ref[0] = jnp.concatenate(outs, axis=1).astype(o_ref.dtype)


def _nsa(z_main, z_small, kvc, ovt):
    nb, s, _ = z_main.shape
    nc = kvc.shape[2]
    gw = NSA_GROUP * HEAD_DIM
    full = lambda col0: pl.BlockSpec(
        (1, s, HEAD_DIM), lambda b, g, i, c=col0 // HEAD_DIM: (b, 0, c + g))
    return pl.pallas_call(
        functools.partial(_nsa_kernel, seq=s),
        out_shape=jax.ShapeDtypeStruct((nb, s, NSA_W), jnp.bfloat16),
        grid=(nb, NSA_KV_HEADS, s // NSA_TQ),
        in_specs=[
            pl.BlockSpec((1, NSA_TQ, gw), lambda b, g, i: (b, i, COL_QN // gw + g)),
            pl.BlockSpec((1, NSA_TQ, Z_SMALL), lambda b, g, i: (b, i, 0)),
            pl.BlockSpec((1, 1, nc, HEAD_DIM), lambda b, g, i: (b, g, 0, 0)),
            pl.BlockSpec((1, 1, nc, HEAD_DIM), lambda b, g, i: (b, NSA_KV_HEADS + g, 0, 0)),
            full(COL_KS), full(COL_VS), full(COL_KW), full(COL_VW),
            pl.BlockSpec(ovt.shape, lambda b, g, i: (0, 0)),
        ],
        out_specs=pl.BlockSpec((1, NSA_TQ, gw), lambda b, g, i: (b, i, g)),
        scratch_shapes=[pltpu.VMEM((s, HEAD_DIM + NSEL_PAD), jnp.bfloat16)],
        compiler_params=_cparams(("arbitrary", "arbitrary", "arbitrary")),
        name="nsa",
    )(z_main, z_small, kvc, kvc, z_main, z_main, z_main, z_main, ovt)


OUT_TM = 512
OUT_RC = 256


def _outproj_kernel(of_ref, on_ref, bf_ref, bn_ref, w_ref, x_ref, g_ref, nw_ref, sc_ref,
                    sh_ref, h_ref, u_ref):
    for r in range(OUT_TM // OUT_RC):
        rows = slice(r * OUT_RC, (r + 1) * OUT_RC)
        yf = _rms(of_ref[0, rows, :].astype(jnp.float32)) * bf_ref[...]
        yn = _rms(on_ref[0, rows, :].astype(jnp.float32)) * bn_ref[...]
        y = jnp.concatenate([yf.astype(jnp.bfloat16), yn.astype(jnp.bfloat16)], axis=1)
        h = x_ref[0, rows, :] + g_ref[0] * _dot(y, w_ref[...])
        h_ref[0, rows, :] = h
        u = _rms(h) * nw_ref[...]
        u_ref[0, rows, :] = (u * (1.0 + sc_ref[0]) + sh_ref[0]).astype(jnp.bfloat16)


def _outproj(o_f, o_n, beta_f, beta_n, w_out, x, gate, norm_w, sc, sh):
    nb, s, d = x.shape
    vec = lambda n: pl.BlockSpec((1, n), lambda b, i: (0, 0))
    bvec = pl.BlockSpec((1, 1, d), lambda b, i: (b, 0, 0))
    rows = lambda n: pl.BlockSpec((1, OUT_TM, n), lambda b, i: (b, i, 0))
    return pl.pallas_call(
        _outproj_kernel,
        out_shape=(jax.ShapeDtypeStruct((nb, s, d), jnp.float32),
                   jax.ShapeDtypeStruct((nb, s, d), jnp.bfloat16)),
        grid=(nb, s // OUT_TM),
        in_specs=[
            rows(FOX_W), rows(NSA_W), vec(FOX_W), vec(NSA_W),
            pl.BlockSpec((FOX_W + NSA_W, d), lambda b, i: (0, 0),
                         pipeline_mode=pl.Buffered(1)),
            rows(d), bvec, vec(d), bvec, bvec,
        ],
        out_specs=(rows(d), rows(d)),
        compiler_params=_cparams(("arbitrary", "arbitrary")),
        name="outproj",
    )(o_f, o_n, beta_f.reshape(1, FOX_W), beta_n.reshape(1, NSA_W), w_out, x,
      gate.reshape(nb, 1, d), norm_w.reshape(1, d), sc.reshape(nb, 1, d),
      sh.reshape(nb, 1, d))


FFN_TM = 1024
FFN_TF = 512
FFN_RC = 512
FFN_RC_LAST = 256


def _ffn_kernel(u_ref, h_ref, g_ref, fw_ref, wg_ref, wu_ref, wd_ref, o_ref, *, final):
    j = pl.program_id(2)
    last = pl.num_programs(2) - 1

    def partial_sum(rows):
        u = u_ref[0, rows, :]
        a = _dot(u, wg_ref[...])
        b = _dot(u, wu_ref[...])
        mid = (a * jax.nn.sigmoid(a) * b).astype(jnp.bfloat16)
        return _dot(mid, wd_ref[...])

    def chunks(rc):
        return [slice(r * rc, (r + 1) * rc) for r in range(FFN_TM // rc)]

    @pl.when(j == 0)
    def _():
        for rows in chunks(FFN_RC):
            o_ref[0, rows, :] = partial_sum(rows)

    @pl.when((j > 0) & (j < last))
    def _():
        for rows in chunks(FFN_RC):
            o_ref[0, rows, :] += partial_sum(rows)

    @pl.when(j == last)
    def _():
        for rows in chunks(FFN_RC_LAST):
            h2 = h_ref[0, rows, :] + g_ref[0] * (o_ref[0, rows, :] + partial_sum(rows))
            o_ref[0, rows, :] = _rms(h2) * fw_ref[...] if final else h2


def _ffn(u2, h1, gate, final_w, w_gate, w_up, w_down, final):
    nb, s, d = h1.shape
    f = w_gate.shape[1]
    return pl.pallas_call(
        functools.partial(_ffn_kernel, final=final),
        out_shape=jax.ShapeDtypeStruct((nb, s, d), jnp.float32),
        grid=(nb, s // FFN_TM, f // FFN_TF),
        in_specs=[
            pl.BlockSpec((1, FFN_TM, d), lambda b, i, j: (b, i, 0)),
            pl.BlockSpec((1, FFN_TM, d), lambda b, i, j: (b, i, 0),
                         pipeline_mode=pl.Buffered(1)),
            pl.BlockSpec((1, 1, d), lambda b, i, j: (b, 0, 0)),
            pl.BlockSpec((1, d), lambda b, i, j: (0, 0)),
            pl.BlockSpec((d, FFN_TF), lambda b, i, j: (0, j)),
            pl.BlockSpec((d, FFN_TF), lambda b, i, j: (0, j)),
            pl.BlockSpec((FFN_TF, d), lambda b, i, j: (j, 0)),
        ],
        out_specs=pl.BlockSpec((1, FFN_TM, d), lambda b, i, j: (b, i, 0)),
        compiler_params=pltpu.CompilerParams(
            dimension_semantics=("arbitrary", "arbitrary", "arbitrary"),
            vmem_limit_bytes=FFN_VMEM_LIMIT),
        name="ffn",
    )(u2, h1, gate.reshape(nb, 1, d), final_w.reshape(1, d), w_gate, w_up, w_down)


def _split_w_in(w_in):
    sizes = ([FOX_W] * 3 + [FOX_HEADS] + [NSA_W] + [KV_W] * 6 + [3 * NSA_HEADS])
    offs = [0]
    for sz in sizes:
        offs.append(offs[-1] + sz)
    wt = w_in.T
    order = (0, 1, 2, 4, 7, 9, 8, 10, 5, 6)
    units = lambda ns, unit: [offs[n] + r for n in ns for r in range(0, sizes[n], unit)]
    main = _gather_rows(wt, units(order, KV_W), KV_W, jnp.bfloat16)
    narrow = _gather_rows(wt, units((3, 11), SUBLANES), SUBLANES, jnp.float32)
    small = jnp.concatenate(
        [narrow, jnp.zeros((Z_SMALL - narrow.shape[0], w_in.shape[0]), jnp.float32)], axis=0)
    return main, small.astype(jnp.bfloat16)


def _cast_kernel(rows_ref, w_ref, o_ref):
    del rows_ref
    o_ref[...] = w_ref[...].astype(o_ref.dtype)


def _gather_rows(wt, src_rows, unit, dtype):
    assert all(r % SUBLANES == 0 for r in src_rows), src_rows
    tiles = jnp.asarray([r // SUBLANES for r in src_rows], jnp.int32)
    n_units = len(src_rows)
    d = wt.shape[1]
    return pl.pallas_call(
        _cast_kernel,
        out_shape=jax.ShapeDtypeStruct((n_units * unit, d), dtype),
        grid_spec=pltpu.PrefetchScalarGridSpec(
            num_scalar_prefetch=1,
            grid=(n_units,),
            in_specs=[pl.BlockSpec((pl.Element(unit), pl.Element(d)),
                                   lambda u, rows: (rows[u] * SUBLANES, 0))],
            out_specs=pl.BlockSpec((unit, d), lambda u, rows: (u, 0)),
        ),
        compiler_params=_cparams(("arbitrary",)),
        name="w_in_prep",
    )(tiles, wt)


def _overlap_t(n_cmp_pad, n_slc):
    cs = jnp.arange(n_cmp_pad) * CMP_STRIDE
    ss = jnp.arange(n_slc) * SLC_LEN
    ov = (jnp.minimum(cs[None, :] + CMP_LEN, ss[:, None] + SLC_LEN)
          - jnp.maximum(cs[None, :], ss[:, None]))
    ov = jnp.clip(ov, 0).astype(jnp.float32) / CMP_STRIDE
    return ov.astype(jnp.bfloat16)


def kernel(x, c, positions, w_ada, b_ada, norm_attn, norm_ffn, w_in, b_fgate, cmp_pos,
           w_kc1, w_kc2, w_vc1, w_vc2, beta_fox, beta_nsa, w_out, w_gate, w_up, w_down,
           final_norm):
    nb, s, d = x.shape
    depth = w_ada.shape[0]
    n_cmp_pad = s // CMP_STRIDE
    cmp_end = jnp.arange(n_cmp_pad) * CMP_STRIDE + (CMP_LEN - 1)
    pos_c = jnp.take(positions, jnp.minimum(cmp_end, s - 1), axis=1)
    cos, sin = _rope_tables(positions, 512)
    cos_c, sin_c = _rope_tables(pos_c, n_cmp_pad)
    ovt = _overlap_t(n_cmp_pad, s // SLC_LEN)

    h = x
    for l in range(depth):
        mod = _ada(c, w_ada[l], b_ada[l])
        sh1, sc1, g1, sh2, sc2, g2 = jnp.split(mod, 6, axis=-1)
        w_main, w_small = _split_w_in(w_in[l])
        z_main, z_small, zc = _inproj(h, norm_attn[l], sc1, sh1, w_main, w_small, cos, sin)
        cum = _decay(z_small, b_fgate[l])

        w1 = jnp.stack([w_kc1[l], w_vc1[l]]).astype(jnp.bfloat16)
        w1 = w1.reshape(2, CMP_LEN, HEAD_DIM, CMP_HIDDEN)
        w2 = jnp.stack([w_kc2[l], w_vc2[l]]).astype(jnp.bfloat16)
        kvc = _compress(zc, w1, w2, cmp_pos[l], cos_c, sin_c)

        o_f, (wo, wg, wu, wd) = _fox(z_main, cum, (w_out[l], w_gate[l], w_up[l], w_down[l]))
        o_n = _nsa(z_main, z_small, kvc, ovt)
        h1, u2 = _outproj(o_f, o_n, beta_fox[l], beta_nsa[l], wo, h, g1, norm_ffn[l], sc2, sh2)
        h = _ffn(u2, h1, g2, final_norm, wg, wu, wd, final=(l == depth - 1))
    return h
```

```python
import functools

import jax
import jax.numpy as jnp
from jax import lax
from jax.experimental import pallas as pl
from jax.experimental.pallas import tpu as pltpu

D_MODEL = 2048
HEAD_DIM = 128
FOX_HEADS = 8
NSA_HEADS = 8
NSA_KV_HEADS = 2
NSA_GROUP = NSA_HEADS // NSA_KV_HEADS
CMP_LEN = 32
CMP_STRIDE = 16
CMP_HIDDEN = 2 * HEAD_DIM
SLC_LEN = 64
SLC_TOPK = 16
WINDOW = 512
ROPE_THETA = 10000.0
NORM_EPS = 1e-6
MASK_VALUE = -1e30
SEL_BONUS = 1e6

FOX_W = FOX_HEADS * HEAD_DIM
NSA_W = NSA_HEADS * HEAD_DIM
KV_W = NSA_KV_HEADS * HEAD_DIM
Z_MAIN = 3 * FOX_W + NSA_W + 6 * KV_W
Z_SMALL = 128
COL_QN = 3 * FOX_W
COL_KS = COL_QN + NSA_W
COL_KW = COL_KS + KV_W
COL_VS = COL_KW + KV_W
COL_VW = COL_VS + KV_W
COL_KC = COL_VW + KV_W

VMEM_LIMIT = 48 * 1024 * 1024
FFN_VMEM_LIMIT = 56 * 1024 * 1024
SCALE = HEAD_DIM ** -0.5
LOG2E = 1.4426950408889634


def _cparams(sem):
    return pltpu.CompilerParams(dimension_semantics=sem, vmem_limit_bytes=VMEM_LIMIT)


def _dot(a, b):
    return jnp.dot(a, b, preferred_element_type=jnp.float32)


def _dot_nt(a, b):
    return lax.dot_general(a, b, (((1,), (1,)), ((), ())), preferred_element_type=jnp.float32)


def _rms(x):
    return x * lax.rsqrt(jnp.mean(x * x, axis=-1, keepdims=True) + NORM_EPS)


ADA_TN = 768
ADA_KC = 256


def _ada_kernel(ct_ref, w_ref, b_ref, *rest, nb, with_rope):
    if with_rope:
        pos_ref, inv_ref, o_ref, cos_ref, sin_ref = rest
        _rope_table_kernel(pos_ref, inv_ref, cos_ref, sin_ref)
    else:
        o_ref, = rest
    tn = w_ref.shape[1]
    ct = ct_ref[...]
    st = ct * jax.nn.sigmoid(ct)

    accs = [jnp.zeros((SUBLANES, tn), jnp.float32) for _ in range(nb)]
    for kc in range(D_MODEL // ADA_KC):
        w = w_ref[kc * ADA_KC:(kc + 1) * ADA_KC, :]
        for b in range(nb):
            s = st[kc * ADA_KC:(kc + 1) * ADA_KC, b:b + 1]
            accs[b] = accs[b] + jnp.sum(
                (w * s).reshape(ADA_KC // SUBLANES, SUBLANES, tn), axis=0)
    for b in range(nb):
        o_ref[b:b + 1, :] = jnp.sum(accs[b], axis=0, keepdims=True) + b_ref[...]


def _ada(c, w_ada, b_ada, positions=None):
    nb = c.shape[0]
    n = w_ada.shape[1]
    n_steps = n // ADA_TN
    in_specs = [
        pl.BlockSpec((D_MODEL, nb), lambda j: (0, 0)),
        pl.BlockSpec((D_MODEL, ADA_TN), lambda j: (0, j)),
        pl.BlockSpec((1, ADA_TN), lambda j: (0, j)),
    ]
    out_shape = [jax.ShapeDtypeStruct((nb, n), jnp.float32)]
    out_specs = [pl.BlockSpec((nb, ADA_TN), lambda j: (0, j))]
    args = [c.T, w_ada, b_ada.reshape(1, n)]
    if positions is not None:
        s = positions.shape[1]
        per_batch = n_steps // nb
        ts = s // per_batch
        assert per_batch * nb == n_steps and ts * per_batch == s and ts % (2 * SUBLANES) == 0
        rows = lambda j: (j // per_batch, j % per_batch, 0)
        in_specs += [pl.BlockSpec((1, ts, 1), rows), pl.BlockSpec((1, HEAD_DIM), lambda j: (0, 0))]
        tab = jax.ShapeDtypeStruct((nb, s, HEAD_DIM), jnp.float32)
        out_shape += [tab, tab]
        out_specs += [pl.BlockSpec((1, ts, HEAD_DIM), rows)] * 2
        args += [positions.reshape(nb, s, 1), _rope_inv_freq()]
    outs = pl.pallas_call(
        functools.partial(_ada_kernel, nb=nb, with_rope=positions is not None),
        out_shape=out_shape,
        grid=(n_steps,),
        in_specs=in_specs,
        out_specs=out_specs,
        compiler_params=_cparams(("arbitrary",)),
        name="ada",
    )(*args)
    return outs if positions is not None else outs[0]


def _rope_table_kernel(pos_ref, inv_ref, cos_ref, sin_ref):
    half = pos_ref.shape[1] // 2
    lane = lax.broadcasted_iota(jnp.int32, (half, HEAD_DIM), 1)
    low = lane < HEAD_DIM // 2
    pos = jnp.where(low, pos_ref[0, 0:half, :], pos_ref[0, half:2 * half, :])
    ang = pos.astype(jnp.float32) * inv_ref[...]
    cos = jnp.cos(ang)
    sin = jnp.sin(ang)
    cos_sw = pltpu.roll(cos, HEAD_DIM // 2, axis=1)
    sin_sw = pltpu.roll(sin, HEAD_DIM // 2, axis=1)
    cos_ref[0, 0:half, :] = jnp.where(low, cos, cos_sw)
    cos_ref[0, half:2 * half, :] = jnp.where(low, cos_sw, cos)
    sin_ref[0, 0:half, :] = jnp.where(low, -sin, sin_sw)
    sin_ref[0, half:2 * half, :] = jnp.where(low, -sin_sw, sin)


def _rope_inv_freq():
    inv = ROPE_THETA ** (-jnp.arange(0, HEAD_DIM, 2, dtype=jnp.float32) / HEAD_DIM)
    return jnp.concatenate([inv, inv]).reshape(1, HEAD_DIM)


def _rope_tables(pos, ts):
    nb, n = pos.shape
    inv2 = _rope_inv_freq()
    shp = jax.ShapeDtypeStruct((nb, n, HEAD_DIM), jnp.float32)
    spec = pl.BlockSpec((1, ts, HEAD_DIM), lambda b, i: (b, i, 0))
    return pl.pallas_call(
        _rope_table_kernel,
        out_shape=(shp, shp),
        grid=(nb, n // ts),
        in_specs=[pl.BlockSpec((1, ts, 1), lambda b, i: (b, i, 0)),
                  pl.BlockSpec((1, HEAD_DIM), lambda b, i: (0, 0))],
        out_specs=(spec, spec),
        compiler_params=_cparams(("arbitrary", "arbitrary")),
        name="rope_tables",
    )(pos.reshape(nb, n, 1), inv2)


def _apply_rope(x, cos, sin):
    return x * cos + pltpu.roll(x, HEAD_DIM // 2, axis=1) * sin


IN_TM = 1024
IN_TN = 512


def _inproj_kernel(x_ref, g_ref, sc_ref, sh_ref, w_ref, ws_ref, cos_ref, sin_ref,
                   z_ref, zs_ref, zc_ref, u_ref):
    j = pl.program_id(2)

    @pl.when(j == 0)
    def _():
        u = _rms(x_ref[0]) * g_ref[...]
        u = u * (1.0 + sc_ref[0]) + sh_ref[0]
        ub = u.astype(jnp.bfloat16)
        u_ref[...] = ub
        zs_ref[0] = _dot_nt(ub, ws_ref[...])

    fox_q = j < FOX_W // IN_TN
    nsa_q = (j >= COL_QN // IN_TN) & (j < COL_KS // IN_TN)
    nsa_k = j == COL_KS // IN_TN
    cmp_kv = j == COL_KC // IN_TN

    def matmul():
        return _dot_nt(u_ref[...], w_ref[...])

    def roped(acc):
        cos = cos_ref[0]
        sin = sin_ref[0]
        parts = [_apply_rope(acc[:, h * HEAD_DIM:(h + 1) * HEAD_DIM], cos, sin)
                 for h in range(IN_TN // HEAD_DIM)]
        return jnp.concatenate(parts, axis=1).astype(z_ref.dtype)

    @pl.when(fox_q)
    def _():
        z_ref[0] = (matmul() * (SCALE * LOG2E)).astype(z_ref.dtype)

    @pl.when(nsa_q)
    def _():
        z_ref[0] = roped(matmul() * (SCALE * LOG2E))

    @pl.when(nsa_k)
    def _():
        z_ref[0] = roped(matmul())

    @pl.when(cmp_kv)
    def _():
        acc = matmul()
        z_ref[0] = acc.astype(z_ref.dtype)
        zc_ref[0] = acc

    @pl.when(jnp.logical_not(fox_q | nsa_q | nsa_k | cmp_kv))
    def _():
        z_ref[0] = matmul().astype(z_ref.dtype)


def _inproj(x, norm_w, sc, sh, w_main, w_small, cos, sin):
    nb, s, d = x.shape
    grid = (nb, s // IN_TM, Z_MAIN // IN_TN)
    return pl.pallas_call(
        _inproj_kernel,
        out_shape=(jax.ShapeDtypeStruct((nb, s, Z_MAIN), jnp.bfloat16),
                   jax.ShapeDtypeStruct((nb, s, Z_SMALL), jnp.float32),
                   jax.ShapeDtypeStruct((nb, s, IN_TN), jnp.float32)),
        grid=grid,
        in_specs=[
            pl.BlockSpec((1, IN_TM, d), lambda b, i, j: (b, i, 0)),
            pl.BlockSpec((1, d), lambda b, i, j: (0, 0)),
            pl.BlockSpec((1, 1, d), lambda b, i, j: (b, 0, 0)),
            pl.BlockSpec((1, 1, d), lambda b, i, j: (b, 0, 0)),
            pl.BlockSpec((IN_TN, d), lambda b, i, j: (j, 0)),
            pl.BlockSpec((Z_SMALL, d), lambda b, i, j: (0, 0)),
            pl.BlockSpec((1, IN_TM, HEAD_DIM), lambda b, i, j: (b, i, 0)),
            pl.BlockSpec((1, IN_TM, HEAD_DIM), lambda b, i, j: (b, i, 0)),
        ],
        out_specs=(pl.BlockSpec((1, IN_TM, IN_TN), lambda b, i, j: (b, i, j)),
                   pl.BlockSpec((1, IN_TM, Z_SMALL), lambda b, i, j: (b, i, 0)),
                   pl.BlockSpec((1, IN_TM, IN_TN), lambda b, i, j: (b, i, 0))),
        scratch_shapes=[pltpu.VMEM((IN_TM, d), jnp.bfloat16)],
        compiler_params=_cparams(("arbitrary", "arbitrary", "arbitrary")),
        name="inproj",
    )(x, norm_w.reshape(1, d), sc.reshape(nb, 1, d), sh.reshape(nb, 1, d),
      w_main, w_small, cos, sin)


DEC_T = 2048


def _decay_kernel(zs_ref, bf_ref, cum_ref, carry_ref):
    @pl.when(pl.program_id(1) == 0)
    def _():
        carry_ref[...] = jnp.zeros_like(carry_ref)

    xt = zs_ref[0].T[0:FOX_HEADS, :] + bf_ref[...]
    lf = jnp.minimum(xt, 0.0) - jnp.log1p(jnp.exp(-jnp.abs(xt)))
    lane = lax.broadcasted_iota(jnp.int32, lf.shape, 1)
    sh = 1
    while sh < DEC_T:
        lf = lf + jnp.where(lane >= sh, pltpu.roll(lf, sh, axis=1), 0.0)
        sh *= 2
    cum = lf + carry_ref[:, 0:1]
    cum_ref[0] = cum
    carry_ref[...] = jnp.broadcast_to(cum[:, DEC_T - 1:DEC_T], carry_ref.shape)


def _decay(z_small, b_fgate):
    nb, s, _ = z_small.shape
    return pl.pallas_call(
        _decay_kernel,
        out_shape=jax.ShapeDtypeStruct((nb, FOX_HEADS, s), jnp.float32),
        grid=(nb, s // DEC_T),
        in_specs=[pl.BlockSpec((1, DEC_T, Z_SMALL), lambda b, i: (b, i, 0)),
                  pl.BlockSpec((FOX_HEADS, 1), lambda b, i: (0, 0))],
        out_specs=pl.BlockSpec((1, FOX_HEADS, DEC_T), lambda b, i: (b, 0, i)),
        scratch_shapes=[pltpu.VMEM((FOX_HEADS, 128), jnp.float32)],
        compiler_params=_cparams(("arbitrary", "arbitrary")),
        name="decay",
    )(z_small, b_fgate.reshape(FOX_HEADS, 1))


def _compress_kernel(x_ref, w1_ref, w2_ref, pos_ref, cos_ref, sin_ref, o_ref):
    kind = pl.program_id(1)
    nc = x_ref.shape[1] // CMP_STRIDE
    p = jnp.zeros((nc, CMP_HIDDEN), jnp.float32)
    q = jnp.zeros((nc, CMP_HIDDEN), jnp.float32)
    for l in range(0, CMP_STRIDE, 2):
        xl = jnp.concatenate(
            [x_ref[0, pl.ds(l + k, nc, stride=CMP_STRIDE), :].astype(jnp.bfloat16)
             for k in range(2)], axis=1)
        pair = lambda l0: w1_ref[0, l0:l0 + 2].reshape(2 * HEAD_DIM, CMP_HIDDEN)
        p = p + _dot(xl, pair(l))
        q = q + _dot(xl, pair(CMP_STRIDE + l))
    posb = jnp.broadcast_to(pos_ref[...], (8, CMP_LEN * HEAD_DIM)).astype(jnp.bfloat16)
    w1_flat = w1_ref[0].reshape(CMP_LEN * HEAD_DIM, CMP_HIDDEN)
    pterm = _dot(posb, w1_flat)[0:1, :]
    h = p + pltpu.roll(q, nc - 1, axis=0) + pterm
    a = jax.nn.gelu(h).astype(jnp.bfloat16)
    out = _dot(a, w2_ref[0])

    @pl.when(kind < NSA_KV_HEADS)
    def _():
        o_ref[0, 0] = _apply_rope(out, cos_ref[0], sin_ref[0]).astype(o_ref.dtype)

    @pl.when(kind >= NSA_KV_HEADS)
    def _():
        o_ref[0, 0] = out.astype(o_ref.dtype)


def _compress(zc, w1, w2, cmp_pos, cos_c, sin_c):
    nb, s, _ = zc.shape
    nk = 2 * NSA_KV_HEADS
    nc = s // CMP_STRIDE
    return pl.pallas_call(
        _compress_kernel,
        out_shape=jax.ShapeDtypeStruct((nb, nk, nc, HEAD_DIM), jnp.bfloat16),
        grid=(nb, nk),
        in_specs=[
            pl.BlockSpec((1, s, HEAD_DIM), lambda b, k: (b, 0, k)),
            pl.BlockSpec((1, CMP_LEN, HEAD_DIM, CMP_HIDDEN),
                         lambda b, k: (k // NSA_KV_HEADS, 0, 0, 0)),
            pl.BlockSpec((1, CMP_HIDDEN, HEAD_DIM), lambda b, k: (k // NSA_KV_HEADS, 0, 0)),
            pl.BlockSpec((1, CMP_LEN * HEAD_DIM), lambda b, k: (0, 0)),
            pl.BlockSpec((1, nc, HEAD_DIM), lambda b, k: (b, 0, 0)),
            pl.BlockSpec((1, nc, HEAD_DIM), lambda b, k: (b, 0, 0)),
        ],
        out_specs=pl.BlockSpec((1, 1, nc, HEAD_DIM), lambda b, k: (b, k, 0, 0)),
        compiler_params=_cparams(("arbitrary", "arbitrary")),
        name="compress",
    )(zc, w1, w2, cmp_pos.reshape(1, CMP_LEN * HEAD_DIM), cos_c, sin_c)


FOX_T = 512


def _softmax_step(s, m, acc, v):
    m_new = jnp.maximum(m, jnp.max(s, axis=1, keepdims=True))
    alpha = jnp.exp2(m - m_new)
    p = jnp.exp2((s - m_new).astype(jnp.bfloat16))
    va = jnp.concatenate([v, jnp.ones_like(v)], axis=1)
    return m_new, alpha * acc + _dot(p, va)


def _softmax_init(rows):
    return (jnp.full((rows, 1), MASK_VALUE, jnp.float32),
            jnp.zeros((rows, 2 * HEAD_DIM), jnp.float32))


def _softmax_result(acc):
    return acc[:, 0:HEAD_DIM] * (1.0 / acc[:, HEAD_DIM:2 * HEAD_DIM])


FOX_HB = 4


def _fox_kernel(q_ref, k_ref, v_ref, cum_ref, *rest, n_cast):
    o_ref = rest[n_cast]
    for w_ref, wb_ref in zip(rest[:n_cast], rest[n_cast + 1:]):
        wb_ref[...] = w_ref[...].astype(wb_ref.dtype)
    hb = pl.program_id(1)
    i = pl.program_id(2)
    t = FOX_T
    cols = [slice(n * HEAD_DIM, (n + 1) * HEAD_DIM) for n in range(FOX_HB)]
    qs = [q_ref[0, :, c] for c in cols]

    def tile(k0, width, carry, diag):
        k0 = pl.multiple_of(k0, t)
        logits = [_dot_nt(qs[n], k_ref[0, pl.ds(k0, width), cols[n]]) for n in range(FOX_HB)]
        out = []
        for n, (m, acc) in enumerate(carry):
            v = v_ref[0, pl.ds(k0, width), cols[n]]
            ck = cum_ref[0, pl.ds(hb * FOX_HB + n, 1), pl.ds(k0, width)] * LOG2E
            s = logits[n] - ck
            if diag:
                r = lax.broadcasted_iota(jnp.int32, s.shape, 0)
                c = lax.broadcasted_iota(jnp.int32, s.shape, 1)
                s = jnp.where(c <= r + (width - t), s, MASK_VALUE)
            out.append(_softmax_step(s, m, acc, v))
        return tuple(out)

    init = tuple(_softmax_init(t) for _ in range(FOX_HB))
    n_pairs = i >> 1
    carry = lax.fori_loop(0, n_pairs, lambda p, c: tile(p * (2 * t), 2 * t, c, False), init)
    k_tail = n_pairs * (2 * t)
    carry = lax.cond((i & 1) == 0,
                     lambda c: tile(k_tail, t, c, True),
                     lambda c: tile(k_tail, 2 * t, c, True), carry)
    for n, (_, acc) in enumerate(carry):
        o_ref[0, :, cols[n]] = _softmax_result(acc).astype(o_ref.dtype)


def _fox(z_main, cum, weights):
    nb, s, _ = z_main.shape
    w = FOX_HB * HEAD_DIM
    grid = (nb, FOX_HEADS // FOX_HB, s // FOX_T)
    n_steps = grid[0] * grid[1] * grid[2]
    step = lambda b, h, i: ((b * grid[1] + h) * grid[2] + i, 0)
    slabs = []
    for wt in weights:
        rows = wt.shape[0] // n_steps
        assert rows * n_steps == wt.shape[0] and rows % BF16_SUBLANES == 0, wt.shape
        slabs.append(pl.BlockSpec((rows, wt.shape[1]), step))
    outs = pl.pallas_call(
        functools.partial(_fox_kernel, n_cast=len(weights)),
        out_shape=[jax.ShapeDtypeStruct((nb, s, FOX_W), jnp.bfloat16)]
        + [jax.ShapeDtypeStruct(wt.shape, jnp.bfloat16) for wt in weights],
        grid=grid,
        in_specs=[
            pl.BlockSpec((1, FOX_T, w), lambda b, h, i: (b, i, h)),
            pl.BlockSpec((1, s, w), lambda b, h, i: (b, 0, FOX_W // w + h)),
            pl.BlockSpec((1, s, w), lambda b, h, i: (b, 0, 2 * FOX_W // w + h)),
            pl.BlockSpec((1, FOX_HEADS, s), lambda b, h, i: (b, 0, 0)),
        ] + slabs,
        out_specs=[pl.BlockSpec((1, FOX_T, w), lambda b, h, i: (b, i, h))] + slabs,
        compiler_params=_cparams(("arbitrary", "arbitrary", "arbitrary")),
        name="fox",
    )(z_main, z_main, z_main, cum, *weights)
    return outs[0], outs[1:]


NSA_TQ = 512
NSA_PARTS = 4
NSEL_PAD = 128
SLC_SHIFT = SLC_LEN.bit_length() - 1
assert 1 << SLC_SHIFT == SLC_LEN
assert WINDOW % NSA_TQ == 0 and NSA_TQ % SLC_LEN == 0
SUBLANES = 8
BF16_SUBLANES = 16


def _block_ranks(score, n_slc):
    tq = score.shape[1]
    groups = [score[SUBLANES * v:SUBLANES * (v + 1)] for v in range(n_slc // SUBLANES)]
    ranks = [jnp.zeros((SUBLANES, tq), jnp.float32) for _ in groups]
    for m in range(n_slc):
        row = jnp.broadcast_to(score[m:m + 1, :], (SUBLANES, tq))
        for v, grp in enumerate(groups):
            lo = SUBLANES * v
            if lo > m:
                before = row >= grp
            elif lo + SUBLANES - 1 <= m:
                before = row > grp
            else:
                n_i = lo + lax.broadcasted_iota(jnp.int32, grp.shape, 0)
                before = (row > grp) | ((row == grp) & (n_i > m))
            ranks[v] = ranks[v] + jnp.where(before, 1.0, 0.0)
    return jnp.concatenate(ranks, axis=0)


def _nsa_kernel(q_ref, zs_ref, kc_ref, vc_ref, ks_ref, vs_ref, kw_ref, vw_ref, ovt_ref,
                o_ref, ka_ref, *, seq):
    g = pl.program_id(1)
    i = pl.program_id(2)
    tq = NSA_TQ
    q0 = i * tq

    @pl.when(i == 0)
    def _():
        ka_ref[:, 0:HEAD_DIM] = ks_ref[0]
        tok = lax.broadcasted_iota(jnp.int32, (seq, NSEL_PAD), 0)
        n_i = lax.broadcasted_iota(jnp.int32, (seq, NSEL_PAD), 1)
        ka_ref[:, HEAD_DIM:HEAD_DIM + NSEL_PAD] = jnp.where(
            n_i == (tok >> SLC_SHIFT), 1.0, 0.0).astype(jnp.bfloat16)

    qb = q_ref[0]
    hpp = NSA_GROUP // NSA_PARTS
    parts = range(NSA_PARTS)
    qh = [jnp.concatenate(
        [qb[:, r * HEAD_DIM:(r + 1) * HEAD_DIM] for r in range(h * hpp, (h + 1) * hpp)], axis=0)
        for h in parts]
    prow = hpp * tq

    def row_query(shape):
        return lax.broadcasted_iota(jnp.int32, shape, 0) & (tq - 1)

    kc = kc_ref[0, 0]
    vc = vc_ref[0, 0]
    s_cs = [_dot_nt(qh[h], kc) for h in parts]
    p_cs = []
    for s_c in s_cs:
        cend = lax.broadcasted_iota(jnp.int32, s_c.shape, 1) * CMP_STRIDE + (CMP_LEN - 1)
        s_c = jnp.where(cend <= q0 + row_query(s_c.shape), s_c, MASK_VALUE)
        m_c = jnp.maximum(jnp.max(s_c, axis=1, keepdims=True), 0.1 * MASK_VALUE)
        e_c = jnp.exp2(s_c - m_c)
        p_cs.append(e_c * (1.0 / jnp.maximum(jnp.sum(e_c, axis=1, keepdims=True), 1e-30)))
    o_c = [_dot(p_c.astype(jnp.bfloat16), vc) for p_c in p_cs]

    psum = None
    for p_c in p_cs:
        for r in range(hpp):
            term = p_c[r * tq:(r + 1) * tq]
            psum = term if psum is None else psum + term
    p_hi = psum.astype(jnp.bfloat16)
    p_lo = (psum - p_hi.astype(jnp.float32)).astype(jnp.bfloat16)
    ovt = ovt_ref[...]
    imp_t = _dot_nt(ovt, p_hi) + _dot_nt(ovt, p_lo)
    n_slc = seq // SLC_LEN
    blk = lax.broadcasted_iota(jnp.int32, imp_t.shape, 0)
    cur = (q0 + lax.broadcasted_iota(jnp.int32, imp_t.shape, 1)) >> SLC_SHIFT
    forced = (blk == 0) | (blk == cur) | (blk == cur - 1)
    score = jnp.where(forced, SEL_BONUS, imp_t)
    score = jnp.where(blk <= cur, score, -SEL_BONUS)
    rank = _block_ranks(score, n_slc)
    bias_t = jnp.where(rank < min(SLC_TOPK, n_slc), 0.0, MASK_VALUE)
    bias_t = jnp.concatenate(
        [bias_t, jnp.zeros((NSEL_PAD - n_slc, tq), jnp.float32)], axis=0)
    bias = bias_t.T.astype(jnp.bfloat16)
    bias_p = jnp.concatenate([bias] * hpp, axis=0)
    qa = [jnp.concatenate([qh[h], bias_p], axis=1) for h in parts]

    def sel_tile(k0, width, carry, diag):
        k0 = pl.multiple_of(k0, tq)
        logits = [_dot_nt(qa[h], ka_ref[pl.ds(k0, width), :]) for h in parts]
        out = []
        for s, (m, acc) in zip(logits, carry):
            if diag:
                c = lax.broadcasted_iota(jnp.int32, s.shape, 1)
                s = jnp.where(c <= row_query(s.shape) + (width - tq), s, MASK_VALUE)
            out.append(_softmax_step(s, m, acc, vs_ref[0, pl.ds(k0, width), :]))
        return tuple(out)

    init = tuple(_softmax_init(prow) for _ in parts)
    n_pairs = i >> 1
    carry = lax.fori_loop(0, n_pairs,
                          lambda p, c: sel_tile(p * (2 * tq), 2 * tq, c, False), init)
    k_tail = n_pairs * (2 * tq)
    carry = lax.cond((i & 1) == 0,
                     lambda c: sel_tile(k_tail, tq, c, True),
                     lambda c: sel_tile(k_tail, 2 * tq, c, True), carry)
    o_s = [_softmax_result(acc) for _, acc in carry]

    n_w = WINDOW // tq + 1
    starts = [q0 - WINDOW + t * tq for t in range(n_w)]
    offs = [pl.multiple_of(jnp.maximum(st, 0), tq) for st in starts]
    s_w = [[_dot_nt(qh[h], kw_ref[0, pl.ds(offs[t], tq), :]) for h in parts] for t in range(n_w)]
    for t in range(n_w):
        for h in parts:
            s = s_w[t][h]
            c = lax.broadcasted_iota(jnp.int32, s.shape, 1)
            if t == 0:
                s = jnp.where(c > row_query(s.shape), s, MASK_VALUE)
            if t == n_w - 1:
                s = jnp.where(c <= row_query(s.shape), s, MASK_VALUE)
            else:
                s = jnp.where(starts[t] >= 0, s, MASK_VALUE)
            s_w[t][h] = s
    o_w = []
    for h in parts:
        m_w = s_w[0][h].max(axis=1, keepdims=True)
        for t in range(1, n_w):
            m_w = jnp.maximum(m_w, s_w[t][h].max(axis=1, keepdims=True))
        acc_w = jnp.zeros((prow, 2 * HEAD_DIM), jnp.float32)
        for t in range(n_w):
            p = jnp.exp2((s_w[t][h] - m_w).astype(jnp.bfloat16))
            v = vw_ref[0, pl.ds(offs[t], tq), :]
            acc_w = acc_w + _dot(p, jnp.concatenate([v, jnp.ones_like(v)], axis=1))
        o_w.append(_softmax_result(acc_w))

    gates = jax.nn.sigmoid(zs_ref[0])
    lane = lax.broadcasted_iota(jnp.int32, gates.shape, 1)
    outs = []
    for r in range(NSA_GROUP):
        h, rows = r // hpp, slice((r % hpp) * tq, (r % hpp + 1) * tq)
        o_r = jnp.zeros((tq, HEAD_DIM), jnp.float32)
        for br, o_b in enumerate((o_c, o_s, o_w)):
            col = FOX_HEADS + 3 * (g * NSA_GROUP + r) + br
            gcol = jnp.sum(jnp.where(lane == col, gates, 0.0), axis=1, keepdims=True)
            o_r = o_r + gcol * o_b[h][rows]
        outs.append(o_r)
    o_ref[0] = jnp.concatenate(outs, axis=1).astype(o_ref.dtype)


def _nsa(z_main, z_small, kvc, ovt):
    nb, s, _ = z_main.shape
    nc = kvc.shape[2]
    gw = NSA_GROUP * HEAD_DIM
    full = lambda col0: pl.BlockSpec(
        (1, s, HEAD_DIM), lambda b, g, i, c=col0 // HEAD_DIM: (b, 0, c + g))
    return pl.pallas_call(
        functools.partial(_nsa_kernel, seq=s),
        out_shape=jax.ShapeDtypeStruct((nb, s, NSA_W), jnp.bfloat16),
        grid=(nb, NSA_KV_HEADS, s // NSA_TQ),
        in_specs=[
            pl.BlockSpec((1, NSA_TQ, gw), lambda b, g, i: (b, i, COL_QN // gw + g)),
            pl.BlockSpec((1, NSA_TQ, Z_SMALL), lambda b, g, i: (b, i, 0)),
            pl.BlockSpec((1, 1, nc, HEAD_DIM), lambda b, g, i: (b, g, 0, 0)),
            pl.BlockSpec((1, 1, nc, HEAD_DIM), lambda b, g, i: (b, NSA_KV_HEADS + g, 0, 0)),
            full(COL_KS), full(COL_VS), full(COL_KW), full(COL_VW),
            pl.BlockSpec(ovt.shape, lambda b, g, i: (0, 0)),
        ],
        out_specs=pl.BlockSpec((1, NSA_TQ, gw), lambda b, g, i: (b, i, g)),
        scratch_shapes=[pltpu.VMEM((s, HEAD_DIM + NSEL_PAD), jnp.bfloat16)],
        compiler_params=_cparams(("arbitrary", "arbitrary", "arbitrary")),
        name="nsa",
    )(z_main, z_small, kvc, kvc, z_main, z_main, z_main, z_main, ovt)


OUT_TM = 512
OUT_RC = 256


def _outproj_kernel(of_ref, on_ref, bf_ref, bn_ref, w_ref, x_ref, g_ref, nw_ref, sc_ref,
                    sh_ref, h_ref, u_ref):
    for r in range(OUT_TM // OUT_RC):
        rows = slice(r * OUT_RC, (r + 1) * OUT_RC)
        yf = _rms(of_ref[0, rows, :].astype(jnp.float32)) * bf_ref[...]
        yn = _rms(on_ref[0, rows, :].astype(jnp.float32)) * bn_ref[...]
        y = jnp.concatenate([yf.astype(jnp.bfloat16), yn.astype(jnp.bfloat16)], axis=1)
        h = x_ref[0, rows, :] + g_ref[0] * _dot(y, w_ref[...])
        h_ref[0, rows, :] = h
        u = _rms(h) * nw_ref[...]
        u_ref[0, rows, :] = (u * (1.0 + sc_ref[0]) + sh_ref[0]).astype(jnp.bfloat16)


def _outproj(o_f, o_n, beta_f, beta_n, w_out, x, gate, norm_w, sc, sh):
    nb, s, d = x.shape
    vec = lambda n: pl.BlockSpec((1, n), lambda b, i: (0, 0))
    bvec = pl.BlockSpec((1, 1, d), lambda b, i: (b, 0, 0))
    rows = lambda n: pl.BlockSpec((1, OUT_TM, n), lambda b, i: (b, i, 0))
    return pl.pallas_call(
        _outproj_kernel,
        out_shape=(jax.ShapeDtypeStruct((nb, s, d), jnp.float32),
                   jax.ShapeDtypeStruct((nb, s, d), jnp.bfloat16)),
        grid=(nb, s // OUT_TM),
        in_specs=[
            rows(FOX_W), rows(NSA_W), vec(FOX_W), vec(NSA_W),
            pl.BlockSpec((FOX_W + NSA_W, d), lambda b, i: (0, 0),
                         pipeline_mode=pl.Buffered(1)),
            rows(d), bvec, vec(d), bvec, bvec,
        ],
        out_specs=(rows(d), rows(d)),
        compiler_params=_cparams(("arbitrary", "arbitrary")),
        name="outproj",
    )(o_f, o_n, beta_f.reshape(1, FOX_W), beta_n.reshape(1, NSA_W), w_out, x,
      gate.reshape(nb, 1, d), norm_w.reshape(1, d), sc.reshape(nb, 1, d),
      sh.reshape(nb, 1, d))


FFN_TM = 1024
FFN_TF = 512
FFN_RC = 512
FFN_RC_LAST = 256


def _ffn_kernel(u_ref, h_ref, g_ref, fw_ref, wg_ref, wu_ref, wd_ref, o_ref, *, final):
    j = pl.program_id(2)
    last = pl.num_programs(2) - 1

    def partial_sum(rows):
        u = u_ref[0, rows, :]
        a = _dot(u, wg_ref[...])
        b = _dot(u, wu_ref[...])
        mid = (a * jax.nn.sigmoid(a) * b).astype(jnp.bfloat16)
        return _dot(mid, wd_ref[...])

    def chunks(rc):
        return [slice(r * rc, (r + 1) * rc) for r in range(FFN_TM // rc)]

    @pl.when(j == 0)
    def _():
        for rows in chunks(FFN_RC):
            o_ref[0, rows, :] = partial_sum(rows)

    @pl.when((j > 0) & (j < last))
    def _():
        for rows in chunks(FFN_RC):
            o_ref[0, rows, :] += partial_sum(rows)

    @pl.when(j == last)
    def _():
        for rows in chunks(FFN_RC_LAST):
            h2 = h_ref[0, rows, :] + g_ref[0] * (o_ref[0, rows, :] + partial_sum(rows))
            o_ref[0, rows, :] = _rms(h2) * fw_ref[...] if final else h2


def _ffn(u2, h1, gate, final_w, w_gate, w_up, w_down, final):
    nb, s, d = h1.shape
    f = w_gate.shape[1]
    return pl.pallas_call(
        functools.partial(_ffn_kernel, final=final),
        out_shape=jax.ShapeDtypeStruct((nb, s, d), jnp.float32),
        grid=(nb, s // FFN_TM, f // FFN_TF),
        in_specs=[
            pl.BlockSpec((1, FFN_TM, d), lambda b, i, j: (b, i, 0)),
            pl.BlockSpec((1, FFN_TM, d), lambda b, i, j: (b, i, 0),
                         pipeline_mode=pl.Buffered(1)),
            pl.BlockSpec((1, 1, d), lambda b, i, j: (b, 0, 0)),
            pl.BlockSpec((1, d), lambda b, i, j: (0, 0)),
            pl.BlockSpec((d, FFN_TF), lambda b, i, j: (0, j)),
            pl.BlockSpec((d, FFN_TF), lambda b, i, j: (0, j)),
            pl.BlockSpec((FFN_TF, d), lambda b, i, j: (j, 0)),
        ],
        out_specs=pl.BlockSpec((1, FFN_TM, d), lambda b, i, j: (b, i, 0)),
        compiler_params=pltpu.CompilerParams(
            dimension_semantics=("arbitrary", "arbitrary", "arbitrary"),
            vmem_limit_bytes=FFN_VMEM_LIMIT),
        name="ffn",
    )(u2, h1, gate.reshape(nb, 1, d), final_w.reshape(1, d), w_gate, w_up, w_down)


def _split_w_in(w_in):
    sizes = ([FOX_W] * 3 + [FOX_HEADS] + [NSA_W] + [KV_W] * 6 + [3 * NSA_HEADS])
    offs = [0]
    for sz in sizes:
        offs.append(offs[-1] + sz)
    wt = w_in.T
    order = (0, 1, 2, 4, 7, 9, 8, 10, 5, 6)
    units = lambda ns, unit: [offs[n] + r for n in ns for r in range(0, sizes[n], unit)]
    main = _gather_rows(wt, units(order, KV_W), KV_W, jnp.bfloat16)
    narrow = _gather_rows(wt, units((3, 11), SUBLANES), SUBLANES, jnp.float32)
    small = jnp.concatenate(
        [narrow, jnp.zeros((Z_SMALL - narrow.shape[0], w_in.shape[0]), jnp.float32)], axis=0)
    return main, small.astype(jnp.bfloat16)


def _cast_kernel(rows_ref, w_ref, o_ref):
    del rows_ref
    o_ref[...] = w_ref[...].astype(o_ref.dtype)


def _gather_rows(wt, src_rows, unit, dtype):
    assert all(r % SUBLANES == 0 for r in src_rows), src_rows
    tiles = jnp.asarray([r // SUBLANES for r in src_rows], jnp.int32)
    n_units = len(src_rows)
    d = wt.shape[1]
    return pl.pallas_call(
        _cast_kernel,
        out_shape=jax.ShapeDtypeStruct((n_units * unit, d), dtype),
        grid_spec=pltpu.PrefetchScalarGridSpec(
            num_scalar_prefetch=1,
            grid=(n_units,),
            in_specs=[pl.BlockSpec((pl.Element(unit), pl.Element(d)),
                                   lambda u, rows: (rows[u] * SUBLANES, 0))],
            out_specs=pl.BlockSpec((unit, d), lambda u, rows: (u, 0)),
        ),
        compiler_params=_cparams(("arbitrary",)),
        name="w_in_prep",
    )(tiles, wt)


def _overlap_t(n_cmp_pad, n_slc):
    cs = jnp.arange(n_cmp_pad) * CMP_STRIDE
    ss = jnp.arange(n_slc) * SLC_LEN
    ov = (jnp.minimum(cs[None, :] + CMP_LEN, ss[:, None] + SLC_LEN)
          - jnp.maximum(cs[None, :], ss[:, None]))
    ov = jnp.clip(ov, 0).astype(jnp.float32) / CMP_STRIDE
    return ov.astype(jnp.bfloat16)


def kernel(x, c, positions, w_ada, b_ada, norm_attn, norm_ffn, w_in, b_fgate, cmp_pos,
           w_kc1, w_kc2, w_vc1, w_vc2, beta_fox, beta_nsa, w_out, w_gate, w_up, w_down,
           final_norm):
    nb, s, d = x.shape
    depth = w_ada.shape[0]
    n_cmp_pad = s // CMP_STRIDE
    cmp_end = jnp.arange(n_cmp_pad) * CMP_STRIDE + (CMP_LEN - 1)
    pos_c = jnp.take(positions, jnp.minimum(cmp_end, s - 1), axis=1)
    cos_c, sin_c = _rope_tables(pos_c, n_cmp_pad)
    ovt = _overlap_t(n_cmp_pad, s // SLC_LEN)

    h = x
    for l in range(depth):
        if l == 0:
            mod, cos, sin = _ada(c, w_ada[l], b_ada[l], positions)
        else:
            mod = _ada(c, w_ada[l], b_ada[l])
        sh1, sc1, g1, sh2, sc2, g2 = jnp.split(mod, 6, axis=-1)
        w_main, w_small = _split_w_in(w_in[l])
        z_main, z_small, zc = _inproj(h, norm_attn[l], sc1, sh1, w_main, w_small, cos, sin)
        cum = _decay(z_small, b_fgate[l])

        w1 = jnp.stack([w_kc1[l], w_vc1[l]]).astype(jnp.bfloat16)
        w1 = w1.reshape(2, CMP_LEN, HEAD_DIM, CMP_HIDDEN)
        w2 = jnp.stack([w_kc2[l], w_vc2[l]]).astype(jnp.bfloat16)
        kvc = _compress(zc, w1, w2, cmp_pos[l], cos_c, sin_c)

        o_f, (wo, wg, wu, wd) = _fox(z_main, cum, (w_out[l], w_gate[l], w_up[l], w_down[l]))
        o_n = _nsa(z_main, z_small, kvc, ovt)
        h1, u2 = _outproj(o_f, o_n, beta_fox[l], beta_nsa[l], wo, h, g1, norm_ffn[l], sc2, sh2)
        h = _ffn(u2, h1, g2, final_norm, wg, wu, wd, final=(l == depth - 1))
    return h
```

```python
import functools

import jax
import jax.numpy as jnp
from jax import lax
from jax.experimental import pallas as pl
from jax.experimental.pallas import tpu as pltpu

D_MODEL = 2048
HEAD_DIM = 128
FOX_HEADS = 8
NSA_HEADS = 8
NSA_KV_HEADS = 2
NSA_GROUP = NSA_HEADS // NSA_KV_HEADS
CMP_LEN = 32
CMP_STRIDE = 16
CMP_HIDDEN = 2 * HEAD_DIM
SLC_LEN = 64
SLC_TOPK = 16
WINDOW = 512
ROPE_THETA = 10000.0
NORM_EPS = 1e-6
MASK_VALUE = -1e30
SEL_BONUS = 1e6

FOX_W = FOX_HEADS * HEAD_DIM
NSA_W = NSA_HEADS * HEAD_DIM
KV_W = NSA_KV_HEADS * HEAD_DIM
Z_MAIN = 3 * FOX_W + NSA_W + 6 * KV_W
Z_SMALL = 128
COL_QN = 3 * FOX_W
COL_KS = COL_QN + NSA_W
COL_KW = COL_KS + KV_W
COL_VS = COL_KW + KV_W
COL_VW = COL_VS + KV_W
COL_KC = COL_VW + KV_W

VMEM_LIMIT = 48 * 1024 * 1024
FFN_VMEM_LIMIT = 56 * 1024 * 1024
SCALE = HEAD_DIM ** -0.5
LOG2E = 1.4426950408889634


def _cparams(sem):
    return pltpu.CompilerParams(dimension_semantics=sem, vmem_limit_bytes=VMEM_LIMIT)


def _dot(a, b):
    return jnp.dot(a, b, preferred_element_type=jnp.float32)


def _dot_nt(a, b):
    return lax.dot_general(a, b, (((1,), (1,)), ((), ())), preferred_element_type=jnp.float32)


def _rms(x):
    return x * lax.rsqrt(jnp.mean(x * x, axis=-1, keepdims=True) + NORM_EPS)


ADA_TN = 768
ADA_KC = 256


def _ada_kernel(ct_ref, w_ref, b_ref, *rest, nb, with_rope):
    if with_rope:
        pos_ref, inv_ref, o_ref, cos_ref, sin_ref = rest
        _rope_table_kernel(pos_ref, inv_ref, cos_ref, sin_ref)
    else:
        o_ref, = rest
    tn = w_ref.shape[1]
    ct = ct_ref[...]
    st = ct * jax.nn.sigmoid(ct)

    accs = [jnp.zeros((SUBLANES, tn), jnp.float32) for _ in range(nb)]
    for kc in range(D_MODEL // ADA_KC):
        w = w_ref[kc * ADA_KC:(kc + 1) * ADA_KC, :]
        for b in range(nb):
            s = st[kc * ADA_KC:(kc + 1) * ADA_KC, b:b + 1]
            accs[b] = accs[b] + jnp.sum(
                (w * s).reshape(ADA_KC // SUBLANES, SUBLANES, tn), axis=0)
    for b in range(nb):
        o_ref[b:b + 1, :] = jnp.sum(accs[b], axis=0, keepdims=True) + b_ref[...]


def _ada(c, w_ada, b_ada, positions=None):
    nb = c.shape[0]
    n = w_ada.shape[1]
    n_steps = n // ADA_TN
    in_specs = [
        pl.BlockSpec((D_MODEL, nb), lambda j: (0, 0)),
        pl.BlockSpec((D_MODEL, ADA_TN), lambda j: (0, j)),
        pl.BlockSpec((1, ADA_TN), lambda j: (0, j)),
    ]
    out_shape = [jax.ShapeDtypeStruct((nb, n), jnp.float32)]
    out_specs = [pl.BlockSpec((nb, ADA_TN), lambda j: (0, j))]
    args = [c.T, w_ada, b_ada.reshape(1, n)]
    if positions is not None:
        s = positions.shape[1]
        per_batch = n_steps // nb
        ts = s // per_batch
        assert per_batch * nb == n_steps and ts * per_batch == s and ts % (2 * SUBLANES) == 0
        rows = lambda j: (j // per_batch, j % per_batch, 0)
        in_specs += [pl.BlockSpec((1, ts, 1), rows), pl.BlockSpec((1, HEAD_DIM), lambda j: (0, 0))]
        tab = jax.ShapeDtypeStruct((nb, s, HEAD_DIM), jnp.float32)
        out_shape += [tab, tab]
        out_specs += [pl.BlockSpec((1, ts, HEAD_DIM), rows)] * 2
        args += [positions.reshape(nb, s, 1), _rope_inv_freq()]
    outs = pl.pallas_call(
        functools.partial(_ada_kernel, nb=nb, with_rope=positions is not None),
        out_shape=out_shape,
        grid=(n_steps,),
        in_specs=in_specs,
        out_specs=out_specs,
        compiler_params=_cparams(("arbitrary",)),
        name="ada",
    )(*args)
    return outs if positions is not None else outs[0]


def _rope_table_kernel(pos_ref, inv_ref, cos_ref, sin_ref):
    half = pos_ref.shape[1] // 2
    lane = lax.broadcasted_iota(jnp.int32, (half, HEAD_DIM), 1)
    low = lane < HEAD_DIM // 2
    pos = jnp.where(low, pos_ref[0, 0:half, :], pos_ref[0, half:2 * half, :])
    ang = pos.astype(jnp.float32) * inv_ref[...]
    cos = jnp.cos(ang)
    sin = jnp.sin(ang)
    cos_sw = pltpu.roll(cos, HEAD_DIM // 2, axis=1)
    sin_sw = pltpu.roll(sin, HEAD_DIM // 2, axis=1)
    cos_ref[0, 0:half, :] = jnp.where(low, cos, cos_sw)
    cos_ref[0, half:2 * half, :] = jnp.where(low, cos_sw, cos)
    sin_ref[0, 0:half, :] = jnp.where(low, -sin, sin_sw)
    sin_ref[0, half:2 * half, :] = jnp.where(low, -sin_sw, sin)


def _rope_inv_freq():
    inv = ROPE_THETA ** (-jnp.arange(0, HEAD_DIM, 2, dtype=jnp.float32) / HEAD_DIM)
    return jnp.concatenate([inv, inv]).reshape(1, HEAD_DIM)


def _rope_tables(pos, ts):
    nb, n = pos.shape
    inv2 = _rope_inv_freq()
    shp = jax.ShapeDtypeStruct((nb, n, HEAD_DIM), jnp.float32)
    spec = pl.BlockSpec((1, ts, HEAD_DIM), lambda b, i: (b, i, 0))
    return pl.pallas_call(
        _rope_table_kernel,
        out_shape=(shp, shp),
        grid=(nb, n // ts),
        in_specs=[pl.BlockSpec((1, ts, 1), lambda b, i: (b, i, 0)),
                  pl.BlockSpec((1, HEAD_DIM), lambda b, i: (0, 0))],
        out_specs=(spec, spec),
        compiler_params=_cparams(("arbitrary", "arbitrary")),
        name="rope_tables",
    )(pos.reshape(nb, n, 1), inv2)


def _apply_rope(x, cos, sin):
    return x * cos + pltpu.roll(x, HEAD_DIM // 2, axis=1) * sin


IN_TM = 1024
IN_TN = 512
IN_RC = 256
assert FOX_W // IN_TN >= 1


def _inproj_kernel(x_ref, g_ref, sc_ref, sh_ref, w_ref, ws_ref, cos_ref, sin_ref,
                   z_ref, zs_ref, zc_ref, u_ref):
    j = pl.program_id(2)

    @pl.when(j == 0)
    def _():
        for r in range(IN_TM // IN_RC):
            rows = slice(r * IN_RC, (r + 1) * IN_RC)
            u = _rms(x_ref[0, rows, :]) * g_ref[...]
            ub = (u * (1.0 + sc_ref[0]) + sh_ref[0]).astype(jnp.bfloat16)
            u_ref[rows, :] = ub
            zs_ref[0, rows, :] = _dot_nt(ub, ws_ref[...])
            z_ref[0, rows, :] = (_dot_nt(ub, w_ref[...]) * (SCALE * LOG2E)).astype(z_ref.dtype)

    fox_q = (j > 0) & (j < FOX_W // IN_TN)
    nsa_q = (j >= COL_QN // IN_TN) & (j < COL_KS // IN_TN)
    nsa_k = j == COL_KS // IN_TN
    cmp_kv = j == COL_KC // IN_TN

    def matmul():
        return _dot_nt(u_ref[...], w_ref[...])

    def roped(acc):
        cos = cos_ref[0]
        sin = sin_ref[0]
        parts = [_apply_rope(acc[:, h * HEAD_DIM:(h + 1) * HEAD_DIM], cos, sin)
                 for h in range(IN_TN // HEAD_DIM)]
        return jnp.concatenate(parts, axis=1).astype(z_ref.dtype)

    @pl.when(fox_q)
    def _():
        z_ref[0] = (matmul() * (SCALE * LOG2E)).astype(z_ref.dtype)

    @pl.when(nsa_q)
    def _():
        z_ref[0] = roped(matmul() * (SCALE * LOG2E))

    @pl.when(nsa_k)
    def _():
        z_ref[0] = roped(matmul())

    @pl.when(cmp_kv)
    def _():
        acc = matmul()
        z_ref[0] = acc.astype(z_ref.dtype)
        zc_ref[0] = acc

    @pl.when(jnp.logical_not((j == 0) | fox_q | nsa_q | nsa_k | cmp_kv))
    def _():
        z_ref[0] = matmul().astype(z_ref.dtype)


def _inproj(x, norm_w, sc, sh, w_main, w_small, cos, sin):
    nb, s, d = x.shape
    grid = (nb, s // IN_TM, Z_MAIN // IN_TN)
    return pl.pallas_call(
        _inproj_kernel,
        out_shape=(jax.ShapeDtypeStruct((nb, s, Z_MAIN), jnp.bfloat16),
                   jax.ShapeDtypeStruct((nb, s, Z_SMALL), jnp.float32),
                   jax.ShapeDtypeStruct((nb, s, IN_TN), jnp.float32)),
        grid=grid,
        in_specs=[
            pl.BlockSpec((1, IN_TM, d), lambda b, i, j: (b, i, 0)),
            pl.BlockSpec((1, d), lambda b, i, j: (0, 0)),
            pl.BlockSpec((1, 1, d), lambda b, i, j: (b, 0, 0)),
            pl.BlockSpec((1, 1, d), lambda b, i, j: (b, 0, 0)),
            pl.BlockSpec((IN_TN, d), lambda b, i, j: (j, 0)),
            pl.BlockSpec((Z_SMALL, d), lambda b, i, j: (0, 0)),
            pl.BlockSpec((1, IN_TM, HEAD_DIM), lambda b, i, j: (b, i, 0)),
            pl.BlockSpec((1, IN_TM, HEAD_DIM), lambda b, i, j: (b, i, 0)),
        ],
        out_specs=(pl.BlockSpec((1, IN_TM, IN_TN), lambda b, i, j: (b, i, j)),
                   pl.BlockSpec((1, IN_TM, Z_SMALL), lambda b, i, j: (b, i, 0)),
                   pl.BlockSpec((1, IN_TM, IN_TN), lambda b, i, j: (b, i, 0))),
        scratch_shapes=[pltpu.VMEM((IN_TM, d), jnp.bfloat16)],
        compiler_params=_cparams(("arbitrary", "arbitrary", "arbitrary")),
        name="inproj",
    )(x, norm_w.reshape(1, d), sc.reshape(nb, 1, d), sh.reshape(nb, 1, d),
      w_main, w_small, cos, sin)


DEC_T = 2048


def _decay_kernel(zs_ref, bf_ref, cum_ref, carry_ref):
    @pl.when(pl.program_id(1) == 0)
    def _():
        carry_ref[...] = jnp.zeros_like(carry_ref)

    xt = zs_ref[0].T[0:FOX_HEADS, :] + bf_ref[...]
    lf = jnp.minimum(xt, 0.0) - jnp.log1p(jnp.exp(-jnp.abs(xt)))
    lane = lax.broadcasted_iota(jnp.int32, lf.shape, 1)
    sh = 1
    while sh < DEC_T:
        lf = lf + jnp.where(lane >= sh, pltpu.roll(lf, sh, axis=1), 0.0)
        sh *= 2
    cum = lf + carry_ref[:, 0:1]
    cum_ref[0] = cum
    carry_ref[...] = jnp.broadcast_to(cum[:, DEC_T - 1:DEC_T], carry_ref.shape)


def _decay(z_small, b_fgate):
    nb, s, _ = z_small.shape
    return pl.pallas_call(
        _decay_kernel,
        out_shape=jax.ShapeDtypeStruct((nb, FOX_HEADS, s), jnp.float32),
        grid=(nb, s // DEC_T),
        in_specs=[pl.BlockSpec((1, DEC_T, Z_SMALL), lambda b, i: (b, i, 0)),
                  pl.BlockSpec((FOX_HEADS, 1), lambda b, i: (0, 0))],
        out_specs=pl.BlockSpec((1, FOX_HEADS, DEC_T), lambda b, i: (b, 0, i)),
        scratch_shapes=[pltpu.VMEM((FOX_HEADS, 128), jnp.float32)],
        compiler_params=_cparams(("arbitrary", "arbitrary")),
        name="decay",
    )(z_small, b_fgate.reshape(FOX_HEADS, 1))


def _compress_kernel(x_ref, w1_ref, w2_ref, pos_ref, cos_ref, sin_ref, o_ref):
    kind = pl.program_id(1)
    nc = x_ref.shape[1] // CMP_STRIDE
    p = jnp.zeros((nc, CMP_HIDDEN), jnp.float32)
    q = jnp.zeros((nc, CMP_HIDDEN), jnp.float32)
    for l in range(0, CMP_STRIDE, 2):
        xl = jnp.concatenate(
            [x_ref[0, pl.ds(l + k, nc, stride=CMP_STRIDE), :].astype(jnp.bfloat16)
             for k in range(2)], axis=1)
        pair = lambda l0: w1_ref[0, l0:l0 + 2].reshape(2 * HEAD_DIM, CMP_HIDDEN)
        p = p + _dot(xl, pair(l))
        q = q + _dot(xl, pair(CMP_STRIDE + l))
    posb = jnp.broadcast_to(pos_ref[...], (8, CMP_LEN * HEAD_DIM)).astype(jnp.bfloat16)
    w1_flat = w1_ref[0].reshape(CMP_LEN * HEAD_DIM, CMP_HIDDEN)
    pterm = _dot(posb, w1_flat)[0:1, :]
    h = p + pltpu.roll(q, nc - 1, axis=0) + pterm
    a = jax.nn.gelu(h).astype(jnp.bfloat16)
    out = _dot(a, w2_ref[0])

    @pl.when(kind < NSA_KV_HEADS)
    def _():
        o_ref[0, 0] = _apply_rope(out, cos_ref[0], sin_ref[0]).astype(o_ref.dtype)

    @pl.when(kind >= NSA_KV_HEADS)
    def _():
        o_ref[0, 0] = out.astype(o_ref.dtype)


def _compress(zc, w1, w2, cmp_pos, cos_c, sin_c):
    nb, s, _ = zc.shape
    nk = 2 * NSA_KV_HEADS
    nc = s // CMP_STRIDE
    return pl.pallas_call(
        _compress_kernel,
        out_shape=jax.ShapeDtypeStruct((nb, nk, nc, HEAD_DIM), jnp.bfloat16),
        grid=(nb, nk),
        in_specs=[
            pl.BlockSpec((1, s, HEAD_DIM), lambda b, k: (b, 0, k)),
            pl.BlockSpec((1, CMP_LEN, HEAD_DIM, CMP_HIDDEN),
                         lambda b, k: (k // NSA_KV_HEADS, 0, 0, 0)),
            pl.BlockSpec((1, CMP_HIDDEN, HEAD_DIM), lambda b, k: (k // NSA_KV_HEADS, 0, 0)),
            pl.BlockSpec((1, CMP_LEN * HEAD_DIM), lambda b, k: (0, 0)),
            pl.BlockSpec((1, nc, HEAD_DIM), lambda b, k: (b, 0, 0)),
            pl.BlockSpec((1, nc, HEAD_DIM), lambda b, k: (b, 0, 0)),
        ],
        out_specs=pl.BlockSpec((1, 1, nc, HEAD_DIM), lambda b, k: (b, k, 0, 0)),
        compiler_params=_cparams(("arbitrary", "arbitrary")),
        name="compress",
    )(zc, w1, w2, cmp_pos.reshape(1, CMP_LEN * HEAD_DIM), cos_c, sin_c)


FOX_T = 512


def _softmax_step(s, m, acc, v):
    m_new = jnp.maximum(m, jnp.max(s, axis=1, keepdims=True))
    alpha = jnp.exp2(m - m_new)
    p = jnp.exp2((s - m_new).astype(jnp.bfloat16))
    va = jnp.concatenate([v, jnp.ones_like(v)], axis=1)
    return m_new, alpha * acc + _dot(p, va)


def _softmax_init(rows):
    return (jnp.full((rows, 1), MASK_VALUE, jnp.float32),
            jnp.zeros((rows, 2 * HEAD_DIM), jnp.float32))


def _softmax_result(acc):
    return acc[:, 0:HEAD_DIM] * (1.0 / acc[:, HEAD_DIM:2 * HEAD_DIM])


FOX_HB = 4


def _fox_kernel(q_ref, k_ref, v_ref, cum_ref, *rest, n_cast):
    o_ref = rest[n_cast]
    for w_ref, wb_ref in zip(rest[:n_cast], rest[n_cast + 1:]):
        wb_ref[...] = w_ref[...].astype(wb_ref.dtype)
    hb = pl.program_id(1)
    i = pl.program_id(2)
    t = FOX_T
    cols = [slice(n * HEAD_DIM, (n + 1) * HEAD_DIM) for n in range(FOX_HB)]
    qs = [q_ref[0, :, c] for c in cols]

    def tile(k0, width, carry, diag):
        k0 = pl.multiple_of(k0, t)
        logits = [_dot_nt(qs[n], k_ref[0, pl.ds(k0, width), cols[n]]) for n in range(FOX_HB)]
        out = []
        for n, (m, acc) in enumerate(carry):
            v = v_ref[0, pl.ds(k0, width), cols[n]]
            ck = cum_ref[0, pl.ds(hb * FOX_HB + n, 1), pl.ds(k0, width)] * LOG2E
            s = logits[n] - ck
            if diag:
                r = lax.broadcasted_iota(jnp.int32, s.shape, 0)
                c = lax.broadcasted_iota(jnp.int32, s.shape, 1)
                s = jnp.where(c <= r + (width - t), s, MASK_VALUE)
            out.append(_softmax_step(s, m, acc, v))
        return tuple(out)

    init = tuple(_softmax_init(t) for _ in range(FOX_HB))
    n_pairs = i >> 1
    carry = lax.fori_loop(0, n_pairs, lambda p, c: tile(p * (2 * t), 2 * t, c, False), init)
    k_tail = n_pairs * (2 * t)
    carry = lax.cond((i & 1) == 0,
                     lambda c: tile(k_tail, t, c, True),
                     lambda c: tile(k_tail, 2 * t, c, True), carry)
    for n, (_, acc) in enumerate(carry):
        o_ref[0, :, cols[n]] = _softmax_result(acc).astype(o_ref.dtype)


def _fox(z_main, cum, weights):
    nb, s, _ = z_main.shape
    w = FOX_HB * HEAD_DIM
    grid = (nb, FOX_HEADS // FOX_HB, s // FOX_T)
    n_steps = grid[0] * grid[1] * grid[2]
    step = lambda b, h, i: ((b * grid[1] + h) * grid[2] + i, 0)
    slabs = []
    for wt in weights:
        rows = wt.shape[0] // n_steps
        assert rows * n_steps == wt.shape[0] and rows % BF16_SUBLANES == 0, wt.shape
        slabs.append(pl.BlockSpec((rows, wt.shape[1]), step))
    outs = pl.pallas_call(
        functools.partial(_fox_kernel, n_cast=len(weights)),
        out_shape=[jax.ShapeDtypeStruct((nb, s, FOX_W), jnp.bfloat16)]
        + [jax.ShapeDtypeStruct(wt.shape, jnp.bfloat16) for wt in weights],
        grid=grid,
        in_specs=[
            pl.BlockSpec((1, FOX_T, w), lambda b, h, i: (b, i, h)),
            pl.BlockSpec((1, s, w), lambda b, h, i: (b, 0, FOX_W // w + h)),
            pl.BlockSpec((1, s, w), lambda b, h, i: (b, 0, 2 * FOX_W // w + h)),
            pl.BlockSpec((1, FOX_HEADS, s), lambda b, h, i: (b, 0, 0)),
        ] + slabs,
        out_specs=[pl.BlockSpec((1, FOX_T, w), lambda b, h, i: (b, i, h))] + slabs,
        compiler_params=_cparams(("arbitrary", "arbitrary", "arbitrary")),
        name="fox",
    )(z_main, z_main, z_main, cum, *weights)
    return outs[0], outs[1:]


NSA_TQ = 512
NSA_PARTS = 4
NSEL_PAD = 128
SLC_SHIFT = SLC_LEN.bit_length() - 1
assert 1 << SLC_SHIFT == SLC_LEN
assert WINDOW % NSA_TQ == 0 and NSA_TQ % SLC_LEN == 0
SUBLANES = 8
BF16_SUBLANES = 16


def _block_ranks(score, n_slc):
    tq = score.shape[1]
    groups = [score[SUBLANES * v:SUBLANES * (v + 1)] for v in range(n_slc // SUBLANES)]
    ranks = [jnp.zeros((SUBLANES, tq), jnp.float32) for _ in groups]
    for m in range(n_slc):
        row = jnp.broadcast_to(score[m:m + 1, :], (SUBLANES, tq))
        for v, grp in enumerate(groups):
            lo = SUBLANES * v
            if lo > m:
                before = row >= grp
            elif lo + SUBLANES - 1 <= m:
                before = row > grp
            else:
                n_i = lo + lax.broadcasted_iota(jnp.int32, grp.shape, 0)
                before = (row > grp) | ((row == grp) & (n_i > m))
            ranks[v] = ranks[v] + jnp.where(before, 1.0, 0.0)
    return jnp.concatenate(ranks, axis=0)


def _nsa_kernel(q_ref, zs_ref, kc_ref, vc_ref, ks_ref, vs_ref, kw_ref, vw_ref, ovt_ref,
                o_ref, ka_ref, *, seq):
    g = pl.program_id(1)
    i = pl.program_id(2)
    tq = NSA_TQ
    q0 = i * tq

    @pl.when(i == 0)
    def _():
        ka_ref[:, 0:HEAD_DIM] = ks_ref[0]
        tok = lax.broadcasted_iota(jnp.int32, (seq, NSEL_PAD), 0)
        n_i = lax.broadcasted_iota(jnp.int32, (seq, NSEL_PAD), 1)
        ka_ref[:, HEAD_DIM:HEAD_DIM + NSEL_PAD] = jnp.where(
            n_i == (tok >> SLC_SHIFT), 1.0, 0.0).astype(jnp.bfloat16)

    qb = q_ref[0]
    hpp = NSA_GROUP // NSA_PARTS
    parts = range(NSA_PARTS)
    qh = [jnp.concatenate(
        [qb[:, r * HEAD_DIM:(r + 1) * HEAD_DIM] for r in range(h * hpp, (h + 1) * hpp)], axis=0)
        for h in parts]
    prow = hpp * tq

    def row_query(shape):
        return lax.broadcasted_iota(jnp.int32, shape, 0) & (tq - 1)

    kc = kc_ref[0, 0]
    vc = vc_ref[0, 0]
    s_cs = [_dot_nt(qh[h], kc) for h in parts]
    p_cs = []
    for s_c in s_cs:
        cend = lax.broadcasted_iota(jnp.int32, s_c.shape, 1) * CMP_STRIDE + (CMP_LEN - 1)
        s_c = jnp.where(cend <= q0 + row_query(s_c.shape), s_c, MASK_VALUE)
        m_c = jnp.maximum(jnp.max(s_c, axis=1, keepdims=True), 0.1 * MASK_VALUE)
        e_c = jnp.exp2(s_c - m_c)
        p_cs.append(e_c * (1.0 / jnp.maximum(jnp.sum(e_c, axis=1, keepdims=True), 1e-30)))
    o_c = [_dot(p_c.astype(jnp.bfloat16), vc) for p_c in p_cs]

    psum = None
    for p_c in p_cs:
        for r in range(hpp):
            term = p_c[r * tq:(r + 1) * tq]
            psum = term if psum is None else psum + term
    p_hi = psum.astype(jnp.bfloat16)
    p_lo = (psum - p_hi.astype(jnp.float32)).astype(jnp.bfloat16)
    ovt = ovt_ref[...]
    imp_t = _dot_nt(ovt, p_hi) + _dot_nt(ovt, p_lo)
    n_slc = seq // SLC_LEN
    blk = lax.broadcasted_iota(jnp.int32, imp_t.shape, 0)
    cur = (q0 + lax.broadcasted_iota(jnp.int32, imp_t.shape, 1)) >> SLC_SHIFT
    forced = (blk == 0) | (blk == cur) | (blk == cur - 1)
    score = jnp.where(forced, SEL_BONUS, imp_t)
    score = jnp.where(blk <= cur, score, -SEL_BONUS)
    rank = _block_ranks(score, n_slc)
    bias_t = jnp.where(rank < min(SLC_TOPK, n_slc), 0.0, MASK_VALUE)
    bias_t = jnp.concatenate(
        [bias_t, jnp.zeros((NSEL_PAD - n_slc, tq), jnp.float32)], axis=0)
    bias = bias_t.T.astype(jnp.bfloat16)
    bias_p = jnp.concatenate([bias] * hpp, axis=0)
    qa = [jnp.concatenate([qh[h], bias_p], axis=1) for h in parts]

    def sel_tile(k0, width, carry, diag):
        k0 = pl.multiple_of(k0, tq)
        logits = [_dot_nt(qa[h], ka_ref[pl.ds(k0, width), :]) for h in parts]
        out = []
        for s, (m, acc) in zip(logits, carry):
            if diag:
                c = lax.broadcasted_iota(jnp.int32, s.shape, 1)
                s = jnp.where(c <= row_query(s.shape) + (width - tq), s, MASK_VALUE)
            out.append(_softmax_step(s, m, acc, vs_ref[0, pl.ds(k0, width), :]))
        return tuple(out)

    init = tuple(_softmax_init(prow) for _ in parts)
    n_pairs = i >> 1
    carry = lax.fori_loop(0, n_pairs,
                          lambda p, c: sel_tile(p * (2 * tq), 2 * tq, c, False), init)
    k_tail = n_pairs * (2 * tq)
    carry = lax.cond((i & 1) == 0,
                     lambda c: sel_tile(k_tail, tq, c, True),
                     lambda c: sel_tile(k_tail, 2 * tq, c, True), carry)
    o_s = [_softmax_result(acc) for _, acc in carry]

    n_w = WINDOW // tq + 1
    starts = [q0 - WINDOW + t * tq for t in range(n_w)]
    offs = [pl.multiple_of(jnp.maximum(st, 0), tq) for st in starts]
    s_w = [[_dot_nt(qh[h], kw_ref[0, pl.ds(offs[t], tq), :]) for h in parts] for t in range(n_w)]
    for t in range(n_w):
        for h in parts:
            s = s_w[t][h]
            c = lax.broadcasted_iota(jnp.int32, s.shape, 1)
            if t == 0:
                s = jnp.where(c > row_query(s.shape), s, MASK_VALUE)
            if t == n_w - 1:
                s = jnp.where(c <= row_query(s.shape), s, MASK_VALUE)
            else:
                s = jnp.where(starts[t] >= 0, s, MASK_VALUE)
            s_w[t][h] = s
    o_w = []
    for h in parts:
        m_w = s_w[0][h].max(axis=1, keepdims=True)
        for t in range(1, n_w):
            m_w = jnp.maximum(m_w, s_w[t][h].max(axis=1, keepdims=True))
        acc_w = jnp.zeros((prow, 2 * HEAD_DIM), jnp.float32)
        for t in range(n_w):
            p = jnp.exp2((s_w[t][h] - m_w).astype(jnp.bfloat16))
            v = vw_ref[0, pl.ds(offs[t], tq), :]
            acc_w = acc_w + _dot(p, jnp.concatenate([v, jnp.ones_like(v)], axis=1))
        o_w.append(_softmax_result(acc_w))

    gates = jax.nn.sigmoid(zs_ref[0])
    lane = lax.broadcasted_iota(jnp.int32, gates.shape, 1)
    outs = []
    for r in range(NSA_GROUP):
        h, rows = r // hpp, slice((r % hpp) * tq, (r % hpp + 1) * tq)
        o_r = jnp.zeros((tq, HEAD_DIM), jnp.float32)
        for br, o_b in enumerate((o_c, o_s, o_w)):
            col = FOX_HEADS + 3 * (g * NSA_GROUP + r) + br
            gcol = jnp.sum(jnp.where(lane == col, gates, 0.0), axis=1, keepdims=True)
            o_r = o_r + gcol * o_b[h][rows]
        outs.append(o_r)
    o_ref[0] = jnp.concatenate(outs, axis=1).astype(o_ref.dtype)


def _nsa(z_main, z_small, kvc, ovt):
    nb, s, _ = z_main.shape
    nc = kvc.shape[2]
    gw = NSA_GROUP * HEAD_DIM
    full = lambda col0: pl.BlockSpec(
        (1, s, HEAD_DIM), lambda b, g, i, c=col0 // HEAD_DIM: (b, 0, c + g))
    return pl.pallas_call(
        functools.partial(_nsa_kernel, seq=s),
        out_shape=jax.ShapeDtypeStruct((nb, s, NSA_W), jnp.bfloat16),
        grid=(nb, NSA_KV_HEADS, s // NSA_TQ),
        in_specs=[
            pl.BlockSpec((1, NSA_TQ, gw), lambda b, g, i: (b, i, COL_QN // gw + g)),
            pl.BlockSpec((1, NSA_TQ, Z_SMALL), lambda b, g, i: (b, i, 0)),
            pl.BlockSpec((1, 1, nc, HEAD_DIM), lambda b, g, i: (b, g, 0, 0)),
            pl.BlockSpec((1, 1, nc, HEAD_DIM), lambda b, g, i: (b, NSA_KV_HEADS + g, 0, 0)),
            full(COL_KS), full(COL_VS), full(COL_KW), full(COL_VW),
            pl.BlockSpec(ovt.shape, lambda b, g, i: (0, 0)),
        ],
        out_specs=pl.BlockSpec((1, NSA_TQ, gw), lambda b, g, i: (b, i, g)),
        scratch_shapes=[pltpu.VMEM((s, HEAD_DIM + NSEL_PAD), jnp.bfloat16)],
        compiler_params=_cparams(("arbitrary", "arbitrary", "arbitrary")),
        name="nsa",
    )(z_main, z_small, kvc, kvc, z_main, z_main, z_main, z_main, ovt)


OUT_TM = 512
OUT_RC = 256


def _outproj_kernel(of_ref, on_ref, bf_ref, bn_ref, w_ref, x_ref, g_ref, nw_ref, sc_ref,
                    sh_ref, h_ref, u_ref):
    for r in range(OUT_TM // OUT_RC):
        rows = slice(r * OUT_RC, (r + 1) * OUT_RC)
        yf = _rms(of_ref[0, rows, :].astype(jnp.float32)) * bf_ref[...]
        yn = _rms(on_ref[0, rows, :].astype(jnp.float32)) * bn_ref[...]
        y = jnp.concatenate([yf.astype(jnp.bfloat16), yn.astype(jnp.bfloat16)], axis=1)
        h = x_ref[0, rows, :] + g_ref[0] * _dot(y, w_ref[...])
        h_ref[0, rows, :] = h
        u = _rms(h) * nw_ref[...]
        u_ref[0, rows, :] = (u * (1.0 + sc_ref[0]) + sh_ref[0]).astype(jnp.bfloat16)


def _outproj(o_f, o_n, beta_f, beta_n, w_out, x, gate, norm_w, sc, sh):
    nb, s, d = x.shape
    vec = lambda n: pl.BlockSpec((1, n), lambda b, i: (0, 0))
    bvec = pl.BlockSpec((1, 1, d), lambda b, i: (b, 0, 0))
    rows = lambda n: pl.BlockSpec((1, OUT_TM, n), lambda b, i: (b, i, 0))
    return pl.pallas_call(
        _outproj_kernel,
        out_shape=(jax.ShapeDtypeStruct((nb, s, d), jnp.float32),
                   jax.ShapeDtypeStruct((nb, s, d), jnp.bfloat16)),
        grid=(nb, s // OUT_TM),
        in_specs=[
            rows(FOX_W), rows(NSA_W), vec(FOX_W), vec(NSA_W),
            pl.BlockSpec((FOX_W + NSA_W, d), lambda b, i: (0, 0),
                         pipeline_mode=pl.Buffered(1)),
            rows(d), bvec, vec(d), bvec, bvec,
        ],
        out_specs=(rows(d), rows(d)),
        compiler_params=_cparams(("arbitrary", "arbitrary")),
        name="outproj",
    )(o_f, o_n, beta_f.reshape(1, FOX_W), beta_n.reshape(1, NSA_W), w_out, x,
      gate.reshape(nb, 1, d), norm_w.reshape(1, d), sc.reshape(nb, 1, d),
      sh.reshape(nb, 1, d))


FFN_TM = 1024
FFN_TF = 512
FFN_RC = 512
FFN_RC_LAST = 256


def _ffn_kernel(u_ref, h_ref, g_ref, fw_ref, wg_ref, wu_ref, wd_ref, o_ref, *, final):
    j = pl.program_id(2)
    last = pl.num_programs(2) - 1

    def partial_sum(rows):
        u = u_ref[0, rows, :]
        a = _dot(u, wg_ref[...])
        b = _dot(u, wu_ref[...])
        mid = (a * jax.nn.sigmoid(a) * b).astype(jnp.bfloat16)
        return _dot(mid, wd_ref[...])

    def chunks(rc):
        return [slice(r * rc, (r + 1) * rc) for r in range(FFN_TM // rc)]

    @pl.when(j == 0)
    def _():
        for rows in chunks(FFN_RC):
            o_ref[0, rows, :] = partial_sum(rows)

    @pl.when((j > 0) & (j < last))
    def _():
        for rows in chunks(FFN_RC):
            o_ref[0, rows, :] += partial_sum(rows)

    @pl.when(j == last)
    def _():
        for rows in chunks(FFN_RC_LAST):
            h2 = h_ref[0, rows, :] + g_ref[0] * (o_ref[0, rows, :] + partial_sum(rows))
            o_ref[0, rows, :] = _rms(h2) * fw_ref[...] if final else h2


def _ffn(u2, h1, gate, final_w, w_gate, w_up, w_down, final):
    nb, s, d = h1.shape
    f = w_gate.shape[1]
    return pl.pallas_call(
        functools.partial(_ffn_kernel, final=final),
        out_shape=jax.ShapeDtypeStruct((nb, s, d), jnp.float32),
        grid=(nb, s // FFN_TM, f // FFN_TF),
        in_specs=[
            pl.BlockSpec((1, FFN_TM, d), lambda b, i, j: (b, i, 0)),
            pl.BlockSpec((1, FFN_TM, d), lambda b, i, j: (b, i, 0),
                         pipeline_mode=pl.Buffered(1)),
            pl.BlockSpec((1, 1, d), lambda b, i, j: (b, 0, 0)),
            pl.BlockSpec((1, d), lambda b, i, j: (0, 0)),
            pl.BlockSpec((d, FFN_TF), lambda b, i, j: (0, j)),
            pl.BlockSpec((d, FFN_TF), lambda b, i, j: (0, j)),
            pl.BlockSpec((FFN_TF, d), lambda b, i, j: (j, 0)),
        ],
        out_specs=pl.BlockSpec((1, FFN_TM, d), lambda b, i, j: (b, i, 0)),
        compiler_params=pltpu.CompilerParams(
            dimension_semantics=("arbitrary", "arbitrary", "arbitrary"),
            vmem_limit_bytes=FFN_VMEM_LIMIT),
        name="ffn",
    )(u2, h1, gate.reshape(nb, 1, d), final_w.reshape(1, d), w_gate, w_up, w_down)


def _split_w_in(w_in):
    sizes = ([FOX_W] * 3 + [FOX_HEADS] + [NSA_W] + [KV_W] * 6 + [3 * NSA_HEADS])
    offs = [0]
    for sz in sizes:
        offs.append(offs[-1] + sz)
    wt = w_in.T
    order = (0, 1, 2, 4, 7, 9, 8, 10, 5, 6)
    units = lambda ns, unit: [offs[n] + r for n in ns for r in range(0, sizes[n], unit)]
    main = _gather_rows(wt, units(order, KV_W), KV_W, jnp.bfloat16)
    narrow = _gather_rows(wt, units((3, 11), SUBLANES), SUBLANES, jnp.float32)
    small = jnp.concatenate(
        [narrow, jnp.zeros((Z_SMALL - narrow.shape[0], w_in.shape[0]), jnp.float32)], axis=0)
    return main, small.astype(jnp.bfloat16)


def _cast_kernel(rows_ref, w_ref, o_ref):
    del rows_ref
    o_ref[...] = w_ref[...].astype(o_ref.dtype)


def _gather_rows(wt, src_rows, unit, dtype):
    assert all(r % SUBLANES == 0 for r in src_rows), src_rows
    tiles = jnp.asarray([r // SUBLANES for r in src_rows], jnp.int32)
    n_units = len(src_rows)
    d = wt.shape[1]
    return pl.pallas_call(
        _cast_kernel,
        out_shape=jax.ShapeDtypeStruct((n_units * unit, d), dtype),
        grid_spec=pltpu.PrefetchScalarGridSpec(
            num_scalar_prefetch=1,
            grid=(n_units,),
            in_specs=[pl.BlockSpec((pl.Element(unit), pl.Element(d)),
                                   lambda u, rows: (rows[u] * SUBLANES, 0))],
            out_specs=pl.BlockSpec((unit, d), lambda u, rows: (u, 0)),
        ),
        compiler_params=_cparams(("arbitrary",)),
        name="w_in_prep",
    )(tiles, wt)


def _overlap_t(n_cmp_pad, n_slc):
    cs = jnp.arange(n_cmp_pad) * CMP_STRIDE
    ss = jnp.arange(n_slc) * SLC_LEN
    ov = (jnp.minimum(cs[None, :] + CMP_LEN, ss[:, None] + SLC_LEN)
          - jnp.maximum(cs[None, :], ss[:, None]))
    ov = jnp.clip(ov, 0).astype(jnp.float32) / CMP_STRIDE
    return ov.astype(jnp.bfloat16)


def kernel(x, c, positions, w_ada, b_ada, norm_attn, norm_ffn, w_in, b_fgate, cmp_pos,
           w_kc1, w_kc2, w_vc1, w_vc2, beta_fox, beta_nsa, w_out, w_gate, w_up, w_down,
           final_norm):
    nb, s, d = x.shape
    depth = w_ada.shape[0]
    n_cmp_pad = s // CMP_STRIDE
    cmp_end = jnp.arange(n_cmp_pad) * CMP_STRIDE + (CMP_LEN - 1)
    pos_c = jnp.take(positions, jnp.minimum(cmp_end, s - 1), axis=1)
    cos_c, sin_c = _rope_tables(pos_c, n_cmp_pad)
    ovt = _overlap_t(n_cmp_pad, s // SLC_LEN)

    h = x
    for l in range(depth):
        if l == 0:
            mod, cos, sin = _ada(c, w_ada[l], b_ada[l], positions)
        else:
            mod = _ada(c, w_ada[l], b_ada[l])
        sh1, sc1, g1, sh2, sc2, g2 = jnp.split(mod, 6, axis=-1)
        w_main, w_small = _split_w_in(w_in[l])
        z_main, z_small, zc = _inproj(h, norm_attn[l], sc1, sh1, w_main, w_small, cos, sin)
        cum = _decay(z_small, b_fgate[l])

        w1 = jnp.stack([w_kc1[l], w_vc1[l]]).astype(jnp.bfloat16)
        w1 = w1.reshape(2, CMP_LEN, HEAD_DIM, CMP_HIDDEN)
        w2 = jnp.stack([w_kc2[l], w_vc2[l]]).astype(jnp.bfloat16)
        kvc = _compress(zc, w1, w2, cmp_pos[l], cos_c, sin_c)

        o_f, (wo, wg, wu, wd) = _fox(z_main, cum, (w_out[l], w_gate[l], w_up[l], w_down[l]))
        o_n = _nsa(z_main, z_small, kvc, ovt)
        h1, u2 = _outproj(o_f, o_n, beta_fox[l], beta_nsa[l], wo, h, g1, norm_ffn[l], sc2, sh2)
        h = _ffn(u2, h1, g2, final_norm, wg, wu, wd, final=(l == depth - 1))
    return h
```

```python
import functools

import jax
import jax.numpy as jnp
from jax import lax
from jax.experimental import pallas as pl
from jax.experimental.pallas import tpu as pltpu

D_MODEL = 2048
HEAD_DIM = 128
FOX_HEADS = 8
NSA_HEADS = 8
NSA_KV_HEADS = 2
NSA_GROUP = NSA_HEADS // NSA_KV_HEADS
CMP_LEN = 32
CMP_STRIDE = 16
CMP_HIDDEN = 2 * HEAD_DIM
SLC_LEN = 64
SLC_TOPK = 16
WINDOW = 512
ROPE_THETA = 10000.0
NORM_EPS = 1e-6
MASK_VALUE = -1e30
SEL_BONUS = 1e6

FOX_W = FOX_HEADS * HEAD_DIM
NSA_W = NSA_HEADS * HEAD_DIM
KV_W = NSA_KV_HEADS * HEAD_DIM
Z_MAIN = 3 * FOX_W + NSA_W + 6 * KV_W
Z_SMALL = 128
COL_QN = 3 * FOX_W
COL_KS = COL_QN + NSA_W
COL_KW = COL_KS + KV_W
COL_VS = COL_KW + KV_W
COL_VW = COL_VS + KV_W
COL_KC = COL_VW + KV_W

VMEM_LIMIT = 48 * 1024 * 1024
BIG_VMEM_LIMIT = 56 * 1024 * 1024
SCALE = HEAD_DIM ** -0.5
LOG2E = 1.4426950408889634


def _cparams(sem, vmem_limit=VMEM_LIMIT):
    return pltpu.CompilerParams(dimension_semantics=sem, vmem_limit_bytes=vmem_limit)


def _dot(a, b):
    return jnp.dot(a, b, preferred_element_type=jnp.float32)


def _dot_nt(a, b):
    return lax.dot_general(a, b, (((1,), (1,)), ((), ())), preferred_element_type=jnp.float32)


def _rms(x):
    return x * lax.rsqrt(jnp.mean(x * x, axis=-1, keepdims=True) + NORM_EPS)


ADA_TN = 512
ADA_KC = 256


def _ada_kernel(ct_ref, w_ref, b_ref, *rest, nb, with_rope):
    if with_rope:
        pos_ref, inv_ref, o_ref, cos_ref, sin_ref = rest
        _rope_table_kernel(pos_ref, inv_ref, cos_ref, sin_ref)
    else:
        o_ref, = rest
    tn = w_ref.shape[1]
    ct = ct_ref[...]
    st = ct * jax.nn.sigmoid(ct)

    accs = [jnp.zeros((SUBLANES, tn), jnp.float32) for _ in range(nb)]
    for kc in range(D_MODEL // ADA_KC):
        w = w_ref[kc * ADA_KC:(kc + 1) * ADA_KC, :]
        for b in range(nb):
            s = st[kc * ADA_KC:(kc + 1) * ADA_KC, b:b + 1]
            accs[b] = accs[b] + jnp.sum(
                (w * s).reshape(ADA_KC // SUBLANES, SUBLANES, tn), axis=0)
    for b in range(nb):
        o_ref[b:b + 1, :] = jnp.sum(accs[b], axis=0, keepdims=True) + b_ref[...]


def _ada(c, w_ada, b_ada, n, positions=None):
    nb = c.shape[0]
    n_steps = n // ADA_TN
    assert n_steps * ADA_TN == n
    in_specs = [
        pl.BlockSpec((D_MODEL, nb), lambda j: (0, 0)),
        pl.BlockSpec((D_MODEL, ADA_TN), lambda j: (0, j)),
        pl.BlockSpec((1, ADA_TN), lambda j: (0, j)),
    ]
    out_shape = [jax.ShapeDtypeStruct((nb, n), jnp.float32)]
    out_specs = [pl.BlockSpec((nb, ADA_TN), lambda j: (0, j))]
    args = [c.T, w_ada, b_ada]
    if positions is not None:
        s = positions.shape[1]
        per_batch = n_steps // nb
        ts = s // per_batch
        assert per_batch * nb == n_steps and ts * per_batch == s and ts % (2 * SUBLANES) == 0
        rows = lambda j: (j // per_batch, j % per_batch, 0)
        in_specs += [pl.BlockSpec((1, ts, 1), rows), pl.BlockSpec((1, HEAD_DIM), lambda j: (0, 0))]
        tab = jax.ShapeDtypeStruct((nb, s, HEAD_DIM), jnp.float32)
        out_shape += [tab, tab]
        out_specs += [pl.BlockSpec((1, ts, HEAD_DIM), rows)] * 2
        args += [positions.reshape(nb, s, 1), _rope_inv_freq()]
    outs = pl.pallas_call(
        functools.partial(_ada_kernel, nb=nb, with_rope=positions is not None),
        out_shape=out_shape,
        grid=(n_steps,),
        in_specs=in_specs,
        out_specs=out_specs,
        compiler_params=_cparams(("arbitrary",)),
        name="ada",
    )(*args)
    return outs if positions is not None else outs[0]


def _rope_table_kernel(pos_ref, inv_ref, cos_ref, sin_ref):
    half = pos_ref.shape[1] // 2
    lane = lax.broadcasted_iota(jnp.int32, (half, HEAD_DIM), 1)
    low = lane < HEAD_DIM // 2
    pos = jnp.where(low, pos_ref[0, 0:half, :], pos_ref[0, half:2 * half, :])
    ang = pos.astype(jnp.float32) * inv_ref[...]
    cos = jnp.cos(ang)
    sin = jnp.sin(ang)
    cos_sw = pltpu.roll(cos, HEAD_DIM // 2, axis=1)
    sin_sw = pltpu.roll(sin, HEAD_DIM // 2, axis=1)
    cos_ref[0, 0:half, :] = jnp.where(low, cos, cos_sw)
    cos_ref[0, half:2 * half, :] = jnp.where(low, cos_sw, cos)
    sin_ref[0, 0:half, :] = jnp.where(low, -sin, sin_sw)
    sin_ref[0, half:2 * half, :] = jnp.where(low, -sin_sw, sin)


def _rope_inv_freq():
    inv = ROPE_THETA ** (-jnp.arange(0, HEAD_DIM, 2, dtype=jnp.float32) / HEAD_DIM)
    return jnp.concatenate([inv, inv]).reshape(1, HEAD_DIM)


def _rope_tables(pos, ts):
    nb, n = pos.shape
    inv2 = _rope_inv_freq()
    shp = jax.ShapeDtypeStruct((nb, n, HEAD_DIM), jnp.float32)
    spec = pl.BlockSpec((1, ts, HEAD_DIM), lambda b, i: (b, i, 0))
    return pl.pallas_call(
        _rope_table_kernel,
        out_shape=(shp, shp),
        grid=(nb, n // ts),
        in_specs=[pl.BlockSpec((1, ts, 1), lambda b, i: (b, i, 0)),
                  pl.BlockSpec((1, HEAD_DIM), lambda b, i: (0, 0))],
        out_specs=(spec, spec),
        compiler_params=_cparams(("arbitrary", "arbitrary")),
        name="rope_tables",
    )(pos.reshape(nb, n, 1), inv2)


def _apply_rope(x, cos, sin):
    return x * cos + pltpu.roll(x, HEAD_DIM // 2, axis=1) * sin


IN_TM = 1024
IN_TN = 512
IN_RC = 256
assert FOX_W // IN_TN >= 1


def _inproj_kernel(x_ref, g_ref, sc_ref, sh_ref, w_ref, ws_ref, cos_ref, sin_ref,
                   z_ref, zs_ref, zc_ref, u_ref):
    j = pl.program_id(2)

    @pl.when(j == 0)
    def _():
        for r in range(IN_TM // IN_RC):
            rows = slice(r * IN_RC, (r + 1) * IN_RC)
            u = _rms(x_ref[0, rows, :]) * g_ref[...]
            ub = (u * (1.0 + sc_ref[0]) + sh_ref[0]).astype(jnp.bfloat16)
            u_ref[rows, :] = ub
            zs_ref[0, rows, :] = _dot_nt(ub, ws_ref[...])
            z_ref[0, rows, :] = (_dot_nt(ub, w_ref[...]) * (SCALE * LOG2E)).astype(z_ref.dtype)

    fox_q = (j > 0) & (j < FOX_W // IN_TN)
    nsa_q = (j >= COL_QN // IN_TN) & (j < COL_KS // IN_TN)
    nsa_k = j == COL_KS // IN_TN
    cmp_kv = j == COL_KC // IN_TN

    def matmul():
        return _dot_nt(u_ref[...], w_ref[...])

    def roped(acc):
        cos = cos_ref[0]
        sin = sin_ref[0]
        parts = [_apply_rope(acc[:, h * HEAD_DIM:(h + 1) * HEAD_DIM], cos, sin)
                 for h in range(IN_TN // HEAD_DIM)]
        return jnp.concatenate(parts, axis=1).astype(z_ref.dtype)

    @pl.when(fox_q)
    def _():
        z_ref[0] = (matmul() * (SCALE * LOG2E)).astype(z_ref.dtype)

    @pl.when(nsa_q)
    def _():
        z_ref[0] = roped(matmul() * (SCALE * LOG2E))

    @pl.when(nsa_k)
    def _():
        z_ref[0] = roped(matmul())

    @pl.when(cmp_kv)
    def _():
        acc = matmul()
        z_ref[0] = acc.astype(z_ref.dtype)
        zc_ref[0] = acc

    @pl.when(jnp.logical_not((j == 0) | fox_q | nsa_q | nsa_k | cmp_kv))
    def _():
        z_ref[0] = matmul().astype(z_ref.dtype)


def _inproj(x, norm_w, sc, sh, w_main, w_small, cos, sin):
    nb, s, d = x.shape
    grid = (nb, s // IN_TM, Z_MAIN // IN_TN)
    return pl.pallas_call(
        _inproj_kernel,
        out_shape=(jax.ShapeDtypeStruct((nb, s, Z_MAIN), jnp.bfloat16),
                   jax.ShapeDtypeStruct((nb, s, Z_SMALL), jnp.float32),
                   jax.ShapeDtypeStruct((nb, s, IN_TN), jnp.float32)),
        grid=grid,
        in_specs=[
            pl.BlockSpec((1, IN_TM, d), lambda b, i, j: (b, i, 0)),
            pl.BlockSpec((1, d), lambda b, i, j: (0, 0)),
            pl.BlockSpec((1, 1, d), lambda b, i, j: (b, 0, 0)),
            pl.BlockSpec((1, 1, d), lambda b, i, j: (b, 0, 0)),
            pl.BlockSpec((IN_TN, d), lambda b, i, j: (j, 0)),
            pl.BlockSpec((Z_SMALL, d), lambda b, i, j: (0, 0)),
            pl.BlockSpec((1, IN_TM, HEAD_DIM), lambda b, i, j: (b, i, 0)),
            pl.BlockSpec((1, IN_TM, HEAD_DIM), lambda b, i, j: (b, i, 0)),
        ],
        out_specs=(pl.BlockSpec((1, IN_TM, IN_TN), lambda b, i, j: (b, i, j)),
                   pl.BlockSpec((1, IN_TM, Z_SMALL), lambda b, i, j: (b, i, 0)),
                   pl.BlockSpec((1, IN_TM, IN_TN), lambda b, i, j: (b, i, 0))),
        scratch_shapes=[pltpu.VMEM((IN_TM, d), jnp.bfloat16)],
        compiler_params=_cparams(("arbitrary", "arbitrary", "arbitrary")),
        name="inproj",
    )(x, norm_w.reshape(1, d), sc.reshape(nb, 1, d), sh.reshape(nb, 1, d),
      w_main, w_small, cos, sin)


DEC_T = 2048


def _decay_kernel(zs_ref, bf_ref, cum_ref, carry_ref):
    @pl.when(pl.program_id(1) == 0)
    def _():
        carry_ref[...] = jnp.zeros_like(carry_ref)

    xt = zs_ref[0].T[0:FOX_HEADS, :] + bf_ref[...]
    lf = jnp.minimum(xt, 0.0) - jnp.log1p(jnp.exp(-jnp.abs(xt)))
    lane = lax.broadcasted_iota(jnp.int32, lf.shape, 1)
    sh = 1
    while sh < DEC_T:
        lf = lf + jnp.where(lane >= sh, pltpu.roll(lf, sh, axis=1), 0.0)
        sh *= 2
    cum = lf + carry_ref[:, 0:1]
    cum_ref[0] = cum
    carry_ref[...] = jnp.broadcast_to(cum[:, DEC_T - 1:DEC_T], carry_ref.shape)


def _decay(z_small, b_fgate):
    nb, s, _ = z_small.shape
    return pl.pallas_call(
        _decay_kernel,
        out_shape=jax.ShapeDtypeStruct((nb, FOX_HEADS, s), jnp.float32),
        grid=(nb, s // DEC_T),
        in_specs=[pl.BlockSpec((1, DEC_T, Z_SMALL), lambda b, i: (b, i, 0)),
                  pl.BlockSpec((FOX_HEADS, 1), lambda b, i: (0, 0))],
        out_specs=pl.BlockSpec((1, FOX_HEADS, DEC_T), lambda b, i: (b, 0, i)),
        scratch_shapes=[pltpu.VMEM((FOX_HEADS, 128), jnp.float32)],
        compiler_params=_cparams(("arbitrary", "arbitrary")),
        name="decay",
    )(z_small, b_fgate.reshape(FOX_HEADS, 1))


def _compress_kernel(x_ref, w1_ref, w2_ref, pos_ref, cos_ref, sin_ref, o_ref):
    kind = pl.program_id(1)
    nc = x_ref.shape[1] // CMP_STRIDE
    p = jnp.zeros((nc, CMP_HIDDEN), jnp.float32)
    q = jnp.zeros((nc, CMP_HIDDEN), jnp.float32)
    for l in range(0, CMP_STRIDE, 2):
        xl = jnp.concatenate(
            [x_ref[0, pl.ds(l + k, nc, stride=CMP_STRIDE), :].astype(jnp.bfloat16)
             for k in range(2)], axis=1)
        pair = lambda l0: w1_ref[0, l0:l0 + 2].reshape(2 * HEAD_DIM, CMP_HIDDEN)
        p = p + _dot(xl, pair(l))
        q = q + _dot(xl, pair(CMP_STRIDE + l))
    posb = jnp.broadcast_to(pos_ref[...], (8, CMP_LEN * HEAD_DIM)).astype(jnp.bfloat16)
    w1_flat = w1_ref[0].reshape(CMP_LEN * HEAD_DIM, CMP_HIDDEN)
    pterm = _dot(posb, w1_flat)[0:1, :]
    h = p + pltpu.roll(q, nc - 1, axis=0) + pterm
    a = jax.nn.gelu(h).astype(jnp.bfloat16)
    out = _dot(a, w2_ref[0])

    @pl.when(kind < NSA_KV_HEADS)
    def _():
        o_ref[0, 0] = _apply_rope(out, cos_ref[0], sin_ref[0]).astype(o_ref.dtype)

    @pl.when(kind >= NSA_KV_HEADS)
    def _():
        o_ref[0, 0] = out.astype(o_ref.dtype)


def _compress(zc, w1, w2, cmp_pos, cos_c, sin_c):
    nb, s, _ = zc.shape
    nk = 2 * NSA_KV_HEADS
    nc = s // CMP_STRIDE
    return pl.pallas_call(
        _compress_kernel,
        out_shape=jax.ShapeDtypeStruct((nb, nk, nc, HEAD_DIM), jnp.bfloat16),
        grid=(nb, nk),
        in_specs=[
            pl.BlockSpec((1, s, HEAD_DIM), lambda b, k: (b, 0, k)),
            pl.BlockSpec((1, CMP_LEN, HEAD_DIM, CMP_HIDDEN),
                         lambda b, k: (k // NSA_KV_HEADS, 0, 0, 0)),
            pl.BlockSpec((1, CMP_HIDDEN, HEAD_DIM), lambda b, k: (k // NSA_KV_HEADS, 0, 0)),
            pl.BlockSpec((1, CMP_LEN * HEAD_DIM), lambda b, k: (0, 0)),
            pl.BlockSpec((1, nc, HEAD_DIM), lambda b, k: (b, 0, 0)),
            pl.BlockSpec((1, nc, HEAD_DIM), lambda b, k: (b, 0, 0)),
        ],
        out_specs=pl.BlockSpec((1, 1, nc, HEAD_DIM), lambda b, k: (b, k, 0, 0)),
        compiler_params=_cparams(("arbitrary", "arbitrary")),
        name="compress",
    )(zc, w1, w2, cmp_pos.reshape(1, CMP_LEN * HEAD_DIM), cos_c, sin_c)


FOX_T = 512


def _softmax_step(s, m, acc, v):
    m_new = jnp.maximum(m, jnp.max(s, axis=1, keepdims=True))
    alpha = jnp.exp2(m - m_new)
    p = jnp.exp2((s - m_new).astype(jnp.bfloat16))
    va = jnp.concatenate([v, jnp.ones_like(v)], axis=1)
    return m_new, alpha * acc + _dot(p, va)


def _softmax_init(rows):
    return (jnp.full((rows, 1), MASK_VALUE, jnp.float32),
            jnp.zeros((rows, 2 * HEAD_DIM), jnp.float32))


def _softmax_result(acc):
    return acc[:, 0:HEAD_DIM] * (1.0 / acc[:, HEAD_DIM:2 * HEAD_DIM])


FOX_HB = 4


def _fox_kernel(q_ref, k_ref, v_ref, cum_ref, ct_ref, wada_ref, bada_ref, *rest, n_cast):
    o_ref, mod_ref = rest[n_cast], rest[n_cast + 1]
    for w_ref, wb_ref in zip(rest[:n_cast], rest[n_cast + 2:]):
        wb_ref[...] = w_ref[...].astype(wb_ref.dtype)
    _ada_kernel(ct_ref, wada_ref, bada_ref, mod_ref, nb=ct_ref.shape[1], with_rope=False)
    hb = pl.program_id(1)
    i = pl.program_id(2)
    t = FOX_T
    cols = [slice(n * HEAD_DIM, (n + 1) * HEAD_DIM) for n in range(FOX_HB)]
    qs = [q_ref[0, :, c] for c in cols]

    def tile(k0, width, carry, diag):
        k0 = pl.multiple_of(k0, t)
        logits = [_dot_nt(qs[n], k_ref[0, pl.ds(k0, width), cols[n]]) for n in range(FOX_HB)]
        out = []
        for n, (m, acc) in enumerate(carry):
            v = v_ref[0, pl.ds(k0, width), cols[n]]
            ck = cum_ref[0, pl.ds(hb * FOX_HB + n, 1), pl.ds(k0, width)] * LOG2E
            s = logits[n] - ck
            if diag:
                r = lax.broadcasted_iota(jnp.int32, s.shape, 0)
                c = lax.broadcasted_iota(jnp.int32, s.shape, 1)
                s = jnp.where(c <= r + (width - t), s, MASK_VALUE)
            out.append(_softmax_step(s, m, acc, v))
        return tuple(out)

    init = tuple(_softmax_init(t) for _ in range(FOX_HB))
    n_pairs = i >> 1
    carry = lax.fori_loop(0, n_pairs, lambda p, c: tile(p * (2 * t), 2 * t, c, False), init)
    k_tail = n_pairs * (2 * t)
    carry = lax.cond((i & 1) == 0,
                     lambda c: tile(k_tail, t, c, True),
                     lambda c: tile(k_tail, 2 * t, c, True), carry)
    for n, (_, acc) in enumerate(carry):
        o_ref[0, :, cols[n]] = _softmax_result(acc).astype(o_ref.dtype)


def _fox(z_main, cum, weights, ct, w_ada, b_ada, n_done):
    nb, s, _ = z_main.shape
    w = FOX_HB * HEAD_DIM
    grid = (nb, FOX_HEADS // FOX_HB, s // FOX_T)
    n_steps = grid[0] * grid[1] * grid[2]
    step = lambda b, h, i: ((b * grid[1] + h) * grid[2] + i, 0)
    n_rest = w_ada.shape[1] - n_done
    tn = n_rest // n_steps
    assert tn * n_steps == n_rest and tn % HEAD_DIM == 0 and n_done % tn == 0, (n_rest, tn)
    mod_in = lambda b, h, i: (0, n_done // tn + step(b, h, i)[0])
    slabs = []
    for wt in weights:
        rows = wt.shape[0] // n_steps
        assert rows * n_steps == wt.shape[0] and rows % BF16_SUBLANES == 0, wt.shape
        slabs.append(pl.BlockSpec((rows, wt.shape[1]), step))
    outs = pl.pallas_call(
        functools.partial(_fox_kernel, n_cast=len(weights)),
        out_shape=[jax.ShapeDtypeStruct((nb, s, FOX_W), jnp.bfloat16),
                   jax.ShapeDtypeStruct((nb, n_rest), jnp.float32)]
        + [jax.ShapeDtypeStruct(wt.shape, jnp.bfloat16) for wt in weights],
        grid=grid,
        in_specs=[
            pl.BlockSpec((1, FOX_T, w), lambda b, h, i: (b, i, h)),
            pl.BlockSpec((1, s, w), lambda b, h, i: (b, 0, FOX_W // w + h)),
            pl.BlockSpec((1, s, w), lambda b, h, i: (b, 0, 2 * FOX_W // w + h)),
            pl.BlockSpec((1, FOX_HEADS, s), lambda b, h, i: (b, 0, 0)),
            pl.BlockSpec(ct.shape, lambda b, h, i: (0, 0)),
            pl.BlockSpec((D_MODEL, tn), mod_in),
            pl.BlockSpec((1, tn), mod_in),
        ] + slabs,
        out_specs=[pl.BlockSpec((1, FOX_T, w), lambda b, h, i: (b, i, h)),
                   pl.BlockSpec((nb, tn), lambda b, h, i: (0, step(b, h, i)[0]))] + slabs,
        compiler_params=_cparams(("arbitrary", "arbitrary", "arbitrary"), BIG_VMEM_LIMIT),
        name="fox",
    )(z_main, z_main, z_main, cum, ct, w_ada, b_ada, *weights)
    return outs[0], outs[1], outs[2:]


NSA_TQ = 512
NSA_PARTS = 4
NSEL_PAD = 128
SLC_SHIFT = SLC_LEN.bit_length() - 1
assert 1 << SLC_SHIFT == SLC_LEN
assert WINDOW % NSA_TQ == 0 and NSA_TQ % SLC_LEN == 0
SUBLANES = 8
BF16_SUBLANES = 16


def _block_ranks(score, n_slc):
    tq = score.shape[1]
    groups = [score[SUBLANES * v:SUBLANES * (v + 1)] for v in range(n_slc // SUBLANES)]
    ranks = [jnp.zeros((SUBLANES, tq), jnp.float32) for _ in groups]
    for m in range(n_slc):
        row = jnp.broadcast_to(score[m:m + 1, :], (SUBLANES, tq))
        for v, grp in enumerate(groups):
            lo = SUBLANES * v
            if lo > m:
                before = row >= grp
            elif lo + SUBLANES - 1 <= m:
                before = row > grp
            else:
                n_i = lo + lax.broadcasted_iota(jnp.int32, grp.shape, 0)
                before = (row > grp) | ((row == grp) & (n_i > m))
            ranks[v] = ranks[v] + jnp.where(before, 1.0, 0.0)
    return jnp.concatenate(ranks, axis=0)


def _nsa_kernel(q_ref, zs_ref, kc_ref, vc_ref, ks_ref, vs_ref, kw_ref, vw_ref, ovt_ref,
                o_ref, ka_ref, *, seq):
    g = pl.program_id(1)
    i = pl.program_id(2)
    tq = NSA_TQ
    q0 = i * tq

    @pl.when(i == 0)
    def _():
        ka_ref[:, 0:HEAD_DIM] = ks_ref[0]
        tok = lax.broadcasted_iota(jnp.int32, (seq, NSEL_PAD), 0)
        n_i = lax.broadcasted_iota(jnp.int32, (seq, NSEL_PAD), 1)
        ka_ref[:, HEAD_DIM:HEAD_DIM + NSEL_PAD] = jnp.where(
            n_i == (tok >> SLC_SHIFT), 1.0, 0.0).astype(jnp.bfloat16)

    qb = q_ref[0]
    hpp = NSA_GROUP // NSA_PARTS
    parts = range(NSA_PARTS)
    qh = [jnp.concatenate(
        [qb[:, r * HEAD_DIM:(r + 1) * HEAD_DIM] for r in range(h * hpp, (h + 1) * hpp)], axis=0)
        for h in parts]
    prow = hpp * tq

    def row_query(shape):
        return lax.broadcasted_iota(jnp.int32, shape, 0) & (tq - 1)

    kc = kc_ref[0, 0]
    vc = vc_ref[0, 0]
    s_cs = [_dot_nt(qh[h], kc) for h in parts]
    p_cs = []
    for s_c in s_cs:
        cend = lax.broadcasted_iota(jnp.int32, s_c.shape, 1) * CMP_STRIDE + (CMP_LEN - 1)
        s_c = jnp.where(cend <= q0 + row_query(s_c.shape), s_c, MASK_VALUE)
        m_c = jnp.maximum(jnp.max(s_c, axis=1, keepdims=True), 0.1 * MASK_VALUE)
        e_c = jnp.exp2(s_c - m_c)
        p_cs.append(e_c * (1.0 / jnp.maximum(jnp.sum(e_c, axis=1, keepdims=True), 1e-30)))
    o_c = [_dot(p_c.astype(jnp.bfloat16), vc) for p_c in p_cs]

    psum = None
    for p_c in p_cs:
        for r in range(hpp):
            term = p_c[r * tq:(r + 1) * tq]
            psum = term if psum is None else psum + term
    p_hi = psum.astype(jnp.bfloat16)
    p_lo = (psum - p_hi.astype(jnp.float32)).astype(jnp.bfloat16)
    ovt = ovt_ref[...]
    imp_t = _dot_nt(ovt, p_hi) + _dot_nt(ovt, p_lo)
    n_slc = seq // SLC_LEN
    blk = lax.broadcasted_iota(jnp.int32, imp_t.shape, 0)
    cur = (q0 + lax.broadcasted_iota(jnp.int32, imp_t.shape, 1)) >> SLC_SHIFT
    forced = (blk == 0) | (blk == cur) | (blk == cur - 1)
    score = jnp.where(forced, SEL_BONUS, imp_t)
    score = jnp.where(blk <= cur, score, -SEL_BONUS)
    rank = _block_ranks(score, n_slc)
    bias_t = jnp.where(rank < min(SLC_TOPK, n_slc), 0.0, MASK_VALUE)
    bias_t = jnp.concatenate(
        [bias_t, jnp.zeros((NSEL_PAD - n_slc, tq), jnp.float32)], axis=0)
    bias = bias_t.T.astype(jnp.bfloat16)
    bias_p = jnp.concatenate([bias] * hpp, axis=0)
    qa = [jnp.concatenate([qh[h], bias_p], axis=1) for h in parts]

    def sel_tile(k0, width, carry, diag):
        k0 = pl.multiple_of(k0, tq)
        logits = [_dot_nt(qa[h], ka_ref[pl.ds(k0, width), :]) for h in parts]
        out = []
        for s, (m, acc) in zip(logits, carry):
            if diag:
                c = lax.broadcasted_iota(jnp.int32, s.shape, 1)
                s = jnp.where(c <= row_query(s.shape) + (width - tq), s, MASK_VALUE)
            out.append(_softmax_step(s, m, acc, vs_ref[0, pl.ds(k0, width), :]))
        return tuple(out)

    init = tuple(_softmax_init(prow) for _ in parts)
    n_pairs = i >> 1
    carry = lax.fori_loop(0, n_pairs,
                          lambda p, c: sel_tile(p * (2 * tq), 2 * tq, c, False), init)
    k_tail = n_pairs * (2 * tq)
    carry = lax.cond((i & 1) == 0,
                     lambda c: sel_tile(k_tail, tq, c, True),
                     lambda c: sel_tile(k_tail, 2 * tq, c, True), carry)
    o_s = [_softmax_result(acc) for _, acc in carry]

    n_w = WINDOW // tq + 1
    starts = [q0 - WINDOW + t * tq for t in range(n_w)]
    offs = [pl.multiple_of(jnp.maximum(st, 0), tq) for st in starts]
    s_w = [[_dot_nt(qh[h], kw_ref[0, pl.ds(offs[t], tq), :]) for h in parts] for t in range(n_w)]
    for t in range(n_w):
        for h in parts:
            s = s_w[t][h]
            c = lax.broadcasted_iota(jnp.int32, s.shape, 1)
            if t == 0:
                s = jnp.where(c > row_query(s.shape), s, MASK_VALUE)
            if t == n_w - 1:
                s = jnp.where(c <= row_query(s.shape), s, MASK_VALUE)
            else:
                s = jnp.where(starts[t] >= 0, s, MASK_VALUE)
            s_w[t][h] = s
    o_w = []
    for h in parts:
        m_w = s_w[0][h].max(axis=1, keepdims=True)
        for t in range(1, n_w):
            m_w = jnp.maximum(m_w, s_w[t][h].max(axis=1, keepdims=True))
        acc_w = jnp.zeros((prow, 2 * HEAD_DIM), jnp.float32)
        for t in range(n_w):
            p = jnp.exp2((s_w[t][h] - m_w).astype(jnp.bfloat16))
            v = vw_ref[0, pl.ds(offs[t], tq), :]
            acc_w = acc_w + _dot(p, jnp.concatenate([v, jnp.ones_like(v)], axis=1))
        o_w.append(_softmax_result(acc_w))

    gates = jax.nn.sigmoid(zs_ref[0])
    lane = lax.broadcasted_iota(jnp.int32, gates.shape, 1)
    outs = []
    for r in range(NSA_GROUP):
        h, rows = r // hpp, slice((r % hpp) * tq, (r % hpp + 1) * tq)
        o_r = jnp.zeros((tq, HEAD_DIM), jnp.float32)
        for br, o_b in enumerate((o_c, o_s, o_w)):
            col = FOX_HEADS + 3 * (g * NSA_GROUP + r) + br
            gcol = jnp.sum(jnp.where(lane == col, gates, 0.0), axis=1, keepdims=True)
            o_r = o_r + gcol * o_b[h][rows]
        outs.append(o_r)
    o_ref[0] = jnp.concatenate(outs, axis=1).astype(o_ref.dtype)


def _nsa(z_main, z_small, kvc, ovt):
    nb, s, _ = z_main.shape
    nc = kvc.shape[2]
    gw = NSA_GROUP * HEAD_DIM
    full = lambda col0: pl.BlockSpec(
        (1, s, HEAD_DIM), lambda b, g, i, c=col0 // HEAD_DIM: (b, 0, c + g))
    return pl.pallas_call(
        functools.partial(_nsa_kernel, seq=s),
        out_shape=jax.ShapeDtypeStruct((nb, s, NSA_W), jnp.bfloat16),
        grid=(nb, NSA_KV_HEADS, s // NSA_TQ),
        in_specs=[
            pl.BlockSpec((1, NSA_TQ, gw), lambda b, g, i: (b, i, COL_QN // gw + g)),
            pl.BlockSpec((1, NSA_TQ, Z_SMALL), lambda b, g, i: (b, i, 0)),
            pl.BlockSpec((1, 1, nc, HEAD_DIM), lambda b, g, i: (b, g, 0, 0)),
            pl.BlockSpec((1, 1, nc, HEAD_DIM), lambda b, g, i: (b, NSA_KV_HEADS + g, 0, 0)),
            full(COL_KS), full(COL_VS), full(COL_KW), full(COL_VW),
            pl.BlockSpec(ovt.shape, lambda b, g, i: (0, 0)),
        ],
        out_specs=pl.BlockSpec((1, NSA_TQ, gw), lambda b, g, i: (b, i, g)),
        scratch_shapes=[pltpu.VMEM((s, HEAD_DIM + NSEL_PAD), jnp.bfloat16)],
        compiler_params=_cparams(("arbitrary", "arbitrary", "arbitrary")),
        name="nsa",
    )(z_main, z_small, kvc, kvc, z_main, z_main, z_main, z_main, ovt)


OUT_TM = 512
OUT_RC = 256


def _outproj_kernel(of_ref, on_ref, bf_ref, bn_ref, w_ref, x_ref, g_ref, nw_ref, sc_ref,
                    sh_ref, h_ref, u_ref):
    for r in range(OUT_TM // OUT_RC):
        rows = slice(r * OUT_RC, (r + 1) * OUT_RC)
        yf = _rms(of_ref[0, rows, :].astype(jnp.float32)) * bf_ref[...]
        yn = _rms(on_ref[0, rows, :].astype(jnp.float32)) * bn_ref[...]
        y = jnp.concatenate([yf.astype(jnp.bfloat16), yn.astype(jnp.bfloat16)], axis=1)
        h = x_ref[0, rows, :] + g_ref[0] * _dot(y, w_ref[...])
        h_ref[0, rows, :] = h
        u = _rms(h) * nw_ref[...]
        u_ref[0, rows, :] = (u * (1.0 + sc_ref[0]) + sh_ref[0]).astype(jnp.bfloat16)


def _outproj(o_f, o_n, beta_f, beta_n, w_out, x, gate, norm_w, sc, sh):
    nb, s, d = x.shape
    vec = lambda n: pl.BlockSpec((1, n), lambda b, i: (0, 0))
    bvec = pl.BlockSpec((1, 1, d), lambda b, i: (b, 0, 0))
    rows = lambda n: pl.BlockSpec((1, OUT_TM, n), lambda b, i: (b, i, 0))
    return pl.pallas_call(
        _outproj_kernel,
        out_shape=(jax.ShapeDtypeStruct((nb, s, d), jnp.float32),
                   jax.ShapeDtypeStruct((nb, s, d), jnp.bfloat16)),
        grid=(nb, s // OUT_TM),
        in_specs=[
            rows(FOX_W), rows(NSA_W), vec(FOX_W), vec(NSA_W),
            pl.BlockSpec((FOX_W + NSA_W, d), lambda b, i: (0, 0),
                         pipeline_mode=pl.Buffered(1)),
            rows(d), bvec, vec(d), bvec, bvec,
        ],
        out_specs=(rows(d), rows(d)),
        compiler_params=_cparams(("arbitrary", "arbitrary")),
        name="outproj",
    )(o_f, o_n, beta_f.reshape(1, FOX_W), beta_n.reshape(1, NSA_W), w_out, x,
      gate.reshape(nb, 1, d), norm_w.reshape(1, d), sc.reshape(nb, 1, d),
      sh.reshape(nb, 1, d))


FFN_TM = 1024
FFN_TF = 512
FFN_RC = 512
FFN_RC_LAST = 256


def _ffn_kernel(u_ref, h_ref, g_ref, fw_ref, wg_ref, wu_ref, wd_ref, o_ref, *, final):
    j = pl.program_id(2)
    last = pl.num_programs(2) - 1

    def partial_sum(rows):
        u = u_ref[0, rows, :]
        a = _dot(u, wg_ref[...])
        b = _dot(u, wu_ref[...])
        mid = (a * jax.nn.sigmoid(a) * b).astype(jnp.bfloat16)
        return _dot(mid, wd_ref[...])

    def chunks(rc):
        return [slice(r * rc, (r + 1) * rc) for r in range(FFN_TM // rc)]

    @pl.when(j == 0)
    def _():
        for rows in chunks(FFN_RC):
            o_ref[0, rows, :] = partial_sum(rows)

    @pl.when((j > 0) & (j < last))
    def _():
        for rows in chunks(FFN_RC):
            o_ref[0, rows, :] += partial_sum(rows)

    @pl.when(j == last)
    def _():
        for rows in chunks(FFN_RC_LAST):
            h2 = h_ref[0, rows, :] + g_ref[0] * (o_ref[0, rows, :] + partial_sum(rows))
            o_ref[0, rows, :] = _rms(h2) * fw_ref[...] if final else h2


def _ffn(u2, h1, gate, final_w, w_gate, w_up, w_down, final):
    nb, s, d = h1.shape
    f = w_gate.shape[1]
    return pl.pallas_call(
        functools.partial(_ffn_kernel, final=final),
        out_shape=jax.ShapeDtypeStruct((nb, s, d), jnp.float32),
        grid=(nb, s // FFN_TM, f // FFN_TF),
        in_specs=[
            pl.BlockSpec((1, FFN_TM, d), lambda b, i, j: (b, i, 0)),
            pl.BlockSpec((1, FFN_TM, d), lambda b, i, j: (b, i, 0),
                         pipeline_mode=pl.Buffered(1)),
            pl.BlockSpec((1, 1, d), lambda b, i, j: (b, 0, 0)),
            pl.BlockSpec((1, d), lambda b, i, j: (0, 0)),
            pl.BlockSpec((d, FFN_TF), lambda b, i, j: (0, j)),
            pl.BlockSpec((d, FFN_TF), lambda b, i, j: (0, j)),
            pl.BlockSpec((FFN_TF, d), lambda b, i, j: (j, 0)),
        ],
        out_specs=pl.BlockSpec((1, FFN_TM, d), lambda b, i, j: (b, i, 0)),
        compiler_params=_cparams(("arbitrary", "arbitrary", "arbitrary"), BIG_VMEM_LIMIT),
        name="ffn",
    )(u2, h1, gate.reshape(nb, 1, d), final_w.reshape(1, d), w_gate, w_up, w_down)


def _split_w_in(w_in):
    sizes = ([FOX_W] * 3 + [FOX_HEADS] + [NSA_W] + [KV_W] * 6 + [3 * NSA_HEADS])
    offs = [0]
    for sz in sizes:
        offs.append(offs[-1] + sz)
    wt = w_in.T
    order = (0, 1, 2, 4, 7, 9, 8, 10, 5, 6)
    units = lambda ns, unit: [offs[n] + r for n in ns for r in range(0, sizes[n], unit)]
    main = _gather_rows(wt, units(order, KV_W), KV_W, jnp.bfloat16)
    narrow = _gather_rows(wt, units((3, 11), SUBLANES), SUBLANES, jnp.float32)
    small = jnp.concatenate(
        [narrow, jnp.zeros((Z_SMALL - narrow.shape[0], w_in.shape[0]), jnp.float32)], axis=0)
    return main, small.astype(jnp.bfloat16)


def _cast_kernel(rows_ref, w_ref, o_ref):
    del rows_ref
    o_ref[...] = w_ref[...].astype(o_ref.dtype)


def _gather_rows(wt, src_rows, unit, dtype):
    assert all(r % SUBLANES == 0 for r in src_rows), src_rows
    tiles = jnp.asarray([r // SUBLANES for r in src_rows], jnp.int32)
    n_units = len(src_rows)
    d = wt.shape[1]
    return pl.pallas_call(
        _cast_kernel,
        out_shape=jax.ShapeDtypeStruct((n_units * unit, d), dtype),
        grid_spec=pltpu.PrefetchScalarGridSpec(
            num_scalar_prefetch=1,
            grid=(n_units,),
            in_specs=[pl.BlockSpec((pl.Element(unit), pl.Element(d)),
                                   lambda u, rows: (rows[u] * SUBLANES, 0))],
            out_specs=pl.BlockSpec((unit, d), lambda u, rows: (u, 0)),
        ),
        compiler_params=_cparams(("arbitrary",)),
        name="w_in_prep",
    )(tiles, wt)


def _overlap_t(n_cmp_pad, n_slc):
    cs = jnp.arange(n_cmp_pad) * CMP_STRIDE
    ss = jnp.arange(n_slc) * SLC_LEN
    ov = (jnp.minimum(cs[None, :] + CMP_LEN, ss[:, None] + SLC_LEN)
          - jnp.maximum(cs[None, :], ss[:, None]))
    ov = jnp.clip(ov, 0).astype(jnp.float32) / CMP_STRIDE
    return ov.astype(jnp.bfloat16)


def kernel(x, c, positions, w_ada, b_ada, norm_attn, norm_ffn, w_in, b_fgate, cmp_pos,
           w_kc1, w_kc2, w_vc1, w_vc2, beta_fox, beta_nsa, w_out, w_gate, w_up, w_down,
           final_norm):
    nb, s, d = x.shape
    depth = w_ada.shape[0]
    n_cmp_pad = s // CMP_STRIDE
    cmp_end = jnp.arange(n_cmp_pad) * CMP_STRIDE + (CMP_LEN - 1)
    pos_c = jnp.take(positions, jnp.minimum(cmp_end, s - 1), axis=1)
    cos_c, sin_c = _rope_tables(pos_c, n_cmp_pad)
    ovt = _overlap_t(n_cmp_pad, s // SLC_LEN)

    h = x
    for l in range(depth):
        b_ada_l = b_ada[l].reshape(1, 6 * d)
        if l == 0:
            mod1, cos, sin = _ada(c, w_ada[l], b_ada_l, 2 * d, positions)
        else:
            mod1 = _ada(c, w_ada[l], b_ada_l, 2 * d)
        sh1, sc1 = jnp.split(mod1, 2, axis=-1)
        w_main, w_small = _split_w_in(w_in[l])
        z_main, z_small, zc = _inproj(h, norm_attn[l], sc1, sh1, w_main, w_small, cos, sin)
        cum = _decay(z_small, b_fgate[l])

        w1 = jnp.stack([w_kc1[l], w_vc1[l]]).astype(jnp.bfloat16)
        w1 = w1.reshape(2, CMP_LEN, HEAD_DIM, CMP_HIDDEN)
        w2 = jnp.stack([w_kc2[l], w_vc2[l]]).astype(jnp.bfloat16)
        kvc = _compress(zc, w1, w2, cmp_pos[l], cos_c, sin_c)

        o_f, mod2, (wo, wg, wu, wd) = _fox(
            z_main, cum, (w_out[l], w_gate[l], w_up[l], w_down[l]), c.T, w_ada[l], b_ada_l, 2 * d)
        g1, sh2, sc2, g2 = jnp.split(mod2, 4, axis=-1)
        o_n = _nsa(z_main, z_small, kvc, ovt)
        h1, u2 = _outproj(o_f, o_n, beta_fox[l], beta_nsa[l], wo, h, g1, norm_ffn[l], sc2, sh2)
        h = _ffn(u2, h1, g2, final_norm, wg, wu, wd, final=(l == depth - 1))
    return h
```

```python
import functools

import jax
import jax.numpy as jnp
from jax import lax
from jax.experimental import pallas as pl
from jax.experimental.pallas import tpu as pltpu

D_MODEL = 2048
HEAD_DIM = 128
FOX_HEADS = 8
NSA_HEADS = 8
NSA_KV_HEADS = 2
NSA_GROUP = NSA_HEADS // NSA_KV_HEADS
CMP_LEN = 32
CMP_STRIDE = 16
CMP_HIDDEN = 2 * HEAD_DIM
SLC_LEN = 64
SLC_TOPK = 16
WINDOW = 512
ROPE_THETA = 10000.0
NORM_EPS = 1e-6
MASK_VALUE = -1e30
SEL_BONUS = 1e6

FOX_W = FOX_HEADS * HEAD_DIM
NSA_W = NSA_HEADS * HEAD_DIM
KV_W = NSA_KV_HEADS * HEAD_DIM
Z_MAIN = 3 * FOX_W + NSA_W + 6 * KV_W
Z_SMALL = 128
COL_QN = 3 * FOX_W
COL_KS = COL_QN + NSA_W
COL_KW = COL_KS + KV_W
COL_VS = COL_KW + KV_W
COL_VW = COL_VS + KV_W
COL_KC = COL_VW + KV_W

VMEM_LIMIT = 48 * 1024 * 1024
BIG_VMEM_LIMIT = 56 * 1024 * 1024
SCALE = HEAD_DIM ** -0.5
LOG2E = 1.4426950408889634


def _cparams(sem, vmem_limit=VMEM_LIMIT):
    return pltpu.CompilerParams(dimension_semantics=sem, vmem_limit_bytes=vmem_limit)


def _dot(a, b):
    return jnp.dot(a, b, preferred_element_type=jnp.float32)


def _dot_nt(a, b):
    return lax.dot_general(a, b, (((1,), (1,)), ((), ())), preferred_element_type=jnp.float32)


def _rms(x):
    return x * lax.rsqrt(jnp.mean(x * x, axis=-1, keepdims=True) + NORM_EPS)


ADA_TN = 768
ADA_KC = 256


def _ada_kernel(ct_ref, w_ref, b_ref, *rest, nb, with_rope):
    if with_rope:
        pos_ref, inv_ref, o_ref, cos_ref, sin_ref = rest
        _rope_table_kernel(pos_ref, inv_ref, cos_ref, sin_ref)
    else:
        o_ref, = rest
    tn = w_ref.shape[1]
    ct = ct_ref[...]
    st = ct * jax.nn.sigmoid(ct)

    accs = [jnp.zeros((SUBLANES, tn), jnp.float32) for _ in range(nb)]
    for kc in range(D_MODEL // ADA_KC):
        w = w_ref[kc * ADA_KC:(kc + 1) * ADA_KC, :]
        for b in range(nb):
            s = st[kc * ADA_KC:(kc + 1) * ADA_KC, b:b + 1]
            accs[b] = accs[b] + jnp.sum(
                (w * s).reshape(ADA_KC // SUBLANES, SUBLANES, tn), axis=0)
    for b in range(nb):
        o_ref[b:b + 1, :] = jnp.sum(accs[b], axis=0, keepdims=True) + b_ref[...]


def _ada(c, w_ada, b_ada, positions=None):
    nb = c.shape[0]
    n = w_ada.shape[1]
    n_steps = n // ADA_TN
    assert n_steps * ADA_TN == n
    in_specs = [
        pl.BlockSpec((D_MODEL, nb), lambda j: (0, 0)),
        pl.BlockSpec((D_MODEL, ADA_TN), lambda j: (0, j)),
        pl.BlockSpec((1, ADA_TN), lambda j: (0, j)),
    ]
    out_shape = [jax.ShapeDtypeStruct((nb, n), jnp.float32)]
    out_specs = [pl.BlockSpec((nb, ADA_TN), lambda j: (0, j))]
    args = [c.T, w_ada, b_ada.reshape(1, n)]
    if positions is not None:
        s = positions.shape[1]
        per_batch = n_steps // nb
        ts = s // per_batch
        assert per_batch * nb == n_steps and ts * per_batch == s and ts % (2 * SUBLANES) == 0
        rows = lambda j: (j // per_batch, j % per_batch, 0)
        in_specs += [pl.BlockSpec((1, ts, 1), rows), pl.BlockSpec((1, HEAD_DIM), lambda j: (0, 0))]
        tab = jax.ShapeDtypeStruct((nb, s, HEAD_DIM), jnp.float32)
        out_shape += [tab, tab]
        out_specs += [pl.BlockSpec((1, ts, HEAD_DIM), rows)] * 2
        args += [positions.reshape(nb, s, 1), _rope_inv_freq()]
    outs = pl.pallas_call(
        functools.partial(_ada_kernel, nb=nb, with_rope=positions is not None),
        out_shape=out_shape,
        grid=(n_steps,),
        in_specs=in_specs,
        out_specs=out_specs,
        compiler_params=_cparams(("arbitrary",)),
        name="ada",
    )(*args)
    return outs if positions is not None else outs[0]


def _rope_table_kernel(pos_ref, inv_ref, cos_ref, sin_ref):
    half = pos_ref.shape[1] // 2
    lane = lax.broadcasted_iota(jnp.int32, (half, HEAD_DIM), 1)
    low = lane < HEAD_DIM // 2
    pos = jnp.where(low, pos_ref[0, 0:half, :], pos_ref[0, half:2 * half, :])
    ang = pos.astype(jnp.float32) * inv_ref[...]
    cos = jnp.cos(ang)
    sin = jnp.sin(ang)
    cos_sw = pltpu.roll(cos, HEAD_DIM // 2, axis=1)
    sin_sw = pltpu.roll(sin, HEAD_DIM // 2, axis=1)
    cos_ref[0, 0:half, :] = jnp.where(low, cos, cos_sw)
    cos_ref[0, half:2 * half, :] = jnp.where(low, cos_sw, cos)
    sin_ref[0, 0:half, :] = jnp.where(low, -sin, sin_sw)
    sin_ref[0, half:2 * half, :] = jnp.where(low, -sin_sw, sin)


def _rope_inv_freq():
    inv = ROPE_THETA ** (-jnp.arange(0, HEAD_DIM, 2, dtype=jnp.float32) / HEAD_DIM)
    return jnp.concatenate([inv, inv]).reshape(1, HEAD_DIM)


def _rope_tables(pos, ts):
    nb, n = pos.shape
    inv2 = _rope_inv_freq()
    shp = jax.ShapeDtypeStruct((nb, n, HEAD_DIM), jnp.float32)
    spec = pl.BlockSpec((1, ts, HEAD_DIM), lambda b, i: (b, i, 0))
    return pl.pallas_call(
        _rope_table_kernel,
        out_shape=(shp, shp),
        grid=(nb, n // ts),
        in_specs=[pl.BlockSpec((1, ts, 1), lambda b, i: (b, i, 0)),
                  pl.BlockSpec((1, HEAD_DIM), lambda b, i: (0, 0))],
        out_specs=(spec, spec),
        compiler_params=_cparams(("arbitrary", "arbitrary")),
        name="rope_tables",
    )(pos.reshape(nb, n, 1), inv2)


def _apply_rope(x, cos, sin):
    return x * cos + pltpu.roll(x, HEAD_DIM // 2, axis=1) * sin


IN_TM = 1024
IN_TN = 512
IN_RC = 256
assert FOX_W // IN_TN >= 1


def _inproj_kernel(x_ref, g_ref, sc_ref, sh_ref, w_ref, ws_ref, cos_ref, sin_ref,
                   z_ref, zs_ref, zc_ref, u_ref):
    j = pl.program_id(2)

    @pl.when(j == 0)
    def _():
        for r in range(IN_TM // IN_RC):
            rows = slice(r * IN_RC, (r + 1) * IN_RC)
            u = _rms(x_ref[0, rows, :]) * g_ref[...]
            ub = (u * (1.0 + sc_ref[0]) + sh_ref[0]).astype(jnp.bfloat16)
            u_ref[rows, :] = ub
            zs_ref[0, rows, :] = _dot_nt(ub, ws_ref[...])
            z_ref[0, rows, :] = (_dot_nt(ub, w_ref[...]) * (SCALE * LOG2E)).astype(z_ref.dtype)

    fox_q = (j > 0) & (j < FOX_W // IN_TN)
    nsa_q = (j >= COL_QN // IN_TN) & (j < COL_KS // IN_TN)
    nsa_k = j == COL_KS // IN_TN
    cmp_kv = j == COL_KC // IN_TN

    def matmul():
        return _dot_nt(u_ref[...], w_ref[...])

    def roped(acc):
        cos = cos_ref[0]
        sin = sin_ref[0]
        parts = [_apply_rope(acc[:, h * HEAD_DIM:(h + 1) * HEAD_DIM], cos, sin)
                 for h in range(IN_TN // HEAD_DIM)]
        return jnp.concatenate(parts, axis=1).astype(z_ref.dtype)

    @pl.when(fox_q)
    def _():
        z_ref[0] = (matmul() * (SCALE * LOG2E)).astype(z_ref.dtype)

    @pl.when(nsa_q)
    def _():
        z_ref[0] = roped(matmul() * (SCALE * LOG2E))

    @pl.when(nsa_k)
    def _():
        z_ref[0] = roped(matmul())

    @pl.when(cmp_kv)
    def _():
        acc = matmul()
        z_ref[0] = acc.astype(z_ref.dtype)
        zc_ref[0] = acc

    @pl.when(jnp.logical_not((j == 0) | fox_q | nsa_q | nsa_k | cmp_kv))
    def _():
        z_ref[0] = matmul().astype(z_ref.dtype)


def _inproj(x, norm_w, sc, sh, w_main, w_small, cos, sin):
    nb, s, d = x.shape
    grid = (nb, s // IN_TM, Z_MAIN // IN_TN)
    return pl.pallas_call(
        _inproj_kernel,
        out_shape=(jax.ShapeDtypeStruct((nb, s, Z_MAIN), jnp.bfloat16),
                   jax.ShapeDtypeStruct((nb, s, Z_SMALL), jnp.float32),
                   jax.ShapeDtypeStruct((nb, s, IN_TN), jnp.float32)),
        grid=grid,
        in_specs=[
            pl.BlockSpec((1, IN_TM, d), lambda b, i, j: (b, i, 0)),
            pl.BlockSpec((1, d), lambda b, i, j: (0, 0)),
            pl.BlockSpec((1, 1, d), lambda b, i, j: (b, 0, 0)),
            pl.BlockSpec((1, 1, d), lambda b, i, j: (b, 0, 0)),
            pl.BlockSpec((IN_TN, d), lambda b, i, j: (j, 0)),
            pl.BlockSpec((Z_SMALL, d), lambda b, i, j: (0, 0)),
            pl.BlockSpec((1, IN_TM, HEAD_DIM), lambda b, i, j: (b, i, 0)),
            pl.BlockSpec((1, IN_TM, HEAD_DIM), lambda b, i, j: (b, i, 0)),
        ],
        out_specs=(pl.BlockSpec((1, IN_TM, IN_TN), lambda b, i, j: (b, i, j)),
                   pl.BlockSpec((1, IN_TM, Z_SMALL), lambda b, i, j: (b, i, 0)),
                   pl.BlockSpec((1, IN_TM, IN_TN), lambda b, i, j: (b, i, 0))),
        scratch_shapes=[pltpu.VMEM((IN_TM, d), jnp.bfloat16)],
        compiler_params=_cparams(("arbitrary", "arbitrary", "arbitrary")),
        name="inproj",
    )(x, norm_w.reshape(1, d), sc.reshape(nb, 1, d), sh.reshape(nb, 1, d),
      w_main, w_small, cos, sin)


DEC_T = 2048


def _decay_kernel(zs_ref, bf_ref, cum_ref, carry_ref):
    @pl.when(pl.program_id(1) == 0)
    def _():
        carry_ref[...] = jnp.zeros_like(carry_ref)

    xt = zs_ref[0].T[0:FOX_HEADS, :] + bf_ref[...]
    lf = jnp.minimum(xt, 0.0) - jnp.log1p(jnp.exp(-jnp.abs(xt)))
    lane = lax.broadcasted_iota(jnp.int32, lf.shape, 1)
    sh = 1
    while sh < DEC_T:
        lf = lf + jnp.where(lane >= sh, pltpu.roll(lf, sh, axis=1), 0.0)
        sh *= 2
    cum = lf + carry_ref[:, 0:1]
    cum_ref[0] = cum
    carry_ref[...] = jnp.broadcast_to(cum[:, DEC_T - 1:DEC_T], carry_ref.shape)


def _decay(z_small, b_fgate):
    nb, s, _ = z_small.shape
    return pl.pallas_call(
        _decay_kernel,
        out_shape=jax.ShapeDtypeStruct((nb, FOX_HEADS, s), jnp.float32),
        grid=(nb, s // DEC_T),
        in_specs=[pl.BlockSpec((1, DEC_T, Z_SMALL), lambda b, i: (b, i, 0)),
                  pl.BlockSpec((FOX_HEADS, 1), lambda b, i: (0, 0))],
        out_specs=pl.BlockSpec((1, FOX_HEADS, DEC_T), lambda b, i: (b, 0, i)),
        scratch_shapes=[pltpu.VMEM((FOX_HEADS, 128), jnp.float32)],
        compiler_params=_cparams(("arbitrary", "arbitrary")),
        name="decay",
    )(z_small, b_fgate.reshape(FOX_HEADS, 1))


def _compress_kernel(x_ref, w1_ref, w2_ref, pos_ref, cos_ref, sin_ref, o_ref):
    kind = pl.program_id(1)
    nc = x_ref.shape[1] // CMP_STRIDE
    p = jnp.zeros((nc, CMP_HIDDEN), jnp.float32)
    q = jnp.zeros((nc, CMP_HIDDEN), jnp.float32)
    for l in range(0, CMP_STRIDE, 2):
        xl = jnp.concatenate(
            [x_ref[0, pl.ds(l + k, nc, stride=CMP_STRIDE), :].astype(jnp.bfloat16)
             for k in range(2)], axis=1)
        pair = lambda l0: w1_ref[0, l0:l0 + 2].reshape(2 * HEAD_DIM, CMP_HIDDEN)
        p = p + _dot(xl, pair(l))
        q = q + _dot(xl, pair(CMP_STRIDE + l))
    posb = jnp.broadcast_to(pos_ref[...], (8, CMP_LEN * HEAD_DIM)).astype(jnp.bfloat16)
    w1_flat = w1_ref[0].reshape(CMP_LEN * HEAD_DIM, CMP_HIDDEN)
    pterm = _dot(posb, w1_flat)[0:1, :]
    h = p + pltpu.roll(q, nc - 1, axis=0) + pterm
    a = jax.nn.gelu(h).astype(jnp.bfloat16)
    out = _dot(a, w2_ref[0])

    @pl.when(kind < NSA_KV_HEADS)
    def _():
        o_ref[0, 0] = _apply_rope(out, cos_ref[0], sin_ref[0]).astype(o_ref.dtype)

    @pl.when(kind >= NSA_KV_HEADS)
    def _():
        o_ref[0, 0] = out.astype(o_ref.dtype)


def _compress(zc, w1, w2, cmp_pos, cos_c, sin_c):
    nb, s, _ = zc.shape
    nk = 2 * NSA_KV_HEADS
    nc = s // CMP_STRIDE
    return pl.pallas_call(
        _compress_kernel,
        out_shape=jax.ShapeDtypeStruct((nb, nk, nc, HEAD_DIM), jnp.bfloat16),
        grid=(nb, nk),
        in_specs=[
            pl.BlockSpec((1, s, HEAD_DIM), lambda b, k: (b, 0, k)),
            pl.BlockSpec((1, CMP_LEN, HEAD_DIM, CMP_HIDDEN),
                         lambda b, k: (k // NSA_KV_HEADS, 0, 0, 0)),
            pl.BlockSpec((1, CMP_HIDDEN, HEAD_DIM), lambda b, k: (k // NSA_KV_HEADS, 0, 0)),
            pl.BlockSpec((1, CMP_LEN * HEAD_DIM), lambda b, k: (0, 0)),
            pl.BlockSpec((1, nc, HEAD_DIM), lambda b, k: (b, 0, 0)),
            pl.BlockSpec((1, nc, HEAD_DIM), lambda b, k: (b, 0, 0)),
        ],
        out_specs=pl.BlockSpec((1, 1, nc, HEAD_DIM), lambda b, k: (b, k, 0, 0)),
        compiler_params=_cparams(("arbitrary", "arbitrary")),
        name="compress",
    )(zc, w1, w2, cmp_pos.reshape(1, CMP_LEN * HEAD_DIM), cos_c, sin_c)


FOX_T = 512


def _softmax_step(s, m, acc, v):
    m_new = jnp.maximum(m, jnp.max(s, axis=1, keepdims=True))
    alpha = jnp.exp2(m - m_new)
    p = jnp.exp2((s - m_new).astype(jnp.bfloat16))
    va = jnp.concatenate([v, jnp.ones_like(v)], axis=1)
    return m_new, alpha * acc + _dot(p, va)


def _softmax_init(rows):
    return (jnp.full((rows, 1), MASK_VALUE, jnp.float32),
            jnp.zeros((rows, 2 * HEAD_DIM), jnp.float32))


def _softmax_result(acc):
    return acc[:, 0:HEAD_DIM] * (1.0 / acc[:, HEAD_DIM:2 * HEAD_DIM])


FOX_HB = 4


def _fox_kernel(q_ref, k_ref, v_ref, cum_ref, *rest, n_cast):
    o_ref = rest[n_cast]
    for w_ref, wb_ref in zip(rest[:n_cast], rest[n_cast + 1:]):
        wb_ref[...] = w_ref[...].astype(wb_ref.dtype)
    hb = pl.program_id(1)
    i = pl.program_id(2)
    t = FOX_T
    cols = [slice(n * HEAD_DIM, (n + 1) * HEAD_DIM) for n in range(FOX_HB)]
    qs = [q_ref[0, :, c] for c in cols]

    def tile(k0, width, carry, diag):
        k0 = pl.multiple_of(k0, t)
        logits = [_dot_nt(qs[n], k_ref[0, pl.ds(k0, width), cols[n]]) for n in range(FOX_HB)]
        out = []
        for n, (m, acc) in enumerate(carry):
            v = v_ref[0, pl.ds(k0, width), cols[n]]
            ck = cum_ref[0, pl.ds(hb * FOX_HB + n, 1), pl.ds(k0, width)] * LOG2E
            s = logits[n] - ck
            if diag:
                r = lax.broadcasted_iota(jnp.int32, s.shape, 0)
                c = lax.broadcasted_iota(jnp.int32, s.shape, 1)
                s = jnp.where(c <= r + (width - t), s, MASK_VALUE)
            out.append(_softmax_step(s, m, acc, v))
        return tuple(out)

    init = tuple(_softmax_init(t) for _ in range(FOX_HB))
    n_pairs = i >> 1
    carry = lax.fori_loop(0, n_pairs, lambda p, c: tile(p * (2 * t), 2 * t, c, False), init)
    k_tail = n_pairs * (2 * t)
    carry = lax.cond((i & 1) == 0,
                     lambda c: tile(k_tail, t, c, True),
                     lambda c: tile(k_tail, 2 * t, c, True), carry)
    for n, (_, acc) in enumerate(carry):
        o_ref[0, :, cols[n]] = _softmax_result(acc).astype(o_ref.dtype)


def _fox(z_main, cum, weights):
    nb, s, _ = z_main.shape
    w = FOX_HB * HEAD_DIM
    grid = (nb, FOX_HEADS // FOX_HB, s // FOX_T)
    n_steps = grid[0] * grid[1] * grid[2]
    step = lambda b, h, i: ((b * grid[1] + h) * grid[2] + i, 0)
    slabs = []
    for wt in weights:
        rows = wt.shape[0] // n_steps
        assert rows * n_steps == wt.shape[0] and rows % BF16_SUBLANES == 0, wt.shape
        slabs.append(pl.BlockSpec((rows, wt.shape[1]), step))
    outs = pl.pallas_call(
        functools.partial(_fox_kernel, n_cast=len(weights)),
        out_shape=[jax.ShapeDtypeStruct((nb, s, FOX_W), jnp.bfloat16)]
        + [jax.ShapeDtypeStruct(wt.shape, jnp.bfloat16) for wt in weights],
        grid=grid,
        in_specs=[
            pl.BlockSpec((1, FOX_T, w), lambda b, h, i: (b, i, h)),
            pl.BlockSpec((1, s, w), lambda b, h, i: (b, 0, FOX_W // w + h)),
            pl.BlockSpec((1, s, w), lambda b, h, i: (b, 0, 2 * FOX_W // w + h)),
            pl.BlockSpec((1, FOX_HEADS, s), lambda b, h, i: (b, 0, 0)),
        ] + slabs,
        out_specs=[pl.BlockSpec((1, FOX_T, w), lambda b, h, i: (b, i, h))] + slabs,
        compiler_params=_cparams(("arbitrary", "arbitrary", "arbitrary")),
        name="fox",
    )(z_main, z_main, z_main, cum, *weights)
    return outs[0], outs[1:]


NSA_TQ = 512
NSA_PARTS = 4
NSEL_PAD = 128
SLC_SHIFT = SLC_LEN.bit_length() - 1
assert 1 << SLC_SHIFT == SLC_LEN
assert WINDOW % NSA_TQ == 0 and NSA_TQ % SLC_LEN == 0
SUBLANES = 8
BF16_SUBLANES = 16


def _block_ranks(score, n_slc):
    tq = score.shape[1]
    groups = [score[SUBLANES * v:SUBLANES * (v + 1)] for v in range(n_slc // SUBLANES)]
    ranks = [jnp.zeros((SUBLANES, tq), jnp.float32) for _ in groups]
    for m in range(n_slc):
        row = jnp.broadcast_to(score[m:m + 1, :], (SUBLANES, tq))
        for v, grp in enumerate(groups):
            lo = SUBLANES * v
            if lo > m:
                before = row >= grp
            elif lo + SUBLANES - 1 <= m:
                before = row > grp
            else:
                n_i = lo + lax.broadcasted_iota(jnp.int32, grp.shape, 0)
                before = (row > grp) | ((row == grp) & (n_i > m))
            ranks[v] = ranks[v] + jnp.where(before, 1.0, 0.0)
    return jnp.concatenate(ranks, axis=0)


def _nsa_kernel(q_ref, zs_ref, kc_ref, vc_ref, ks_ref, vs_ref, kw_ref, vw_ref, ovt_ref,
                o_ref, ka_ref, *, seq):
    g = pl.program_id(1)
    i = pl.program_id(2)
    tq = NSA_TQ
    q0 = i * tq

    @pl.when(i == 0)
    def _():
        ka_ref[:, 0:HEAD_DIM] = ks_ref[0]
        tok = lax.broadcasted_iota(jnp.int32, (seq, NSEL_PAD), 0)
        n_i = lax.broadcasted_iota(jnp.int32, (seq, NSEL_PAD), 1)
        ka_ref[:, HEAD_DIM:HEAD_DIM + NSEL_PAD] = jnp.where(
            n_i == (tok >> SLC_SHIFT), 1.0, 0.0).astype(jnp.bfloat16)

    qb = q_ref[0]
    hpp = NSA_GROUP // NSA_PARTS
    parts = range(NSA_PARTS)
    qh = [jnp.concatenate(
        [qb[:, r * HEAD_DIM:(r + 1) * HEAD_DIM] for r in range(h * hpp, (h + 1) * hpp)], axis=0)
        for h in parts]
    prow = hpp * tq

    def row_query(shape):
        return lax.broadcasted_iota(jnp.int32, shape, 0) & (tq - 1)

    kc = kc_ref[0, 0]
    vc = vc_ref[0, 0]
    s_cs = [_dot_nt(qh[h], kc) for h in parts]
    p_cs = []
    for s_c in s_cs:
        cend = lax.broadcasted_iota(jnp.int32, s_c.shape, 1) * CMP_STRIDE + (CMP_LEN - 1)
        s_c = jnp.where(cend <= q0 + row_query(s_c.shape), s_c, MASK_VALUE)
        m_c = jnp.maximum(jnp.max(s_c, axis=1, keepdims=True), 0.1 * MASK_VALUE)
        e_c = jnp.exp2(s_c - m_c)
        p_cs.append(e_c * (1.0 / jnp.maximum(jnp.sum(e_c, axis=1, keepdims=True), 1e-30)))
    o_c = [_dot(p_c.astype(jnp.bfloat16), vc) for p_c in p_cs]

    psum = None
    for p_c in p_cs:
        for r in range(hpp):
            term = p_c[r * tq:(r + 1) * tq]
            psum = term if psum is None else psum + term
    p_hi = psum.astype(jnp.bfloat16)
    p_lo = (psum - p_hi.astype(jnp.float32)).astype(jnp.bfloat16)
    ovt = ovt_ref[...]
    imp_t = _dot_nt(ovt, p_hi) + _dot_nt(ovt, p_lo)
    n_slc = seq // SLC_LEN
    blk = lax.broadcasted_iota(jnp.int32, imp_t.shape, 0)
    cur = (q0 + lax.broadcasted_iota(jnp.int32, imp_t.shape, 1)) >> SLC_SHIFT
    forced = (blk == 0) | (blk == cur) | (blk == cur - 1)
    score = jnp.where(forced, SEL_BONUS, imp_t)
    score = jnp.where(blk <= cur, score, -SEL_BONUS)
    rank = _block_ranks(score, n_slc)
    bias_t = jnp.where(rank < min(SLC_TOPK, n_slc), 0.0, MASK_VALUE)
    bias_t = jnp.concatenate(
        [bias_t, jnp.zeros((NSEL_PAD - n_slc, tq), jnp.float32)], axis=0)
    bias = bias_t.T.astype(jnp.bfloat16)
    bias_p = jnp.concatenate([bias] * hpp, axis=0)
    qa = [jnp.concatenate([qh[h], bias_p], axis=1) for h in parts]

    def sel_tile(k0, width, carry, diag):
        k0 = pl.multiple_of(k0, tq)
        logits = [_dot_nt(qa[h], ka_ref[pl.ds(k0, width), :]) for h in parts]
        out = []
        for s, (m, acc) in zip(logits, carry):
            if diag:
                c = lax.broadcasted_iota(jnp.int32, s.shape, 1)
                s = jnp.where(c <= row_query(s.shape) + (width - tq), s, MASK_VALUE)
            out.append(_softmax_step(s, m, acc, vs_ref[0, pl.ds(k0, width), :]))
        return tuple(out)

    init = tuple(_softmax_init(prow) for _ in parts)
    n_pairs = i >> 1
    carry = lax.fori_loop(0, n_pairs,
                          lambda p, c: sel_tile(p * (2 * tq), 2 * tq, c, False), init)
    k_tail = n_pairs * (2 * tq)
    carry = lax.cond((i & 1) == 0,
                     lambda c: sel_tile(k_tail, tq, c, True),
                     lambda c: sel_tile(k_tail, 2 * tq, c, True), carry)
    o_s = [_softmax_result(acc) for _, acc in carry]

    n_w = WINDOW // tq + 1
    starts = [q0 - WINDOW + t * tq for t in range(n_w)]
    offs = [pl.multiple_of(jnp.maximum(st, 0), tq) for st in starts]
    s_w = [[_dot_nt(qh[h], kw_ref[0, pl.ds(offs[t], tq), :]) for h in parts] for t in range(n_w)]
    for t in range(n_w):
        for h in parts:
            s = s_w[t][h]
            c = lax.broadcasted_iota(jnp.int32, s.shape, 1)
            if t == 0:
                s = jnp.where(c > row_query(s.shape), s, MASK_VALUE)
            if t == n_w - 1:
                s = jnp.where(c <= row_query(s.shape), s, MASK_VALUE)
            else:
                s = jnp.where(starts[t] >= 0, s, MASK_VALUE)
            s_w[t][h] = s
    o_w = []
    for h in parts:
        m_w = s_w[0][h].max(axis=1, keepdims=True)
        for t in range(1, n_w):
            m_w = jnp.maximum(m_w, s_w[t][h].max(axis=1, keepdims=True))
        acc_w = jnp.zeros((prow, 2 * HEAD_DIM), jnp.float32)
        for t in range(n_w):
            p = jnp.exp2((s_w[t][h] - m_w).astype(jnp.bfloat16))
            v = vw_ref[0, pl.ds(offs[t], tq), :]
            acc_w = acc_w + _dot(p, jnp.concatenate([v, jnp.ones_like(v)], axis=1))
        o_w.append(_softmax_result(acc_w))

    gates = jax.nn.sigmoid(zs_ref[0])
    lane = lax.broadcasted_iota(jnp.int32, gates.shape, 1)
    outs = []
    for r in range(NSA_GROUP):
        h, rows = r // hpp, slice((r % hpp) * tq, (r % hpp + 1) * tq)
        o_r = jnp.zeros((tq, HEAD_DIM), jnp.float32)
        for br, o_b in enumerate((o_c, o_s, o_w)):
            col = FOX_HEADS + 3 * (g * NSA_GROUP + r) + br
            gcol = jnp.sum(jnp.where(lane == col, gates, 0.0), axis=1, keepdims=True)
            o_r = o_r + gcol * o_b[h][rows]
        outs.append(o_r)
    o_ref[0] = jnp.concatenate(outs, axis=1).astype(o_ref.dtype)


def _nsa(z_main, z_small, kvc, ovt):
    nb, s, _ = z_main.shape
    nc = kvc.shape[2]
    gw = NSA_GROUP * HEAD_DIM
    full = lambda col0: pl.BlockSpec(
        (1, s, HEAD_DIM), lambda b, g, i, c=col0 // HEAD_DIM: (b, 0, c + g))
    return pl.pallas_call(
        functools.partial(_nsa_kernel, seq=s),
        out_shape=jax.ShapeDtypeStruct((nb, s, NSA_W), jnp.bfloat16),
        grid=(nb, NSA_KV_HEADS, s // NSA_TQ),
        in_specs=[
            pl.BlockSpec((1, NSA_TQ, gw), lambda b, g, i: (b, i, COL_QN // gw + g)),
            pl.BlockSpec((1, NSA_TQ, Z_SMALL), lambda b, g, i: (b, i, 0)),
            pl.BlockSpec((1, 1, nc, HEAD_DIM), lambda b, g, i: (b, g, 0, 0)),
            pl.BlockSpec((1, 1, nc, HEAD_DIM), lambda b, g, i: (b, NSA_KV_HEADS + g, 0, 0)),
            full(COL_KS), full(COL_VS), full(COL_KW), full(COL_VW),
            pl.BlockSpec(ovt.shape, lambda b, g, i: (0, 0)),
        ],
        out_specs=pl.BlockSpec((1, NSA_TQ, gw), lambda b, g, i: (b, i, g)),
        scratch_shapes=[pltpu.VMEM((s, HEAD_DIM + NSEL_PAD), jnp.bfloat16)],
        compiler_params=_cparams(("arbitrary", "arbitrary", "arbitrary")),
        name="nsa",
    )(z_main, z_small, kvc, kvc, z_main, z_main, z_main, z_main, ovt)


OUT_TM = 512
OUT_RC = 256


def _outproj_kernel(of_ref, on_ref, bf_ref, bn_ref, w_ref, x_ref, g_ref, nw_ref, sc_ref,
                    sh_ref, h_ref, u_ref):
    for r in range(OUT_TM // OUT_RC):
        rows = slice(r * OUT_RC, (r + 1) * OUT_RC)
        yf = _rms(of_ref[0, rows, :].astype(jnp.float32)) * bf_ref[...]
        yn = _rms(on_ref[0, rows, :].astype(jnp.float32)) * bn_ref[...]
        y = jnp.concatenate([yf.astype(jnp.bfloat16), yn.astype(jnp.bfloat16)], axis=1)
        h = x_ref[0, rows, :] + g_ref[0] * _dot(y, w_ref[...])
        h_ref[0, rows, :] = h
        u = _rms(h) * nw_ref[...]
        u_ref[0, rows, :] = (u * (1.0 + sc_ref[0]) + sh_ref[0]).astype(jnp.bfloat16)


def _outproj(o_f, o_n, beta_f, beta_n, w_out, x, gate, norm_w, sc, sh):
    nb, s, d = x.shape
    vec = lambda n: pl.BlockSpec((1, n), lambda b, i: (0, 0))
    bvec = pl.BlockSpec((1, 1, d), lambda b, i: (b, 0, 0))
    rows = lambda n: pl.BlockSpec((1, OUT_TM, n), lambda b, i: (b, i, 0))
    return pl.pallas_call(
        _outproj_kernel,
        out_shape=(jax.ShapeDtypeStruct((nb, s, d), jnp.float32),
                   jax.ShapeDtypeStruct((nb, s, d), jnp.bfloat16)),
        grid=(nb, s // OUT_TM),
        in_specs=[
            rows(FOX_W), rows(NSA_W), vec(FOX_W), vec(NSA_W),
            pl.BlockSpec((FOX_W + NSA_W, d), lambda b, i: (0, 0),
                         pipeline_mode=pl.Buffered(1)),
            rows(d), bvec, vec(d), bvec, bvec,
        ],
        out_specs=(rows(d), rows(d)),
        compiler_params=_cparams(("arbitrary", "arbitrary")),
        name="outproj",
    )(o_f, o_n, beta_f.reshape(1, FOX_W), beta_n.reshape(1, NSA_W), w_out, x,
      gate.reshape(nb, 1, d), norm_w.reshape(1, d), sc.reshape(nb, 1, d),
      sh.reshape(nb, 1, d))


FFN_TM = 1024
FFN_TF = 512
FFN_RC = 512
FFN_RC_LAST = 256


def _ffn_kernel(u_ref, h_ref, g_ref, fw_ref, wg_ref, wu_ref, wd_ref, o_ref, *, final):
    j = pl.program_id(2)
    last = pl.num_programs(2) - 1

    def partial_sum(rows):
        u = u_ref[0, rows, :]
        a = _dot(u, wg_ref[...])
        b = _dot(u, wu_ref[...])
        mid = (a * jax.nn.sigmoid(a) * b).astype(jnp.bfloat16)
        return _dot(mid, wd_ref[...])

    def chunks(rc):
        return [slice(r * rc, (r + 1) * rc) for r in range(FFN_TM // rc)]

    @pl.when(j == 0)
    def _():
        for rows in chunks(FFN_RC):
            o_ref[0, rows, :] = partial_sum(rows)

    @pl.when((j > 0) & (j < last))
    def _():
        for rows in chunks(FFN_RC):
            o_ref[0, rows, :] += partial_sum(rows)

    @pl.when(j == last)
    def _():
        for rows in chunks(FFN_RC_LAST):
            h2 = h_ref[0, rows, :] + g_ref[0] * (o_ref[0, rows, :] + partial_sum(rows))
            o_ref[0, rows, :] = _rms(h2) * fw_ref[...] if final else h2


def _ffn(u2, h1, gate, final_w, w_gate, w_up, w_down, final):
    nb, s, d = h1.shape
    f = w_gate.shape[1]
    return pl.pallas_call(
        functools.partial(_ffn_kernel, final=final),
        out_shape=jax.ShapeDtypeStruct((nb, s, d), jnp.float32),
        grid=(nb, s // FFN_TM, f // FFN_TF),
        in_specs=[
            pl.BlockSpec((1, FFN_TM, d), lambda b, i, j: (b, i, 0)),
            pl.BlockSpec((1, FFN_TM, d), lambda b, i, j: (b, i, 0),
                         pipeline_mode=pl.Buffered(1)),
            pl.BlockSpec((1, 1, d), lambda b, i, j: (b, 0, 0)),
            pl.BlockSpec((1, d), lambda b, i, j: (0, 0)),
            pl.BlockSpec((d, FFN_TF), lambda b, i, j: (0, j)),
            pl.BlockSpec((d, FFN_TF), lambda b, i, j: (0, j)),
            pl.BlockSpec((FFN_TF, d), lambda b, i, j: (j, 0)),
        ],
        out_specs=pl.BlockSpec((1, FFN_TM, d), lambda b, i, j: (b, i, 0)),
        compiler_params=_cparams(("arbitrary", "arbitrary", "arbitrary"), BIG_VMEM_LIMIT),
        name="ffn",
    )(u2, h1, gate.reshape(nb, 1, d), final_w.reshape(1, d), w_gate, w_up, w_down)


def _split_w_in(w_in):
    sizes = ([FOX_W] * 3 + [FOX_HEADS] + [NSA_W] + [KV_W] * 6 + [3 * NSA_HEADS])
    offs = [0]
    for sz in sizes:
        offs.append(offs[-1] + sz)
    wt = w_in.T
    order = (0, 1, 2, 4, 7, 9, 8, 10, 5, 6)
    units = lambda ns, unit: [offs[n] + r for n in ns for r in range(0, sizes[n], unit)]
    main = _gather_rows(wt, units(order, KV_W), KV_W, jnp.bfloat16)
    narrow = _gather_rows(wt, units((3, 11), SUBLANES), SUBLANES, jnp.float32)
    small = jnp.concatenate(
        [narrow, jnp.zeros((Z_SMALL - narrow.shape[0], w_in.shape[0]), jnp.float32)], axis=0)
    return main, small.astype(jnp.bfloat16)


def _cast_kernel(rows_ref, w_ref, o_ref):
    del rows_ref
    o_ref[...] = w_ref[...].astype(o_ref.dtype)


def _gather_rows(wt, src_rows, unit, dtype):
    assert all(r % SUBLANES == 0 for r in src_rows), src_rows
    tiles = jnp.asarray([r // SUBLANES for r in src_rows], jnp.int32)
    n_units = len(src_rows)
    d = wt.shape[1]
    return pl.pallas_call(
        _cast_kernel,
        out_shape=jax.ShapeDtypeStruct((n_units * unit, d), dtype),
        grid_spec=pltpu.PrefetchScalarGridSpec(
            num_scalar_prefetch=1,
            grid=(n_units,),
            in_specs=[pl.BlockSpec((pl.Element(unit), pl.Element(d)),
                                   lambda u, rows: (rows[u] * SUBLANES, 0))],
            out_specs=pl.BlockSpec((unit, d), lambda u, rows: (u, 0)),
        ),
        compiler_params=_cparams(("arbitrary",)),
        name="w_in_prep",
    )(tiles, wt)


def _overlap_t(n_cmp_pad, n_slc):
    cs = jnp.arange(n_cmp_pad) * CMP_STRIDE
    ss = jnp.arange(n_slc) * SLC_LEN
    ov = (jnp.minimum(cs[None, :] + CMP_LEN, ss[:, None] + SLC_LEN)
          - jnp.maximum(cs[None, :], ss[:, None]))
    ov = jnp.clip(ov, 0).astype(jnp.float32) / CMP_STRIDE
    return ov.astype(jnp.bfloat16)


def kernel(x, c, positions, w_ada, b_ada, norm_attn, norm_ffn, w_in, b_fgate, cmp_pos,
           w_kc1, w_kc2, w_vc1, w_vc2, beta_fox, beta_nsa, w_out, w_gate, w_up, w_down,
           final_norm):
    nb, s, d = x.shape
    depth = w_ada.shape[0]
    n_cmp_pad = s // CMP_STRIDE
    cmp_end = jnp.arange(n_cmp_pad) * CMP_STRIDE + (CMP_LEN - 1)
    pos_c = jnp.take(positions, jnp.minimum(cmp_end, s - 1), axis=1)
    cos_c, sin_c = _rope_tables(pos_c, n_cmp_pad)
    ovt = _overlap_t(n_cmp_pad, s // SLC_LEN)

    h = x
    for l in range(depth):
        if l == 0:
            mod, cos, sin = _ada(c, w_ada[l], b_ada[l], positions)
        else:
            mod = _ada(c, w_ada[l], b_ada[l])
        sh1, sc1, g1, sh2, sc2, g2 = jnp.split(mod, 6, axis=-1)
        w_main, w_small = _split_w_in(w_in[l])
        z_main, z_small, zc = _inproj(h, norm_attn[l], sc1, sh1, w_main, w_small, cos, sin)
        cum = _decay(z_small, b_fgate[l])

        w1 = jnp.stack([w_kc1[l], w_vc1[l]]).astype(jnp.bfloat16)
        w1 = w1.reshape(2, CMP_LEN, HEAD_DIM, CMP_HIDDEN)
        w2 = jnp.stack([w_kc2[l], w_vc2[l]]).astype(jnp.bfloat16)
        kvc = _compress(zc, w1, w2, cmp_pos[l], cos_c, sin_c)

        o_f, (wo, wg, wu, wd) = _fox(z_main, cum, (w_out[l], w_gate[l], w_up[l], w_down[l]))
        o_n = _nsa(z_main, z_small, kvc, ovt)
        h1, u2 = _outproj(o_f, o_n, beta_fox[l], beta_nsa[l], wo, h, g1, norm_ffn[l], sc2, sh2)
        h = _ffn(u2, h1, g2, final_norm, wg, wu, wd, final=(l == depth - 1))
    return h
```

```python
import functools

import jax
import jax.numpy as jnp
from jax import lax
from jax.experimental import pallas as pl
from jax.experimental.pallas import tpu as pltpu

D_MODEL = 2048
HEAD_DIM = 128
FOX_HEADS = 8
NSA_HEADS = 8
NSA_KV_HEADS = 2
NSA_GROUP = NSA_HEADS // NSA_KV_HEADS
CMP_LEN = 32
CMP_STRIDE = 16
CMP_HIDDEN = 2 * HEAD_DIM
SLC_LEN = 64
SLC_TOPK = 16
WINDOW = 512
ROPE_THETA = 10000.0
NORM_EPS = 1e-6
MASK_VALUE = -1e30
SEL_BONUS = 1e6

FOX_W = FOX_HEADS * HEAD_DIM
NSA_W = NSA_HEADS * HEAD_DIM
KV_W = NSA_KV_HEADS * HEAD_DIM
Z_MAIN = 3 * FOX_W + NSA_W + 6 * KV_W
Z_SMALL = 128
COL_QN = 3 * FOX_W
COL_KS = COL_QN + NSA_W
COL_KW = COL_KS + KV_W
COL_VS = COL_KW + KV_W
COL_VW = COL_VS + KV_W
COL_KC = COL_VW + KV_W

VMEM_LIMIT = 48 * 1024 * 1024
BIG_VMEM_LIMIT = 56 * 1024 * 1024
SCALE = HEAD_DIM ** -0.5
LOG2E = 1.4426950408889634


def _cparams(sem, vmem_limit=VMEM_LIMIT):
    return pltpu.CompilerParams(dimension_semantics=sem, vmem_limit_bytes=vmem_limit)


def _dot(a, b):
    return jnp.dot(a, b, preferred_element_type=jnp.float32)


def _dot_nt(a, b):
    return lax.dot_general(a, b, (((1,), (1,)), ((), ())), preferred_element_type=jnp.float32)


def _rms(x):
    return x * lax.rsqrt(jnp.mean(x * x, axis=-1, keepdims=True) + NORM_EPS)


ADA_TN = 768
ADA_KC = 256


def _ada_kernel(ct_ref, w_ref, b_ref, *rest, nb, with_rope):
    if with_rope:
        pos_ref, inv_ref, o_ref, cos_ref, sin_ref = rest
        _rope_table_kernel(pos_ref, inv_ref, cos_ref, sin_ref)
    else:
        o_ref, = rest
    tn = w_ref.shape[1]
    ct = ct_ref[...]
    st = ct * jax.nn.sigmoid(ct)

    accs = [jnp.zeros((SUBLANES, tn), jnp.float32) for _ in range(nb)]
    for kc in range(D_MODEL // ADA_KC):
        w = w_ref[kc * ADA_KC:(kc + 1) * ADA_KC, :]
        for b in range(nb):
            s = st[kc * ADA_KC:(kc + 1) * ADA_KC, b:b + 1]
            accs[b] = accs[b] + jnp.sum(
                (w * s).reshape(ADA_KC // SUBLANES, SUBLANES, tn), axis=0)
    for b in range(nb):
        o_ref[b:b + 1, :] = jnp.sum(accs[b], axis=0, keepdims=True) + b_ref[...]


def _ada(c, w_ada, b_ada, positions=None):
    nb = c.shape[0]
    n = w_ada.shape[1]
    n_steps = n // ADA_TN
    assert n_steps * ADA_TN == n
    in_specs = [
        pl.BlockSpec((D_MODEL, nb), lambda j: (0, 0)),
        pl.BlockSpec((D_MODEL, ADA_TN), lambda j: (0, j)),
        pl.BlockSpec((1, ADA_TN), lambda j: (0, j)),
    ]
    out_shape = [jax.ShapeDtypeStruct((nb, n), jnp.float32)]
    out_specs = [pl.BlockSpec((nb, ADA_TN), lambda j: (0, j))]
    args = [c.T, w_ada, b_ada.reshape(1, n)]
    if positions is not None:
        s = positions.shape[1]
        per_batch = n_steps // nb
        ts = s // per_batch
        assert per_batch * nb == n_steps and ts * per_batch == s and ts % (2 * SUBLANES) == 0
        rows = lambda j: (j // per_batch, j % per_batch, 0)
        in_specs += [pl.BlockSpec((1, ts, 1), rows), pl.BlockSpec((1, HEAD_DIM), lambda j: (0, 0))]
        tab = jax.ShapeDtypeStruct((nb, s, HEAD_DIM), jnp.float32)
        out_shape += [tab, tab]
        out_specs += [pl.BlockSpec((1, ts, HEAD_DIM), rows)] * 2
        args += [positions.reshape(nb, s, 1), _rope_inv_freq()]
    outs = pl.pallas_call(
        functools.partial(_ada_kernel, nb=nb, with_rope=positions is not None),
        out_shape=out_shape,
        grid=(n_steps,),
        in_specs=in_specs,
        out_specs=out_specs,
        compiler_params=_cparams(("arbitrary",)),
        name="ada",
    )(*args)
    return outs if positions is not None else outs[0]


def _rope_table_kernel(pos_ref, inv_ref, cos_ref, sin_ref):
    half = pos_ref.shape[1] // 2
    lane = lax.broadcasted_iota(jnp.int32, (half, HEAD_DIM), 1)
    low = lane < HEAD_DIM // 2
    pos = jnp.where(low, pos_ref[0, 0:half, :], pos_ref[0, half:2 * half, :])
    ang = pos.astype(jnp.float32) * inv_ref[...]
    cos = jnp.cos(ang)
    sin = jnp.sin(ang)
    cos_sw = pltpu.roll(cos, HEAD_DIM // 2, axis=1)
    sin_sw = pltpu.roll(sin, HEAD_DIM // 2, axis=1)
    cos_ref[0, 0:half, :] = jnp.where(low, cos, cos_sw)
    cos_ref[0, half:2 * half, :] = jnp.where(low, cos_sw, cos)
    sin_ref[0, 0:half, :] = jnp.where(low, -sin, sin_sw)
    sin_ref[0, half:2 * half, :] = jnp.where(low, -sin_sw, sin)


def _rope_inv_freq():
    inv = ROPE_THETA ** (-jnp.arange(0, HEAD_DIM, 2, dtype=jnp.float32) / HEAD_DIM)
    return jnp.concatenate([inv, inv]).reshape(1, HEAD_DIM)


def _rope_tables(pos, ts):
    nb, n = pos.shape
    inv2 = _rope_inv_freq()
    shp = jax.ShapeDtypeStruct((nb, n, HEAD_DIM), jnp.float32)
    spec = pl.BlockSpec((1, ts, HEAD_DIM), lambda b, i: (b, i, 0))
    return pl.pallas_call(
        _rope_table_kernel,
        out_shape=(shp, shp),
        grid=(nb, n // ts),
        in_specs=[pl.BlockSpec((1, ts, 1), lambda b, i: (b, i, 0)),
                  pl.BlockSpec((1, HEAD_DIM), lambda b, i: (0, 0))],
        out_specs=(spec, spec),
        compiler_params=_cparams(("arbitrary", "arbitrary")),
        name="rope_tables",
    )(pos.reshape(nb, n, 1), inv2)


def _apply_rope(x, cos, sin):
    return x * cos + pltpu.roll(x, HEAD_DIM // 2, axis=1) * sin


IN_TM = 1024
IN_TN = 512
IN_RC = 256
assert FOX_W // IN_TN >= 1


def _inproj_kernel(x_ref, g_ref, sc_ref, sh_ref, w_ref, ws_ref, cos_ref, sin_ref,
                   z_ref, zs_ref, zc_ref, u_ref):
    j = pl.program_id(2)

    @pl.when(j == 0)
    def _():
        for r in range(IN_TM // IN_RC):
            rows = slice(r * IN_RC, (r + 1) * IN_RC)
            u = _rms(x_ref[0, rows, :]) * g_ref[...]
            ub = (u * (1.0 + sc_ref[0]) + sh_ref[0]).astype(jnp.bfloat16)
            u_ref[rows, :] = ub
            zs_ref[0, rows, :] = _dot_nt(ub, ws_ref[...])
            z_ref[0, rows, :] = (_dot_nt(ub, w_ref[...]) * (SCALE * LOG2E)).astype(z_ref.dtype)

    fox_q = (j > 0) & (j < FOX_W // IN_TN)
    nsa_q = (j >= COL_QN // IN_TN) & (j < COL_KS // IN_TN)
    nsa_k = j == COL_KS // IN_TN
    cmp_kv = j == COL_KC // IN_TN

    def matmul():
        return _dot_nt(u_ref[...], w_ref[...])

    def roped(acc):
        cos = cos_ref[0]
        sin = sin_ref[0]
        parts = [_apply_rope(acc[:, h * HEAD_DIM:(h + 1) * HEAD_DIM], cos, sin)
                 for h in range(IN_TN // HEAD_DIM)]
        return jnp.concatenate(parts, axis=1).astype(z_ref.dtype)

    @pl.when(fox_q)
    def _():
        z_ref[0] = (matmul() * (SCALE * LOG2E)).astype(z_ref.dtype)

    @pl.when(nsa_q)
    def _():
        z_ref[0] = roped(matmul() * (SCALE * LOG2E))

    @pl.when(nsa_k)
    def _():
        z_ref[0] = roped(matmul())

    @pl.when(cmp_kv)
    def _():
        acc = matmul()
        z_ref[0] = acc.astype(z_ref.dtype)
        zc_ref[0] = acc

    @pl.when(jnp.logical_not((j == 0) | fox_q | nsa_q | nsa_k | cmp_kv))
    def _():
        z_ref[0] = matmul().astype(z_ref.dtype)


def _inproj(x, norm_w, sc, sh, w_main, w_small, cos, sin):
    nb, s, d = x.shape
    grid = (nb, s // IN_TM, Z_MAIN // IN_TN)
    return pl.pallas_call(
        _inproj_kernel,
        out_shape=(jax.ShapeDtypeStruct((nb, s, Z_MAIN), jnp.bfloat16),
                   jax.ShapeDtypeStruct((nb, s, Z_SMALL), jnp.float32),
                   jax.ShapeDtypeStruct((nb, s, IN_TN), jnp.float32)),
        grid=grid,
        in_specs=[
            pl.BlockSpec((1, IN_TM, d), lambda b, i, j: (b, i, 0)),
            pl.BlockSpec((1, d), lambda b, i, j: (0, 0)),
            pl.BlockSpec((1, 1, d), lambda b, i, j: (b, 0, 0)),
            pl.BlockSpec((1, 1, d), lambda b, i, j: (b, 0, 0)),
            pl.BlockSpec((IN_TN, d), lambda b, i, j: (j, 0)),
            pl.BlockSpec((Z_SMALL, d), lambda b, i, j: (0, 0)),
            pl.BlockSpec((1, IN_TM, HEAD_DIM), lambda b, i, j: (b, i, 0)),
            pl.BlockSpec((1, IN_TM, HEAD_DIM), lambda b, i, j: (b, i, 0)),
        ],
        out_specs=(pl.BlockSpec((1, IN_TM, IN_TN), lambda b, i, j: (b, i, j)),
                   pl.BlockSpec((1, IN_TM, Z_SMALL), lambda b, i, j: (b, i, 0)),
                   pl.BlockSpec((1, IN_TM, IN_TN), lambda b, i, j: (b, i, 0))),
        scratch_shapes=[pltpu.VMEM((IN_TM, d), jnp.bfloat16)],
        compiler_params=_cparams(("arbitrary", "arbitrary", "arbitrary")),
        name="inproj",
    )(x, norm_w.reshape(1, d), sc.reshape(nb, 1, d), sh.reshape(nb, 1, d),
      w_main, w_small, cos, sin)


DEC_T = 2048


def _decay_kernel(zs_ref, bf_ref, cum_ref, carry_ref):
    @pl.when(pl.program_id(1) == 0)
    def _():
        carry_ref[...] = jnp.zeros_like(carry_ref)

    xt = zs_ref[0].T[0:FOX_HEADS, :] + bf_ref[...]
    lf = jnp.minimum(xt, 0.0) - jnp.log1p(jnp.exp(-jnp.abs(xt)))
    lane = lax.broadcasted_iota(jnp.int32, lf.shape, 1)
    sh = 1
    while sh < DEC_T:
        lf = lf + jnp.where(lane >= sh, pltpu.roll(lf, sh, axis=1), 0.0)
        sh *= 2
    cum = lf + carry_ref[:, 0:1]
    cum_ref[0] = cum
    carry_ref[...] = jnp.broadcast_to(cum[:, DEC_T - 1:DEC_T], carry_ref.shape)


def _decay(z_small, b_fgate):
    nb, s, _ = z_small.shape
    return pl.pallas_call(
        _decay_kernel,
        out_shape=jax.ShapeDtypeStruct((nb, FOX_HEADS, s), jnp.float32),
        grid=(nb, s // DEC_T),
        in_specs=[pl.BlockSpec((1, DEC_T, Z_SMALL), lambda b, i: (b, i, 0)),
                  pl.BlockSpec((FOX_HEADS, 1), lambda b, i: (0, 0))],
        out_specs=pl.BlockSpec((1, FOX_HEADS, DEC_T), lambda b, i: (b, 0, i)),
        scratch_shapes=[pltpu.VMEM((FOX_HEADS, 128), jnp.float32)],
        compiler_params=_cparams(("arbitrary", "arbitrary")),
        name="decay",
    )(z_small, b_fgate.reshape(FOX_HEADS, 1))


def _compress_kernel(x_ref, w1_ref, w2_ref, pos_ref, cos_ref, sin_ref, o_ref):
    kind = pl.program_id(1)
    nc = x_ref.shape[1] // CMP_STRIDE
    p = jnp.zeros((nc, CMP_HIDDEN), jnp.float32)
    q = jnp.zeros((nc, CMP_HIDDEN), jnp.float32)
    for l in range(0, CMP_STRIDE, 2):
        xl = jnp.concatenate(
            [x_ref[0, pl.ds(l + k, nc, stride=CMP_STRIDE), :].astype(jnp.bfloat16)
             for k in range(2)], axis=1)
        pair = lambda l0: w1_ref[0, l0:l0 + 2].reshape(2 * HEAD_DIM, CMP_HIDDEN)
        p = p + _dot(xl, pair(l))
        q = q + _dot(xl, pair(CMP_STRIDE + l))
    posb = jnp.broadcast_to(pos_ref[...], (8, CMP_LEN * HEAD_DIM)).astype(jnp.bfloat16)
    w1_flat = w1_ref[0].reshape(CMP_LEN * HEAD_DIM, CMP_HIDDEN)
    pterm = _dot(posb, w1_flat)[0:1, :]
    h = p + pltpu.roll(q, nc - 1, axis=0) + pterm
    a = jax.nn.gelu(h).astype(jnp.bfloat16)
    out = _dot(a, w2_ref[0])

    @pl.when(kind < NSA_KV_HEADS)
    def _():
        o_ref[0, 0] = _apply_rope(out, cos_ref[0], sin_ref[0]).astype(o_ref.dtype)

    @pl.when(kind >= NSA_KV_HEADS)
    def _():
        o_ref[0, 0] = out.astype(o_ref.dtype)


def _compress(zc, w1, w2, cmp_pos, cos_c, sin_c):
    nb, s, _ = zc.shape
    nk = 2 * NSA_KV_HEADS
    nc = s // CMP_STRIDE
    return pl.pallas_call(
        _compress_kernel,
        out_shape=jax.ShapeDtypeStruct((nb, nk, nc, HEAD_DIM), jnp.bfloat16),
        grid=(nb, nk),
        in_specs=[
            pl.BlockSpec((1, s, HEAD_DIM), lambda b, k: (b, 0, k)),
            pl.BlockSpec((1, CMP_LEN, HEAD_DIM, CMP_HIDDEN),
                         lambda b, k: (k // NSA_KV_HEADS, 0, 0, 0)),
            pl.BlockSpec((1, CMP_HIDDEN, HEAD_DIM), lambda b, k: (k // NSA_KV_HEADS, 0, 0)),
            pl.BlockSpec((1, CMP_LEN * HEAD_DIM), lambda b, k: (0, 0)),
            pl.BlockSpec((1, nc, HEAD_DIM), lambda b, k: (b, 0, 0)),
            pl.BlockSpec((1, nc, HEAD_DIM), lambda b, k: (b, 0, 0)),
        ],
        out_specs=pl.BlockSpec((1, 1, nc, HEAD_DIM), lambda b, k: (b, k, 0, 0)),
        compiler_params=_cparams(("arbitrary", "arbitrary")),
        name="compress",
    )(zc, w1, w2, cmp_pos.reshape(1, CMP_LEN * HEAD_DIM), cos_c, sin_c)


FOX_T = 512


def _softmax_step(s, m, acc, v):
    m_new = jnp.maximum(m, jnp.max(s, axis=1, keepdims=True))
    alpha = jnp.exp2(m - m_new)
    p = jnp.exp2((s - m_new).astype(jnp.bfloat16))
    va = jnp.concatenate([v, jnp.ones_like(v)], axis=1)
    return m_new, alpha * acc + _dot(p, va)


def _softmax_init(rows):
    return (jnp.full((rows, 1), MASK_VALUE, jnp.float32),
            jnp.zeros((rows, 2 * HEAD_DIM), jnp.float32))


def _softmax_result(acc):
    return acc[:, 0:HEAD_DIM] * (1.0 / acc[:, HEAD_DIM:2 * HEAD_DIM])


FOX_HB = 4


def _fox_kernel(q_ref, k_ref, v_ref, cum_ref, *rest, n_cast):
    o_ref = rest[n_cast]
    for w_ref, wb_ref in zip(rest[:n_cast], rest[n_cast + 1:]):
        wb_ref[...] = w_ref[...].astype(wb_ref.dtype)
    hb = pl.program_id(1)
    i = pl.program_id(2)
    t = FOX_T
    cols = [slice(n * HEAD_DIM, (n + 1) * HEAD_DIM) for n in range(FOX_HB)]
    qs = [q_ref[0, :, c] for c in cols]

    def tile(k0, width, carry, diag):
        k0 = pl.multiple_of(k0, t)
        logits = [_dot_nt(qs[n], k_ref[0, pl.ds(k0, width), cols[n]]) for n in range(FOX_HB)]
        out = []
        for n, (m, acc) in enumerate(carry):
            v = v_ref[0, pl.ds(k0, width), cols[n]]
            ck = cum_ref[0, pl.ds(hb * FOX_HB + n, 1), pl.ds(k0, width)] * LOG2E
            s = logits[n] - ck
            if diag:
                r = lax.broadcasted_iota(jnp.int32, s.shape, 0)
                c = lax.broadcasted_iota(jnp.int32, s.shape, 1)
                s = jnp.where(c <= r + (width - t), s, MASK_VALUE)
            out.append(_softmax_step(s, m, acc, v))
        return tuple(out)

    init = tuple(_softmax_init(t) for _ in range(FOX_HB))
    n_pairs = i >> 1
    carry = lax.fori_loop(0, n_pairs, lambda p, c: tile(p * (2 * t), 2 * t, c, False), init)
    k_tail = n_pairs * (2 * t)
    carry = lax.cond((i & 1) == 0,
                     lambda c: tile(k_tail, t, c, True),
                     lambda c: tile(k_tail, 2 * t, c, True), carry)
    for n, (_, acc) in enumerate(carry):
        o_ref[0, :, cols[n]] = _softmax_result(acc).astype(o_ref.dtype)


def _fox(z_main, cum, weights):
    nb, s, _ = z_main.shape
    w = FOX_HB * HEAD_DIM
    grid = (nb, FOX_HEADS // FOX_HB, s // FOX_T)
    n_steps = grid[0] * grid[1] * grid[2]
    step = lambda b, h, i: ((b * grid[1] + h) * grid[2] + i, 0)
    slabs = []
    for wt in weights:
        rows = wt.shape[0] // n_steps
        assert rows * n_steps == wt.shape[0] and rows % BF16_SUBLANES == 0, wt.shape
        slabs.append(pl.BlockSpec((rows, wt.shape[1]), step))
    outs = pl.pallas_call(
        functools.partial(_fox_kernel, n_cast=len(weights)),
        out_shape=[jax.ShapeDtypeStruct((nb, s, FOX_W), jnp.bfloat16)]
        + [jax.ShapeDtypeStruct(wt.shape, jnp.bfloat16) for wt in weights],
        grid=grid,
        in_specs=[
            pl.BlockSpec((1, FOX_T, w), lambda b, h, i: (b, i, h)),
            pl.BlockSpec((1, s, w), lambda b, h, i: (b, 0, FOX_W // w + h)),
            pl.BlockSpec((1, s, w), lambda b, h, i: (b, 0, 2 * FOX_W // w + h)),
            pl.BlockSpec((1, FOX_HEADS, s), lambda b, h, i: (b, 0, 0)),
        ] + slabs,
        out_specs=[pl.BlockSpec((1, FOX_T, w), lambda b, h, i: (b, i, h))] + slabs,
        compiler_params=_cparams(("arbitrary", "arbitrary", "arbitrary")),
        name="fox",
    )(z_main, z_main, z_main, cum, *weights)
    return outs[0], outs[1:]


NSA_TQ = 512
NSA_PARTS = 4
NSEL_PAD = 128
SLC_SHIFT = SLC_LEN.bit_length() - 1
assert 1 << SLC_SHIFT == SLC_LEN
assert WINDOW % NSA_TQ == 0 and NSA_TQ % SLC_LEN == 0
SUBLANES = 8
BF16_SUBLANES = 16


def _block_ranks(score, n_slc):
    tq = score.shape[1]
    groups = [score[SUBLANES * v:SUBLANES * (v + 1)] for v in range(n_slc // SUBLANES)]
    ranks = [jnp.zeros((SUBLANES, tq), jnp.float32) for _ in groups]
    for m in range(n_slc):
        row = jnp.broadcast_to(score[m:m + 1, :], (SUBLANES, tq))
        for v, grp in enumerate(groups):
            lo = SUBLANES * v
            if lo > m:
                before = row >= grp
            elif lo + SUBLANES - 1 <= m:
                before = row > grp
            else:
                n_i = lo + lax.broadcasted_iota(jnp.int32, grp.shape, 0)
                before = (row > grp) | ((row == grp) & (n_i > m))
            ranks[v] = ranks[v] + jnp.where(before, 1.0, 0.0)
    return jnp.concatenate(ranks, axis=0)


def _nsa_kernel(q_ref, zs_ref, kc_ref, vc_ref, ks_ref, vs_ref, kw_ref, vw_ref, ovt_ref,
                o_ref, ka_ref, *, seq):
    g = pl.program_id(1)
    i = pl.program_id(2)
    tq = NSA_TQ
    q0 = i * tq

    @pl.when(i == 0)
    def _():
        ka_ref[:, 0:HEAD_DIM] = ks_ref[0]
        tok = lax.broadcasted_iota(jnp.int32, (seq, NSEL_PAD), 0)
        n_i = lax.broadcasted_iota(jnp.int32, (seq, NSEL_PAD), 1)
        ka_ref[:, HEAD_DIM:HEAD_DIM + NSEL_PAD] = jnp.where(
            n_i == (tok >> SLC_SHIFT), 1.0, 0.0).astype(jnp.bfloat16)

    qb = q_ref[0]
    hpp = NSA_GROUP // NSA_PARTS
    parts = range(NSA_PARTS)
    qh = [jnp.concatenate(
        [qb[:, r * HEAD_DIM:(r + 1) * HEAD_DIM] for r in range(h * hpp, (h + 1) * hpp)], axis=0)
        for h in parts]
    prow = hpp * tq

    def row_query(shape):
        return lax.broadcasted_iota(jnp.int32, shape, 0) & (tq - 1)

    kc = kc_ref[0, 0]
    vc = vc_ref[0, 0]
    s_cs = [_dot_nt(qh[h], kc) for h in parts]
    p_cs = []
    for s_c in s_cs:
        cend = lax.broadcasted_iota(jnp.int32, s_c.shape, 1) * CMP_STRIDE + (CMP_LEN - 1)
        s_c = jnp.where(cend <= q0 + row_query(s_c.shape), s_c, MASK_VALUE)
        m_c = jnp.maximum(jnp.max(s_c, axis=1, keepdims=True), 0.1 * MASK_VALUE)
        e_c = jnp.exp2(s_c - m_c)
        p_cs.append(e_c * (1.0 / jnp.maximum(jnp.sum(e_c, axis=1, keepdims=True), 1e-30)))
    o_c = [_dot(p_c.astype(jnp.bfloat16), vc) for p_c in p_cs]

    psum = None
    for p_c in p_cs:
        for r in range(hpp):
            term = p_c[r * tq:(r + 1) * tq]
            psum = term if psum is None else psum + term
    p_hi = psum.astype(jnp.bfloat16)
    p_lo = (psum - p_hi.astype(jnp.float32)).astype(jnp.bfloat16)
    ovt = ovt_ref[...]
    imp_t = _dot_nt(ovt, p_hi) + _dot_nt(ovt, p_lo)
    n_slc = seq // SLC_LEN
    blk = lax.broadcasted_iota(jnp.int32, imp_t.shape, 0)
    cur = (q0 + lax.broadcasted_iota(jnp.int32, imp_t.shape, 1)) >> SLC_SHIFT
    forced = (blk == 0) | (blk == cur) | (blk == cur - 1)
    score = jnp.where(forced, SEL_BONUS, imp_t)
    score = jnp.where(blk <= cur, score, -SEL_BONUS)
    def live_ranks(k):
        n_live = min((k + 1) * (tq // SLC_LEN), n_slc)

        def ranks():
            r = _block_ranks(score[0:n_live], n_live)
            if n_live < n_slc:
                r = jnp.concatenate([r, jnp.zeros((n_slc - n_live, tq), jnp.float32)], axis=0)
            return r
        return ranks

    rank = lax.switch(i, [live_ranks(k) for k in range(seq // tq)])
    bias_t = jnp.where(rank < min(SLC_TOPK, n_slc), 0.0, MASK_VALUE)
    bias_t = jnp.concatenate(
        [bias_t, jnp.zeros((NSEL_PAD - n_slc, tq), jnp.float32)], axis=0)
    bias = bias_t.T.astype(jnp.bfloat16)
    bias_p = jnp.concatenate([bias] * hpp, axis=0)
    qa = [jnp.concatenate([qh[h], bias_p], axis=1) for h in parts]

    def sel_tile(k0, width, carry, diag):
        k0 = pl.multiple_of(k0, tq)
        logits = [_dot_nt(qa[h], ka_ref[pl.ds(k0, width), :]) for h in parts]
        out = []
        for s, (m, acc) in zip(logits, carry):
            if diag:
                c = lax.broadcasted_iota(jnp.int32, s.shape, 1)
                s = jnp.where(c <= row_query(s.shape) + (width - tq), s, MASK_VALUE)
            out.append(_softmax_step(s, m, acc, vs_ref[0, pl.ds(k0, width), :]))
        return tuple(out)

    init = tuple(_softmax_init(prow) for _ in parts)
    n_pairs = i >> 1
    carry = lax.fori_loop(0, n_pairs,
                          lambda p, c: sel_tile(p * (2 * tq), 2 * tq, c, False), init)
    k_tail = n_pairs * (2 * tq)
    carry = lax.cond((i & 1) == 0,
                     lambda c: sel_tile(k_tail, tq, c, True),
                     lambda c: sel_tile(k_tail, 2 * tq, c, True), carry)
    o_s = [_softmax_result(acc) for _, acc in carry]

    n_w = WINDOW // tq + 1
    starts = [q0 - WINDOW + t * tq for t in range(n_w)]
    offs = [pl.multiple_of(jnp.maximum(st, 0), tq) for st in starts]
    s_w = [[_dot_nt(qh[h], kw_ref[0, pl.ds(offs[t], tq), :]) for h in parts] for t in range(n_w)]
    for t in range(n_w):
        for h in parts:
            s = s_w[t][h]
            c = lax.broadcasted_iota(jnp.int32, s.shape, 1)
            if t == 0:
                s = jnp.where(c > row_query(s.shape), s, MASK_VALUE)
            if t == n_w - 1:
                s = jnp.where(c <= row_query(s.shape), s, MASK_VALUE)
            else:
                s = jnp.where(starts[t] >= 0, s, MASK_VALUE)
            s_w[t][h] = s
    o_w = []
    for h in parts:
        m_w = s_w[0][h].max(axis=1, keepdims=True)
        for t in range(1, n_w):
            m_w = jnp.maximum(m_w, s_w[t][h].max(axis=1, keepdims=True))
        acc_w = jnp.zeros((prow, 2 * HEAD_DIM), jnp.float32)
        for t in range(n_w):
            p = jnp.exp2((s_w[t][h] - m_w).astype(jnp.bfloat16))
            v = vw_ref[0, pl.ds(offs[t], tq), :]
            acc_w = acc_w + _dot(p, jnp.concatenate([v, jnp.ones_like(v)], axis=1))
        o_w.append(_softmax_result(acc_w))

    gates = jax.nn.sigmoid(zs_ref[0])
    lane = lax.broadcasted_iota(jnp.int32, gates.shape, 1)
    outs = []
    for r in range(NSA_GROUP):
        h, rows = r // hpp, slice((r % hpp) * tq, (r % hpp + 1) * tq)
        o_r = jnp.zeros((tq, HEAD_DIM), jnp.float32)
        for br, o_b in enumerate((o_c, o_s, o_w)):
            col = FOX_HEADS + 3 * (g * NSA_GROUP + r) + br
            gcol = jnp.sum(jnp.where(lane == col, gates, 0.0), axis=1, keepdims=True)
            o_r = o_r + gcol * o_b[h][rows]
        outs.append(o_r)
    o_ref[0] = jnp.concatenate(outs, axis=1).astype(o_ref.dtype)


def _nsa(z_main, z_small, kvc, ovt):
    nb, s, _ = z_main.shape
    nc = kvc.shape[2]
    gw = NSA_GROUP * HEAD_DIM
    full = lambda col0: pl.BlockSpec(
        (1, s, HEAD_DIM), lambda b, g, i, c=col0 // HEAD_DIM: (b, 0, c + g))
    return pl.pallas_call(
        functools.partial(_nsa_kernel, seq=s),
        out_shape=jax.ShapeDtypeStruct((nb, s, NSA_W), jnp.bfloat16),
        grid=(nb, NSA_KV_HEADS, s // NSA_TQ),
        in_specs=[
            pl.BlockSpec((1, NSA_TQ, gw), lambda b, g, i: (b, i, COL_QN // gw + g)),
            pl.BlockSpec((1, NSA_TQ, Z_SMALL), lambda b, g, i: (b, i, 0)),
            pl.BlockSpec((1, 1, nc, HEAD_DIM), lambda b, g, i: (b, g, 0, 0)),
            pl.BlockSpec((1, 1, nc, HEAD_DIM), lambda b, g, i: (b, NSA_KV_HEADS + g, 0, 0)),
            full(COL_KS), full(COL_VS), full(COL_KW), full(COL_VW),
            pl.BlockSpec(ovt.shape, lambda b, g, i: (0, 0)),
        ],
        out_specs=pl.BlockSpec((1, NSA_TQ, gw), lambda b, g, i: (b, i, g)),
        scratch_shapes=[pltpu.VMEM((s, HEAD_DIM + NSEL_PAD), jnp.bfloat16)],
        compiler_params=_cparams(("arbitrary", "arbitrary", "arbitrary")),
        name="nsa",
    )(z_main, z_small, kvc, kvc, z_main, z_main, z_main, z_main, ovt)


OUT_TM = 512
OUT_RC = 256


def _outproj_kernel(of_ref, on_ref, bf_ref, bn_ref, w_ref, x_ref, g_ref, nw_ref, sc_ref,
                    sh_ref, h_ref, u_ref):
    for r in range(OUT_TM // OUT_RC):
        rows = slice(r * OUT_RC, (r + 1) * OUT_RC)
        yf = _rms(of_ref[0, rows, :].astype(jnp.float32)) * bf_ref[...]
        yn = _rms(on_ref[0, rows, :].astype(jnp.float32)) * bn_ref[...]
        y = jnp.concatenate([yf.astype(jnp.bfloat16), yn.astype(jnp.bfloat16)], axis=1)
        h = x_ref[0, rows, :] + g_ref[0] * _dot(y, w_ref[...])
        h_ref[0, rows, :] = h
        u = _rms(h) * nw_ref[...]
        u_ref[0, rows, :] = (u * (1.0 + sc_ref[0]) + sh_ref[0]).astype(jnp.bfloat16)


def _outproj(o_f, o_n, beta_f, beta_n, w_out, x, gate, norm_w, sc, sh):
    nb, s, d = x.shape
    vec = lambda n: pl.BlockSpec((1, n), lambda b, i: (0, 0))
    bvec = pl.BlockSpec((1, 1, d), lambda b, i: (b, 0, 0))
    rows = lambda n: pl.BlockSpec((1, OUT_TM, n), lambda b, i: (b, i, 0))
    return pl.pallas_call(
        _outproj_kernel,
        out_shape=(jax.ShapeDtypeStruct((nb, s, d), jnp.float32),
                   jax.ShapeDtypeStruct((nb, s, d), jnp.bfloat16)),
        grid=(nb, s // OUT_TM),
        in_specs=[
            rows(FOX_W), rows(NSA_W), vec(FOX_W), vec(NSA_W),
            pl.BlockSpec((FOX_W + NSA_W, d), lambda b, i: (0, 0),
                         pipeline_mode=pl.Buffered(1)),
            rows(d), bvec, vec(d), bvec, bvec,
        ],
        out_specs=(rows(d), rows(d)),
        compiler_params=_cparams(("arbitrary", "arbitrary")),
        name="outproj",
    )(o_f, o_n, beta_f.reshape(1, FOX_W), beta_n.reshape(1, NSA_W), w_out, x,
      gate.reshape(nb, 1, d), norm_w.reshape(1, d), sc.reshape(nb, 1, d),
      sh.reshape(nb, 1, d))


FFN_TM = 1024
FFN_TF = 512
FFN_RC = 512
FFN_RC_LAST = 256


def _ffn_kernel(u_ref, h_ref, g_ref, fw_ref, wg_ref, wu_ref, wd_ref, o_ref, *, final):
    j = pl.program_id(2)
    last = pl.num_programs(2) - 1

    def partial_sum(rows):
        u = u_ref[0, rows, :]
        a = _dot(u, wg_ref[...])
        b = _dot(u, wu_ref[...])
        mid = (a * jax.nn.sigmoid(a) * b).astype(jnp.bfloat16)
        return _dot(mid, wd_ref[...])

    def chunks(rc):
        return [slice(r * rc, (r + 1) * rc) for r in range(FFN_TM // rc)]

    @pl.when(j == 0)
    def _():
        for rows in chunks(FFN_RC):
            o_ref[0, rows, :] = partial_sum(rows)

    @pl.when((j > 0) & (j < last))
    def _():
        for rows in chunks(FFN_RC):
            o_ref[0, rows, :] += partial_sum(rows)

    @pl.when(j == last)
    def _():
        for rows in chunks(FFN_RC_LAST):
            h2 = h_ref[0, rows, :] + g_ref[0] * (o_ref[0, rows, :] + partial_sum(rows))
            o_ref[0, rows, :] = _rms(h2) * fw_ref[...] if final else h2


def _ffn(u2, h1, gate, final_w, w_gate, w_up, w_down, final):
    nb, s, d = h1.shape
    f = w_gate.shape[1]
    return pl.pallas_call(
        functools.partial(_ffn_kernel, final=final),
        out_shape=jax.ShapeDtypeStruct((nb, s, d), jnp.float32),
        grid=(nb, s // FFN_TM, f // FFN_TF),
        in_specs=[
            pl.BlockSpec((1, FFN_TM, d), lambda b, i, j: (b, i, 0)),
            pl.BlockSpec((1, FFN_TM, d), lambda b, i, j: (b, i, 0),
                         pipeline_mode=pl.Buffered(1)),
            pl.BlockSpec((1, 1, d), lambda b, i, j: (b, 0, 0)),
            pl.BlockSpec((1, d), lambda b, i, j: (0, 0)),
            pl.BlockSpec((d, FFN_TF), lambda b, i, j: (0, j)),
            pl.BlockSpec((d, FFN_TF), lambda b, i, j: (0, j)),
            pl.BlockSpec((FFN_TF, d), lambda b, i, j: (j, 0)),
        ],
        out_specs=pl.BlockSpec((1, FFN_TM, d), lambda b, i, j: (b, i, 0)),
        compiler_params=_cparams(("arbitrary", "arbitrary", "arbitrary"), BIG_VMEM_LIMIT),
        name="ffn",
    )(u2, h1, gate.reshape(nb, 1, d), final_w.reshape(1, d), w_gate, w_up, w_down)


def _split_w_in(w_in):
    sizes = ([FOX_W] * 3 + [FOX_HEADS] + [NSA_W] + [KV_W] * 6 + [3 * NSA_HEADS])
    offs = [0]
    for sz in sizes:
        offs.append(offs[-1] + sz)
    wt = w_in.T
    order = (0, 1, 2, 4, 7, 9, 8, 10, 5, 6)
    units = lambda ns, unit: [offs[n] + r for n in ns for r in range(0, sizes[n], unit)]
    main = _gather_rows(wt, units(order, KV_W), KV_W, jnp.bfloat16)
    narrow = _gather_rows(wt, units((3, 11), SUBLANES), SUBLANES, jnp.float32)
    small = jnp.concatenate(
        [narrow, jnp.zeros((Z_SMALL - narrow.shape[0], w_in.shape[0]), jnp.float32)], axis=0)
    return main, small.astype(jnp.bfloat16)


def _cast_kernel(rows_ref, w_ref, o_ref):
    del rows_ref
    o_ref[...] = w_ref[...].astype(o_ref.dtype)


def _gather_rows(wt, src_rows, unit, dtype):
    assert all(r % SUBLANES == 0 for r in src_rows), src_rows
    tiles = jnp.asarray([r // SUBLANES for r in src_rows], jnp.int32)
    n_units = len(src_rows)
    d = wt.shape[1]
    return pl.pallas_call(
        _cast_kernel,
        out_shape=jax.ShapeDtypeStruct((n_units * unit, d), dtype),
        grid_spec=pltpu.PrefetchScalarGridSpec(
            num_scalar_prefetch=1,
            grid=(n_units,),
            in_specs=[pl.BlockSpec((pl.Element(unit), pl.Element(d)),
                                   lambda u, rows: (rows[u] * SUBLANES, 0))],
            out_specs=pl.BlockSpec((unit, d), lambda u, rows: (u, 0)),
        ),
        compiler_params=_cparams(("arbitrary",)),
        name="w_in_prep",
    )(tiles, wt)


def _overlap_t(n_cmp_pad, n_slc):
    cs = jnp.arange(n_cmp_pad) * CMP_STRIDE
    ss = jnp.arange(n_slc) * SLC_LEN
    ov = (jnp.minimum(cs[None, :] + CMP_LEN, ss[:, None] + SLC_LEN)
          - jnp.maximum(cs[None, :], ss[:, None]))
    ov = jnp.clip(ov, 0).astype(jnp.float32) / CMP_STRIDE
    return ov.astype(jnp.bfloat16)


def kernel(x, c, positions, w_ada, b_ada, norm_attn, norm_ffn, w_in, b_fgate, cmp_pos,
           w_kc1, w_kc2, w_vc1, w_vc2, beta_fox, beta_nsa, w_out, w_gate, w_up, w_down,
           final_norm):
    nb, s, d = x.shape
    depth = w_ada.shape[0]
    n_cmp_pad = s // CMP_STRIDE
    cmp_end = jnp.arange(n_cmp_pad) * CMP_STRIDE + (CMP_LEN - 1)
    pos_c = jnp.take(positions, jnp.minimum(cmp_end, s - 1), axis=1)
    cos_c, sin_c = _rope_tables(pos_c, n_cmp_pad)
    ovt = _overlap_t(n_cmp_pad, s // SLC_LEN)

    h = x
    for l in range(depth):
        if l == 0:
            mod, cos, sin = _ada(c, w_ada[l], b_ada[l], positions)
        else:
            mod = _ada(c, w_ada[l], b_ada[l])
        sh1, sc1, g1, sh2, sc2, g2 = jnp.split(mod, 6, axis=-1)
        w_main, w_small = _split_w_in(w_in[l])
        z_main, z_small, zc = _inproj(h, norm_attn[l], sc1, sh1, w_main, w_small, cos, sin)
        cum = _decay(z_small, b_fgate[l])

        w1 = jnp.stack([w_kc1[l], w_vc1[l]]).astype(jnp.bfloat16)
        w1 = w1.reshape(2, CMP_LEN, HEAD_DIM, CMP_HIDDEN)
        w2 = jnp.stack([w_kc2[l], w_vc2[l]]).astype(jnp.bfloat16)
        kvc = _compress(zc, w1, w2, cmp_pos[l], cos_c, sin_c)

        o_f, (wo, wg, wu, wd) = _fox(z_main, cum, (w_out[l], w_gate[l], w_up[l], w_down[l]))
        o_n = _nsa(z_main, z_small, kvc, ovt)
        h1, u2 = _outproj(o_f, o_n, beta_fox[l], beta_nsa[l], wo, h, g1, norm_ffn[l], sc2, sh2)
        h = _ffn(u2, h1, g2, final_norm, wg, wu, wd, final=(l == depth - 1))
    return h
```

```python
import functools

import jax
import jax.numpy as jnp
from jax import lax
from jax.experimental import pallas as pl
from jax.experimental.pallas import tpu as pltpu

D_MODEL = 2048
HEAD_DIM = 128
FOX_HEADS = 8
NSA_HEADS = 8
NSA_KV_HEADS = 2
NSA_GROUP = NSA_HEADS // NSA_KV_HEADS
CMP_LEN = 32
CMP_STRIDE = 16
CMP_HIDDEN = 2 * HEAD_DIM
SLC_LEN = 64
SLC_TOPK = 16
WINDOW = 512
ROPE_THETA = 10000.0
NORM_EPS = 1e-6
MASK_VALUE = -1e30
SEL_BONUS = 1e6

FOX_W = FOX_HEADS * HEAD_DIM
NSA_W = NSA_HEADS * HEAD_DIM
KV_W = NSA_KV_HEADS * HEAD_DIM
Z_MAIN = 3 * FOX_W + NSA_W + 6 * KV_W
Z_SMALL = 128
COL_QN = 3 * FOX_W
COL_KS = COL_QN + NSA_W
COL_KW = COL_KS + KV_W
COL_VS = COL_KW + KV_W
COL_VW = COL_VS + KV_W
COL_KC = COL_VW + KV_W

VMEM_LIMIT = 48 * 1024 * 1024
BIG_VMEM_LIMIT = 56 * 1024 * 1024
SCALE = HEAD_DIM ** -0.5
LOG2E = 1.4426950408889634


def _cparams(sem, vmem_limit=VMEM_LIMIT):
    return pltpu.CompilerParams(dimension_semantics=sem, vmem_limit_bytes=vmem_limit)


def _dot(a, b):
    return jnp.dot(a, b, preferred_element_type=jnp.float32)


def _dot_nt(a, b):
    return lax.dot_general(a, b, (((1,), (1,)), ((), ())), preferred_element_type=jnp.float32)


def _rms(x):
    return x * lax.rsqrt(jnp.mean(x * x, axis=-1, keepdims=True) + NORM_EPS)


ADA_TN = 768
ADA_KC = 256


def _ada_kernel(ct_ref, w_ref, b_ref, *rest, nb, with_rope):
    if with_rope:
        pos_ref, inv_ref, o_ref, cos_ref, sin_ref = rest
        _rope_table_kernel(pos_ref, inv_ref, cos_ref, sin_ref)
    else:
        o_ref, = rest
    tn = w_ref.shape[1]
    ct = ct_ref[...]
    st = ct * jax.nn.sigmoid(ct)

    accs = [jnp.zeros((SUBLANES, tn), jnp.float32) for _ in range(nb)]
    for kc in range(D_MODEL // ADA_KC):
        w = w_ref[kc * ADA_KC:(kc + 1) * ADA_KC, :]
        for b in range(nb):
            s = st[kc * ADA_KC:(kc + 1) * ADA_KC, b:b + 1]
            accs[b] = accs[b] + jnp.sum(
                (w * s).reshape(ADA_KC // SUBLANES, SUBLANES, tn), axis=0)
    for b in range(nb):
        o_ref[b:b + 1, :] = jnp.sum(accs[b], axis=0, keepdims=True) + b_ref[...]


def _ada(c, w_ada, b_ada, positions=None):
    nb = c.shape[0]
    n = w_ada.shape[1]
    n_steps = n // ADA_TN
    assert n_steps * ADA_TN == n
    in_specs = [
        pl.BlockSpec((D_MODEL, nb), lambda j: (0, 0)),
        pl.BlockSpec((D_MODEL, ADA_TN), lambda j: (0, j)),
        pl.BlockSpec((1, ADA_TN), lambda j: (0, j)),
    ]
    out_shape = [jax.ShapeDtypeStruct((nb, n), jnp.float32)]
    out_specs = [pl.BlockSpec((nb, ADA_TN), lambda j: (0, j))]
    args = [c.T, w_ada, b_ada.reshape(1, n)]
    if positions is not None:
        s = positions.shape[1]
        per_batch = n_steps // nb
        ts = s // per_batch
        assert per_batch * nb == n_steps and ts * per_batch == s and ts % (2 * SUBLANES) == 0
        rows = lambda j: (j // per_batch, j % per_batch, 0)
        in_specs += [pl.BlockSpec((1, ts, 1), rows), pl.BlockSpec((1, HEAD_DIM), lambda j: (0, 0))]
        tab = jax.ShapeDtypeStruct((nb, s, HEAD_DIM), jnp.float32)
        out_shape += [tab, tab]
        out_specs += [pl.BlockSpec((1, ts, HEAD_DIM), rows)] * 2
        args += [positions.reshape(nb, s, 1), _rope_inv_freq()]
    outs = pl.pallas_call(
        functools.partial(_ada_kernel, nb=nb, with_rope=positions is not None),
        out_shape=out_shape,
        grid=(n_steps,),
        in_specs=in_specs,
        out_specs=out_specs,
        compiler_params=_cparams(("arbitrary",)),
        name="ada",
    )(*args)
    return outs if positions is not None else outs[0]


def _rope_table_kernel(pos_ref, inv_ref, cos_ref, sin_ref):
    half = pos_ref.shape[1] // 2
    lane = lax.broadcasted_iota(jnp.int32, (half, HEAD_DIM), 1)
    low = lane < HEAD_DIM // 2
    pos = jnp.where(low, pos_ref[0, 0:half, :], pos_ref[0, half:2 * half, :])
    ang = pos.astype(jnp.float32) * inv_ref[...]
    cos = jnp.cos(ang)
    sin = jnp.sin(ang)
    cos_sw = pltpu.roll(cos, HEAD_DIM // 2, axis=1)
    sin_sw = pltpu.roll(sin, HEAD_DIM // 2, axis=1)
    cos_ref[0, 0:half, :] = jnp.where(low, cos, cos_sw)
    cos_ref[0, half:2 * half, :] = jnp.where(low, cos_sw, cos)
    sin_ref[0, 0:half, :] = jnp.where(low, -sin, sin_sw)
    sin_ref[0, half:2 * half, :] = jnp.where(low, -sin_sw, sin)


def _rope_inv_freq():
    inv = ROPE_THETA ** (-jnp.arange(0, HEAD_DIM, 2, dtype=jnp.float32) / HEAD_DIM)
    return jnp.concatenate([inv, inv]).reshape(1, HEAD_DIM)


def _rope_tables(pos, ts):
    nb, n = pos.shape
    inv2 = _rope_inv_freq()
    shp = jax.ShapeDtypeStruct((nb, n, HEAD_DIM), jnp.float32)
    spec = pl.BlockSpec((1, ts, HEAD_DIM), lambda b, i: (b, i, 0))
    return pl.pallas_call(
        _rope_table_kernel,
        out_shape=(shp, shp),
        grid=(nb, n // ts),
        in_specs=[pl.BlockSpec((1, ts, 1), lambda b, i: (b, i, 0)),
                  pl.BlockSpec((1, HEAD_DIM), lambda b, i: (0, 0))],
        out_specs=(spec, spec),
        compiler_params=_cparams(("arbitrary", "arbitrary")),
        name="rope_tables",
    )(pos.reshape(nb, n, 1), inv2)


def _apply_rope(x, cos, sin):
    return x * cos + pltpu.roll(x, HEAD_DIM // 2, axis=1) * sin


IN_TM = 1024
IN_TN = 512
IN_RC = 256
assert FOX_W // IN_TN >= 1


def _inproj_kernel(x_ref, g_ref, sc_ref, sh_ref, w_ref, ws_ref, cos_ref, sin_ref,
                   z_ref, zs_ref, zc_ref, u_ref):
    j = pl.program_id(2)

    @pl.when(j == 0)
    def _():
        for r in range(IN_TM // IN_RC):
            rows = slice(r * IN_RC, (r + 1) * IN_RC)
            u = _rms(x_ref[0, rows, :]) * g_ref[...]
            ub = (u * (1.0 + sc_ref[0]) + sh_ref[0]).astype(jnp.bfloat16)
            u_ref[rows, :] = ub
            zs_ref[0, rows, :] = _dot_nt(ub, ws_ref[...])
            z_ref[0, rows, :] = (_dot_nt(ub, w_ref[...]) * (SCALE * LOG2E)).astype(z_ref.dtype)

    fox_q = (j > 0) & (j < FOX_W // IN_TN)
    nsa_q = (j >= COL_QN // IN_TN) & (j < COL_KS // IN_TN)
    nsa_k = j == COL_KS // IN_TN
    cmp_kv = j == COL_KC // IN_TN

    def matmul():
        return _dot_nt(u_ref[...], w_ref[...])

    def roped(acc):
        cos = cos_ref[0]
        sin = sin_ref[0]
        parts = [_apply_rope(acc[:, h * HEAD_DIM:(h + 1) * HEAD_DIM], cos, sin)
                 for h in range(IN_TN // HEAD_DIM)]
        return jnp.concatenate(parts, axis=1).astype(z_ref.dtype)

    @pl.when(fox_q)
    def _():
        z_ref[0] = (matmul() * (SCALE * LOG2E)).astype(z_ref.dtype)

    @pl.when(nsa_q)
    def _():
        z_ref[0] = roped(matmul() * (SCALE * LOG2E))

    @pl.when(nsa_k)
    def _():
        z_ref[0] = roped(matmul())

    @pl.when(cmp_kv)
    def _():
        acc = matmul()
        z_ref[0] = acc.astype(z_ref.dtype)
        zc_ref[0] = acc

    @pl.when(jnp.logical_not((j == 0) | fox_q | nsa_q | nsa_k | cmp_kv))
    def _():
        z_ref[0] = matmul().astype(z_ref.dtype)


def _inproj(x, norm_w, sc, sh, w_main, w_small, cos, sin):
    nb, s, d = x.shape
    grid = (nb, s // IN_TM, Z_MAIN // IN_TN)
    return pl.pallas_call(
        _inproj_kernel,
        out_shape=(jax.ShapeDtypeStruct((nb, s, Z_MAIN), jnp.bfloat16),
                   jax.ShapeDtypeStruct((nb, s, Z_SMALL), jnp.float32),
                   jax.ShapeDtypeStruct((nb, s, IN_TN), jnp.float32)),
        grid=grid,
        in_specs=[
            pl.BlockSpec((1, IN_TM, d), lambda b, i, j: (b, i, 0)),
            pl.BlockSpec((1, d), lambda b, i, j: (0, 0)),
            pl.BlockSpec((1, 1, d), lambda b, i, j: (b, 0, 0)),
            pl.BlockSpec((1, 1, d), lambda b, i, j: (b, 0, 0)),
            pl.BlockSpec((IN_TN, d), lambda b, i, j: (j, 0)),
            pl.BlockSpec((Z_SMALL, d), lambda b, i, j: (0, 0)),
            pl.BlockSpec((1, IN_TM, HEAD_DIM), lambda b, i, j: (b, i, 0)),
            pl.BlockSpec((1, IN_TM, HEAD_DIM), lambda b, i, j: (b, i, 0)),
        ],
        out_specs=(pl.BlockSpec((1, IN_TM, IN_TN), lambda b, i, j: (b, i, j)),
                   pl.BlockSpec((1, IN_TM, Z_SMALL), lambda b, i, j: (b, i, 0)),
                   pl.BlockSpec((1, IN_TM, IN_TN), lambda b, i, j: (b, i, 0))),
        scratch_shapes=[pltpu.VMEM((IN_TM, d), jnp.bfloat16)],
        compiler_params=_cparams(("arbitrary", "arbitrary", "arbitrary")),
        name="inproj",
    )(x, norm_w.reshape(1, d), sc.reshape(nb, 1, d), sh.reshape(nb, 1, d),
      w_main, w_small, cos, sin)


DEC_T = 2048


def _decay_kernel(zs_ref, bf_ref, cum_ref, carry_ref):
    @pl.when(pl.program_id(1) == 0)
    def _():
        carry_ref[...] = jnp.zeros_like(carry_ref)

    xt = zs_ref[0].T[0:FOX_HEADS, :] + bf_ref[...]
    lf = jnp.minimum(xt, 0.0) - jnp.log1p(jnp.exp(-jnp.abs(xt)))
    lane = lax.broadcasted_iota(jnp.int32, lf.shape, 1)
    sh = 1
    while sh < DEC_T:
        lf = lf + jnp.where(lane >= sh, pltpu.roll(lf, sh, axis=1), 0.0)
        sh *= 2
    cum = lf + carry_ref[:, 0:1]
    cum_ref[0] = cum
    carry_ref[...] = jnp.broadcast_to(cum[:, DEC_T - 1:DEC_T], carry_ref.shape)


def _decay(z_small, b_fgate):
    nb, s, _ = z_small.shape
    return pl.pallas_call(
        _decay_kernel,
        out_shape=jax.ShapeDtypeStruct((nb, FOX_HEADS, s), jnp.float32),
        grid=(nb, s // DEC_T),
        in_specs=[pl.BlockSpec((1, DEC_T, Z_SMALL), lambda b, i: (b, i, 0)),
                  pl.BlockSpec((FOX_HEADS, 1), lambda b, i: (0, 0))],
        out_specs=pl.BlockSpec((1, FOX_HEADS, DEC_T), lambda b, i: (b, 0, i)),
        scratch_shapes=[pltpu.VMEM((FOX_HEADS, 128), jnp.float32)],
        compiler_params=_cparams(("arbitrary", "arbitrary")),
        name="decay",
    )(z_small, b_fgate.reshape(FOX_HEADS, 1))


def _compress_kernel(x_ref, w1_ref, w2_ref, pos_ref, cos_ref, sin_ref, o_ref):
    kind = pl.program_id(1)
    nc = x_ref.shape[1] // CMP_STRIDE
    p = jnp.zeros((nc, CMP_HIDDEN), jnp.float32)
    q = jnp.zeros((nc, CMP_HIDDEN), jnp.float32)
    for l in range(0, CMP_STRIDE, 2):
        xl = jnp.concatenate(
            [x_ref[0, pl.ds(l + k, nc, stride=CMP_STRIDE), :].astype(jnp.bfloat16)
             for k in range(2)], axis=1)
        pair = lambda l0: w1_ref[0, l0:l0 + 2].reshape(2 * HEAD_DIM, CMP_HIDDEN)
        p = p + _dot(xl, pair(l))
        q = q + _dot(xl, pair(CMP_STRIDE + l))
    posb = jnp.broadcast_to(pos_ref[...], (8, CMP_LEN * HEAD_DIM)).astype(jnp.bfloat16)
    w1_flat = w1_ref[0].reshape(CMP_LEN * HEAD_DIM, CMP_HIDDEN)
    pterm = _dot(posb, w1_flat)[0:1, :]
    h = p + pltpu.roll(q, nc - 1, axis=0) + pterm
    a = jax.nn.gelu(h).astype(jnp.bfloat16)
    out = _dot(a, w2_ref[0])

    @pl.when(kind < NSA_KV_HEADS)
    def _():
        o_ref[0, 0] = _apply_rope(out, cos_ref[0], sin_ref[0]).astype(o_ref.dtype)

    @pl.when(kind >= NSA_KV_HEADS)
    def _():
        o_ref[0, 0] = out.astype(o_ref.dtype)


def _compress(zc, w1, w2, cmp_pos, cos_c, sin_c):
    nb, s, _ = zc.shape
    nk = 2 * NSA_KV_HEADS
    nc = s // CMP_STRIDE
    return pl.pallas_call(
        _compress_kernel,
        out_shape=jax.ShapeDtypeStruct((nb, nk, nc, HEAD_DIM), jnp.bfloat16),
        grid=(nb, nk),
        in_specs=[
            pl.BlockSpec((1, s, HEAD_DIM), lambda b, k: (b, 0, k)),
            pl.BlockSpec((1, CMP_LEN, HEAD_DIM, CMP_HIDDEN),
                         lambda b, k: (k // NSA_KV_HEADS, 0, 0, 0)),
            pl.BlockSpec((1, CMP_HIDDEN, HEAD_DIM), lambda b, k: (k // NSA_KV_HEADS, 0, 0)),
            pl.BlockSpec((1, CMP_LEN * HEAD_DIM), lambda b, k: (0, 0)),
            pl.BlockSpec((1, nc, HEAD_DIM), lambda b, k: (b, 0, 0)),
            pl.BlockSpec((1, nc, HEAD_DIM), lambda b, k: (b, 0, 0)),
        ],
        out_specs=pl.BlockSpec((1, 1, nc, HEAD_DIM), lambda b, k: (b, k, 0, 0)),
        compiler_params=_cparams(("arbitrary", "arbitrary")),
        name="compress",
    )(zc, w1, w2, cmp_pos.reshape(1, CMP_LEN * HEAD_DIM), cos_c, sin_c)


FOX_T = 512


def _softmax_step(s, m, acc, v):
    m_new = jnp.maximum(m, jnp.max(s, axis=1, keepdims=True))
    alpha = jnp.exp2(m - m_new)
    p = jnp.exp2((s - m_new).astype(jnp.bfloat16))
    va = jnp.concatenate([v, jnp.ones_like(v)], axis=1)
    return m_new, alpha * acc + _dot(p, va)


def _softmax_init(rows):
    return (jnp.full((rows, 1), MASK_VALUE, jnp.float32),
            jnp.zeros((rows, 2 * HEAD_DIM), jnp.float32))


def _softmax_result(acc):
    return acc[:, 0:HEAD_DIM] * (1.0 / acc[:, HEAD_DIM:2 * HEAD_DIM])


FOX_HB = 4


def _fox_kernel(q_ref, k_ref, v_ref, cum_ref, *rest, n_cast):
    o_ref = rest[n_cast]
    for w_ref, wb_ref in zip(rest[:n_cast], rest[n_cast + 1:]):
        wb_ref[...] = w_ref[...].astype(wb_ref.dtype)
    hb = pl.program_id(1)
    i = pl.program_id(2)
    t = FOX_T
    cols = [slice(n * HEAD_DIM, (n + 1) * HEAD_DIM) for n in range(FOX_HB)]
    qs = [q_ref[0, :, c] for c in cols]

    def tile(k0, width, carry, diag):
        k0 = pl.multiple_of(k0, t)
        logits = [_dot_nt(qs[n], k_ref[0, pl.ds(k0, width), cols[n]]) for n in range(FOX_HB)]
        out = []
        for n, (m, acc) in enumerate(carry):
            v = v_ref[0, pl.ds(k0, width), cols[n]]
            ck = cum_ref[0, pl.ds(hb * FOX_HB + n, 1), pl.ds(k0, width)] * LOG2E
            s = logits[n] - ck
            if diag:
                r = lax.broadcasted_iota(jnp.int32, s.shape, 0)
                c = lax.broadcasted_iota(jnp.int32, s.shape, 1)
                s = jnp.where(c <= r + (width - t), s, MASK_VALUE)
            out.append(_softmax_step(s, m, acc, v))
        return tuple(out)

    init = tuple(_softmax_init(t) for _ in range(FOX_HB))
    n_pairs = i >> 1
    carry = lax.fori_loop(0, n_pairs, lambda p, c: tile(p * (2 * t), 2 * t, c, False), init)
    k_tail = n_pairs * (2 * t)
    carry = lax.cond((i & 1) == 0,
                     lambda c: tile(k_tail, t, c, True),
                     lambda c: tile(k_tail, 2 * t, c, True), carry)
    for n, (_, acc) in enumerate(carry):
        o_ref[0, :, cols[n]] = _softmax_result(acc).astype(o_ref.dtype)


def _fox(z_main, cum, weights):
    nb, s, _ = z_main.shape
    w = FOX_HB * HEAD_DIM
    grid = (nb, FOX_HEADS // FOX_HB, s // FOX_T)
    n_steps = grid[0] * grid[1] * grid[2]
    step = lambda b, h, i: ((b * grid[1] + h) * grid[2] + i, 0)
    slabs = []
    for wt in weights:
        rows = wt.shape[0] // n_steps
        assert rows * n_steps == wt.shape[0] and rows % BF16_SUBLANES == 0, wt.shape
        slabs.append(pl.BlockSpec((rows, wt.shape[1]), step))
    outs = pl.pallas_call(
        functools.partial(_fox_kernel, n_cast=len(weights)),
        out_shape=[jax.ShapeDtypeStruct((nb, s, FOX_W), jnp.bfloat16)]
        + [jax.ShapeDtypeStruct(wt.shape, jnp.bfloat16) for wt in weights],
        grid=grid,
        in_specs=[
            pl.BlockSpec((1, FOX_T, w), lambda b, h, i: (b, i, h)),
            pl.BlockSpec((1, s, w), lambda b, h, i: (b, 0, FOX_W // w + h)),
            pl.BlockSpec((1, s, w), lambda b, h, i: (b, 0, 2 * FOX_W // w + h)),
            pl.BlockSpec((1, FOX_HEADS, s), lambda b, h, i: (b, 0, 0)),
        ] + slabs,
        out_specs=[pl.BlockSpec((1, FOX_T, w), lambda b, h, i: (b, i, h))] + slabs,
        compiler_params=_cparams(("arbitrary", "arbitrary", "arbitrary")),
        name="fox",
    )(z_main, z_main, z_main, cum, *weights)
    return outs[0], outs[1:]


NSA_TQ = 512
NSA_PARTS = 4
NSEL_PAD = 128
SLC_SHIFT = SLC_LEN.bit_length() - 1
assert 1 << SLC_SHIFT == SLC_LEN
assert WINDOW % NSA_TQ == 0 and NSA_TQ % SLC_LEN == 0
WIN_T = 256
assert NSA_TQ % WIN_T == 0 and WINDOW % WIN_T == 0
SUBLANES = 8
BF16_SUBLANES = 16


def _block_ranks(score, n_slc):
    tq = score.shape[1]
    groups = [score[SUBLANES * v:SUBLANES * (v + 1)] for v in range(n_slc // SUBLANES)]
    ranks = [jnp.zeros((SUBLANES, tq), jnp.float32) for _ in groups]
    for m in range(n_slc):
        row = jnp.broadcast_to(score[m:m + 1, :], (SUBLANES, tq))
        for v, grp in enumerate(groups):
            lo = SUBLANES * v
            if lo > m:
                before = row >= grp
            elif lo + SUBLANES - 1 <= m:
                before = row > grp
            else:
                n_i = lo + lax.broadcasted_iota(jnp.int32, grp.shape, 0)
                before = (row > grp) | ((row == grp) & (n_i > m))
            ranks[v] = ranks[v] + jnp.where(before, 1.0, 0.0)
    return jnp.concatenate(ranks, axis=0)


def _nsa_kernel(q_ref, zs_ref, kc_ref, vc_ref, ks_ref, vs_ref, kw_ref, vw_ref, ovt_ref,
                o_ref, ka_ref, *, seq):
    g = pl.program_id(1)
    i = pl.program_id(2)
    tq = NSA_TQ
    q0 = i * tq

    @pl.when(i == 0)
    def _():
        ka_ref[:, 0:HEAD_DIM] = ks_ref[0]
        tok = lax.broadcasted_iota(jnp.int32, (seq, NSEL_PAD), 0)
        n_i = lax.broadcasted_iota(jnp.int32, (seq, NSEL_PAD), 1)
        ka_ref[:, HEAD_DIM:HEAD_DIM + NSEL_PAD] = jnp.where(
            n_i == (tok >> SLC_SHIFT), 1.0, 0.0).astype(jnp.bfloat16)

    qb = q_ref[0]
    hpp = NSA_GROUP // NSA_PARTS
    parts = range(NSA_PARTS)
    qh = [jnp.concatenate(
        [qb[:, r * HEAD_DIM:(r + 1) * HEAD_DIM] for r in range(h * hpp, (h + 1) * hpp)], axis=0)
        for h in parts]
    prow = hpp * tq

    def row_query(shape):
        return lax.broadcasted_iota(jnp.int32, shape, 0) & (tq - 1)

    kc = kc_ref[0, 0]
    vc = vc_ref[0, 0]
    s_cs = [_dot_nt(qh[h], kc) for h in parts]
    p_cs = []
    for s_c in s_cs:
        cend = lax.broadcasted_iota(jnp.int32, s_c.shape, 1) * CMP_STRIDE + (CMP_LEN - 1)
        s_c = jnp.where(cend <= q0 + row_query(s_c.shape), s_c, MASK_VALUE)
        m_c = jnp.maximum(jnp.max(s_c, axis=1, keepdims=True), 0.1 * MASK_VALUE)
        e_c = jnp.exp2(s_c - m_c)
        p_cs.append(e_c * (1.0 / jnp.maximum(jnp.sum(e_c, axis=1, keepdims=True), 1e-30)))
    o_c = [_dot(p_c.astype(jnp.bfloat16), vc) for p_c in p_cs]

    psum = None
    for p_c in p_cs:
        for r in range(hpp):
            term = p_c[r * tq:(r + 1) * tq]
            psum = term if psum is None else psum + term
    p_hi = psum.astype(jnp.bfloat16)
    p_lo = (psum - p_hi.astype(jnp.float32)).astype(jnp.bfloat16)
    ovt = ovt_ref[...]
    imp_t = _dot_nt(ovt, p_hi) + _dot_nt(ovt, p_lo)
    n_slc = seq // SLC_LEN
    blk = lax.broadcasted_iota(jnp.int32, imp_t.shape, 0)
    cur = (q0 + lax.broadcasted_iota(jnp.int32, imp_t.shape, 1)) >> SLC_SHIFT
    forced = (blk == 0) | (blk == cur) | (blk == cur - 1)
    score = jnp.where(forced, SEL_BONUS, imp_t)
    score = jnp.where(blk <= cur, score, -SEL_BONUS)
    def live_ranks(k):
        n_live = min((k + 1) * (tq // SLC_LEN), n_slc)

        def ranks():
            r = _block_ranks(score[0:n_live], n_live)
            if n_live < n_slc:
                r = jnp.concatenate([r, jnp.zeros((n_slc - n_live, tq), jnp.float32)], axis=0)
            return r
        return ranks

    rank = lax.switch(i, [live_ranks(k) for k in range(seq // tq)])
    bias_t = jnp.where(rank < min(SLC_TOPK, n_slc), 0.0, MASK_VALUE)
    bias_t = jnp.concatenate(
        [bias_t, jnp.zeros((NSEL_PAD - n_slc, tq), jnp.float32)], axis=0)
    bias = bias_t.T.astype(jnp.bfloat16)
    bias_p = jnp.concatenate([bias] * hpp, axis=0)
    qa = [jnp.concatenate([qh[h], bias_p], axis=1) for h in parts]

    def sel_tile(k0, width, carry, diag):
        k0 = pl.multiple_of(k0, tq)
        logits = [_dot_nt(qa[h], ka_ref[pl.ds(k0, width), :]) for h in parts]
        out = []
        for s, (m, acc) in zip(logits, carry):
            if diag:
                c = lax.broadcasted_iota(jnp.int32, s.shape, 1)
                s = jnp.where(c <= row_query(s.shape) + (width - tq), s, MASK_VALUE)
            out.append(_softmax_step(s, m, acc, vs_ref[0, pl.ds(k0, width), :]))
        return tuple(out)

    init = tuple(_softmax_init(prow) for _ in parts)
    n_pairs = i >> 1
    carry = lax.fori_loop(0, n_pairs,
                          lambda p, c: sel_tile(p * (2 * tq), 2 * tq, c, False), init)
    k_tail = n_pairs * (2 * tq)
    carry = lax.cond((i & 1) == 0,
                     lambda c: sel_tile(k_tail, tq, c, True),
                     lambda c: sel_tile(k_tail, 2 * tq, c, True), carry)
    o_s = [_softmax_result(acc) for _, acc in carry]

    wt = WIN_T
    n_sub, n_wt = tq // wt, WINDOW // wt + 1
    tile_start = [q0 - WINDOW + k * wt for k in range(n_sub + n_wt - 1)]
    tile_off = [pl.multiple_of(jnp.maximum(st, 0), wt) for st in tile_start]

    def sub_rows(x, j):
        return jnp.concatenate(
            [x[hd * tq + j * wt:hd * tq + (j + 1) * wt] for hd in range(hpp)], axis=0)

    s_w = {(j, t, h): _dot_nt(sub_rows(qh[h], j), kw_ref[0, pl.ds(tile_off[j + t], wt), :])
           for j in range(n_sub) for t in range(n_wt) for h in parts}
    for (j, t, h), s in s_w.items():
        r = lax.broadcasted_iota(jnp.int32, s.shape, 0) & (wt - 1)
        c = lax.broadcasted_iota(jnp.int32, s.shape, 1)
        if t == 0:
            s = jnp.where(c > r, s, MASK_VALUE)
        if t == n_wt - 1:
            s = jnp.where(c <= r, s, MASK_VALUE)
        else:
            s = jnp.where(tile_start[j + t] >= 0, s, MASK_VALUE)
        s_w[j, t, h] = s
    o_w = []
    for h in parts:
        subs = []
        for j in range(n_sub):
            m_w = s_w[j, 0, h].max(axis=1, keepdims=True)
            for t in range(1, n_wt):
                m_w = jnp.maximum(m_w, s_w[j, t, h].max(axis=1, keepdims=True))
            acc_w = jnp.zeros((hpp * wt, 2 * HEAD_DIM), jnp.float32)
            for t in range(n_wt):
                p = jnp.exp2((s_w[j, t, h] - m_w).astype(jnp.bfloat16))
                v = vw_ref[0, pl.ds(tile_off[j + t], wt), :]
                acc_w = acc_w + _dot(p, jnp.concatenate([v, jnp.ones_like(v)], axis=1))
            subs.append(_softmax_result(acc_w))
        o_w.append(jnp.concatenate(
            [subs[j][hd * wt:(hd + 1) * wt] for hd in range(hpp) for j in range(n_sub)], axis=0))

    gates = jax.nn.sigmoid(zs_ref[0])
    lane = lax.broadcasted_iota(jnp.int32, gates.shape, 1)
    outs = []
    for r in range(NSA_GROUP):
        h, rows = r // hpp, slice((r % hpp) * tq, (r % hpp + 1) * tq)
        o_r = jnp.zeros((tq, HEAD_DIM), jnp.float32)
        for br, o_b in enumerate((o_c, o_s, o_w)):
            col = FOX_HEADS + 3 * (g * NSA_GROUP + r) + br
            gcol = jnp.sum(jnp.where(lane == col, gates, 0.0), axis=1, keepdims=True)
            o_r = o_r + gcol * o_b[h][rows]
        outs.append(o_r)
    o_ref[0] = jnp.concatenate(outs, axis=1).astype(o_ref.dtype)


def _nsa(z_main, z_small, kvc, ovt):
    nb, s, _ = z_main.shape
    nc = kvc.shape[2]
    gw = NSA_GROUP * HEAD_DIM
    full = lambda col0: pl.BlockSpec(
        (1, s, HEAD_DIM), lambda b, g, i, c=col0 // HEAD_DIM: (b, 0, c + g))
    return pl.pallas_call(
        functools.partial(_nsa_kernel, seq=s),
        out_shape=jax.ShapeDtypeStruct((nb, s, NSA_W), jnp.bfloat16),
        grid=(nb, NSA_KV_HEADS, s // NSA_TQ),
        in_specs=[
            pl.BlockSpec((1, NSA_TQ, gw), lambda b, g, i: (b, i, COL_QN // gw + g)),
            pl.BlockSpec((1, NSA_TQ, Z_SMALL), lambda b, g, i: (b, i, 0)),
            pl.BlockSpec((1, 1, nc, HEAD_DIM), lambda b, g, i: (b, g, 0, 0)),
            pl.BlockSpec((1, 1, nc, HEAD_DIM), lambda b, g, i: (b, NSA_KV_HEADS + g, 0, 0)),
            full(COL_KS), full(COL_VS), full(COL_KW), full(COL_VW),
            pl.BlockSpec(ovt.shape, lambda b, g, i: (0, 0)),
        ],
        out_specs=pl.BlockSpec((1, NSA_TQ, gw), lambda b, g, i: (b, i, g)),
        scratch_shapes=[pltpu.VMEM((s, HEAD_DIM + NSEL_PAD), jnp.bfloat16)],
        compiler_params=_cparams(("arbitrary", "arbitrary", "arbitrary")),
        name="nsa",
    )(z_main, z_small, kvc, kvc, z_main, z_main, z_main, z_main, ovt)


OUT_TM = 512
OUT_RC = 256


def _outproj_kernel(of_ref, on_ref, bf_ref, bn_ref, w_ref, x_ref, g_ref, nw_ref, sc_ref,
                    sh_ref, h_ref, u_ref):
    for r in range(OUT_TM // OUT_RC):
        rows = slice(r * OUT_RC, (r + 1) * OUT_RC)
        yf = _rms(of_ref[0, rows, :].astype(jnp.float32)) * bf_ref[...]
        yn = _rms(on_ref[0, rows, :].astype(jnp.float32)) * bn_ref[...]
        y = jnp.concatenate([yf.astype(jnp.bfloat16), yn.astype(jnp.bfloat16)], axis=1)
        h = x_ref[0, rows, :] + g_ref[0] * _dot(y, w_ref[...])
        h_ref[0, rows, :] = h
        u = _rms(h) * nw_ref[...]
        u_ref[0, rows, :] = (u * (1.0 + sc_ref[0]) + sh_ref[0]).astype(jnp.bfloat16)


def _outproj(o_f, o_n, beta_f, beta_n, w_out, x, gate, norm_w, sc, sh):
    nb, s, d = x.shape
    vec = lambda n: pl.BlockSpec((1, n), lambda b, i: (0, 0))
    bvec = pl.BlockSpec((1, 1, d), lambda b, i: (b, 0, 0))
    rows = lambda n: pl.BlockSpec((1, OUT_TM, n), lambda b, i: (b, i, 0))
    return pl.pallas_call(
        _outproj_kernel,
        out_shape=(jax.ShapeDtypeStruct((nb, s, d), jnp.float32),
                   jax.ShapeDtypeStruct((nb, s, d), jnp.bfloat16)),
        grid=(nb, s // OUT_TM),
        in_specs=[
            rows(FOX_W), rows(NSA_W), vec(FOX_W), vec(NSA_W),
            pl.BlockSpec((FOX_W + NSA_W, d), lambda b, i: (0, 0),
                         pipeline_mode=pl.Buffered(1)),
            rows(d), bvec, vec(d), bvec, bvec,
        ],
        out_specs=(rows(d), rows(d)),
        compiler_params=_cparams(("arbitrary", "arbitrary")),
        name="outproj",
    )(o_f, o_n, beta_f.reshape(1, FOX_W), beta_n.reshape(1, NSA_W), w_out, x,
      gate.reshape(nb, 1, d), norm_w.reshape(1, d), sc.reshape(nb, 1, d),
      sh.reshape(nb, 1, d))


FFN_TM = 1024
FFN_TF = 512
FFN_RC = 512
FFN_RC_LAST = 256


def _ffn_kernel(u_ref, h_ref, g_ref, fw_ref, wg_ref, wu_ref, wd_ref, o_ref, *, final):
    j = pl.program_id(2)
    last = pl.num_programs(2) - 1

    def partial_sum(rows):
        u = u_ref[0, rows, :]
        a = _dot(u, wg_ref[...])
        b = _dot(u, wu_ref[...])
        mid = (a * jax.nn.sigmoid(a) * b).astype(jnp.bfloat16)
        return _dot(mid, wd_ref[...])

    def chunks(rc):
        return [slice(r * rc, (r + 1) * rc) for r in range(FFN_TM // rc)]

    @pl.when(j == 0)
    def _():
        for rows in chunks(FFN_RC):
            o_ref[0, rows, :] = partial_sum(rows)

    @pl.when((j > 0) & (j < last))
    def _():
        for rows in chunks(FFN_RC):
            o_ref[0, rows, :] += partial_sum(rows)

    @pl.when(j == last)
    def _():
        for rows in chunks(FFN_RC_LAST):
            h2 = h_ref[0, rows, :] + g_ref[0] * (o_ref[0, rows, :] + partial_sum(rows))
            o_ref[0, rows, :] = _rms(h2) * fw_ref[...] if final else h2


def _ffn(u2, h1, gate, final_w, w_gate, w_up, w_down, final):
    nb, s, d = h1.shape
    f = w_gate.shape[1]
    return pl.pallas_call(
        functools.partial(_ffn_kernel, final=final),
        out_shape=jax.ShapeDtypeStruct((nb, s, d), jnp.float32),
        grid=(nb, s // FFN_TM, f // FFN_TF),
        in_specs=[
            pl.BlockSpec((1, FFN_TM, d), lambda b, i, j: (b, i, 0)),
            pl.BlockSpec((1, FFN_TM, d), lambda b, i, j: (b, i, 0),
                         pipeline_mode=pl.Buffered(1)),
            pl.BlockSpec((1, 1, d), lambda b, i, j: (b, 0, 0)),
            pl.BlockSpec((1, d), lambda b, i, j: (0, 0)),
            pl.BlockSpec((d, FFN_TF), lambda b, i, j: (0, j)),
            pl.BlockSpec((d, FFN_TF), lambda b, i, j: (0, j)),
            pl.BlockSpec((FFN_TF, d), lambda b, i, j: (j, 0)),
        ],
        out_specs=pl.BlockSpec((1, FFN_TM, d), lambda b, i, j: (b, i, 0)),
        compiler_params=_cparams(("arbitrary", "arbitrary", "arbitrary"), BIG_VMEM_LIMIT),
        name="ffn",
    )(u2, h1, gate.reshape(nb, 1, d), final_w.reshape(1, d), w_gate, w_up, w_down)


def _split_w_in(w_in):
    sizes = ([FOX_W] * 3 + [FOX_HEADS] + [NSA_W] + [KV_W] * 6 + [3 * NSA_HEADS])
    offs = [0]
    for sz in sizes:
        offs.append(offs[-1] + sz)
    wt = w_in.T
    order = (0, 1, 2, 4, 7, 9, 8, 10, 5, 6)
    units = lambda ns, unit: [offs[n] + r for n in ns for r in range(0, sizes[n], unit)]
    main = _gather_rows(wt, units(order, KV_W), KV_W, jnp.bfloat16)
    narrow = _gather_rows(wt, units((3, 11), SUBLANES), SUBLANES, jnp.float32)
    small = jnp.concatenate(
        [narrow, jnp.zeros((Z_SMALL - narrow.shape[0], w_in.shape[0]), jnp.float32)], axis=0)
    return main, small.astype(jnp.bfloat16)


def _cast_kernel(rows_ref, w_ref, o_ref):
    del rows_ref
    o_ref[...] = w_ref[...].astype(o_ref.dtype)


def _gather_rows(wt, src_rows, unit, dtype):
    assert all(r % SUBLANES == 0 for r in src_rows), src_rows
    tiles = jnp.asarray([r // SUBLANES for r in src_rows], jnp.int32)
    n_units = len(src_rows)
    d = wt.shape[1]
    return pl.pallas_call(
        _cast_kernel,
        out_shape=jax.ShapeDtypeStruct((n_units * unit, d), dtype),
        grid_spec=pltpu.PrefetchScalarGridSpec(
            num_scalar_prefetch=1,
            grid=(n_units,),
            in_specs=[pl.BlockSpec((pl.Element(unit), pl.Element(d)),
                                   lambda u, rows: (rows[u] * SUBLANES, 0))],
            out_specs=pl.BlockSpec((unit, d), lambda u, rows: (u, 0)),
        ),
        compiler_params=_cparams(("arbitrary",)),
        name="w_in_prep",
    )(tiles, wt)


def _overlap_t(n_cmp_pad, n_slc):
    cs = jnp.arange(n_cmp_pad) * CMP_STRIDE
    ss = jnp.arange(n_slc) * SLC_LEN
    ov = (jnp.minimum(cs[None, :] + CMP_LEN, ss[:, None] + SLC_LEN)
          - jnp.maximum(cs[None, :], ss[:, None]))
    ov = jnp.clip(ov, 0).astype(jnp.float32) / CMP_STRIDE
    return ov.astype(jnp.bfloat16)


def kernel(x, c, positions, w_ada, b_ada, norm_attn, norm_ffn, w_in, b_fgate, cmp_pos,
           w_kc1, w_kc2, w_vc1, w_vc2, beta_fox, beta_nsa, w_out, w_gate, w_up, w_down,
           final_norm):
    nb, s, d = x.shape
    depth = w_ada.shape[0]
    n_cmp_pad = s // CMP_STRIDE
    cmp_end = jnp.arange(n_cmp_pad) * CMP_STRIDE + (CMP_LEN - 1)
    pos_c = jnp.take(positions, jnp.minimum(cmp_end, s - 1), axis=1)
    cos_c, sin_c = _rope_tables(pos_c, n_cmp_pad)
    ovt = _overlap_t(n_cmp_pad, s // SLC_LEN)

    h = x
    for l in range(depth):
        if l == 0:
            mod, cos, sin = _ada(c, w_ada[l], b_ada[l], positions)
        else:
            mod = _ada(c, w_ada[l], b_ada[l])
        sh1, sc1, g1, sh2, sc2, g2 = jnp.split(mod, 6, axis=-1)
        w_main, w_small = _split_w_in(w_in[l])
        z_main, z_small, zc = _inproj(h, norm_attn[l], sc1, sh1, w_main, w_small, cos, sin)
        cum = _decay(z_small, b_fgate[l])

        w1 = jnp.stack([w_kc1[l], w_vc1[l]]).astype(jnp.bfloat16)
        w1 = w1.reshape(2, CMP_LEN, HEAD_DIM, CMP_HIDDEN)
        w2 = jnp.stack([w_kc2[l], w_vc2[l]]).astype(jnp.bfloat16)
        kvc = _compress(zc, w1, w2, cmp_pos[l], cos_c, sin_c)

        o_f, (wo, wg, wu, wd) = _fox(z_main, cum, (w_out[l], w_gate[l], w_up[l], w_down[l]))
        o_n = _nsa(z_main, z_small, kvc, ovt)
        h1, u2 = _outproj(o_f, o_n, beta_fox[l], beta_nsa[l], wo, h, g1, norm_ffn[l], sc2, sh2)
        h = _ffn(u2, h1, g2, final_norm, wg, wu, wd, final=(l == depth - 1))
    return h
```

```python
import functools

import jax
import jax.numpy as jnp
from jax import lax
from jax.experimental import pallas as pl
from jax.experimental.pallas import tpu as pltpu

D_MODEL = 2048
HEAD_DIM = 128
FOX_HEADS = 8
NSA_HEADS = 8
NSA_KV_HEADS = 2
NSA_GROUP = NSA_HEADS // NSA_KV_HEADS
CMP_LEN = 32
CMP_STRIDE = 16
CMP_HIDDEN = 2 * HEAD_DIM
SLC_LEN = 64
SLC_TOPK = 16
WINDOW = 512
ROPE_THETA = 10000.0
NORM_EPS = 1e-6
MASK_VALUE = -1e30
SEL_BONUS = 1e6

FOX_W = FOX_HEADS * HEAD_DIM
NSA_W = NSA_HEADS * HEAD_DIM
KV_W = NSA_KV_HEADS * HEAD_DIM
Z_MAIN = 3 * FOX_W + NSA_W + 6 * KV_W
Z_SMALL = 128
COL_QN = 3 * FOX_W
COL_KS = COL_QN + NSA_W
COL_KW = COL_KS + KV_W
COL_VS = COL_KW + KV_W
COL_VW = COL_VS + KV_W
COL_KC = COL_VW + KV_W

VMEM_LIMIT = 48 * 1024 * 1024
BIG_VMEM_LIMIT = 56 * 1024 * 1024
SCALE = HEAD_DIM ** -0.5
LOG2E = 1.4426950408889634


def _cparams(sem, vmem_limit=VMEM_LIMIT):
    return pltpu.CompilerParams(dimension_semantics=sem, vmem_limit_bytes=vmem_limit)


def _dot(a, b):
    return jnp.dot(a, b, preferred_element_type=jnp.float32)


def _dot_nt(a, b):
    return lax.dot_general(a, b, (((1,), (1,)), ((), ())), preferred_element_type=jnp.float32)


def _rms(x):
    return x * lax.rsqrt(jnp.mean(x * x, axis=-1, keepdims=True) + NORM_EPS)


ADA_TN = 768
ADA_KC = 256


def _ada_kernel(ct_ref, w_ref, b_ref, *rest, nb, with_rope):
    if with_rope:
        pos_ref, inv_ref, o_ref, cos_ref, sin_ref = rest
        _rope_table_kernel(pos_ref, inv_ref, cos_ref, sin_ref)
    else:
        o_ref, = rest
    tn = w_ref.shape[1]
    ct = ct_ref[...]
    st = ct * jax.nn.sigmoid(ct)

    accs = [jnp.zeros((SUBLANES, tn), jnp.float32) for _ in range(nb)]
    for kc in range(D_MODEL // ADA_KC):
        w = w_ref[kc * ADA_KC:(kc + 1) * ADA_KC, :]
        for b in range(nb):
            s = st[kc * ADA_KC:(kc + 1) * ADA_KC, b:b + 1]
            accs[b] = accs[b] + jnp.sum(
                (w * s).reshape(ADA_KC // SUBLANES, SUBLANES, tn), axis=0)
    for b in range(nb):
        o_ref[b:b + 1, :] = jnp.sum(accs[b], axis=0, keepdims=True) + b_ref[...]


def _ada(c, w_ada, b_ada, positions=None):
    nb = c.shape[0]
    n = w_ada.shape[1]
    n_steps = n // ADA_TN
    assert n_steps * ADA_TN == n
    in_specs = [
        pl.BlockSpec((D_MODEL, nb), lambda j: (0, 0)),
        pl.BlockSpec((D_MODEL, ADA_TN), lambda j: (0, j)),
        pl.BlockSpec((1, ADA_TN), lambda j: (0, j)),
    ]
    out_shape = [jax.ShapeDtypeStruct((nb, n), jnp.float32)]
    out_specs = [pl.BlockSpec((nb, ADA_TN), lambda j: (0, j))]
    args = [c.T, w_ada, b_ada.reshape(1, n)]
    if positions is not None:
        s = positions.shape[1]
        per_batch = n_steps // nb
        ts = s // per_batch
        assert per_batch * nb == n_steps and ts * per_batch == s and ts % (2 * SUBLANES) == 0
        rows = lambda j: (j // per_batch, j % per_batch, 0)
        in_specs += [pl.BlockSpec((1, ts, 1), rows), pl.BlockSpec((1, HEAD_DIM), lambda j: (0, 0))]
        tab = jax.ShapeDtypeStruct((nb, s, HEAD_DIM), jnp.float32)
        out_shape += [tab, tab]
        out_specs += [pl.BlockSpec((1, ts, HEAD_DIM), rows)] * 2
        args += [positions.reshape(nb, s, 1), _rope_inv_freq()]
    outs = pl.pallas_call(
        functools.partial(_ada_kernel, nb=nb, with_rope=positions is not None),
        out_shape=out_shape,
        grid=(n_steps,),
        in_specs=in_specs,
        out_specs=out_specs,
        compiler_params=_cparams(("arbitrary",)),
        name="ada",
    )(*args)
    return outs if positions is not None else outs[0]


def _rope_table_kernel(pos_ref, inv_ref, cos_ref, sin_ref):
    half = pos_ref.shape[1] // 2
    lane = lax.broadcasted_iota(jnp.int32, (half, HEAD_DIM), 1)
    low = lane < HEAD_DIM // 2
    pos = jnp.where(low, pos_ref[0, 0:half, :], pos_ref[0, half:2 * half, :])
    ang = pos.astype(jnp.float32) * inv_ref[...]
    cos = jnp.cos(ang)
    sin = jnp.sin(ang)
    cos_sw = pltpu.roll(cos, HEAD_DIM // 2, axis=1)
    sin_sw = pltpu.roll(sin, HEAD_DIM // 2, axis=1)
    cos_ref[0, 0:half, :] = jnp.where(low, cos, cos_sw)
    cos_ref[0, half:2 * half, :] = jnp.where(low, cos_sw, cos)
    sin_ref[0, 0:half, :] = jnp.where(low, -sin, sin_sw)
    sin_ref[0, half:2 * half, :] = jnp.where(low, -sin_sw, sin)


def _rope_inv_freq():
    inv = ROPE_THETA ** (-jnp.arange(0, HEAD_DIM, 2, dtype=jnp.float32) / HEAD_DIM)
    return jnp.concatenate([inv, inv]).reshape(1, HEAD_DIM)


def _rope_tables(pos, ts):
    nb, n = pos.shape
    inv2 = _rope_inv_freq()
    shp = jax.ShapeDtypeStruct((nb, n, HEAD_DIM), jnp.float32)
    spec = pl.BlockSpec((1, ts, HEAD_DIM), lambda b, i: (b, i, 0))
    return pl.pallas_call(
        _rope_table_kernel,
        out_shape=(shp, shp),
        grid=(nb, n // ts),
        in_specs=[pl.BlockSpec((1, ts, 1), lambda b, i: (b, i, 0)),
                  pl.BlockSpec((1, HEAD_DIM), lambda b, i: (0, 0))],
        out_specs=(spec, spec),
        compiler_params=_cparams(("arbitrary", "arbitrary")),
        name="rope_tables",
    )(pos.reshape(nb, n, 1), inv2)


def _apply_rope(x, cos, sin):
    return x * cos + pltpu.roll(x, HEAD_DIM // 2, axis=1) * sin


IN_TM = 1024
IN_TN = 512
IN_RC = 256
assert FOX_W // IN_TN >= 1


def _inproj_kernel(x_ref, g_ref, sc_ref, sh_ref, w_ref, ws_ref, cos_ref, sin_ref,
                   z_ref, zs_ref, zc_ref, u_ref):
    j = pl.program_id(2)

    @pl.when(j == 0)
    def _():
        for r in range(IN_TM // IN_RC):
            rows = slice(r * IN_RC, (r + 1) * IN_RC)
            u = _rms(x_ref[0, rows, :]) * g_ref[...]
            ub = (u * (1.0 + sc_ref[0]) + sh_ref[0]).astype(jnp.bfloat16)
            u_ref[rows, :] = ub
            zs_ref[0, rows, :] = _dot_nt(ub, ws_ref[...])
            z_ref[0, rows, :] = (_dot_nt(ub, w_ref[...]) * (SCALE * LOG2E)).astype(z_ref.dtype)

    fox_q = (j > 0) & (j < FOX_W // IN_TN)
    nsa_q = (j >= COL_QN // IN_TN) & (j < COL_KS // IN_TN)
    nsa_k = j == COL_KS // IN_TN
    cmp_kv = j == COL_KC // IN_TN

    def matmul():
        return _dot_nt(u_ref[...], w_ref[...])

    def roped(acc):
        cos = cos_ref[0]
        sin = sin_ref[0]
        parts = [_apply_rope(acc[:, h * HEAD_DIM:(h + 1) * HEAD_DIM], cos, sin)
                 for h in range(IN_TN // HEAD_DIM)]
        return jnp.concatenate(parts, axis=1).astype(z_ref.dtype)

    @pl.when(fox_q)
    def _():
        z_ref[0] = (matmul() * (SCALE * LOG2E)).astype(z_ref.dtype)

    @pl.when(nsa_q)
    def _():
        z_ref[0] = roped(matmul() * (SCALE * LOG2E))

    @pl.when(nsa_k)
    def _():
        z_ref[0] = roped(matmul())

    @pl.when(cmp_kv)
    def _():
        acc = matmul()
        z_ref[0] = acc.astype(z_ref.dtype)
        zc_ref[0] = acc

    @pl.when(jnp.logical_not((j == 0) | fox_q | nsa_q | nsa_k | cmp_kv))
    def _():
        z_ref[0] = matmul().astype(z_ref.dtype)


def _inproj(x, norm_w, sc, sh, w_main, w_small, cos, sin):
    nb, s, d = x.shape
    grid = (nb, s // IN_TM, Z_MAIN // IN_TN)
    return pl.pallas_call(
        _inproj_kernel,
        out_shape=(jax.ShapeDtypeStruct((nb, s, Z_MAIN), jnp.bfloat16),
                   jax.ShapeDtypeStruct((nb, s, Z_SMALL), jnp.float32),
                   jax.ShapeDtypeStruct((nb, s, IN_TN), jnp.float32)),
        grid=grid,
        in_specs=[
            pl.BlockSpec((1, IN_TM, d), lambda b, i, j: (b, i, 0)),
            pl.BlockSpec((1, d), lambda b, i, j: (0, 0)),
            pl.BlockSpec((1, 1, d), lambda b, i, j: (b, 0, 0)),
            pl.BlockSpec((1, 1, d), lambda b, i, j: (b, 0, 0)),
            pl.BlockSpec((IN_TN, d), lambda b, i, j: (j, 0)),
            pl.BlockSpec((Z_SMALL, d), lambda b, i, j: (0, 0)),
            pl.BlockSpec((1, IN_TM, HEAD_DIM), lambda b, i, j: (b, i, 0)),
            pl.BlockSpec((1, IN_TM, HEAD_DIM), lambda b, i, j: (b, i, 0)),
        ],
        out_specs=(pl.BlockSpec((1, IN_TM, IN_TN), lambda b, i, j: (b, i, j)),
                   pl.BlockSpec((1, IN_TM, Z_SMALL), lambda b, i, j: (b, i, 0)),
                   pl.BlockSpec((1, IN_TM, IN_TN), lambda b, i, j: (b, i, 0))),
        scratch_shapes=[pltpu.VMEM((IN_TM, d), jnp.bfloat16)],
        compiler_params=_cparams(("arbitrary", "arbitrary", "arbitrary")),
        name="inproj",
    )(x, norm_w.reshape(1, d), sc.reshape(nb, 1, d), sh.reshape(nb, 1, d),
      w_main, w_small, cos, sin)


DEC_T = 2048


def _decay_kernel(zs_ref, bf_ref, cum_ref, carry_ref):
    @pl.when(pl.program_id(1) == 0)
    def _():
        carry_ref[...] = jnp.zeros_like(carry_ref)

    xt = zs_ref[0].T[0:FOX_HEADS, :] + bf_ref[...]
    lf = jnp.minimum(xt, 0.0) - jnp.log1p(jnp.exp(-jnp.abs(xt)))
    lane = lax.broadcasted_iota(jnp.int32, lf.shape, 1)
    sh = 1
    while sh < DEC_T:
        lf = lf + jnp.where(lane >= sh, pltpu.roll(lf, sh, axis=1), 0.0)
        sh *= 2
    cum = lf + carry_ref[:, 0:1]
    cum_ref[0] = cum
    carry_ref[...] = jnp.broadcast_to(cum[:, DEC_T - 1:DEC_T], carry_ref.shape)


def _decay(z_small, b_fgate):
    nb, s, _ = z_small.shape
    return pl.pallas_call(
        _decay_kernel,
        out_shape=jax.ShapeDtypeStruct((nb, FOX_HEADS, s), jnp.float32),
        grid=(nb, s // DEC_T),
        in_specs=[pl.BlockSpec((1, DEC_T, Z_SMALL), lambda b, i: (b, i, 0)),
                  pl.BlockSpec((FOX_HEADS, 1), lambda b, i: (0, 0))],
        out_specs=pl.BlockSpec((1, FOX_HEADS, DEC_T), lambda b, i: (b, 0, i)),
        scratch_shapes=[pltpu.VMEM((FOX_HEADS, 128), jnp.float32)],
        compiler_params=_cparams(("arbitrary", "arbitrary")),
        name="decay",
    )(z_small, b_fgate.reshape(FOX_HEADS, 1))


def _compress_kernel(x_ref, w1_ref, w2_ref, pos_ref, cos_ref, sin_ref, o_ref):
    kind = pl.program_id(1)
    nc = x_ref.shape[1] // CMP_STRIDE
    p = jnp.zeros((nc, CMP_HIDDEN), jnp.float32)
    q = jnp.zeros((nc, CMP_HIDDEN), jnp.float32)
    for l in range(0, CMP_STRIDE, 2):
        xl = jnp.concatenate(
            [x_ref[0, pl.ds(l + k, nc, stride=CMP_STRIDE), :].astype(jnp.bfloat16)
             for k in range(2)], axis=1)
        pair = lambda l0: w1_ref[0, l0:l0 + 2].reshape(2 * HEAD_DIM, CMP_HIDDEN)
        p = p + _dot(xl, pair(l))
        q = q + _dot(xl, pair(CMP_STRIDE + l))
    posb = jnp.broadcast_to(pos_ref[...], (8, CMP_LEN * HEAD_DIM)).astype(jnp.bfloat16)
    w1_flat = w1_ref[0].reshape(CMP_LEN * HEAD_DIM, CMP_HIDDEN)
    pterm = _dot(posb, w1_flat)[0:1, :]
    h = p + pltpu.roll(q, nc - 1, axis=0) + pterm
    a = jax.nn.gelu(h).astype(jnp.bfloat16)
    out = _dot(a, w2_ref[0])

    @pl.when(kind < NSA_KV_HEADS)
    def _():
        o_ref[0, 0] = _apply_rope(out, cos_ref[0], sin_ref[0]).astype(o_ref.dtype)

    @pl.when(kind >= NSA_KV_HEADS)
    def _():
        o_ref[0, 0] = out.astype(o_ref.dtype)


def _compress(zc, w1, w2, cmp_pos, cos_c, sin_c):
    nb, s, _ = zc.shape
    nk = 2 * NSA_KV_HEADS
    nc = s // CMP_STRIDE
    return pl.pallas_call(
        _compress_kernel,
        out_shape=jax.ShapeDtypeStruct((nb, nk, nc, HEAD_DIM), jnp.bfloat16),
        grid=(nb, nk),
        in_specs=[
            pl.BlockSpec((1, s, HEAD_DIM), lambda b, k: (b, 0, k)),
            pl.BlockSpec((1, CMP_LEN, HEAD_DIM, CMP_HIDDEN),
                         lambda b, k: (k // NSA_KV_HEADS, 0, 0, 0)),
            pl.BlockSpec((1, CMP_HIDDEN, HEAD_DIM), lambda b, k: (k // NSA_KV_HEADS, 0, 0)),
            pl.BlockSpec((1, CMP_LEN * HEAD_DIM), lambda b, k: (0, 0)),
            pl.BlockSpec((1, nc, HEAD_DIM), lambda b, k: (b, 0, 0)),
            pl.BlockSpec((1, nc, HEAD_DIM), lambda b, k: (b, 0, 0)),
        ],
        out_specs=pl.BlockSpec((1, 1, nc, HEAD_DIM), lambda b, k: (b, k, 0, 0)),
        compiler_params=_cparams(("arbitrary", "arbitrary")),
        name="compress",
    )(zc, w1, w2, cmp_pos.reshape(1, CMP_LEN * HEAD_DIM), cos_c, sin_c)


FOX_T = 512


def _softmax_step(s, m, acc, v):
    m_new = jnp.maximum(m, jnp.max(s, axis=1, keepdims=True))
    alpha = jnp.exp2(m - m_new)
    p = jnp.exp2((s - m_new).astype(jnp.bfloat16))
    va = jnp.concatenate([v, jnp.ones_like(v)], axis=1)
    return m_new, alpha * acc + _dot(p, va)


def _softmax_init(rows):
    return (jnp.full((rows, 1), MASK_VALUE, jnp.float32),
            jnp.zeros((rows, 2 * HEAD_DIM), jnp.float32))


def _softmax_result(acc):
    return acc[:, 0:HEAD_DIM] * (1.0 / acc[:, HEAD_DIM:2 * HEAD_DIM])


FOX_HB = 4
FOX_SUB = 256


def _fox_kernel(q_ref, k_ref, v_ref, cum_ref, *rest, n_cast):
    o_ref = rest[n_cast]
    for w_ref, wb_ref in zip(rest[:n_cast], rest[n_cast + 1:]):
        wb_ref[...] = w_ref[...].astype(wb_ref.dtype)
    hb = pl.program_id(1)
    i = pl.program_id(2)
    t = FOX_T
    cols = [slice(n * HEAD_DIM, (n + 1) * HEAD_DIM) for n in range(FOX_HB)]
    qs = [q_ref[0, :, c] for c in cols]

    def tile(k0, width, carry, diag):
        k0 = pl.multiple_of(k0, t)
        logits = [_dot_nt(qs[n], k_ref[0, pl.ds(k0, width), cols[n]]) for n in range(FOX_HB)]
        out = []
        for n, (m, acc) in enumerate(carry):
            v = v_ref[0, pl.ds(k0, width), cols[n]]
            ck = cum_ref[0, pl.ds(hb * FOX_HB + n, 1), pl.ds(k0, width)] * LOG2E
            s = logits[n] - ck
            if diag:
                r = lax.broadcasted_iota(jnp.int32, s.shape, 0)
                c = lax.broadcasted_iota(jnp.int32, s.shape, 1)
                s = jnp.where(c <= r + (width - t), s, MASK_VALUE)
            out.append(_softmax_step(s, m, acc, v))
        return tuple(out)

    def diag_tile(k0, width, carry):
        k0 = pl.multiple_of(k0, t)
        sub = FOX_SUB
        subs = [(slice(j * sub, (j + 1) * sub), width - t + (j + 1) * sub) for j in range(t // sub)]
        logits = [[_dot_nt(qs[n][rows], k_ref[0, pl.ds(k0, wj), cols[n]]) for rows, wj in subs]
                  for n in range(FOX_HB)]
        r = lax.broadcasted_iota(jnp.int32, (sub, sub), 0)
        c = lax.broadcasted_iota(jnp.int32, (sub, sub), 1)
        out = []
        for n, (m, acc) in enumerate(carry):
            ms, accs = [], []
            for (rows, wj), lg in zip(subs, logits[n]):
                ck = cum_ref[0, pl.ds(hb * FOX_HB + n, 1), pl.ds(k0, wj)] * LOG2E
                s = lg - ck
                own = jnp.where(c <= r, s[:, wj - sub:], MASK_VALUE)
                s = own if wj == sub else jnp.concatenate([s[:, :wj - sub], own], axis=1)
                m_j, acc_j = _softmax_step(s, m[rows], acc[rows],
                                           v_ref[0, pl.ds(k0, wj), cols[n]])
                ms.append(m_j)
                accs.append(acc_j)
            out.append((jnp.concatenate(ms, axis=0), jnp.concatenate(accs, axis=0)))
        return tuple(out)

    init = tuple(_softmax_init(t) for _ in range(FOX_HB))
    n_pairs = i >> 1
    carry = lax.fori_loop(0, n_pairs, lambda p, c: tile(p * (2 * t), 2 * t, c, False), init)
    k_tail = n_pairs * (2 * t)
    carry = lax.cond((i & 1) == 0,
                     lambda c: diag_tile(k_tail, t, c),
                     lambda c: diag_tile(k_tail, 2 * t, c), carry)
    for n, (_, acc) in enumerate(carry):
        o_ref[0, :, cols[n]] = _softmax_result(acc).astype(o_ref.dtype)


def _fox(z_main, cum, weights):
    nb, s, _ = z_main.shape
    w = FOX_HB * HEAD_DIM
    grid = (nb, FOX_HEADS // FOX_HB, s // FOX_T)
    n_steps = grid[0] * grid[1] * grid[2]
    step = lambda b, h, i: ((b * grid[1] + h) * grid[2] + i, 0)
    slabs = []
    for wt in weights:
        rows = wt.shape[0] // n_steps
        assert rows * n_steps == wt.shape[0] and rows % BF16_SUBLANES == 0, wt.shape
        slabs.append(pl.BlockSpec((rows, wt.shape[1]), step))
    outs = pl.pallas_call(
        functools.partial(_fox_kernel, n_cast=len(weights)),
        out_shape=[jax.ShapeDtypeStruct((nb, s, FOX_W), jnp.bfloat16)]
        + [jax.ShapeDtypeStruct(wt.shape, jnp.bfloat16) for wt in weights],
        grid=grid,
        in_specs=[
            pl.BlockSpec((1, FOX_T, w), lambda b, h, i: (b, i, h)),
            pl.BlockSpec((1, s, w), lambda b, h, i: (b, 0, FOX_W // w + h)),
            pl.BlockSpec((1, s, w), lambda b, h, i: (b, 0, 2 * FOX_W // w + h)),
            pl.BlockSpec((1, FOX_HEADS, s), lambda b, h, i: (b, 0, 0)),
        ] + slabs,
        out_specs=[pl.BlockSpec((1, FOX_T, w), lambda b, h, i: (b, i, h))] + slabs,
        compiler_params=_cparams(("arbitrary", "arbitrary", "arbitrary")),
        name="fox",
    )(z_main, z_main, z_main, cum, *weights)
    return outs[0], outs[1:]


NSA_TQ = 512
NSA_PARTS = 4
NSEL_PAD = 128
SLC_SHIFT = SLC_LEN.bit_length() - 1
assert 1 << SLC_SHIFT == SLC_LEN
assert WINDOW % NSA_TQ == 0 and NSA_TQ % SLC_LEN == 0
WIN_T = 256
assert NSA_TQ % WIN_T == 0 and WINDOW % WIN_T == 0
SUBLANES = 8
BF16_SUBLANES = 16


def _block_ranks(score, n_slc):
    tq = score.shape[1]
    groups = [score[SUBLANES * v:SUBLANES * (v + 1)] for v in range(n_slc // SUBLANES)]
    ranks = [jnp.zeros((SUBLANES, tq), jnp.float32) for _ in groups]
    for m in range(n_slc):
        row = jnp.broadcast_to(score[m:m + 1, :], (SUBLANES, tq))
        for v, grp in enumerate(groups):
            lo = SUBLANES * v
            if lo > m:
                before = row >= grp
            elif lo + SUBLANES - 1 <= m:
                before = row > grp
            else:
                n_i = lo + lax.broadcasted_iota(jnp.int32, grp.shape, 0)
                before = (row > grp) | ((row == grp) & (n_i > m))
            ranks[v] = ranks[v] + jnp.where(before, 1.0, 0.0)
    return jnp.concatenate(ranks, axis=0)


def _nsa_kernel(q_ref, zs_ref, kc_ref, vc_ref, ks_ref, vs_ref, kw_ref, vw_ref, ovt_ref,
                o_ref, ka_ref, *, seq):
    g = pl.program_id(1)
    i = pl.program_id(2)
    tq = NSA_TQ
    q0 = i * tq

    @pl.when(i == 0)
    def _():
        ka_ref[:, 0:HEAD_DIM] = ks_ref[0]
        tok = lax.broadcasted_iota(jnp.int32, (seq, NSEL_PAD), 0)
        n_i = lax.broadcasted_iota(jnp.int32, (seq, NSEL_PAD), 1)
        ka_ref[:, HEAD_DIM:HEAD_DIM + NSEL_PAD] = jnp.where(
            n_i == (tok >> SLC_SHIFT), 1.0, 0.0).astype(jnp.bfloat16)

    qb = q_ref[0]
    hpp = NSA_GROUP // NSA_PARTS
    parts = range(NSA_PARTS)
    qh = [jnp.concatenate(
        [qb[:, r * HEAD_DIM:(r + 1) * HEAD_DIM] for r in range(h * hpp, (h + 1) * hpp)], axis=0)
        for h in parts]
    prow = hpp * tq

    def row_query(shape):
        return lax.broadcasted_iota(jnp.int32, shape, 0) & (tq - 1)

    kc = kc_ref[0, 0]
    vc = vc_ref[0, 0]
    s_cs = [_dot_nt(qh[h], kc) for h in parts]
    p_cs = []
    for s_c in s_cs:
        cend = lax.broadcasted_iota(jnp.int32, s_c.shape, 1) * CMP_STRIDE + (CMP_LEN - 1)
        s_c = jnp.where(cend <= q0 + row_query(s_c.shape), s_c, MASK_VALUE)
        m_c = jnp.maximum(jnp.max(s_c, axis=1, keepdims=True), 0.1 * MASK_VALUE)
        e_c = jnp.exp2(s_c - m_c)
        p_cs.append(e_c * (1.0 / jnp.maximum(jnp.sum(e_c, axis=1, keepdims=True), 1e-30)))
    o_c = [_dot(p_c.astype(jnp.bfloat16), vc) for p_c in p_cs]

    psum = None
    for p_c in p_cs:
        for r in range(hpp):
            term = p_c[r * tq:(r + 1) * tq]
            psum = term if psum is None else psum + term
    p_hi = psum.astype(jnp.bfloat16)
    p_lo = (psum - p_hi.astype(jnp.float32)).astype(jnp.bfloat16)
    ovt = ovt_ref[...]
    imp_t = _dot_nt(ovt, p_hi) + _dot_nt(ovt, p_lo)
    n_slc = seq // SLC_LEN
    blk = lax.broadcasted_iota(jnp.int32, imp_t.shape, 0)
    cur = (q0 + lax.broadcasted_iota(jnp.int32, imp_t.shape, 1)) >> SLC_SHIFT
    forced = (blk == 0) | (blk == cur) | (blk == cur - 1)
    score = jnp.where(forced, SEL_BONUS, imp_t)
    score = jnp.where(blk <= cur, score, -SEL_BONUS)
    def live_ranks(k):
        n_live = min((k + 1) * (tq // SLC_LEN), n_slc)

        def ranks():
            r = _block_ranks(score[0:n_live], n_live)
            if n_live < n_slc:
                r = jnp.concatenate([r, jnp.zeros((n_slc - n_live, tq), jnp.float32)], axis=0)
            return r
        return ranks

    rank = lax.switch(i, [live_ranks(k) for k in range(seq // tq)])
    bias_t = jnp.where(rank < min(SLC_TOPK, n_slc), 0.0, MASK_VALUE)
    bias_t = jnp.concatenate(
        [bias_t, jnp.zeros((NSEL_PAD - n_slc, tq), jnp.float32)], axis=0)
    bias = bias_t.T.astype(jnp.bfloat16)
    bias_p = jnp.concatenate([bias] * hpp, axis=0)
    qa = [jnp.concatenate([qh[h], bias_p], axis=1) for h in parts]

    def sel_tile(k0, width, carry, diag):
        k0 = pl.multiple_of(k0, tq)
        logits = [_dot_nt(qa[h], ka_ref[pl.ds(k0, width), :]) for h in parts]
        out = []
        for s, (m, acc) in zip(logits, carry):
            if diag:
                c = lax.broadcasted_iota(jnp.int32, s.shape, 1)
                s = jnp.where(c <= row_query(s.shape) + (width - tq), s, MASK_VALUE)
            out.append(_softmax_step(s, m, acc, vs_ref[0, pl.ds(k0, width), :]))
        return tuple(out)

    init = tuple(_softmax_init(prow) for _ in parts)
    n_pairs = i >> 1
    carry = lax.fori_loop(0, n_pairs,
                          lambda p, c: sel_tile(p * (2 * tq), 2 * tq, c, False), init)
    k_tail = n_pairs * (2 * tq)
    carry = lax.cond((i & 1) == 0,
                     lambda c: sel_tile(k_tail, tq, c, True),
                     lambda c: sel_tile(k_tail, 2 * tq, c, True), carry)
    o_s = [_softmax_result(acc) for _, acc in carry]

    wt = WIN_T
    n_sub, n_wt = tq // wt, WINDOW // wt + 1
    tile_start = [q0 - WINDOW + k * wt for k in range(n_sub + n_wt - 1)]
    tile_off = [pl.multiple_of(jnp.maximum(st, 0), wt) for st in tile_start]

    def sub_rows(x, j):
        return jnp.concatenate(
            [x[hd * tq + j * wt:hd * tq + (j + 1) * wt] for hd in range(hpp)], axis=0)

    s_w = {(j, t, h): _dot_nt(sub_rows(qh[h], j), kw_ref[0, pl.ds(tile_off[j + t], wt), :])
           for j in range(n_sub) for t in range(n_wt) for h in parts}
    for (j, t, h), s in s_w.items():
        r = lax.broadcasted_iota(jnp.int32, s.shape, 0) & (wt - 1)
        c = lax.broadcasted_iota(jnp.int32, s.shape, 1)
        if t == 0:
            s = jnp.where(c > r, s, MASK_VALUE)
        if t == n_wt - 1:
            s = jnp.where(c <= r, s, MASK_VALUE)
        else:
            s = jnp.where(tile_start[j + t] >= 0, s, MASK_VALUE)
        s_w[j, t, h] = s
    o_w = []
    for h in parts:
        subs = []
        for j in range(n_sub):
            m_w = s_w[j, 0, h].max(axis=1, keepdims=True)
            for t in range(1, n_wt):
                m_w = jnp.maximum(m_w, s_w[j, t, h].max(axis=1, keepdims=True))
            acc_w = jnp.zeros((hpp * wt, 2 * HEAD_DIM), jnp.float32)
            for t in range(n_wt):
                p = jnp.exp2((s_w[j, t, h] - m_w).astype(jnp.bfloat16))
                v = vw_ref[0, pl.ds(tile_off[j + t], wt), :]
                acc_w = acc_w + _dot(p, jnp.concatenate([v, jnp.ones_like(v)], axis=1))
            subs.append(_softmax_result(acc_w))
        o_w.append(jnp.concatenate(
            [subs[j][hd * wt:(hd + 1) * wt] for hd in range(hpp) for j in range(n_sub)], axis=0))

    gates = jax.nn.sigmoid(zs_ref[0])
    lane = lax.broadcasted_iota(jnp.int32, gates.shape, 1)
    outs = []
    for r in range(NSA_GROUP):
        h, rows = r // hpp, slice((r % hpp) * tq, (r % hpp + 1) * tq)
        o_r = jnp.zeros((tq, HEAD_DIM), jnp.float32)
        for br, o_b in enumerate((o_c, o_s, o_w)):
            col = FOX_HEADS + 3 * (g * NSA_GROUP + r) + br
            gcol = jnp.sum(jnp.where(lane == col, gates, 0.0), axis=1, keepdims=True)
            o_r = o_r + gcol * o_b[h][rows]
        outs.append(o_r)
    o_ref[0] = jnp.concatenate(outs, axis=1).astype(o_ref.dtype)


def _nsa(z_main, z_small, kvc, ovt):
    nb, s, _ = z_main.shape
    nc = kvc.shape[2]
    gw = NSA_GROUP * HEAD_DIM
    full = lambda col0: pl.BlockSpec(
        (1, s, HEAD_DIM), lambda b, g, i, c=col0 // HEAD_DIM: (b, 0, c + g))
    return pl.pallas_call(
        functools.partial(_nsa_kernel, seq=s),
        out_shape=jax.ShapeDtypeStruct((nb, s, NSA_W), jnp.bfloat16),
        grid=(nb, NSA_KV_HEADS, s // NSA_TQ),
        in_specs=[
            pl.BlockSpec((1, NSA_TQ, gw), lambda b, g, i: (b, i, COL_QN // gw + g)),
            pl.BlockSpec((1, NSA_TQ, Z_SMALL), lambda b, g, i: (b, i, 0)),
            pl.BlockSpec((1, 1, nc, HEAD_DIM), lambda b, g, i: (b, g, 0, 0)),
            pl.BlockSpec((1, 1, nc, HEAD_DIM), lambda b, g, i: (b, NSA_KV_HEADS + g, 0, 0)),
            full(COL_KS), full(COL_VS), full(COL_KW), full(COL_VW),
            pl.BlockSpec(ovt.shape, lambda b, g, i: (0, 0)),
        ],
        out_specs=pl.BlockSpec((1, NSA_TQ, gw), lambda b, g, i: (b, i, g)),
        scratch_shapes=[pltpu.VMEM((s, HEAD_DIM + NSEL_PAD), jnp.bfloat16)],
        compiler_params=_cparams(("arbitrary", "arbitrary", "arbitrary")),
        name="nsa",
    )(z_main, z_small, kvc, kvc, z_main, z_main, z_main, z_main, ovt)


OUT_TM = 512
OUT_RC = 256


def _outproj_kernel(of_ref, on_ref, bf_ref, bn_ref, w_ref, x_ref, g_ref, nw_ref, sc_ref,
                    sh_ref, h_ref, u_ref):
    for r in range(OUT_TM // OUT_RC):
        rows = slice(r * OUT_RC, (r + 1) * OUT_RC)
        yf = _rms(of_ref[0, rows, :].astype(jnp.float32)) * bf_ref[...]
        yn = _rms(on_ref[0, rows, :].astype(jnp.float32)) * bn_ref[...]
        y = jnp.concatenate([yf.astype(jnp.bfloat16), yn.astype(jnp.bfloat16)], axis=1)
        h = x_ref[0, rows, :] + g_ref[0] * _dot(y, w_ref[...])
        h_ref[0, rows, :] = h
        u = _rms(h) * nw_ref[...]
        u_ref[0, rows, :] = (u * (1.0 + sc_ref[0]) + sh_ref[0]).astype(jnp.bfloat16)


def _outproj(o_f, o_n, beta_f, beta_n, w_out, x, gate, norm_w, sc, sh):
    nb, s, d = x.shape
    vec = lambda n: pl.BlockSpec((1, n), lambda b, i: (0, 0))
    bvec = pl.BlockSpec((1, 1, d), lambda b, i: (b, 0, 0))
    rows = lambda n: pl.BlockSpec((1, OUT_TM, n), lambda b, i: (b, i, 0))
    return pl.pallas_call(
        _outproj_kernel,
        out_shape=(jax.ShapeDtypeStruct((nb, s, d), jnp.float32),
                   jax.ShapeDtypeStruct((nb, s, d), jnp.bfloat16)),
        grid=(nb, s // OUT_TM),
        in_specs=[
            rows(FOX_W), rows(NSA_W), vec(FOX_W), vec(NSA_W),
            pl.BlockSpec((FOX_W + NSA_W, d), lambda b, i: (0, 0),
                         pipeline_mode=pl.Buffered(1)),
            rows(d), bvec, vec(d), bvec, bvec,
        ],
        out_specs=(rows(d), rows(d)),
        compiler_params=_cparams(("arbitrary", "arbitrary")),
        name="outproj",
    )(o_f, o_n, beta_f.reshape(1, FOX_W), beta_n.reshape(1, NSA_W), w_out, x,
      gate.reshape(nb, 1, d), norm_w.reshape(1, d), sc.reshape(nb, 1, d),
      sh.reshape(nb, 1, d))


FFN_TM = 1024
FFN_TF = 512
FFN_RC = 512
FFN_RC_LAST = 256


def _ffn_kernel(u_ref, h_ref, g_ref, fw_ref, wg_ref, wu_ref, wd_ref, o_ref, *, final):
    j = pl.program_id(2)
    last = pl.num_programs(2) - 1

    def partial_sum(rows):
        u = u_ref[0, rows, :]
        a = _dot(u, wg_ref[...])
        b = _dot(u, wu_ref[...])
        mid = (a * jax.nn.sigmoid(a) * b).astype(jnp.bfloat16)
        return _dot(mid, wd_ref[...])

    def chunks(rc):
        return [slice(r * rc, (r + 1) * rc) for r in range(FFN_TM // rc)]

    @pl.when(j == 0)
    def _():
        for rows in chunks(FFN_RC):
            o_ref[0, rows, :] = partial_sum(rows)

    @pl.when((j > 0) & (j < last))
    def _():
        for rows in chunks(FFN_RC):
            o_ref[0, rows, :] += partial_sum(rows)

    @pl.when(j == last)
    def _():
        for rows in chunks(FFN_RC_LAST):
            h2 = h_ref[0, rows, :] + g_ref[0] * (o_ref[0, rows, :] + partial_sum(rows))
            o_ref[0, rows, :] = _rms(h2) * fw_ref[...] if final else h2


def _ffn(u2, h1, gate, final_w, w_gate, w_up, w_down, final):
    nb, s, d = h1.shape
    f = w_gate.shape[1]
    return pl.pallas_call(
        functools.partial(_ffn_kernel, final=final),
        out_shape=jax.ShapeDtypeStruct((nb, s, d), jnp.float32),
        grid=(nb, s // FFN_TM, f // FFN_TF),
        in_specs=[
            pl.BlockSpec((1, FFN_TM, d), lambda b, i, j: (b, i, 0)),
            pl.BlockSpec((1, FFN_TM, d), lambda b, i, j: (b, i, 0),
                         pipeline_mode=pl.Buffered(1)),
            pl.BlockSpec((1, 1, d), lambda b, i, j: (b, 0, 0)),
            pl.BlockSpec((1, d), lambda b, i, j: (0, 0)),
            pl.BlockSpec((d, FFN_TF), lambda b, i, j: (0, j)),
            pl.BlockSpec((d, FFN_TF), lambda b, i, j: (0, j)),
            pl.BlockSpec((FFN_TF, d), lambda b, i, j: (j, 0)),
        ],
        out_specs=pl.BlockSpec((1, FFN_TM, d), lambda b, i, j: (b, i, 0)),
        compiler_params=_cparams(("arbitrary", "arbitrary", "arbitrary"), BIG_VMEM_LIMIT),
        name="ffn",
    )(u2, h1, gate.reshape(nb, 1, d), final_w.reshape(1, d), w_gate, w_up, w_down)


def _split_w_in(w_in):
    sizes = ([FOX_W] * 3 + [FOX_HEADS] + [NSA_W] + [KV_W] * 6 + [3 * NSA_HEADS])
    offs = [0]
    for sz in sizes:
        offs.append(offs[-1] + sz)
    wt = w_in.T
    order = (0, 1, 2, 4, 7, 9, 8, 10, 5, 6)
    units = lambda ns, unit: [offs[n] + r for n in ns for r in range(0, sizes[n], unit)]
    main = _gather_rows(wt, units(order, KV_W), KV_W, jnp.bfloat16)
    narrow = _gather_rows(wt, units((3, 11), SUBLANES), SUBLANES, jnp.float32)
    small = jnp.concatenate(
        [narrow, jnp.zeros((Z_SMALL - narrow.shape[0], w_in.shape[0]), jnp.float32)], axis=0)
    return main, small.astype(jnp.bfloat16)


def _cast_kernel(rows_ref, w_ref, o_ref):
    del rows_ref
    o_ref[...] = w_ref[...].astype(o_ref.dtype)


def _gather_rows(wt, src_rows, unit, dtype):
    assert all(r % SUBLANES == 0 for r in src_rows), src_rows
    tiles = jnp.asarray([r // SUBLANES for r in src_rows], jnp.int32)
    n_units = len(src_rows)
    d = wt.shape[1]
    return pl.pallas_call(
        _cast_kernel,
        out_shape=jax.ShapeDtypeStruct((n_units * unit, d), dtype),
        grid_spec=pltpu.PrefetchScalarGridSpec(
            num_scalar_prefetch=1,
            grid=(n_units,),
            in_specs=[pl.BlockSpec((pl.Element(unit), pl.Element(d)),
                                   lambda u, rows: (rows[u] * SUBLANES, 0))],
            out_specs=pl.BlockSpec((unit, d), lambda u, rows: (u, 0)),
        ),
        compiler_params=_cparams(("arbitrary",)),
        name="w_in_prep",
    )(tiles, wt)


def _overlap_t(n_cmp_pad, n_slc):
    cs = jnp.arange(n_cmp_pad) * CMP_STRIDE
    ss = jnp.arange(n_slc) * SLC_LEN
    ov = (jnp.minimum(cs[None, :] + CMP_LEN, ss[:, None] + SLC_LEN)
          - jnp.maximum(cs[None, :], ss[:, None]))
    ov = jnp.clip(ov, 0).astype(jnp.float32) / CMP_STRIDE
    return ov.astype(jnp.bfloat16)


def kernel(x, c, positions, w_ada, b_ada, norm_attn, norm_ffn, w_in, b_fgate, cmp_pos,
           w_kc1, w_kc2, w_vc1, w_vc2, beta_fox, beta_nsa, w_out, w_gate, w_up, w_down,
           final_norm):
    nb, s, d = x.shape
    depth = w_ada.shape[0]
    n_cmp_pad = s // CMP_STRIDE
    cmp_end = jnp.arange(n_cmp_pad) * CMP_STRIDE + (CMP_LEN - 1)
    pos_c = jnp.take(positions, jnp.minimum(cmp_end, s - 1), axis=1)
    cos_c, sin_c = _rope_tables(pos_c, n_cmp_pad)
    ovt = _overlap_t(n_cmp_pad, s // SLC_LEN)

    h = x
    for l in range(depth):
        if l == 0:
            mod, cos, sin = _ada(c, w_ada[l], b_ada[l], positions)
        else:
            mod = _ada(c, w_ada[l], b_ada[l])
        sh1, sc1, g1, sh2, sc2, g2 = jnp.split(mod, 6, axis=-1)
        w_main, w_small = _split_w_in(w_in[l])
        z_main, z_small, zc = _inproj(h, norm_attn[l], sc1, sh1, w_main, w_small, cos, sin)
        cum = _decay(z_small, b_fgate[l])

        w1 = jnp.stack([w_kc1[l], w_vc1[l]]).astype(jnp.bfloat16)
        w1 = w1.reshape(2, CMP_LEN, HEAD_DIM, CMP_HIDDEN)
        w2 = jnp.stack([w_kc2[l], w_vc2[l]]).astype(jnp.bfloat16)
        kvc = _compress(zc, w1, w2, cmp_pos[l], cos_c, sin_c)

        o_f, (wo, wg, wu, wd) = _fox(z_main, cum, (w_out[l], w_gate[l], w_up[l], w_down[l]))
        o_n = _nsa(z_main, z_small, kvc, ovt)
        h1, u2 = _outproj(o_f, o_n, beta_fox[l], beta_nsa[l], wo, h, g1, norm_ffn[l], sc2, sh2)
        h = _ffn(u2, h1, g2, final_norm, wg, wu, wd, final=(l == depth - 1))
    return h
```
